```python
import math
import jax
import jax.numpy as jnp
from jax import lax
import numpy as np

D_MODEL = 1024
BATCH = 32
SEQ = 256
DEPTH = 1
DEC_BATCH = 4
DEC_SEQ = 4096
PAST_LEN = 512

GRID_W = 64
MIX_WIDTH = D_MODEL
S5_WIDTH = MIX_WIDTH // 2
S5_GROUP = 16
S5_GROUPS = S5_WIDTH // S5_GROUP
S5_STATE = 64
ATTN_WIDTH = MIX_WIDTH - S5_WIDTH
DIFF_HEAD_DIM = 64
VALUE_DIM = 2 * DIFF_HEAD_DIM
N_DIFF_HEADS = ATTN_WIDTH // VALUE_DIM
IN_WIDTH = S5_WIDTH + 3 * ATTN_WIDTH
ROT_PAIRS = DIFF_HEAD_DIM // 4
ROPE_THETA = 10000.0
Q_BLOCK = 128
N_EXPERTS = 64
TOP_K = 8
N_EXPERT_GROUPS = 8
TOPK_GROUPS = 4
EXPERT_FF = D_MODEL // 4
ROUTED_SCALE = 2.5
NORM_EPS = 1e-6

kernel_name = 'hymba_s5_diffattn_moe_dit_step'


def rmsnorm(x, w):
    x32 = x.astype(jnp.float32)
    y = x32 * lax.rsqrt(jnp.mean(x32 * x32, axis=-1, keepdims=True) + NORM_EPS)
    return (y * w.astype(jnp.float32)).astype(x.dtype)


def swiglu(t, w_gate, w_up, w_down):
    return (jax.nn.silu(t @ w_gate) * (t @ w_up)) @ w_down


def axial_rope_angles(n_tokens):
    rows = n_tokens // GRID_W
    row = jnp.repeat(jnp.arange(rows, dtype=jnp.float32), GRID_W)
    col = jnp.tile(jnp.arange(GRID_W, dtype=jnp.float32), rows)
    freqs = ROPE_THETA ** (-jnp.arange(ROT_PAIRS, dtype=jnp.float32) / ROT_PAIRS)
    return row[:, None] * freqs, col[:, None] * freqs


def rotate_pairs(x, ang):
    cos = jnp.cos(ang)[:, None, None, :].astype(x.dtype)
    sin = jnp.sin(ang)[:, None, None, :].astype(x.dtype)
    x1, x2 = x[..., :ROT_PAIRS], x[..., ROT_PAIRS:]
    return jnp.concatenate([x1 * cos - x2 * sin, x2 * cos + x1 * sin], axis=-1)


def axial_rope(x, ang_row, ang_col):
    half = DIFF_HEAD_DIM // 2
    return jnp.concatenate([rotate_pairs(x[..., :half], ang_row),
                            rotate_pairs(x[..., half:], ang_col)], axis=-1)


def ssm_combine(e1, e2):
    a1, b1 = e1
    a2, b2 = e2
    return a1 * a2, a2 * b1 + b2


def to_complex(re, im):
    return lax.complex(re.astype(jnp.float32), im.astype(jnp.float32))


def s5_direction(u_c, lam_re, lam_im, log_dt, b_re, b_im, c_re, c_im, h0, reverse):
    lam = to_complex(lam_re, lam_im)
    dt = jnp.exp(log_dt.astype(jnp.float32))[:, None]
    lam_bar = jnp.exp(lam * dt)
    b_bar = ((lam_bar - 1.0) / lam)[:, :, None] * to_complex(b_re, b_im)
    bu = jnp.einsum('blgh,gnh->blgn', u_c, b_bar)
    a = jnp.broadcast_to(lam_bar, bu.shape)
    a_cum, h = lax.associative_scan(ssm_combine, (a, bu), axis=1, reverse=reverse)
    if h0 is not None:
        h = h + a_cum * h0[:, None]
    y = jnp.einsum('blgn,ghn->blgh', h, to_complex(c_re, c_im)).real
    h_final = h[:, 0] if reverse else h[:, -1]
    return y, h_final


def s5_mixer(u, p, h0_re, h0_im):
    bsz, seq_len, _ = u.shape
    u32 = u.astype(jnp.float32)
    u_c = u32.reshape(bsz, seq_len, S5_GROUPS, S5_GROUP).astype(jnp.complex64)
    y = u32 * p['ssm_d'].astype(jnp.float32)
    finals = []
    for d, reverse in enumerate((False, True)):
        h0 = None if h0_re is None else to_complex(h0_re[:, d], h0_im[:, d])
        y_d, h_fin = s5_direction(u_c, p['ssm_lambda_re'][d], p['ssm_lambda_im'][d], p['ssm_log_dt'][d],
                                  p['ssm_b_re'][d], p['ssm_b_im'][d], p['ssm_c_re'][d], p['ssm_c_im'][d],
                                  h0, reverse)
        y = y + y_d.reshape(bsz, seq_len, S5_WIDTH)
        finals.append(h_fin)
    g = jax.nn.gelu(y)
    out = g * jax.nn.sigmoid(g @ p['ssm_w_glu'].astype(jnp.float32))
    h_fin = jnp.stack(finals, axis=1)
    return out.astype(u.dtype), h_fin.real, h_fin.imag


def diff_attn_block(q, k, v, lam):
    s = jnp.einsum('bqhmd,bkhmd->bhmqk', q, k, preferred_element_type=jnp.float32) * (DIFF_HEAD_DIM ** -0.5)
    pr = jax.nn.softmax(s, axis=-1)
    w = pr[:, :, 0] - lam * pr[:, :, 1]
    return jnp.einsum('bhqk,bkhe->bqhe', w.astype(v.dtype), v)


def mixing(h, p, lambda_init, ctx_k, ctx_v, h0_re, h0_im):
    bsz, seq_len, _ = h.shape
    proj = h @ p['w_in']
    u, q, k, v = jnp.split(proj, [S5_WIDTH, S5_WIDTH + ATTN_WIDTH, S5_WIDTH + 2 * ATTN_WIDTH], axis=-1)
    q = q.reshape(bsz, seq_len, N_DIFF_HEADS, 2, DIFF_HEAD_DIM)
    k = k.reshape(bsz, seq_len, N_DIFF_HEADS, 2, DIFF_HEAD_DIM)
    v = v.reshape(bsz, seq_len, N_DIFF_HEADS, VALUE_DIM)
    s5_out, h_re, h_im = s5_mixer(u, p, h0_re, h0_im)
    lq = p['diff_lambda_q'].astype(jnp.float32)
    lk = p['diff_lambda_k'].astype(jnp.float32)
    lam = jnp.exp(jnp.sum(lq[0] * lk[0])) - jnp.exp(jnp.sum(lq[1] * lk[1])) + lambda_init
    k_ctx_layout = k.reshape(bsz, seq_len, N_DIFF_HEADS, VALUE_DIM)
    if ctx_k is None:
        o = diff_attn_block(q, k, v, lam)
    else:
        ang_row, ang_col = axial_rope_angles(seq_len)
        q = axial_rope(q, ang_row, ang_col)
        k = axial_rope(k, ang_row, ang_col)
        ctx_len = ctx_k.shape[1]
        k_all = jnp.concatenate(
            [k, ctx_k.reshape(bsz, ctx_len, N_DIFF_HEADS, 2, DIFF_HEAD_DIM).astype(k.dtype)], axis=1)
        v_all = jnp.concatenate([v, ctx_v.astype(v.dtype)], axis=1)
        n_blocks = seq_len // Q_BLOCK
        q_blocks = jnp.moveaxis(q.reshape(bsz, n_blocks, Q_BLOCK, N_DIFF_HEADS, 2, DIFF_HEAD_DIM), 1, 0)
        o = lax.map(lambda qb: diff_attn_block(qb, k_all, v_all, lam), q_blocks)
        o = jnp.moveaxis(o, 0, 1).reshape(bsz, seq_len, N_DIFF_HEADS, VALUE_DIM)
    o = rmsnorm(o, p['diff_subln_w']) * (1.0 - lambda_init)
    mixed = jnp.concatenate([s5_out, o.reshape(bsz, seq_len, ATTN_WIDTH)], axis=-1) @ p['w_out']
    return mixed, (k_ctx_layout, v, h_re, h_im)


def moe(h, p):
    bsz, seq_len, dm = h.shape
    t = h.reshape(bsz * seq_len, dm)
    scores = jax.nn.sigmoid((t @ p['w_router']).astype(jnp.float32))
    biased = scores + p['router_bias'].astype(jnp.float32)
    grouped = biased.reshape(-1, N_EXPERT_GROUPS, N_EXPERTS // N_EXPERT_GROUPS)
    group_score = lax.top_k(grouped, 2)[0].sum(-1)
    _, top_groups = lax.top_k(group_score, TOPK_GROUPS)
    group_mask = jax.nn.one_hot(top_groups, N_EXPERT_GROUPS, dtype=jnp.float32).sum(-2)
    expert_mask = jnp.repeat(group_mask, N_EXPERTS // N_EXPERT_GROUPS, axis=-1)
    masked = jnp.where(expert_mask > 0, biased, -jnp.inf)
    _, top_idx = lax.top_k(masked, TOP_K)
    sel = jnp.take_along_axis(scores, top_idx, axis=-1)
    wts = sel / jnp.sum(sel, axis=-1, keepdims=True) * ROUTED_SCALE
    gates = jnp.einsum('nk,nke->ne', wts, jax.nn.one_hot(top_idx, N_EXPERTS, dtype=jnp.float32))

    def expert_step(acc, e):
        wg, wu, wd, g = e
        return acc + g[:, None].astype(t.dtype) * swiglu(t, wg, wu, wd), None

    routed, _ = lax.scan(expert_step, jnp.zeros_like(t),
                         (p['w_exp_gate'], p['w_exp_up'], p['w_exp_down'], gates.T))
    out = routed + swiglu(t, p['w_sh_gate'], p['w_sh_up'], p['w_sh_down'])
    return out.reshape(bsz, seq_len, dm)


def block(x, cond, p, lambda_init, ctx_k, ctx_v, h0_re, h0_im):
    mod = jax.nn.silu(cond) @ p['w_ada'] + p['b_ada']
    shift1, scale1, gate1, shift2, scale2, gate2 = jnp.split(mod[:, None, :], 6, axis=-1)
    h = rmsnorm(x, p['norm1_w']) * (1.0 + scale1) + shift1
    mixed, ctx_state = mixing(h, p, lambda_init, ctx_k, ctx_v, h0_re, h0_im)
    x = x + gate1 * mixed
    h = rmsnorm(x, p['norm2_w']) * (1.0 + scale2) + shift2
    x = x + gate2 * moe(h, p)
    return x, ctx_state


def setup_inputs(seed: int = 0) -> dict:
    key = jax.random.key(seed)
    ks = jax.random.split(key, 40)
    f32 = jnp.float32

    def nrm(k, shape, scale):
        return jax.random.normal(k, shape, f32) * scale

    lam_im_base = jnp.pi * jnp.arange(S5_STATE, dtype=f32)
    return {
        'x_prompt': nrm(ks[0], (BATCH, SEQ, D_MODEL), 1.0),
        'x_sample': nrm(ks[1], (DEC_BATCH, DEC_SEQ, D_MODEL), 1.0),
        'c': nrm(ks[2], (DEC_BATCH, D_MODEL), 1.0),
        'cache_k': nrm(ks[3], (DEC_BATCH, DEPTH, PAST_LEN, N_DIFF_HEADS, VALUE_DIM), 1.0),
        'cache_v': nrm(ks[4], (DEC_BATCH, DEPTH, PAST_LEN, N_DIFF_HEADS, VALUE_DIM), 1.0),
        'state_ssm_re': nrm(ks[5], (DEC_BATCH, DEPTH, 2, S5_GROUPS, S5_STATE), 0.5),
        'state_ssm_im': nrm(ks[6], (DEC_BATCH, DEPTH, 2, S5_GROUPS, S5_STATE), 0.5),
        'c_ctx': nrm(ks[7], (D_MODEL,), 1.0),
        'w_ada': nrm(ks[8], (DEPTH, D_MODEL, 6 * D_MODEL), 0.5 * D_MODEL ** -0.5),
        'b_ada': nrm(ks[9], (DEPTH, 6 * D_MODEL), 0.02),
        'norm1_w': 1.0 + nrm(ks[10], (DEPTH, D_MODEL), 0.02),
        'w_in': nrm(ks[11], (DEPTH, D_MODEL, IN_WIDTH), D_MODEL ** -0.5),
        'ssm_lambda_re': -0.5 + nrm(ks[12], (DEPTH, 2, S5_GROUPS, S5_STATE), 0.02),
        'ssm_lambda_im': lam_im_base + nrm(ks[13], (DEPTH, 2, S5_GROUPS, S5_STATE), 0.02),
        'ssm_log_dt': jax.random.uniform(ks[14], (DEPTH, 2, S5_GROUPS), f32, math.log(1e-3), math.log(1e-1)),
        'ssm_b_re': nrm(ks[15], (DEPTH, 2, S5_GROUPS, S5_STATE, S5_GROUP), S5_GROUP ** -0.5),
        'ssm_b_im': nrm(ks[16], (DEPTH, 2, S5_GROUPS, S5_STATE, S5_GROUP), S5_GROUP ** -0.5),
        'ssm_c_re': nrm(ks[17], (DEPTH, 2, S5_GROUPS, S5_GROUP, S5_STATE), S5_STATE ** -0.5),
        'ssm_c_im': nrm(ks[18], (DEPTH, 2, S5_GROUPS, S5_GROUP, S5_STATE), S5_STATE ** -0.5),
        'ssm_d': nrm(ks[19], (DEPTH, S5_WIDTH), 1.0),
        'ssm_w_glu': nrm(ks[20], (DEPTH, S5_WIDTH, S5_WIDTH), S5_WIDTH ** -0.5),
        'diff_lambda_q': nrm(ks[21], (DEPTH, 2, DIFF_HEAD_DIM), 0.1),
        'diff_lambda_k': nrm(ks[22], (DEPTH, 2, DIFF_HEAD_DIM), 0.1),
        'diff_subln_w': 1.0 + nrm(ks[23], (DEPTH, VALUE_DIM), 0.02),
        'w_out': nrm(ks[24], (DEPTH, MIX_WIDTH, D_MODEL), MIX_WIDTH ** -0.5),
        'norm2_w': 1.0 + nrm(ks[25], (DEPTH, D_MODEL), 0.02),
        'w_router': nrm(ks[26], (DEPTH, D_MODEL, N_EXPERTS), D_MODEL ** -0.5),
        'router_bias': nrm(ks[27], (DEPTH, N_EXPERTS), 0.01),
        'w_exp_gate': nrm(ks[28], (DEPTH, N_EXPERTS, D_MODEL, EXPERT_FF), D_MODEL ** -0.5),
        'w_exp_up': nrm(ks[29], (DEPTH, N_EXPERTS, D_MODEL, EXPERT_FF), D_MODEL ** -0.5),
        'w_exp_down': nrm(ks[30], (DEPTH, N_EXPERTS, EXPERT_FF, D_MODEL), EXPERT_FF ** -0.5),
        'w_sh_gate': nrm(ks[31], (DEPTH, D_MODEL, EXPERT_FF), D_MODEL ** -0.5),
        'w_sh_up': nrm(ks[32], (DEPTH, D_MODEL, EXPERT_FF), D_MODEL ** -0.5),
        'w_sh_down': nrm(ks[33], (DEPTH, EXPERT_FF, D_MODEL), EXPERT_FF ** -0.5),
        'final_norm_w': 1.0 + nrm(ks[34], (D_MODEL,), 0.02),
    }


def reference(x_prompt, x_sample, c, cache_k, cache_v, state_ssm_re, state_ssm_im,
              c_ctx, w_ada, b_ada, norm1_w, w_in, ssm_lambda_re, ssm_lambda_im, ssm_log_dt,
              ssm_b_re, ssm_b_im, ssm_c_re, ssm_c_im, ssm_d, ssm_w_glu,
              diff_lambda_q, diff_lambda_k, diff_subln_w, w_out, norm2_w,
              w_router, router_bias, w_exp_gate, w_exp_up, w_exp_down,
              w_sh_gate, w_sh_up, w_sh_down, final_norm_w):
    xp, xs = x_prompt, x_sample
    ks, vs, hres, hims = [], [], [], []
    for l in range(DEPTH):
        p = {
            'w_ada': w_ada[l], 'b_ada': b_ada[l], 'norm1_w': norm1_w[l], 'w_in': w_in[l],
            'ssm_lambda_re': ssm_lambda_re[l], 'ssm_lambda_im': ssm_lambda_im[l], 'ssm_log_dt': ssm_log_dt[l],
            'ssm_b_re': ssm_b_re[l], 'ssm_b_im': ssm_b_im[l], 'ssm_c_re': ssm_c_re[l], 'ssm_c_im': ssm_c_im[l],
            'ssm_d': ssm_d[l], 'ssm_w_glu': ssm_w_glu[l],
            'diff_lambda_q': diff_lambda_q[l], 'diff_lambda_k': diff_lambda_k[l], 'diff_subln_w': diff_subln_w[l],
            'w_out': w_out[l], 'norm2_w': norm2_w[l], 'w_router': w_router[l], 'router_bias': router_bias[l],
            'w_exp_gate': w_exp_gate[l], 'w_exp_up': w_exp_up[l], 'w_exp_down': w_exp_down[l],
            'w_sh_gate': w_sh_gate[l], 'w_sh_up': w_sh_up[l], 'w_sh_down': w_sh_down[l],
        }
        lambda_init = 0.8 - 0.6 * math.exp(-0.3 * l)
        xp, (k_new, v_new, h_re, h_im) = block(xp, c_ctx[None, :], p, lambda_init, None, None, None, None)
        xs, _ = block(xs, c, p, lambda_init, cache_k[:, l], cache_v[:, l], state_ssm_re[:, l], state_ssm_im[:, l])
        ks.append(k_new)
        vs.append(v_new)
        hres.append(h_re)
        hims.append(h_im)
    y_prompt = rmsnorm(xp, final_norm_w)
    y_sample = rmsnorm(xs, final_norm_w)
    new_cache_k = jnp.stack(ks, axis=1)
    new_cache_v = jnp.stack(vs, axis=1)
    new_state_ssm_re = jnp.stack(hres, axis=1)
    new_state_ssm_im = jnp.stack(hims, axis=1)
    return (y_prompt, y_sample, new_cache_k, new_cache_v, new_state_ssm_re, new_state_ssm_im)
```

```python
import functools
import math

import jax
import jax.numpy as jnp
from jax import lax
from jax.experimental import pallas as pl
from jax.experimental.pallas import tpu as pltpu

F32 = jnp.float32
BF16 = jnp.bfloat16

D_MODEL = 1024
GRID_W = 64
S5_WIDTH = 512
S5_GROUP = 16
S5_GROUPS = 32
S5_STATE = 64
ATTN_WIDTH = 512
DIFF_HEAD_DIM = 64
VALUE_DIM = 128
N_DIFF_HEADS = 4
IN_WIDTH = S5_WIDTH + 3 * ATTN_WIDTH
ROT_PAIRS = DIFF_HEAD_DIM // 4
ROPE_THETA = 10000.0
N_EXPERTS = 64
TOP_K = 8
N_EXPERT_GROUPS = 8
TOPK_GROUPS = 4
EXPERT_FF = 256
ROUTED_SCALE = 2.5
NORM_EPS = 1e-6

LANES = 128
SUBLANES = 8
S5_COLBLK = 8 * S5_GROUP
S5_BLK_STATES = 8 * S5_STATE
N_COLBLK = S5_WIDTH // S5_COLBLK
VMEM_LIMIT = 56 * 1024 * 1024


def _cparams(sem):
    return pltpu.CompilerParams(dimension_semantics=sem, vmem_limit_bytes=VMEM_LIMIT)


def _ada_kernel(cond_ref, w_ref, b_ref, o_ref):
    c = cond_ref[...]
    s = c * jax.nn.sigmoid(c)
    o_ref[...] = jnp.dot(s.astype(BF16), w_ref[...].astype(BF16),
                         preferred_element_type=F32) + b_ref[...]


def _ada_call(cond8, w_ada, b_ada):
    n = w_ada.shape[1]
    tn = 1536
    return pl.pallas_call(
        _ada_kernel,
        grid=(n // tn,),
        in_specs=[pl.BlockSpec((SUBLANES, D_MODEL), lambda j: (0, 0)),
                  pl.BlockSpec((D_MODEL, tn), lambda j: (0, j)),
                  pl.BlockSpec((1, tn), lambda j: (0, j))],
        out_specs=pl.BlockSpec((SUBLANES, tn), lambda j: (0, j)),
        out_shape=jax.ShapeDtypeStruct((SUBLANES, n), F32),
        compiler_params=_cparams(("arbitrary",)),
        name="adaln",
    )(cond8, w_ada, b_ada.reshape(1, n))


def _rope_apply(t, cos, sin):
    parts = []
    for cidx in range(ATTN_WIDTH // LANES):
        xc = t[:, cidx * LANES:(cidx + 1) * LANES]
        up = pltpu.roll(xc, LANES - ROT_PAIRS, 1)
        dn = pltpu.roll(xc, ROT_PAIRS, 1)
        lane = lax.broadcasted_iota(jnp.int32, xc.shape, 1)
        partner = jnp.where((lane % (2 * ROT_PAIRS)) < ROT_PAIRS, up, dn)
        parts.append(xc * cos + partner * sin)
    return jnp.concatenate(parts, axis=1)


def _inproj_kernel(*refs, rope, kv_dtype):
    if rope:
        x_ref, mod_ref, n1_ref, w_ref, cos_ref, sin_ref, u_ref, q_ref, k_ref, v_ref = refs
    else:
        x_ref, mod_ref, n1_ref, w_ref, u_ref, q_ref, k_ref, v_ref = refs
    x = x_ref[...]
    ms = jnp.mean(x * x, axis=-1, keepdims=True)
    xn = x * lax.rsqrt(ms + NORM_EPS) * n1_ref[...]
    mod = mod_ref[0]
    shift = mod[:, 0:D_MODEL]
    scale = mod[:, D_MODEL:2 * D_MODEL]
    h = xn * (1.0 + scale) + shift
    proj = jnp.dot(h.astype(BF16), w_ref[...], preferred_element_type=F32)
    u_ref[...] = proj[:, :S5_WIDTH]
    q = proj[:, S5_WIDTH:S5_WIDTH + ATTN_WIDTH]
    k = proj[:, S5_WIDTH + ATTN_WIDTH:S5_WIDTH + 2 * ATTN_WIDTH]
    v = proj[:, S5_WIDTH + 2 * ATTN_WIDTH:]
    if rope:
        cos = cos_ref[...]
        sin = sin_ref[...]
        q = _rope_apply(q, cos, sin)
        k = _rope_apply(k, cos, sin)
    q_ref[...] = (q * (DIFF_HEAD_DIM ** -0.5)).astype(BF16)
    k_ref[...] = k.astype(kv_dtype)
    v_ref[...] = v.astype(kv_dtype)


def _inproj_call(x2, mod3, norm1_w, w_in_bf, rope_tabs, seq_len, kv_dtype, tm=256):
    n = x2.shape[0]
    bpb = seq_len // tm
    rope = rope_tabs is not None
    in_specs = [pl.BlockSpec((tm, D_MODEL), lambda i: (i, 0)),
                pl.BlockSpec((1, 1, 6 * D_MODEL), lambda i: (i // bpb, 0, 0)),
                pl.BlockSpec((1, D_MODEL), lambda i: (0, 0)),
                pl.BlockSpec((D_MODEL, IN_WIDTH), lambda i: (0, 0))]
    args = [x2, mod3, norm1_w.reshape(1, D_MODEL), w_in_bf]
    if rope:
        in_specs += [pl.BlockSpec((tm, LANES), lambda i: (i % bpb, 0)),
                     pl.BlockSpec((tm, LANES), lambda i: (i % bpb, 0))]
        args += list(rope_tabs)
    ospec = pl.BlockSpec((tm, S5_WIDTH), lambda i: (i, 0))
    return pl.pallas_call(
        functools.partial(_inproj_kernel, rope=rope, kv_dtype=kv_dtype),
        grid=(n // tm,),
        in_specs=in_specs,
        out_specs=[ospec, ospec, ospec, ospec],
        out_shape=[jax.ShapeDtypeStruct((n, S5_WIDTH), F32),
                   jax.ShapeDtypeStruct((n, ATTN_WIDTH), BF16),
                   jax.ShapeDtypeStruct((n, ATTN_WIDTH), kv_dtype),
                   jax.ShapeDtypeStruct((n, ATTN_WIDTH), kv_dtype)],
        compiler_params=_cparams(("parallel",)),
        name="inproj_rope" if rope else "inproj",
    )(*args)


def _s5_slot(b, q, nseq):
    if nseq == SUBLANES:
        return b, q * 8
    half = q // 2
    return half * nseq + b, (q % 2) * 8


def _s5_kernel(u_ref, b_ref, c_ref, lr_ref, li_ref, h0_ref, y_ref, hfin_ref, buh, hst,
               *, nseq, tb, stride, nqq):
    d = pl.program_id(1)
    c = pl.program_id(2)
    nc = pl.num_programs(2)

    @pl.when(c == 0)
    def _():
        hst[...] = h0_ref[0, 0]
        buh[...] = jnp.zeros(buh.shape, F32)

    u = u_ref[...].reshape(nseq * tb, S5_WIDTH).astype(BF16)
    for q in range(N_COLBLK):
        res = jnp.dot(u[:, q * S5_COLBLK:(q + 1) * S5_COLBLK], b_ref[0, q],
                      preferred_element_type=F32)
        for b in range(nseq):
            slot, ct0 = _s5_slot(b, q, nseq)
            for j in range(8):
                buh[ct0 + j, slot * stride:slot * stride + tb, :] = (
                    res[b * tb:(b + 1) * tb, j * LANES:(j + 1) * LANES])

    for qq in range(nqq):
        lr = [lr_ref[0, qq, :, i * LANES:(i + 1) * LANES] for i in range(4)]
        li = [li_ref[0, qq, :, i * LANES:(i + 1) * LANES] for i in range(4)]
        h_init = tuple(hst[qq * 8 + i] for i in range(8))

        def body(s, carry, qq=qq, lr=lr, li=li):
            t = jnp.where(d == 0, s, tb - 1 - s)
            idx = pl.ds(t, SUBLANES, stride=stride)
            new_r, new_i = [], []
            for i in range(4):
                hr, hi = carry[i], carry[4 + i]
                bur = buh[qq * 8 + i, idx, :]
                bui = buh[qq * 8 + 4 + i, idx, :]
                nr = lr[i] * hr - li[i] * hi + bur
                ni = lr[i] * hi + li[i] * hr + bui
                buh[qq * 8 + i, idx, :] = nr
                buh[qq * 8 + 4 + i, idx, :] = ni
                new_r.append(nr)
                new_i.append(ni)
            return tuple(new_r) + tuple(new_i)

        h_fin = lax.fori_loop(0, tb, body, h_init)
        for i in range(8):
            hst[qq * 8 + i] = h_fin[i]

    rows = nseq * stride
    for q in range(N_COLBLK):
        slot0, ct0 = _s5_slot(0, q, nseq)
        hmat = jnp.concatenate(
            [buh[ct0 + j, slot0 * stride:slot0 * stride + rows, :] for j in range(8)], axis=1)
        yq = jnp.dot(hmat.astype(BF16), c_ref[0, q], preferred_element_type=F32)
        for b in range(nseq):
            y_ref[0, b, :, q * S5_COLBLK:(q + 1) * S5_COLBLK] = yq[b * stride:b * stride + tb, :]

    @pl.when(c == nc - 1)
    def _():
        hfin_ref[0, 0] = hst[...]


def _s5_call(u3, bblk, cblk, lr, li, h0, nseq, tb=64):
    nseq_total, seq_len, _ = u3.shape
    g = nseq_total // nseq
    nc = seq_len // tb
    stride = tb + SUBLANES
    nqq = lr.shape[1]
    nct = nqq * 8

    def tmap(gi, d, c):
        return c + d * (nc - 1 - 2 * c)

    return pl.pallas_call(
        functools.partial(_s5_kernel, nseq=nseq, tb=tb, stride=stride, nqq=nqq),
        grid=(g, 2, nc),
        in_specs=[
            pl.BlockSpec((nseq, tb, S5_WIDTH), lambda gi, d, c: (gi, tmap(gi, d, c), 0)),
            pl.BlockSpec((1, N_COLBLK, S5_COLBLK, 2 * S5_BLK_STATES), lambda gi, d, c: (d, 0, 0, 0)),
            pl.BlockSpec((1, N_COLBLK, 2 * S5_BLK_STATES, S5_COLBLK), lambda gi, d, c: (d, 0, 0, 0)),
            pl.BlockSpec((1, nqq, SUBLANES, S5_BLK_STATES), lambda gi, d, c: (d, 0, 0, 0)),
            pl.BlockSpec((1, nqq, SUBLANES, S5_BLK_STATES), lambda gi, d, c: (d, 0, 0, 0)),
            pl.BlockSpec((1, 1, nct, SUBLANES, LANES), lambda gi, d, c: (d, gi, 0, 0, 0)),
        ],
        out_specs=[
            pl.BlockSpec((1, nseq, tb, S5_WIDTH), lambda gi, d, c: (d, gi, tmap(gi, d, c), 0)),
            pl.BlockSpec((1, 1, nct, SUBLANES, LANES), lambda gi, d, c: (d, gi, 0, 0, 0)),
        ],
        out_shape=[jax.ShapeDtypeStruct((2, nseq_total, seq_len, S5_WIDTH), F32),
                   jax.ShapeDtypeStruct((2, g, nct, SUBLANES, LANES), F32)],
        scratch_shapes=[pltpu.VMEM((nct, SUBLANES * stride, LANES), F32),
                        pltpu.VMEM((nct, SUBLANES, LANES), F32)],
        compiler_params=_cparams(("parallel", "parallel", "arbitrary")),
        name="s5_scan_%d" % nseq,
    )(u3, bblk, cblk, lr, li, h0)


def _s5_weights(lam_re, lam_im, log_dt, b_re, b_im, c_re, c_im):
    dt = jnp.exp(log_dt.astype(F32))[..., None]
    lr = lam_re.astype(F32)
    li = lam_im.astype(F32)
    mag = jnp.exp(lr * dt)
    lbr = mag * jnp.cos(li * dt)
    lbi = mag * jnp.sin(li * dt)
    a = lbr - 1.0
    den = lr * lr + li * li
    cr = (a * lr + lbi * li) / den
    ci = (lbi * lr - a * li) / den
    bbr = cr[..., None] * b_re - ci[..., None] * b_im
    bbi = cr[..., None] * b_im + ci[..., None] * b_re
    eye = jnp.eye(8, dtype=F32)

    def pack_b(x):
        x = x.reshape(2, N_COLBLK, 8, S5_STATE, S5_GROUP).transpose(0, 1, 2, 4, 3)
        x = jnp.einsum('dqghn,gk->dqghkn', x, eye)
        return x.reshape(2, N_COLBLK, S5_COLBLK, S5_BLK_STATES)

    def pack_c(x):
        x = x.reshape(2, N_COLBLK, 8, S5_GROUP, S5_STATE).transpose(0, 1, 2, 4, 3)
        x = jnp.einsum('dqgnh,gk->dqgnkh', x, eye)
        return x.reshape(2, N_COLBLK, S5_BLK_STATES, S5_COLBLK)

    bblk = jnp.concatenate([pack_b(bbr), pack_b(bbi)], axis=-1).astype(BF16)
    cblk = jnp.concatenate([pack_c(c_re.astype(F32)), pack_c(-c_im.astype(F32))], axis=-2).astype(BF16)
    lam_r = lbr.reshape(2, N_COLBLK, S5_BLK_STATES)
    lam_i = lbi.reshape(2, N_COLBLK, S5_BLK_STATES)
    return bblk, cblk, lam_r, lam_i


def _diff_attention(q, ks, vs, lam, subln, lambda_init):
    lane = lax.broadcasted_iota(jnp.int32, q.shape, 1)
    dn = (((1,), (1,)), ((), ()))
    probs = []
    for m in range(2):
        sel = (lane < DIFF_HEAD_DIM) if m == 0 else (lane >= DIFF_HEAD_DIM)
        qm = jnp.where(sel, q, jnp.zeros_like(q))
        ss = [lax.dot_general(qm, k, dn, preferred_element_type=F32) for k in ks]
        mx = ss[0].max(axis=-1, keepdims=True)
        for s in ss[1:]:
            mx = jnp.maximum(mx, s.max(axis=-1, keepdims=True))
        ps = [jnp.exp(s - mx) for s in ss]
        l = ps[0].sum(axis=-1, keepdims=True)
        for p in ps[1:]:
            l = l + p.sum(axis=-1, keepdims=True)
        probs.append((ps, l))
    a0 = 1.0 / probs[0][1]
    a1 = lam / probs[1][1]
    o = None
    for si, v in enumerate(vs):
        w = probs[0][0][si] * a0 - probs[1][0][si] * a1
        t = jnp.dot(w.astype(BF16), v, preferred_element_type=F32)
        o = t if o is None else o + t
    ms = jnp.mean(o * o, axis=-1, keepdims=True)
    return o * lax.rsqrt(ms + NORM_EPS) * subln * (1.0 - lambda_init)


def _attn_ctx_kernel(lam_ref, q_ref, k_ref, v_ref, w_ref, o_ref, *, lambda_init):
    o = _diff_attention(q_ref[0], [k_ref[0].astype(BF16)], [v_ref[0].astype(BF16)],
                        lam_ref[0], w_ref[...], lambda_init)
    o_ref[0] = o.astype(BF16)


def _attn_lat_kernel(lam_ref, q_ref, k_ref, v_ref, ck_ref, cv_ref, w_ref, o_ref, *, lambda_init):
    o = _diff_attention(q_ref[0], [k_ref[0], ck_ref[0].astype(BF16)],
                        [v_ref[0], cv_ref[0].astype(BF16)], lam_ref[0], w_ref[...], lambda_init)
    o_ref[0] = o.astype(BF16)


def _attn_ctx_call(lam, q3, k3, v3, subln, lambda_init):
    bsz, seq_len, _ = q3.shape
    spec = pl.BlockSpec((1, seq_len, VALUE_DIM), lambda b, h: (b, 0, h))
    return pl.pallas_call(
        functools.partial(_attn_ctx_kernel, lambda_init=lambda_init),
        grid=(bsz, N_DIFF_HEADS),
        in_specs=[pl.BlockSpec(memory_space=pltpu.SMEM), spec, spec, spec,
                  pl.BlockSpec((1, VALUE_DIM), lambda b, h: (0, 0))],
        out_specs=spec,
        out_shape=jax.ShapeDtypeStruct((bsz, seq_len, ATTN_WIDTH), BF16),
        compiler_params=_cparams(("parallel", "parallel")),
        name="attn_ctx",
    )(lam, q3, k3, v3, subln)


def _attn_lat_call(lam, q3, k3, v3, ck3, cv3, subln, lambda_init, tq=256):
    bsz, seq_len, _ = q3.shape
    ctx_len = ck3.shape[1]
    qspec = pl.BlockSpec((1, tq, VALUE_DIM), lambda b, h, i: (b, i, h))
    kspec = pl.BlockSpec((1, seq_len, VALUE_DIM), lambda b, h, i: (b, 0, h))
    cspec = pl.BlockSpec((1, ctx_len, VALUE_DIM), lambda b, h, i: (b, 0, h))
    return pl.pallas_call(
        functools.partial(_attn_lat_kernel, lambda_init=lambda_init),
        grid=(bsz, N_DIFF_HEADS, seq_len // tq),
        in_specs=[pl.BlockSpec(memory_space=pltpu.SMEM), qspec, kspec, kspec, cspec, cspec,
                  pl.BlockSpec((1, VALUE_DIM), lambda b, h, i: (0, 0))],
        out_specs=qspec,
        out_shape=jax.ShapeDtypeStruct((bsz, seq_len, ATTN_WIDTH), BF16),
        compiler_params=_cparams(("parallel", "parallel", "arbitrary")),
        name="attn_lat",
    )(lam, q3, k3, v3, ck3, cv3, subln)


def _route(logits_t, bias3, tm):
    ng, ge = N_EXPERT_GROUPS, N_EXPERTS // N_EXPERT_GROUPS
    neg = jnp.float32(-jnp.inf)
    sc = jax.nn.sigmoid(logits_t).reshape(ng, ge, tm)
    bi = sc + bias3
    eio = lax.broadcasted_iota(jnp.int32, (ng, ge, tm), 1).astype(F32)
    gio = lax.broadcasted_iota(jnp.int32, (ng, ge, tm), 0).astype(F32)
    m1 = bi.max(axis=1, keepdims=True)
    i1 = jnp.where(bi == m1, eio, float(ge)).min(axis=1, keepdims=True)
    m2 = jnp.where(eio == i1, neg, bi).max(axis=1, keepdims=True)
    gs = jnp.broadcast_to(m1 + m2, (ng, ge, tm))
    cnt = jnp.zeros((ng, ge, tm), F32)
    for g2 in range(ng):
        o = gs[g2:g2 + 1]
        better = (o > gs) | ((o == gs) & (gio > float(g2)))
        cnt = cnt + jnp.where(better, 1.0, 0.0)
    v = jnp.where(cnt < float(TOPK_GROUPS), bi, neg)
    eidx = gio * float(ge) + eio
    selm = jnp.zeros((ng, ge, tm), F32)
    for _ in range(TOP_K):
        m = v.max(axis=0, keepdims=True).max(axis=1, keepdims=True)
        ix = jnp.where(v == m, eidx, float(N_EXPERTS)).min(axis=0, keepdims=True).min(axis=1, keepdims=True)
        oh = eidx == ix
        selm = jnp.where(oh, 1.0, selm)
        v = jnp.where(oh, neg, v)
    selsc = selm * sc
    ssum = selsc.sum(axis=0, keepdims=True).sum(axis=1, keepdims=True)
    return (selsc / ssum * ROUTED_SCALE).reshape(N_EXPERTS, tm)


def _post_kernel(x_ref, u_ref, yf_ref, yb_ref, ao_ref, mod_ref, d_ref, wglu_ref, wout_ref, n2_ref,
                 wrt_ref, rb_ref, x1_ref, h2_ref, gates_ref, *, tm):
    u = u_ref[...]
    y = u * d_ref[...] + yf_ref[0] + yb_ref[0]
    g = jax.nn.gelu(y)
    s5 = g * jax.nn.sigmoid(jnp.dot(g.astype(BF16), wglu_ref[...], preferred_element_type=F32))
    mixed = (jnp.dot(s5.astype(BF16), wout_ref[0:S5_WIDTH, :], preferred_element_type=F32)
             + jnp.dot(ao_ref[...], wout_ref[S5_WIDTH:, :], preferred_element_type=F32))
    mod = mod_ref[0]
    gate1 = mod[:, 2 * D_MODEL:3 * D_MODEL]
    shift2 = mod[:, 3 * D_MODEL:4 * D_MODEL]
    scale2 = mod[:, 4 * D_MODEL:5 * D_MODEL]
    x1 = x_ref[...] + gate1 * mixed
    x1_ref[...] = x1
    ms = jnp.mean(x1 * x1, axis=-1, keepdims=True)
    h2 = x1 * lax.rsqrt(ms + NORM_EPS) * n2_ref[...] * (1.0 + scale2) + shift2
    h2_ref[...] = h2.astype(BF16)
    logits_t = lax.dot_general(wrt_ref[...], h2, (((1,), (1,)), ((), ())),
                               preferred_element_type=F32, precision=lax.Precision.HIGHEST)
    gates_t = _route(logits_t, rb_ref[...], tm)
    gates_ref[...] = gates_t.T


def _post_call(x2, u2, y4, ao2, mod3, ssm_d, wglu_bf, wout_bf, norm2_w, wr_t, rbias3, seq_len, tm=256):
    n = x2.shape[0]
    bpb = seq_len // tm
    y3 = y4.reshape(2, n, S5_WIDTH)
    row = lambda i: (i, 0)
    const2 = lambda i: (0, 0)
    return pl.pallas_call(
        functools.partial(_post_kernel, tm=tm),
        grid=(n // tm,),
        in_specs=[pl.BlockSpec((tm, D_MODEL), row),
                  pl.BlockSpec((tm, S5_WIDTH), row),
                  pl.BlockSpec((1, tm, S5_WIDTH), lambda i: (0, i, 0)),
                  pl.BlockSpec((1, tm, S5_WIDTH), lambda i: (1, i, 0)),
                  pl.BlockSpec((tm, ATTN_WIDTH), row),
                  pl.BlockSpec((1, 1, 6 * D_MODEL), lambda i: (i // bpb, 0, 0)),
                  pl.BlockSpec((1, S5_WIDTH), const2),
                  pl.BlockSpec((S5_WIDTH, S5_WIDTH), const2),
                  pl.BlockSpec((D_MODEL, D_MODEL), const2),
                  pl.BlockSpec((1, D_MODEL), const2),
                  pl.BlockSpec((N_EXPERTS, D_MODEL), const2),
                  pl.BlockSpec((N_EXPERT_GROUPS, N_EXPERTS // N_EXPERT_GROUPS, 1), lambda i: (0, 0, 0))],
        out_specs=[pl.BlockSpec((tm, D_MODEL), row),
                   pl.BlockSpec((tm, D_MODEL), row),
                   pl.BlockSpec((tm, N_EXPERTS), row)],
        out_shape=[jax.ShapeDtypeStruct((n, D_MODEL), F32),
                   jax.ShapeDtypeStruct((n, D_MODEL), BF16),
                   jax.ShapeDtypeStruct((n, N_EXPERTS), F32)],
        compiler_params=_cparams(("parallel",)),
        name="post_mix_route",
    )(x2, u2, y3, y3, ao2, mod3, ssm_d.reshape(1, S5_WIDTH), wglu_bf, wout_bf,
      norm2_w.reshape(1, D_MODEL), wr_t, rbias3)


def _swiglu_bf(t, wg, wu, wd):
    a = jnp.dot(t, wg, preferred_element_type=F32)
    b = jnp.dot(t, wu, preferred_element_type=F32)
    act = a * jax.nn.sigmoid(a) * b
    return jnp.dot(act.astype(BF16), wd, preferred_element_type=F32)


def _moe_kernel(h_ref, x1_ref, gates_ref, mod_ref, wg_ref, wu_ref, wd_ref, sg_ref, su_ref, sd_ref,
                fn_ref, o_ref, acc):
    e = pl.program_id(1)
    t = h_ref[...]

    @pl.when(e == 0)
    def _():
        acc[...] = _swiglu_bf(t, sg_ref[...], su_ref[...], sd_ref[...])

    gates = gates_ref[...]
    lane = lax.broadcasted_iota(jnp.int32, gates.shape, 1)
    g = jnp.sum(jnp.where(lane == e, gates, 0.0), axis=1, keepdims=True)
    acc[...] += g * _swiglu_bf(t, wg_ref[0], wu_ref[0], wd_ref[0])

    @pl.when(e == pl.num_programs(1) - 1)
    def _():
        gate2 = mod_ref[0][:, 5 * D_MODEL:6 * D_MODEL]
        x2 = x1_ref[...] + gate2 * acc[...]
        ms = jnp.mean(x2 * x2, axis=-1, keepdims=True)
        o_ref[...] = x2 * lax.rsqrt(ms + NORM_EPS) * fn_ref[...]


def _moe_call(h2, x1, gates, mod3, wg, wu, wd, sg, su, sd, final_w, seq_len, tm):
    n = h2.shape[0]
    bpb = seq_len // tm
    row = lambda i, e: (i, 0)
    const2 = lambda i, e: (0, 0)
    return pl.pallas_call(
        _moe_kernel,
        grid=(n // tm, N_EXPERTS),
        in_specs=[pl.BlockSpec((tm, D_MODEL), row),
                  pl.BlockSpec((tm, D_MODEL), row),
                  pl.BlockSpec((tm, N_EXPERTS), row),
                  pl.BlockSpec((1, 1, 6 * D_MODEL), lambda i, e: (i // bpb, 0, 0)),
                  pl.BlockSpec((1, D_MODEL, EXPERT_FF), lambda i, e: (e, 0, 0)),
                  pl.BlockSpec((1, D_MODEL, EXPERT_FF), lambda i, e: (e, 0, 0)),
                  pl.BlockSpec((1, EXPERT_FF, D_MODEL), lambda i, e: (e, 0, 0)),
                  pl.BlockSpec((D_MODEL, EXPERT_FF), const2),
                  pl.BlockSpec((D_MODEL, EXPERT_FF), const2),
                  pl.BlockSpec((EXPERT_FF, D_MODEL), const2),
                  pl.BlockSpec((1, D_MODEL), const2)],
        out_specs=pl.BlockSpec((tm, D_MODEL), row),
        out_shape=jax.ShapeDtypeStruct((n, D_MODEL), F32),
        scratch_shapes=[pltpu.VMEM((tm, D_MODEL), F32)],
        compiler_params=_cparams(("parallel", "arbitrary")),
        name="moe",
    )(h2, x1, gates, mod3, wg, wu, wd, sg, su, sd, final_w.reshape(1, D_MODEL))


def _rope_tables(n_tokens):
    rows = n_tokens // GRID_W
    row = jnp.repeat(jnp.arange(rows, dtype=F32), GRID_W)
    col = jnp.tile(jnp.arange(GRID_W, dtype=F32), rows)
    freqs = ROPE_THETA ** (-jnp.arange(ROT_PAIRS, dtype=F32) / ROT_PAIRS)
    ar = row[:, None] * freqs
    ac = col[:, None] * freqs
    cos = jnp.concatenate([jnp.cos(ar), jnp.cos(ar), jnp.cos(ac), jnp.cos(ac)], axis=1)
    sin = jnp.concatenate([-jnp.sin(ar), jnp.sin(ar), -jnp.sin(ac), jnp.sin(ac)], axis=1)
    return jnp.tile(cos, (1, 2)), jnp.tile(sin, (1, 2))


def kernel(x_prompt, x_sample, c, cache_k, cache_v, state_ssm_re, state_ssm_im, c_ctx, w_ada, b_ada, norm1_w, w_in, ssm_lambda_re, ssm_lambda_im, ssm_log_dt, ssm_b_re, ssm_b_im, ssm_c_re, ssm_c_im, ssm_d, ssm_w_glu, diff_lambda_q, diff_lambda_k, diff_subln_w, w_out, norm2_w, w_router, router_bias, w_exp_gate, w_exp_up, w_exp_down, w_sh_gate, w_sh_up, w_sh_down, final_norm_w):
    depth = w_ada.shape[0]
    assert depth == 1
    l = 0
    lambda_init = 0.8 - 0.6 * math.exp(-0.3 * l)
    bsz, seq, _ = x_prompt.shape
    dbs, dseq, _ = x_sample.shape
    n_p, n_s = bsz * seq, dbs * dseq

    cond8 = jnp.zeros((SUBLANES, D_MODEL), F32).at[:dbs].set(c).at[dbs].set(c_ctx)
    mod = _ada_call(cond8, w_ada[l], b_ada[l])
    mod_s = mod[:dbs].reshape(dbs, 1, 6 * D_MODEL)
    mod_p = mod[dbs:dbs + 1].reshape(1, 1, 6 * D_MODEL)

    w_in_bf = w_in[l].astype(BF16)
    xp2 = x_prompt.reshape(n_p, D_MODEL)
    xs2 = x_sample.reshape(n_s, D_MODEL)
    u_p, q_p, k_p, v_p = _inproj_call(xp2, mod_p, norm1_w[l], w_in_bf, None, n_p, F32)
    u_s, q_s, k_s, v_s = _inproj_call(xs2, mod_s, norm1_w[l], w_in_bf, _rope_tables(dseq), dseq, BF16)

    bblk, cblk, lam_r, lam_i = _s5_weights(ssm_lambda_re[l], ssm_lambda_im[l], ssm_log_dt[l],
                                           ssm_b_re[l], ssm_b_im[l], ssm_c_re[l], ssm_c_im[l])
    lr_p = jnp.broadcast_to(lam_r[:, :, None, :], (2, N_COLBLK, SUBLANES, S5_BLK_STATES))
    li_p = jnp.broadcast_to(lam_i[:, :, None, :], (2, N_COLBLK, SUBLANES, S5_BLK_STATES))
    g_p = bsz // SUBLANES
    h0_p = jnp.zeros((2, g_p, N_COLBLK * 8, SUBLANES, LANES), F32)
    y_p, hfin = _s5_call(u_p.reshape(bsz, seq, S5_WIDTH), bblk, cblk, lr_p, li_p, h0_p, SUBLANES)

    def halves(x):
        x = x.reshape(2, 2, 2, 1, S5_BLK_STATES)
        x = jnp.broadcast_to(x, (2, 2, 2, dbs, S5_BLK_STATES))
        return x.transpose(0, 2, 1, 3, 4).reshape(2, 2, 2 * dbs, S5_BLK_STATES)

    def h0_tiles(s):
        s = s.astype(F32).reshape(dbs, 2, 2, 2, 4, LANES)
        return s.transpose(1, 3, 4, 2, 0, 5).reshape(2, 1, 2, 4, 2 * dbs, LANES)

    h0_s = jnp.concatenate([h0_tiles(state_ssm_re[:, l]), h0_tiles(state_ssm_im[:, l])], axis=3)
    h0_s = h0_s.reshape(2, 1, 16, SUBLANES, LANES)
    y_s, _ = _s5_call(u_s.reshape(dbs, dseq, S5_WIDTH), bblk, cblk, halves(lam_r), halves(lam_i), h0_s, dbs)

    lq = diff_lambda_q[l].astype(F32)
    lk = diff_lambda_k[l].astype(F32)
    lam = (jnp.exp(jnp.sum(lq[0] * lk[0])) - jnp.exp(jnp.sum(lq[1] * lk[1])) + lambda_init).reshape(1)
    subln = diff_subln_w[l].astype(F32).reshape(1, VALUE_DIM)
    ao_p = _attn_ctx_call(lam, q_p.reshape(bsz, seq, ATTN_WIDTH), k_p.reshape(bsz, seq, ATTN_WIDTH),
                          v_p.reshape(bsz, seq, ATTN_WIDTH), subln, lambda_init)
    past = cache_k.shape[2]
    ao_s = _attn_lat_call(lam, q_s.reshape(dbs, dseq, ATTN_WIDTH), k_s.reshape(dbs, dseq, ATTN_WIDTH),
                          v_s.reshape(dbs, dseq, ATTN_WIDTH),
                          cache_k[:, l].reshape(dbs, past, ATTN_WIDTH),
                          cache_v[:, l].reshape(dbs, past, ATTN_WIDTH), subln, lambda_init)

    wglu_bf = ssm_w_glu[l].astype(BF16)
    wout_bf = w_out[l].astype(BF16)
    wr_t = w_router[l].astype(F32).T
    rbias3 = router_bias[l].astype(F32).reshape(N_EXPERT_GROUPS, N_EXPERTS // N_EXPERT_GROUPS, 1)
    x1_p, h2_p, gates_p = _post_call(xp2, u_p, y_p, ao_p.reshape(n_p, ATTN_WIDTH), mod_p, ssm_d[l],
                                     wglu_bf, wout_bf, norm2_w[l], wr_t, rbias3, n_p)
    x1_s, h2_s, gates_s = _post_call(xs2, u_s, y_s, ao_s.reshape(n_s, ATTN_WIDTH), mod_s, ssm_d[l],
                                     wglu_bf, wout_bf, norm2_w[l], wr_t, rbias3, dseq)

    wg = w_exp_gate[l].astype(BF16)
    wu = w_exp_up[l].astype(BF16)
    wd = w_exp_down[l].astype(BF16)
    sg = w_sh_gate[l].astype(BF16)
    su = w_sh_up[l].astype(BF16)
    sd = w_sh_down[l].astype(BF16)
    y_prompt = _moe_call(h2_p, x1_p, gates_p, mod_p, wg, wu, wd, sg, su, sd, final_norm_w, n_p, 1024)
    y_sample = _moe_call(h2_s, x1_s, gates_s, mod_s, wg, wu, wd, sg, su, sd, final_norm_w, dseq, 1024)

    new_cache_k = k_p.reshape(bsz, 1, seq, N_DIFF_HEADS, VALUE_DIM)
    new_cache_v = v_p.reshape(bsz, 1, seq, N_DIFF_HEADS, VALUE_DIM)
    hf = hfin.reshape(2, g_p, N_COLBLK, 2, 4, SUBLANES, LANES)
    hf = hf.transpose(3, 1, 5, 0, 2, 4, 6).reshape(2, bsz, 1, 2, S5_GROUPS, S5_STATE)
    return (y_prompt.reshape(bsz, seq, D_MODEL), y_sample.reshape(dbs, dseq, D_MODEL),
            new_cache_k, new_cache_v, hf[0], hf[1])
```

```python
import functools
import math

import jax
import jax.numpy as jnp
from jax import lax
from jax.experimental import pallas as pl
from jax.experimental.pallas import tpu as pltpu

F32 = jnp.float32
BF16 = jnp.bfloat16

D_MODEL = 1024
GRID_W = 64
S5_WIDTH = 512
S5_GROUP = 16
S5_GROUPS = 32
S5_STATE = 64
ATTN_WIDTH = 512
DIFF_HEAD_DIM = 64
VALUE_DIM = 128
N_DIFF_HEADS = 4
IN_WIDTH = S5_WIDTH + 3 * ATTN_WIDTH
ROT_PAIRS = DIFF_HEAD_DIM // 4
ROPE_THETA = 10000.0
N_EXPERTS = 64
TOP_K = 8
N_EXPERT_GROUPS = 8
TOPK_GROUPS = 4
EXPERT_FF = 256
ROUTED_SCALE = 2.5
NORM_EPS = 1e-6

LANES = 128
SUBLANES = 8
S5_COLBLK = 8 * S5_GROUP
S5_BLK_STATES = 8 * S5_STATE
N_COLBLK = S5_WIDTH // S5_COLBLK
VMEM_LIMIT = 56 * 1024 * 1024


def _cparams(sem):
    return pltpu.CompilerParams(dimension_semantics=sem, vmem_limit_bytes=VMEM_LIMIT)


def _ada_kernel(cond_ref, w_ref, b_ref, o_ref):
    c = cond_ref[...]
    s = c * jax.nn.sigmoid(c)
    o_ref[...] = jnp.dot(s.astype(BF16), w_ref[...].astype(BF16),
                         preferred_element_type=F32) + b_ref[...]


def _ada_call(cond8, w_ada, b_ada):
    n = w_ada.shape[1]
    tn = 1536
    return pl.pallas_call(
        _ada_kernel,
        grid=(n // tn,),
        in_specs=[pl.BlockSpec((SUBLANES, D_MODEL), lambda j: (0, 0)),
                  pl.BlockSpec((D_MODEL, tn), lambda j: (0, j)),
                  pl.BlockSpec((1, tn), lambda j: (0, j))],
        out_specs=pl.BlockSpec((SUBLANES, tn), lambda j: (0, j)),
        out_shape=jax.ShapeDtypeStruct((SUBLANES, n), F32),
        compiler_params=_cparams(("arbitrary",)),
        name="adaln",
    )(cond8, w_ada, b_ada.reshape(1, n))


def _rope_apply(t, cos, sin):
    parts = []
    for cidx in range(ATTN_WIDTH // LANES):
        xc = t[:, cidx * LANES:(cidx + 1) * LANES]
        up = pltpu.roll(xc, LANES - ROT_PAIRS, 1)
        dn = pltpu.roll(xc, ROT_PAIRS, 1)
        lane = lax.broadcasted_iota(jnp.int32, xc.shape, 1)
        partner = jnp.where((lane % (2 * ROT_PAIRS)) < ROT_PAIRS, up, dn)
        parts.append(xc * cos + partner * sin)
    return jnp.concatenate(parts, axis=1)


def _inproj_kernel(*refs, rope, kv_dtype):
    if rope:
        x_ref, mod_ref, n1_ref, w_ref, cos_ref, sin_ref, u_ref, q_ref, k_ref, v_ref = refs
    else:
        x_ref, mod_ref, n1_ref, w_ref, u_ref, q_ref, k_ref, v_ref = refs
    x = x_ref[...]
    ms = jnp.mean(x * x, axis=-1, keepdims=True)
    xn = x * lax.rsqrt(ms + NORM_EPS) * n1_ref[...]
    mod = mod_ref[0]
    shift = mod[:, 0:D_MODEL]
    scale = mod[:, D_MODEL:2 * D_MODEL]
    h = xn * (1.0 + scale) + shift
    proj = jnp.dot(h.astype(BF16), w_ref[...], preferred_element_type=F32)
    u_ref[...] = proj[:, :S5_WIDTH]
    q = proj[:, S5_WIDTH:S5_WIDTH + ATTN_WIDTH]
    k = proj[:, S5_WIDTH + ATTN_WIDTH:S5_WIDTH + 2 * ATTN_WIDTH]
    v = proj[:, S5_WIDTH + 2 * ATTN_WIDTH:]
    if rope:
        cos = cos_ref[...]
        sin = sin_ref[...]
        q = _rope_apply(q, cos, sin)
        k = _rope_apply(k, cos, sin)
    q_ref[...] = (q * (DIFF_HEAD_DIM ** -0.5)).astype(BF16)
    k_ref[...] = k.astype(kv_dtype)
    v_ref[...] = v.astype(kv_dtype)


def _inproj_call(x2, mod3, norm1_w, w_in_bf, rope_tabs, seq_len, kv_dtype, tm=256):
    n = x2.shape[0]
    bpb = seq_len // tm
    rope = rope_tabs is not None
    in_specs = [pl.BlockSpec((tm, D_MODEL), lambda i: (i, 0)),
                pl.BlockSpec((1, 1, 6 * D_MODEL), lambda i: (i // bpb, 0, 0)),
                pl.BlockSpec((1, D_MODEL), lambda i: (0, 0)),
                pl.BlockSpec((D_MODEL, IN_WIDTH), lambda i: (0, 0))]
    args = [x2, mod3, norm1_w.reshape(1, D_MODEL), w_in_bf]
    if rope:
        in_specs += [pl.BlockSpec((tm, LANES), lambda i: (i % bpb, 0)),
                     pl.BlockSpec((tm, LANES), lambda i: (i % bpb, 0))]
        args += list(rope_tabs)
    ospec = pl.BlockSpec((tm, S5_WIDTH), lambda i: (i, 0))
    return pl.pallas_call(
        functools.partial(_inproj_kernel, rope=rope, kv_dtype=kv_dtype),
        grid=(n // tm,),
        in_specs=in_specs,
        out_specs=[ospec, ospec, ospec, ospec],
        out_shape=[jax.ShapeDtypeStruct((n, S5_WIDTH), F32),
                   jax.ShapeDtypeStruct((n, ATTN_WIDTH), BF16),
                   jax.ShapeDtypeStruct((n, ATTN_WIDTH), kv_dtype),
                   jax.ShapeDtypeStruct((n, ATTN_WIDTH), kv_dtype)],
        compiler_params=_cparams(("parallel",)),
        name="inproj_rope" if rope else "inproj",
    )(*args)


def _s5_slot(b, q, nseq):
    if nseq == SUBLANES:
        return b, q * 8
    half = q // 2
    return half * nseq + b, (q % 2) * 8


def _s5_kernel(u_ref, b_ref, c_ref, lr_ref, li_ref, h0_ref, y_ref, hfin_ref, buh, hst,
               *, nseq, tb, stride, nqq):
    d = pl.program_id(1)
    c = pl.program_id(2)
    nc = pl.num_programs(2)

    @pl.when(c == 0)
    def _():
        hst[...] = h0_ref[0, 0]
        buh[...] = jnp.zeros(buh.shape, F32)

    u = u_ref[...].reshape(nseq * tb, S5_WIDTH).astype(BF16)
    for q in range(N_COLBLK):
        res = jnp.dot(u[:, q * S5_COLBLK:(q + 1) * S5_COLBLK], b_ref[0, q],
                      preferred_element_type=F32)
        for b in range(nseq):
            slot, ct0 = _s5_slot(b, q, nseq)
            for j in range(8):
                buh[ct0 + j, slot * stride:slot * stride + tb, :] = (
                    res[b * tb:(b + 1) * tb, j * LANES:(j + 1) * LANES])

    for qq in range(nqq):
        lr = [lr_ref[0, qq, :, i * LANES:(i + 1) * LANES] for i in range(4)]
        li = [li_ref[0, qq, :, i * LANES:(i + 1) * LANES] for i in range(4)]
        h_init = tuple(hst[qq * 8 + i] for i in range(8))

        def body(s, carry, qq=qq, lr=lr, li=li):
            t = jnp.where(d == 0, s, tb - 1 - s)
            idx = pl.ds(t, SUBLANES, stride=stride)
            new_r, new_i = [], []
            for i in range(4):
                hr, hi = carry[i], carry[4 + i]
                bur = buh[qq * 8 + i, idx, :]
                bui = buh[qq * 8 + 4 + i, idx, :]
                nr = lr[i] * hr - li[i] * hi + bur
                ni = lr[i] * hi + li[i] * hr + bui
                buh[qq * 8 + i, idx, :] = nr
                buh[qq * 8 + 4 + i, idx, :] = ni
                new_r.append(nr)
                new_i.append(ni)
            return tuple(new_r) + tuple(new_i)

        h_fin = lax.fori_loop(0, tb, body, h_init)
        for i in range(8):
            hst[qq * 8 + i] = h_fin[i]

    rows = nseq * stride
    for q in range(N_COLBLK):
        slot0, ct0 = _s5_slot(0, q, nseq)
        hmat = jnp.concatenate(
            [buh[ct0 + j, slot0 * stride:slot0 * stride + rows, :] for j in range(8)], axis=1)
        yq = jnp.dot(hmat.astype(BF16), c_ref[0, q], preferred_element_type=F32)
        for b in range(nseq):
            y_ref[0, b, :, q * S5_COLBLK:(q + 1) * S5_COLBLK] = yq[b * stride:b * stride + tb, :]

    @pl.when(c == nc - 1)
    def _():
        hfin_ref[0, 0] = hst[...]


def _s5_call(u3, bblk, cblk, lr, li, h0, nseq, tb=64):
    nseq_total, seq_len, _ = u3.shape
    g = nseq_total // nseq
    nc = seq_len // tb
    stride = tb + SUBLANES
    nqq = lr.shape[1]
    nct = nqq * 8

    def tmap(gi, d, c):
        return c + d * (nc - 1 - 2 * c)

    return pl.pallas_call(
        functools.partial(_s5_kernel, nseq=nseq, tb=tb, stride=stride, nqq=nqq),
        grid=(g, 2, nc),
        in_specs=[
            pl.BlockSpec((nseq, tb, S5_WIDTH), lambda gi, d, c: (gi, tmap(gi, d, c), 0)),
            pl.BlockSpec((1, N_COLBLK, S5_COLBLK, 2 * S5_BLK_STATES), lambda gi, d, c: (d, 0, 0, 0)),
            pl.BlockSpec((1, N_COLBLK, 2 * S5_BLK_STATES, S5_COLBLK), lambda gi, d, c: (d, 0, 0, 0)),
            pl.BlockSpec((1, nqq, SUBLANES, S5_BLK_STATES), lambda gi, d, c: (d, 0, 0, 0)),
            pl.BlockSpec((1, nqq, SUBLANES, S5_BLK_STATES), lambda gi, d, c: (d, 0, 0, 0)),
            pl.BlockSpec((1, 1, nct, SUBLANES, LANES), lambda gi, d, c: (d, gi, 0, 0, 0)),
        ],
        out_specs=[
            pl.BlockSpec((1, nseq, tb, S5_WIDTH), lambda gi, d, c: (d, gi, tmap(gi, d, c), 0)),
            pl.BlockSpec((1, 1, nct, SUBLANES, LANES), lambda gi, d, c: (d, gi, 0, 0, 0)),
        ],
        out_shape=[jax.ShapeDtypeStruct((2, nseq_total, seq_len, S5_WIDTH), F32),
                   jax.ShapeDtypeStruct((2, g, nct, SUBLANES, LANES), F32)],
        scratch_shapes=[pltpu.VMEM((nct, SUBLANES * stride, LANES), F32),
                        pltpu.VMEM((nct, SUBLANES, LANES), F32)],
        compiler_params=_cparams(("parallel", "parallel", "arbitrary")),
        name="s5_scan_%d" % nseq,
    )(u3, bblk, cblk, lr, li, h0)


def _s5_weights(lam_re, lam_im, log_dt, b_re, b_im, c_re, c_im):
    dt = jnp.exp(log_dt.astype(F32))[..., None]
    lr = lam_re.astype(F32)
    li = lam_im.astype(F32)
    mag = jnp.exp(lr * dt)
    lbr = mag * jnp.cos(li * dt)
    lbi = mag * jnp.sin(li * dt)
    a = lbr - 1.0
    den = lr * lr + li * li
    cr = (a * lr + lbi * li) / den
    ci = (lbi * lr - a * li) / den
    bbr = cr[..., None] * b_re - ci[..., None] * b_im
    bbi = cr[..., None] * b_im + ci[..., None] * b_re
    eye = jnp.eye(8, dtype=F32)

    def pack_b(x):
        x = x.reshape(2, N_COLBLK, 8, S5_STATE, S5_GROUP).transpose(0, 1, 2, 4, 3)
        x = jnp.einsum('dqghn,gk->dqghkn', x, eye)
        return x.reshape(2, N_COLBLK, S5_COLBLK, S5_BLK_STATES)

    def pack_c(x):
        x = x.reshape(2, N_COLBLK, 8, S5_GROUP, S5_STATE).transpose(0, 1, 2, 4, 3)
        x = jnp.einsum('dqgnh,gk->dqgnkh', x, eye)
        return x.reshape(2, N_COLBLK, S5_BLK_STATES, S5_COLBLK)

    bblk = jnp.concatenate([pack_b(bbr), pack_b(bbi)], axis=-1).astype(BF16)
    cblk = jnp.concatenate([pack_c(c_re.astype(F32)), pack_c(-c_im.astype(F32))], axis=-2).astype(BF16)
    lam_r = lbr.reshape(2, N_COLBLK, S5_BLK_STATES)
    lam_i = lbi.reshape(2, N_COLBLK, S5_BLK_STATES)
    return bblk, cblk, lam_r, lam_i


def _diff_attention(q, ks, vs, lam, subln, lambda_init):
    lane = lax.broadcasted_iota(jnp.int32, q.shape, 1)
    dn = (((1,), (1,)), ((), ()))
    probs = []
    for m in range(2):
        sel = (lane < DIFF_HEAD_DIM) if m == 0 else (lane >= DIFF_HEAD_DIM)
        qm = jnp.where(sel, q, jnp.zeros_like(q))
        ss = [lax.dot_general(qm, k, dn, preferred_element_type=F32) for k in ks]
        mx = ss[0].max(axis=-1, keepdims=True)
        for s in ss[1:]:
            mx = jnp.maximum(mx, s.max(axis=-1, keepdims=True))
        ps = [jnp.exp(s - mx) for s in ss]
        l = ps[0].sum(axis=-1, keepdims=True)
        for p in ps[1:]:
            l = l + p.sum(axis=-1, keepdims=True)
        probs.append((ps, l))
    a0 = 1.0 / probs[0][1]
    a1 = lam / probs[1][1]
    o = None
    for si, v in enumerate(vs):
        w = probs[0][0][si] * a0 - probs[1][0][si] * a1
        t = jnp.dot(w.astype(BF16), v, preferred_element_type=F32)
        o = t if o is None else o + t
    ms = jnp.mean(o * o, axis=-1, keepdims=True)
    return o * lax.rsqrt(ms + NORM_EPS) * subln * (1.0 - lambda_init)


def _attn_ctx_kernel(lam_ref, q_ref, k_ref, v_ref, w_ref, o_ref, *, lambda_init):
    o = _diff_attention(q_ref[0], [k_ref[0].astype(BF16)], [v_ref[0].astype(BF16)],
                        lam_ref[0], w_ref[...], lambda_init)
    o_ref[0] = o.astype(BF16)


def _attn_lat_kernel(lam_ref, q_ref, k_ref, v_ref, ck_ref, cv_ref, w_ref, o_ref, *, lambda_init):
    o = _diff_attention(q_ref[0], [k_ref[0], ck_ref[0].astype(BF16)],
                        [v_ref[0], cv_ref[0].astype(BF16)], lam_ref[0], w_ref[...], lambda_init)
    o_ref[0] = o.astype(BF16)


def _attn_ctx_call(lam, q3, k3, v3, subln, lambda_init):
    bsz, seq_len, _ = q3.shape
    spec = pl.BlockSpec((1, seq_len, VALUE_DIM), lambda b, h: (b, 0, h))
    return pl.pallas_call(
        functools.partial(_attn_ctx_kernel, lambda_init=lambda_init),
        grid=(bsz, N_DIFF_HEADS),
        in_specs=[pl.BlockSpec(memory_space=pltpu.SMEM), spec, spec, spec,
                  pl.BlockSpec((1, VALUE_DIM), lambda b, h: (0, 0))],
        out_specs=spec,
        out_shape=jax.ShapeDtypeStruct((bsz, seq_len, ATTN_WIDTH), BF16),
        compiler_params=_cparams(("parallel", "parallel")),
        name="attn_ctx",
    )(lam, q3, k3, v3, subln)


def _attn_lat_call(lam, q3, k3, v3, ck3, cv3, subln, lambda_init, tq=256):
    bsz, seq_len, _ = q3.shape
    ctx_len = ck3.shape[1]
    qspec = pl.BlockSpec((1, tq, VALUE_DIM), lambda b, h, i: (b, i, h))
    kspec = pl.BlockSpec((1, seq_len, VALUE_DIM), lambda b, h, i: (b, 0, h))
    cspec = pl.BlockSpec((1, ctx_len, VALUE_DIM), lambda b, h, i: (b, 0, h))
    return pl.pallas_call(
        functools.partial(_attn_lat_kernel, lambda_init=lambda_init),
        grid=(bsz, N_DIFF_HEADS, seq_len // tq),
        in_specs=[pl.BlockSpec(memory_space=pltpu.SMEM), qspec, kspec, kspec, cspec, cspec,
                  pl.BlockSpec((1, VALUE_DIM), lambda b, h, i: (0, 0))],
        out_specs=qspec,
        out_shape=jax.ShapeDtypeStruct((bsz, seq_len, ATTN_WIDTH), BF16),
        compiler_params=_cparams(("parallel", "parallel", "arbitrary")),
        name="attn_lat",
    )(lam, q3, k3, v3, ck3, cv3, subln)


def _route(logits_t, bias3, tm):
    ng, ge = N_EXPERT_GROUPS, N_EXPERTS // N_EXPERT_GROUPS
    neg = jnp.float32(-jnp.inf)
    sc = jax.nn.sigmoid(logits_t).reshape(ng, ge, tm)
    bi = sc + bias3
    eio = lax.broadcasted_iota(jnp.int32, (ng, ge, tm), 1).astype(F32)
    gio = lax.broadcasted_iota(jnp.int32, (ng, ge, tm), 0).astype(F32)
    m1 = bi.max(axis=1, keepdims=True)
    i1 = jnp.where(bi == m1, eio, float(ge)).min(axis=1, keepdims=True)
    m2 = jnp.where(eio == i1, neg, bi).max(axis=1, keepdims=True)
    gs = jnp.broadcast_to(m1 + m2, (ng, ge, tm))
    cnt = jnp.zeros((ng, ge, tm), F32)
    for g2 in range(ng):
        o = gs[g2:g2 + 1]
        better = (o > gs) | ((o == gs) & (gio > float(g2)))
        cnt = cnt + jnp.where(better, 1.0, 0.0)
    v = jnp.where(cnt < float(TOPK_GROUPS), bi, neg)
    eidx = gio * float(ge) + eio
    selm = jnp.zeros((ng, ge, tm), F32)
    for _ in range(TOP_K):
        m = v.max(axis=0, keepdims=True).max(axis=1, keepdims=True)
        ix = jnp.where(v == m, eidx, float(N_EXPERTS)).min(axis=0, keepdims=True).min(axis=1, keepdims=True)
        oh = eidx == ix
        selm = jnp.where(oh, 1.0, selm)
        v = jnp.where(oh, neg, v)
    selsc = selm * sc
    ssum = selsc.sum(axis=0, keepdims=True).sum(axis=1, keepdims=True)
    return (selsc / ssum * ROUTED_SCALE).reshape(N_EXPERTS, tm)


def _post_kernel(x_ref, u_ref, yf_ref, yb_ref, ao_ref, mod_ref, d_ref, wglu_ref, wout_ref, n2_ref,
                 wrt_ref, rb_ref, x1_ref, h2_ref, gates_ref, *, tm):
    u = u_ref[...]
    y = u * d_ref[...] + yf_ref[0] + yb_ref[0]
    g = jax.nn.gelu(y)
    s5 = g * jax.nn.sigmoid(jnp.dot(g.astype(BF16), wglu_ref[...], preferred_element_type=F32))
    mixed = (jnp.dot(s5.astype(BF16), wout_ref[0:S5_WIDTH, :], preferred_element_type=F32)
             + jnp.dot(ao_ref[...], wout_ref[S5_WIDTH:, :], preferred_element_type=F32))
    mod = mod_ref[0]
    gate1 = mod[:, 2 * D_MODEL:3 * D_MODEL]
    shift2 = mod[:, 3 * D_MODEL:4 * D_MODEL]
    scale2 = mod[:, 4 * D_MODEL:5 * D_MODEL]
    x1 = x_ref[...] + gate1 * mixed
    x1_ref[...] = x1
    ms = jnp.mean(x1 * x1, axis=-1, keepdims=True)
    h2 = x1 * lax.rsqrt(ms + NORM_EPS) * n2_ref[...] * (1.0 + scale2) + shift2
    h2_ref[...] = h2.astype(BF16)
    logits_t = lax.dot_general(wrt_ref[...], h2, (((1,), (1,)), ((), ())),
                               preferred_element_type=F32, precision=lax.Precision.HIGHEST)
    gates_ref[...] = _route(logits_t, rb_ref[...], tm)


def _post_call(x2, u2, y4, ao2, mod3, ssm_d, wglu_bf, wout_bf, norm2_w, wr_t, rbias3, seq_len, tm=256):
    n = x2.shape[0]
    bpb = seq_len // tm
    y3 = y4.reshape(2, n, S5_WIDTH)
    row = lambda i: (i, 0)
    const2 = lambda i: (0, 0)
    return pl.pallas_call(
        functools.partial(_post_kernel, tm=tm),
        grid=(n // tm,),
        in_specs=[pl.BlockSpec((tm, D_MODEL), row),
                  pl.BlockSpec((tm, S5_WIDTH), row),
                  pl.BlockSpec((1, tm, S5_WIDTH), lambda i: (0, i, 0)),
                  pl.BlockSpec((1, tm, S5_WIDTH), lambda i: (1, i, 0)),
                  pl.BlockSpec((tm, ATTN_WIDTH), row),
                  pl.BlockSpec((1, 1, 6 * D_MODEL), lambda i: (i // bpb, 0, 0)),
                  pl.BlockSpec((1, S5_WIDTH), const2),
                  pl.BlockSpec((S5_WIDTH, S5_WIDTH), const2),
                  pl.BlockSpec((D_MODEL, D_MODEL), const2),
                  pl.BlockSpec((1, D_MODEL), const2),
                  pl.BlockSpec((N_EXPERTS, D_MODEL), const2),
                  pl.BlockSpec((N_EXPERT_GROUPS, N_EXPERTS // N_EXPERT_GROUPS, 1), lambda i: (0, 0, 0))],
        out_specs=[pl.BlockSpec((tm, D_MODEL), row),
                   pl.BlockSpec((tm, D_MODEL), row),
                   pl.BlockSpec((N_EXPERTS, tm), lambda i: (0, i))],
        out_shape=[jax.ShapeDtypeStruct((n, D_MODEL), F32),
                   jax.ShapeDtypeStruct((n, D_MODEL), BF16),
                   jax.ShapeDtypeStruct((N_EXPERTS, n), F32)],
        compiler_params=_cparams(("parallel",)),
        name="post_mix_route",
    )(x2, u2, y3, y3, ao2, mod3, ssm_d.reshape(1, S5_WIDTH), wglu_bf, wout_bf,
      norm2_w.reshape(1, D_MODEL), wr_t, rbias3)


def _swiglu_bf(t, wg, wu, wd):
    a = jnp.dot(t, wg, preferred_element_type=F32)
    b = jnp.dot(t, wu, preferred_element_type=F32)
    act = a * jax.nn.sigmoid(a) * b
    return jnp.dot(act.astype(BF16), wd, preferred_element_type=F32)


MOE_SB = 256
MOE_SEG = 16
MOE_TF = 512
MOE_XW = D_MODEL + 2 * N_EXPERTS
MOE_STATIC_CHUNKS = 3
MOE_CHUNK = 512
MOE_DMA_SIZES = (256, 128, 64, 32, 16)
MOE_RPAD = -(-(TOP_K * MOE_SB + N_EXPERTS * (MOE_SEG - 1) + MOE_STATIC_CHUNKS * MOE_SEG) // MOE_CHUNK) * MOE_CHUNK


def _seg_copies(cnt_ref, loc_ref, goff_ref, sb, make_copy, act):
    def body(e, carry):
        n = cnt_ref[sb * N_EXPERTS + e]
        off = loc_ref[sb * N_EXPERTS + e]
        g = goff_ref[sb * N_EXPERTS + e]
        for b in MOE_DMA_SIZES:
            @pl.when((n & b) != 0)
            def _(b=b):
                done = n & (-2 * b)
                act(make_copy(pl.multiple_of(off + done, MOE_SEG), pl.multiple_of(g + done, MOE_SEG), b))
        return carry
    lax.fori_loop(0, N_EXPERTS, body, 0)


def _build_onehot(gt, p_ref, cnt_ref, loc_ref, sb):
    t = gt.shape[1]
    selm = gt > 0.0
    r = lax.broadcasted_iota(jnp.int32, (t, t), 0)
    c = lax.broadcasted_iota(jnp.int32, (t, t), 1)
    upper = jnp.where(r < c, 1.0, 0.0).astype(BF16)
    rank = jnp.dot(jnp.where(selm, 1.0, 0.0).astype(BF16), upper, preferred_element_type=F32)
    key = jnp.where(selm, rank, -1.0)
    j16 = lax.broadcasted_iota(jnp.int32, (MOE_SEG, t), 0).astype(F32)
    p_ref[...] = jnp.zeros(p_ref.shape, BF16)
    for e in range(N_EXPERTS):
        off = loc_ref[sb * N_EXPERTS + e]
        n = cnt_ref[sb * N_EXPERTS + e]
        row = key[e:e + 1, :]

        def put(cidx, row=row, off=off):
            first = cidx * MOE_SEG
            firstf = float(first) if isinstance(first, int) else first.astype(F32)
            tile = jnp.where(row == j16 + firstf, 1.0, 0.0).astype(BF16)
            p_ref[pl.ds(pl.multiple_of(off + first, MOE_SEG), MOE_SEG), :] = tile

        for cidx in range(MOE_STATIC_CHUNKS):
            put(cidx)

        def extra(cidx, carry, put=put):
            put(cidx)
            return carry
        lax.fori_loop(MOE_STATIC_CHUNKS, n // MOE_SEG, extra, 0)


def _dispatch_kernel(cnt_ref, loc_ref, goff_ref, padoff_ref, padn_ref, hp_ref, hs_ref, gt_ref, xe_ref,
                     p_ref, xg_ref, z_ref, sem, *, nsb_p, n_sb):
    s = pl.program_id(0)
    slot = s % 2

    def copies(sb, slot_, act):
        def mk(off, g, b):
            return pltpu.make_async_copy(xg_ref.at[slot_, pl.ds(off, b)], xe_ref.at[pl.ds(g, b)], sem.at[slot_])
        _seg_copies(cnt_ref, loc_ref, goff_ref, sb, mk, act)

    @pl.when(s >= 2)
    def _():
        copies(s - 2, slot, lambda cp: cp.wait())

    gt = gt_ref[...]
    _build_onehot(gt, p_ref, cnt_ref, loc_ref, s)
    x = jnp.where(s < nsb_p, hp_ref[...], hs_ref[...])
    gtt = gt.T
    ghi = gtt.astype(BF16)
    glo = (gtt - ghi.astype(F32)).astype(BF16)
    xext = jnp.concatenate([x, ghi, glo], axis=1)
    last = s * N_EXPERTS + N_EXPERTS - 1
    rows = loc_ref[last] + cnt_ref[last]
    for i in range(MOE_RPAD // MOE_CHUNK):
        @pl.when(i * MOE_CHUNK < rows)
        def _(i=i):
            xg_ref[slot, i * MOE_CHUNK:(i + 1) * MOE_CHUNK, :] = jnp.dot(
                p_ref[i * MOE_CHUNK:(i + 1) * MOE_CHUNK, :], xext, preferred_element_type=F32).astype(BF16)
    copies(s, slot, lambda cp: cp.start())

    @pl.when(s == n_sb - 1)
    def _():
        if n_sb >= 2:
            copies(s - 1, 1 - slot, lambda cp: cp.wait())
        copies(s, slot, lambda cp: cp.wait())
        z_ref[...] = jnp.zeros(z_ref.shape, BF16)

        def pads(act):
            def body(e, carry):
                n = padn_ref[e]
                off = padoff_ref[e]
                for b in MOE_DMA_SIZES:
                    @pl.when((n & b) != 0)
                    def _(b=b):
                        done = n & (-2 * b)
                        act(pltpu.make_async_copy(
                            z_ref.at[pl.ds(0, b)],
                            xe_ref.at[pl.ds(pl.multiple_of(off + done, MOE_SEG), b)], sem.at[0]))
                return carry
            lax.fori_loop(0, N_EXPERTS, body, 0)
        pads(lambda cp: cp.start())
        pads(lambda cp: cp.wait())


def _ffn_kernel(te_ref, tidx_ref, tvalid_ref, xe_ref, wg_ref, wu_ref, wd_ref, ye_ref):
    i = pl.program_id(0)

    @pl.when(tvalid_ref[i] != 0)
    def _():
        e = te_ref[i]
        g = xe_ref[:, D_MODEL:].astype(F32)
        lane = lax.broadcasted_iota(jnp.int32, g.shape, 1)
        w = jnp.sum(jnp.where((lane == e) | (lane == e + N_EXPERTS), g, 0.0), axis=1, keepdims=True)
        y = _swiglu_bf(xe_ref[:, :D_MODEL], wg_ref[0], wu_ref[0], wd_ref[0])
        ye_ref[...] = (w * y).astype(BF16)


def _combine_kernel(cnt_ref, loc_ref, goff_ref, ye_ref, gt_ref, h_ref, x1_ref, mod_ref, sg_ref, su_ref, sd_ref,
                    fn_ref, o_ref, p_ref, ys_ref, acc_ref, sem, *, sb0, n_steps):
    i = pl.program_id(0)
    slot = i % 2
    sb = sb0 + i

    def copies(sb_, slot_, act):
        def mk(off, g, b):
            return pltpu.make_async_copy(ye_ref.at[pl.ds(g, b)], ys_ref.at[slot_, pl.ds(off, b)], sem.at[slot_])
        _seg_copies(cnt_ref, loc_ref, goff_ref, sb_, mk, act)

    @pl.when(i == 0)
    def _():
        ys_ref[...] = jnp.zeros(ys_ref.shape, BF16)
        copies(sb, slot, lambda cp: cp.start())

    @pl.when(i + 1 < n_steps)
    def _():
        copies(sb + 1, 1 - slot, lambda cp: cp.start())

    acc_ref[...] = _swiglu_bf(h_ref[...], sg_ref[...], su_ref[...], sd_ref[...])
    _build_onehot(gt_ref[...], p_ref, cnt_ref, loc_ref, sb)
    copies(sb, slot, lambda cp: cp.wait())
    last = sb * N_EXPERTS + N_EXPERTS - 1
    rows = loc_ref[last] + cnt_ref[last]
    for c in range(MOE_RPAD // MOE_CHUNK):
        @pl.when(c * MOE_CHUNK < rows)
        def _(c=c):
            acc_ref[...] += lax.dot_general(
                p_ref[c * MOE_CHUNK:(c + 1) * MOE_CHUNK, :], ys_ref[slot, c * MOE_CHUNK:(c + 1) * MOE_CHUNK, :],
                (((0,), (0,)), ((), ())), preferred_element_type=F32)
    gate2 = mod_ref[0][:, 5 * D_MODEL:6 * D_MODEL]
    x2 = x1_ref[...] + gate2 * acc_ref[...]
    ms = jnp.mean(x2 * x2, axis=-1, keepdims=True)
    o_ref[...] = x2 * lax.rsqrt(ms + NORM_EPS) * fn_ref[...]


def _moe_plan(gates_t):
    ne, n = gates_t.shape
    n_sb = n // MOE_SB
    cnt = jnp.sum((gates_t > 0.0).reshape(ne, n_sb, MOE_SB), axis=-1, dtype=jnp.int32).T
    cnt16 = (cnt + MOE_SEG - 1) // MOE_SEG * MOE_SEG
    loc = jnp.cumsum(cnt16, axis=1) - cnt16
    tot = jnp.sum(cnt16, axis=0)
    totp = (tot + MOE_TF - 1) // MOE_TF * MOE_TF
    ends = jnp.cumsum(totp)
    base = ends - totp
    goff = base[None, :] + jnp.cumsum(cnt16, axis=0) - cnt16
    rows_max = TOP_K * n + n_sb * ne * (MOE_SEG - 1) + ne * (MOE_TF - MOE_SEG)
    nt_max = -(-rows_max // MOE_TF)
    tiles = jnp.arange(nt_max, dtype=jnp.int32)
    used = ends[-1]
    valid = tiles * MOE_TF < used
    tidx = jnp.where(valid, tiles, used // MOE_TF - 1)
    te = jnp.minimum(jnp.searchsorted(ends, tidx * MOE_TF, side='right'), ne - 1).astype(jnp.int32)
    return dict(cnt=cnt16.reshape(-1), loc=loc.reshape(-1).astype(jnp.int32), goff=goff.reshape(-1).astype(jnp.int32),
                padoff=(base + tot).astype(jnp.int32), padn=(totp - tot).astype(jnp.int32),
                te=te, tidx=tidx.astype(jnp.int32), tvalid=valid.astype(jnp.int32), nt_max=nt_max)


def _dispatch_call(plan, h2_p, h2_s, gates_t):
    nsb_p = h2_p.shape[0] // MOE_SB
    n_sb = gates_t.shape[1] // MOE_SB
    grid_spec = pltpu.PrefetchScalarGridSpec(
        num_scalar_prefetch=5, grid=(n_sb,),
        in_specs=[pl.BlockSpec((MOE_SB, D_MODEL), lambda s, *_: (jnp.minimum(s, nsb_p - 1), 0)),
                  pl.BlockSpec((MOE_SB, D_MODEL), lambda s, *_: (jnp.maximum(s - nsb_p, 0), 0)),
                  pl.BlockSpec((N_EXPERTS, MOE_SB), lambda s, *_: (0, s))],
        out_specs=pl.BlockSpec(memory_space=pl.ANY),
        scratch_shapes=[pltpu.VMEM((MOE_RPAD, MOE_SB), BF16),
                        pltpu.VMEM((2, MOE_RPAD, MOE_XW), BF16),
                        pltpu.VMEM((MOE_DMA_SIZES[0], MOE_XW), BF16),
                        pltpu.SemaphoreType.DMA((2,))])
    return pl.pallas_call(
        functools.partial(_dispatch_kernel, nsb_p=nsb_p, n_sb=n_sb),
        grid_spec=grid_spec,
        out_shape=jax.ShapeDtypeStruct((plan['nt_max'] * MOE_TF, MOE_XW), BF16),
        compiler_params=_cparams(("arbitrary",)),
        name="moe_dispatch",
    )(plan['cnt'], plan['loc'], plan['goff'], plan['padoff'], plan['padn'], h2_p, h2_s, gates_t)


def _ffn_call(plan, xe, wg, wu, wd):
    grid_spec = pltpu.PrefetchScalarGridSpec(
        num_scalar_prefetch=3, grid=(plan['nt_max'],),
        in_specs=[pl.BlockSpec((MOE_TF, MOE_XW), lambda i, te, tidx, tv: (tidx[i], 0)),
                  pl.BlockSpec((1, D_MODEL, EXPERT_FF), lambda i, te, tidx, tv: (te[i], 0, 0)),
                  pl.BlockSpec((1, D_MODEL, EXPERT_FF), lambda i, te, tidx, tv: (te[i], 0, 0)),
                  pl.BlockSpec((1, EXPERT_FF, D_MODEL), lambda i, te, tidx, tv: (te[i], 0, 0))],
        out_specs=pl.BlockSpec((MOE_TF, D_MODEL), lambda i, te, tidx, tv: (tidx[i], 0)))
    return pl.pallas_call(
        _ffn_kernel,
        grid_spec=grid_spec,
        out_shape=jax.ShapeDtypeStruct((xe.shape[0], D_MODEL), BF16),
        compiler_params=_cparams(("arbitrary",)),
        name="moe_ffn",
    )(plan['te'], plan['tidx'], plan['tvalid'], xe, wg, wu, wd)


def _combine_call(plan, ye, gates_t, h2, x1, mod3, sg, su, sd, final_w, sb0, seq_len):
    n = h2.shape[0]
    n_steps = n // MOE_SB
    bpb = seq_len // MOE_SB
    row = lambda i, *_: (i, 0)
    const2 = lambda i, *_: (0, 0)
    grid_spec = pltpu.PrefetchScalarGridSpec(
        num_scalar_prefetch=3, grid=(n_steps,),
        in_specs=[pl.BlockSpec(memory_space=pl.ANY),
                  pl.BlockSpec((N_EXPERTS, MOE_SB), lambda i, *_: (0, sb0 + i)),
                  pl.BlockSpec((MOE_SB, D_MODEL), row),
                  pl.BlockSpec((MOE_SB, D_MODEL), row),
                  pl.BlockSpec((1, 1, 6 * D_MODEL), lambda i, *_: (i // bpb, 0, 0)),
                  pl.BlockSpec((D_MODEL, EXPERT_FF), const2),
                  pl.BlockSpec((D_MODEL, EXPERT_FF), const2),
                  pl.BlockSpec((EXPERT_FF, D_MODEL), const2),
                  pl.BlockSpec((1, D_MODEL), const2)],
        out_specs=pl.BlockSpec((MOE_SB, D_MODEL), row),
        scratch_shapes=[pltpu.VMEM((MOE_RPAD, MOE_SB), BF16),
                        pltpu.VMEM((2, MOE_RPAD, D_MODEL), BF16),
                        pltpu.VMEM((MOE_SB, D_MODEL), F32),
                        pltpu.SemaphoreType.DMA((2,))])
    return pl.pallas_call(
        functools.partial(_combine_kernel, sb0=sb0, n_steps=n_steps),
        grid_spec=grid_spec,
        out_shape=jax.ShapeDtypeStruct((n, D_MODEL), F32),
        compiler_params=_cparams(("arbitrary",)),
        name="moe_combine",
    )(plan['cnt'], plan['loc'], plan['goff'], ye, gates_t, h2, x1, mod3, sg, su, sd,
      final_w.reshape(1, D_MODEL))


def _moe_sparse(h2_p, h2_s, gt_p, gt_s, x1_p, x1_s, mod_p, mod_s, wg, wu, wd, sg, su, sd, final_w, dseq):
    gates_t = jnp.concatenate([gt_p, gt_s], axis=1)
    plan = _moe_plan(gates_t)
    xe = _dispatch_call(plan, h2_p, h2_s, gates_t)
    ye = _ffn_call(plan, xe, wg, wu, wd)
    n_p = h2_p.shape[0]
    y_p = _combine_call(plan, ye, gates_t, h2_p, x1_p, mod_p, sg, su, sd, final_w, 0, n_p)
    y_s = _combine_call(plan, ye, gates_t, h2_s, x1_s, mod_s, sg, su, sd, final_w, n_p // MOE_SB, dseq)
    return y_p, y_s


def _rope_tables(n_tokens):
    rows = n_tokens // GRID_W
    row = jnp.repeat(jnp.arange(rows, dtype=F32), GRID_W)
    col = jnp.tile(jnp.arange(GRID_W, dtype=F32), rows)
    freqs = ROPE_THETA ** (-jnp.arange(ROT_PAIRS, dtype=F32) / ROT_PAIRS)
    ar = row[:, None] * freqs
    ac = col[:, None] * freqs
    cos = jnp.concatenate([jnp.cos(ar), jnp.cos(ar), jnp.cos(ac), jnp.cos(ac)], axis=1)
    sin = jnp.concatenate([-jnp.sin(ar), jnp.sin(ar), -jnp.sin(ac), jnp.sin(ac)], axis=1)
    return jnp.tile(cos, (1, 2)), jnp.tile(sin, (1, 2))


def kernel(x_prompt, x_sample, c, cache_k, cache_v, state_ssm_re, state_ssm_im, c_ctx, w_ada, b_ada, norm1_w, w_in, ssm_lambda_re, ssm_lambda_im, ssm_log_dt, ssm_b_re, ssm_b_im, ssm_c_re, ssm_c_im, ssm_d, ssm_w_glu, diff_lambda_q, diff_lambda_k, diff_subln_w, w_out, norm2_w, w_router, router_bias, w_exp_gate, w_exp_up, w_exp_down, w_sh_gate, w_sh_up, w_sh_down, final_norm_w):
    depth = w_ada.shape[0]
    assert depth == 1
    l = 0
    lambda_init = 0.8 - 0.6 * math.exp(-0.3 * l)
    bsz, seq, _ = x_prompt.shape
    dbs, dseq, _ = x_sample.shape
    n_p, n_s = bsz * seq, dbs * dseq

    cond8 = jnp.zeros((SUBLANES, D_MODEL), F32).at[:dbs].set(c).at[dbs].set(c_ctx)
    mod = _ada_call(cond8, w_ada[l], b_ada[l])
    mod_s = mod[:dbs].reshape(dbs, 1, 6 * D_MODEL)
    mod_p = mod[dbs:dbs + 1].reshape(1, 1, 6 * D_MODEL)

    w_in_bf = w_in[l].astype(BF16)
    xp2 = x_prompt.reshape(n_p, D_MODEL)
    xs2 = x_sample.reshape(n_s, D_MODEL)
    u_p, q_p, k_p, v_p = _inproj_call(xp2, mod_p, norm1_w[l], w_in_bf, None, n_p, F32)
    u_s, q_s, k_s, v_s = _inproj_call(xs2, mod_s, norm1_w[l], w_in_bf, _rope_tables(dseq), dseq, BF16)

    bblk, cblk, lam_r, lam_i = _s5_weights(ssm_lambda_re[l], ssm_lambda_im[l], ssm_log_dt[l],
                                           ssm_b_re[l], ssm_b_im[l], ssm_c_re[l], ssm_c_im[l])
    lr_p = jnp.broadcast_to(lam_r[:, :, None, :], (2, N_COLBLK, SUBLANES, S5_BLK_STATES))
    li_p = jnp.broadcast_to(lam_i[:, :, None, :], (2, N_COLBLK, SUBLANES, S5_BLK_STATES))
    g_p = bsz // SUBLANES
    h0_p = jnp.zeros((2, g_p, N_COLBLK * 8, SUBLANES, LANES), F32)
    y_p, hfin = _s5_call(u_p.reshape(bsz, seq, S5_WIDTH), bblk, cblk, lr_p, li_p, h0_p, SUBLANES)

    def halves(x):
        x = x.reshape(2, 2, 2, 1, S5_BLK_STATES)
        x = jnp.broadcast_to(x, (2, 2, 2, dbs, S5_BLK_STATES))
        return x.transpose(0, 2, 1, 3, 4).reshape(2, 2, 2 * dbs, S5_BLK_STATES)

    def h0_tiles(s):
        s = s.astype(F32).reshape(dbs, 2, 2, 2, 4, LANES)
        return s.transpose(1, 3, 4, 2, 0, 5).reshape(2, 1, 2, 4, 2 * dbs, LANES)

    h0_s = jnp.concatenate([h0_tiles(state_ssm_re[:, l]), h0_tiles(state_ssm_im[:, l])], axis=3)
    h0_s = h0_s.reshape(2, 1, 16, SUBLANES, LANES)
    y_s, _ = _s5_call(u_s.reshape(dbs, dseq, S5_WIDTH), bblk, cblk, halves(lam_r), halves(lam_i), h0_s, dbs)

    lq = diff_lambda_q[l].astype(F32)
    lk = diff_lambda_k[l].astype(F32)
    lam = (jnp.exp(jnp.sum(lq[0] * lk[0])) - jnp.exp(jnp.sum(lq[1] * lk[1])) + lambda_init).reshape(1)
    subln = diff_subln_w[l].astype(F32).reshape(1, VALUE_DIM)
    ao_p = _attn_ctx_call(lam, q_p.reshape(bsz, seq, ATTN_WIDTH), k_p.reshape(bsz, seq, ATTN_WIDTH),
                          v_p.reshape(bsz, seq, ATTN_WIDTH), subln, lambda_init)
    past = cache_k.shape[2]
    ao_s = _attn_lat_call(lam, q_s.reshape(dbs, dseq, ATTN_WIDTH), k_s.reshape(dbs, dseq, ATTN_WIDTH),
                          v_s.reshape(dbs, dseq, ATTN_WIDTH),
                          cache_k[:, l].reshape(dbs, past, ATTN_WIDTH),
                          cache_v[:, l].reshape(dbs, past, ATTN_WIDTH), subln, lambda_init)

    wglu_bf = ssm_w_glu[l].astype(BF16)
    wout_bf = w_out[l].astype(BF16)
    wr_t = w_router[l].astype(F32).T
    rbias3 = router_bias[l].astype(F32).reshape(N_EXPERT_GROUPS, N_EXPERTS // N_EXPERT_GROUPS, 1)
    x1_p, h2_p, gates_p = _post_call(xp2, u_p, y_p, ao_p.reshape(n_p, ATTN_WIDTH), mod_p, ssm_d[l],
                                     wglu_bf, wout_bf, norm2_w[l], wr_t, rbias3, n_p)
    x1_s, h2_s, gates_s = _post_call(xs2, u_s, y_s, ao_s.reshape(n_s, ATTN_WIDTH), mod_s, ssm_d[l],
                                     wglu_bf, wout_bf, norm2_w[l], wr_t, rbias3, dseq)

    wg = w_exp_gate[l].astype(BF16)
    wu = w_exp_up[l].astype(BF16)
    wd = w_exp_down[l].astype(BF16)
    sg = w_sh_gate[l].astype(BF16)
    su = w_sh_up[l].astype(BF16)
    sd = w_sh_down[l].astype(BF16)
    y_prompt, y_sample = _moe_sparse(h2_p, h2_s, gates_p, gates_s, x1_p, x1_s, mod_p, mod_s,
                                     wg, wu, wd, sg, su, sd, final_norm_w, dseq)

    new_cache_k = k_p.reshape(bsz, 1, seq, N_DIFF_HEADS, VALUE_DIM)
    new_cache_v = v_p.reshape(bsz, 1, seq, N_DIFF_HEADS, VALUE_DIM)
    hf = hfin.reshape(2, g_p, N_COLBLK, 2, 4, SUBLANES, LANES)
    hf = hf.transpose(3, 1, 5, 0, 2, 4, 6).reshape(2, bsz, 1, 2, S5_GROUPS, S5_STATE)
    return (y_prompt.reshape(bsz, seq, D_MODEL), y_sample.reshape(dbs, dseq, D_MODEL),
            new_cache_k, new_cache_v, hf[0], hf[1])
```

```python
import functools
import math

import jax
import jax.numpy as jnp
from jax import lax
from jax.experimental import pallas as pl
from jax.experimental.pallas import tpu as pltpu

F32 = jnp.float32
BF16 = jnp.bfloat16

D_MODEL = 1024
GRID_W = 64
S5_WIDTH = 512
S5_GROUP = 16
S5_GROUPS = 32
S5_STATE = 64
ATTN_WIDTH = 512
DIFF_HEAD_DIM = 64
VALUE_DIM = 128
N_DIFF_HEADS = 4
IN_WIDTH = S5_WIDTH + 3 * ATTN_WIDTH
ROT_PAIRS = DIFF_HEAD_DIM // 4
ROPE_THETA = 10000.0
N_EXPERTS = 64
TOP_K = 8
N_EXPERT_GROUPS = 8
TOPK_GROUPS = 4
EXPERT_FF = 256
ROUTED_SCALE = 2.5
NORM_EPS = 1e-6

LANES = 128
SUBLANES = 8
S5_COLBLK = 8 * S5_GROUP
S5_BLK_STATES = 8 * S5_STATE
N_COLBLK = S5_WIDTH // S5_COLBLK
VMEM_LIMIT = 56 * 1024 * 1024


def _cparams(sem):
    return pltpu.CompilerParams(dimension_semantics=sem, vmem_limit_bytes=VMEM_LIMIT)


def _ada_kernel(cond_ref, w_ref, b_ref, o_ref):
    c = cond_ref[...]
    s = c * jax.nn.sigmoid(c)
    o_ref[...] = jnp.dot(s.astype(BF16), w_ref[...].astype(BF16),
                         preferred_element_type=F32) + b_ref[...]


def _ada_call(cond8, w_ada, b_ada):
    n = w_ada.shape[1]
    tn = 1536
    return pl.pallas_call(
        _ada_kernel,
        grid=(n // tn,),
        in_specs=[pl.BlockSpec((SUBLANES, D_MODEL), lambda j: (0, 0)),
                  pl.BlockSpec((D_MODEL, tn), lambda j: (0, j)),
                  pl.BlockSpec((1, tn), lambda j: (0, j))],
        out_specs=pl.BlockSpec((SUBLANES, tn), lambda j: (0, j)),
        out_shape=jax.ShapeDtypeStruct((SUBLANES, n), F32),
        compiler_params=_cparams(("arbitrary",)),
        name="adaln",
    )(cond8, w_ada, b_ada.reshape(1, n))


def _rope_apply(t, cos, sin):
    parts = []
    for cidx in range(ATTN_WIDTH // LANES):
        xc = t[:, cidx * LANES:(cidx + 1) * LANES]
        up = pltpu.roll(xc, LANES - ROT_PAIRS, 1)
        dn = pltpu.roll(xc, ROT_PAIRS, 1)
        lane = lax.broadcasted_iota(jnp.int32, xc.shape, 1)
        partner = jnp.where((lane % (2 * ROT_PAIRS)) < ROT_PAIRS, up, dn)
        parts.append(xc * cos + partner * sin)
    return jnp.concatenate(parts, axis=1)


def _inproj_kernel(*refs, rope, kv_dtype):
    if rope:
        x_ref, mod_ref, n1_ref, w_ref, cos_ref, sin_ref, u_ref, q_ref, k_ref, v_ref = refs
    else:
        x_ref, mod_ref, n1_ref, w_ref, u_ref, q_ref, k_ref, v_ref = refs
    x = x_ref[...]
    ms = jnp.mean(x * x, axis=-1, keepdims=True)
    xn = x * lax.rsqrt(ms + NORM_EPS) * n1_ref[...]
    mod = mod_ref[0]
    shift = mod[:, 0:D_MODEL]
    scale = mod[:, D_MODEL:2 * D_MODEL]
    h = xn * (1.0 + scale) + shift
    proj = jnp.dot(h.astype(BF16), w_ref[...], preferred_element_type=F32)
    u_ref[...] = proj[:, :S5_WIDTH]
    q = proj[:, S5_WIDTH:S5_WIDTH + ATTN_WIDTH]
    k = proj[:, S5_WIDTH + ATTN_WIDTH:S5_WIDTH + 2 * ATTN_WIDTH]
    v = proj[:, S5_WIDTH + 2 * ATTN_WIDTH:]
    if rope:
        cos = cos_ref[...]
        sin = sin_ref[...]
        q = _rope_apply(q, cos, sin)
        k = _rope_apply(k, cos, sin)
    q_ref[...] = (q * (DIFF_HEAD_DIM ** -0.5)).astype(BF16)
    k_ref[...] = k.astype(kv_dtype)
    v_ref[...] = v.astype(kv_dtype)


def _inproj_call(x2, mod3, norm1_w, w_in_bf, rope_tabs, seq_len, kv_dtype, tm=256):
    n = x2.shape[0]
    bpb = seq_len // tm
    rope = rope_tabs is not None
    in_specs = [pl.BlockSpec((tm, D_MODEL), lambda i: (i, 0)),
                pl.BlockSpec((1, 1, 6 * D_MODEL), lambda i: (i // bpb, 0, 0)),
                pl.BlockSpec((1, D_MODEL), lambda i: (0, 0)),
                pl.BlockSpec((D_MODEL, IN_WIDTH), lambda i: (0, 0))]
    args = [x2, mod3, norm1_w.reshape(1, D_MODEL), w_in_bf]
    if rope:
        in_specs += [pl.BlockSpec((tm, LANES), lambda i: (i % bpb, 0)),
                     pl.BlockSpec((tm, LANES), lambda i: (i % bpb, 0))]
        args += list(rope_tabs)
    ospec = pl.BlockSpec((tm, S5_WIDTH), lambda i: (i, 0))
    return pl.pallas_call(
        functools.partial(_inproj_kernel, rope=rope, kv_dtype=kv_dtype),
        grid=(n // tm,),
        in_specs=in_specs,
        out_specs=[ospec, ospec, ospec, ospec],
        out_shape=[jax.ShapeDtypeStruct((n, S5_WIDTH), F32),
                   jax.ShapeDtypeStruct((n, ATTN_WIDTH), BF16),
                   jax.ShapeDtypeStruct((n, ATTN_WIDTH), kv_dtype),
                   jax.ShapeDtypeStruct((n, ATTN_WIDTH), kv_dtype)],
        compiler_params=_cparams(("parallel",)),
        name="inproj_rope" if rope else "inproj",
    )(*args)


def _s5_slot(b, q, nseq):
    if nseq == SUBLANES:
        return b, q * 8
    half = q // 2
    return half * nseq + b, (q % 2) * 8


def _s5_kernel(u_ref, b_ref, c_ref, lr_ref, li_ref, h0_ref, y_ref, hfin_ref, buh, hst,
               *, nseq, tb, stride, nqq):
    d = pl.program_id(1)
    c = pl.program_id(2)
    nc = pl.num_programs(2)

    @pl.when(c == 0)
    def _():
        hst[...] = h0_ref[0, 0]
        buh[...] = jnp.zeros(buh.shape, F32)

    u = u_ref[...].reshape(nseq * tb, S5_WIDTH).astype(BF16)
    for q in range(N_COLBLK):
        res = jnp.dot(u[:, q * S5_COLBLK:(q + 1) * S5_COLBLK], b_ref[0, q],
                      preferred_element_type=F32)
        for b in range(nseq):
            slot, ct0 = _s5_slot(b, q, nseq)
            for j in range(8):
                buh[ct0 + j, slot * stride:slot * stride + tb, :] = (
                    res[b * tb:(b + 1) * tb, j * LANES:(j + 1) * LANES])

    for qq in range(nqq):
        lr = [lr_ref[0, qq, :, i * LANES:(i + 1) * LANES] for i in range(4)]
        li = [li_ref[0, qq, :, i * LANES:(i + 1) * LANES] for i in range(4)]
        h_init = tuple(hst[qq * 8 + i] for i in range(8))

        def body(s, carry, qq=qq, lr=lr, li=li):
            t = jnp.where(d == 0, s, tb - 1 - s)
            idx = pl.ds(t, SUBLANES, stride=stride)
            new_r, new_i = [], []
            for i in range(4):
                hr, hi = carry[i], carry[4 + i]
                bur = buh[qq * 8 + i, idx, :]
                bui = buh[qq * 8 + 4 + i, idx, :]
                nr = lr[i] * hr - li[i] * hi + bur
                ni = lr[i] * hi + li[i] * hr + bui
                buh[qq * 8 + i, idx, :] = nr
                buh[qq * 8 + 4 + i, idx, :] = ni
                new_r.append(nr)
                new_i.append(ni)
            return tuple(new_r) + tuple(new_i)

        h_fin = lax.fori_loop(0, tb, body, h_init)
        for i in range(8):
            hst[qq * 8 + i] = h_fin[i]

    rows = nseq * stride
    for q in range(N_COLBLK):
        slot0, ct0 = _s5_slot(0, q, nseq)
        hmat = jnp.concatenate(
            [buh[ct0 + j, slot0 * stride:slot0 * stride + rows, :] for j in range(8)], axis=1)
        yq = jnp.dot(hmat.astype(BF16), c_ref[0, q], preferred_element_type=F32)
        for b in range(nseq):
            y_ref[0, b, :, q * S5_COLBLK:(q + 1) * S5_COLBLK] = yq[b * stride:b * stride + tb, :]

    @pl.when(c == nc - 1)
    def _():
        hfin_ref[0, 0] = hst[...]


def _s5_call(u3, bblk, cblk, lr, li, h0, nseq, tb=64):
    nseq_total, seq_len, _ = u3.shape
    g = nseq_total // nseq
    nc = seq_len // tb
    stride = tb + SUBLANES
    nqq = lr.shape[1]
    nct = nqq * 8

    def tmap(gi, d, c):
        return c + d * (nc - 1 - 2 * c)

    return pl.pallas_call(
        functools.partial(_s5_kernel, nseq=nseq, tb=tb, stride=stride, nqq=nqq),
        grid=(g, 2, nc),
        in_specs=[
            pl.BlockSpec((nseq, tb, S5_WIDTH), lambda gi, d, c: (gi, tmap(gi, d, c), 0)),
            pl.BlockSpec((1, N_COLBLK, S5_COLBLK, 2 * S5_BLK_STATES), lambda gi, d, c: (d, 0, 0, 0)),
            pl.BlockSpec((1, N_COLBLK, 2 * S5_BLK_STATES, S5_COLBLK), lambda gi, d, c: (d, 0, 0, 0)),
            pl.BlockSpec((1, nqq, SUBLANES, S5_BLK_STATES), lambda gi, d, c: (d, 0, 0, 0)),
            pl.BlockSpec((1, nqq, SUBLANES, S5_BLK_STATES), lambda gi, d, c: (d, 0, 0, 0)),
            pl.BlockSpec((1, 1, nct, SUBLANES, LANES), lambda gi, d, c: (d, gi, 0, 0, 0)),
        ],
        out_specs=[
            pl.BlockSpec((1, nseq, tb, S5_WIDTH), lambda gi, d, c: (d, gi, tmap(gi, d, c), 0)),
            pl.BlockSpec((1, 1, nct, SUBLANES, LANES), lambda gi, d, c: (d, gi, 0, 0, 0)),
        ],
        out_shape=[jax.ShapeDtypeStruct((2, nseq_total, seq_len, S5_WIDTH), F32),
                   jax.ShapeDtypeStruct((2, g, nct, SUBLANES, LANES), F32)],
        scratch_shapes=[pltpu.VMEM((nct, SUBLANES * stride, LANES), F32),
                        pltpu.VMEM((nct, SUBLANES, LANES), F32)],
        compiler_params=_cparams(("parallel", "parallel", "arbitrary")),
        name="s5_scan_%d" % nseq,
    )(u3, bblk, cblk, lr, li, h0)


def _s5_weights(lam_re, lam_im, log_dt, b_re, b_im, c_re, c_im):
    dt = jnp.exp(log_dt.astype(F32))[..., None]
    lr = lam_re.astype(F32)
    li = lam_im.astype(F32)
    mag = jnp.exp(lr * dt)
    lbr = mag * jnp.cos(li * dt)
    lbi = mag * jnp.sin(li * dt)
    a = lbr - 1.0
    den = lr * lr + li * li
    cr = (a * lr + lbi * li) / den
    ci = (lbi * lr - a * li) / den
    bbr = cr[..., None] * b_re - ci[..., None] * b_im
    bbi = cr[..., None] * b_im + ci[..., None] * b_re
    eye = jnp.eye(8, dtype=F32)

    def pack_b(x):
        x = x.reshape(2, N_COLBLK, 8, S5_STATE, S5_GROUP).transpose(0, 1, 2, 4, 3)
        x = jnp.einsum('dqghn,gk->dqghkn', x, eye)
        return x.reshape(2, N_COLBLK, S5_COLBLK, S5_BLK_STATES)

    def pack_c(x):
        x = x.reshape(2, N_COLBLK, 8, S5_GROUP, S5_STATE).transpose(0, 1, 2, 4, 3)
        x = jnp.einsum('dqgnh,gk->dqgnkh', x, eye)
        return x.reshape(2, N_COLBLK, S5_BLK_STATES, S5_COLBLK)

    bblk = jnp.concatenate([pack_b(bbr), pack_b(bbi)], axis=-1).astype(BF16)
    cblk = jnp.concatenate([pack_c(c_re.astype(F32)), pack_c(-c_im.astype(F32))], axis=-2).astype(BF16)
    lam_r = lbr.reshape(2, N_COLBLK, S5_BLK_STATES)
    lam_i = lbi.reshape(2, N_COLBLK, S5_BLK_STATES)
    return bblk, cblk, lam_r, lam_i


def _diff_attention(q, ks, vs, lam, subln, lambda_init):
    lane = lax.broadcasted_iota(jnp.int32, q.shape, 1)
    dn = (((1,), (1,)), ((), ()))
    probs = []
    for m in range(2):
        sel = (lane < DIFF_HEAD_DIM) if m == 0 else (lane >= DIFF_HEAD_DIM)
        qm = jnp.where(sel, q, jnp.zeros_like(q))
        ss = [lax.dot_general(qm, k, dn, preferred_element_type=F32) for k in ks]
        mx = ss[0].max(axis=-1, keepdims=True)
        for s in ss[1:]:
            mx = jnp.maximum(mx, s.max(axis=-1, keepdims=True))
        ps = [jnp.exp(s - mx) for s in ss]
        l = ps[0].sum(axis=-1, keepdims=True)
        for p in ps[1:]:
            l = l + p.sum(axis=-1, keepdims=True)
        probs.append((ps, l))
    a0 = 1.0 / probs[0][1]
    a1 = lam / probs[1][1]
    o = None
    for si, v in enumerate(vs):
        w = probs[0][0][si] * a0 - probs[1][0][si] * a1
        t = jnp.dot(w.astype(BF16), v, preferred_element_type=F32)
        o = t if o is None else o + t
    ms = jnp.mean(o * o, axis=-1, keepdims=True)
    return o * lax.rsqrt(ms + NORM_EPS) * subln * (1.0 - lambda_init)


def _attn_ctx_kernel(lam_ref, q_ref, k_ref, v_ref, w_ref, o_ref, *, lambda_init):
    o = _diff_attention(q_ref[0], [k_ref[0].astype(BF16)], [v_ref[0].astype(BF16)],
                        lam_ref[0], w_ref[...], lambda_init)
    o_ref[0] = o.astype(BF16)


def _attn_lat_kernel(lam_ref, q_ref, k_ref, v_ref, ck_ref, cv_ref, w_ref, o_ref, *, lambda_init):
    o = _diff_attention(q_ref[0], [k_ref[0], ck_ref[0].astype(BF16)],
                        [v_ref[0], cv_ref[0].astype(BF16)], lam_ref[0], w_ref[...], lambda_init)
    o_ref[0] = o.astype(BF16)


def _attn_ctx_call(lam, q3, k3, v3, subln, lambda_init):
    bsz, seq_len, _ = q3.shape
    spec = pl.BlockSpec((1, seq_len, VALUE_DIM), lambda b, h: (b, 0, h))
    return pl.pallas_call(
        functools.partial(_attn_ctx_kernel, lambda_init=lambda_init),
        grid=(bsz, N_DIFF_HEADS),
        in_specs=[pl.BlockSpec(memory_space=pltpu.SMEM), spec, spec, spec,
                  pl.BlockSpec((1, VALUE_DIM), lambda b, h: (0, 0))],
        out_specs=spec,
        out_shape=jax.ShapeDtypeStruct((bsz, seq_len, ATTN_WIDTH), BF16),
        compiler_params=_cparams(("parallel", "parallel")),
        name="attn_ctx",
    )(lam, q3, k3, v3, subln)


def _attn_lat_call(lam, q3, k3, v3, ck3, cv3, subln, lambda_init, tq=256):
    bsz, seq_len, _ = q3.shape
    ctx_len = ck3.shape[1]
    qspec = pl.BlockSpec((1, tq, VALUE_DIM), lambda b, h, i: (b, i, h))
    kspec = pl.BlockSpec((1, seq_len, VALUE_DIM), lambda b, h, i: (b, 0, h))
    cspec = pl.BlockSpec((1, ctx_len, VALUE_DIM), lambda b, h, i: (b, 0, h))
    return pl.pallas_call(
        functools.partial(_attn_lat_kernel, lambda_init=lambda_init),
        grid=(bsz, N_DIFF_HEADS, seq_len // tq),
        in_specs=[pl.BlockSpec(memory_space=pltpu.SMEM), qspec, kspec, kspec, cspec, cspec,
                  pl.BlockSpec((1, VALUE_DIM), lambda b, h, i: (0, 0))],
        out_specs=qspec,
        out_shape=jax.ShapeDtypeStruct((bsz, seq_len, ATTN_WIDTH), BF16),
        compiler_params=_cparams(("parallel", "parallel", "arbitrary")),
        name="attn_lat",
    )(lam, q3, k3, v3, ck3, cv3, subln)


def _route(logits_t, bias3, tm):
    ng, ge = N_EXPERT_GROUPS, N_EXPERTS // N_EXPERT_GROUPS
    neg = jnp.float32(-jnp.inf)
    sc = jax.nn.sigmoid(logits_t).reshape(ng, ge, tm)
    bi = sc + bias3
    eio = lax.broadcasted_iota(jnp.int32, (ng, ge, tm), 1).astype(F32)
    gio = lax.broadcasted_iota(jnp.int32, (ng, ge, tm), 0).astype(F32)
    m1 = bi.max(axis=1, keepdims=True)
    i1 = jnp.where(bi == m1, eio, float(ge)).min(axis=1, keepdims=True)
    m2 = jnp.where(eio == i1, neg, bi).max(axis=1, keepdims=True)
    gs = jnp.broadcast_to(m1 + m2, (ng, ge, tm))
    cnt = jnp.zeros((ng, ge, tm), F32)
    for g2 in range(ng):
        o = gs[g2:g2 + 1]
        better = (o > gs) | ((o == gs) & (gio > float(g2)))
        cnt = cnt + jnp.where(better, 1.0, 0.0)
    v = jnp.where(cnt < float(TOPK_GROUPS), bi, neg)
    eidx = gio * float(ge) + eio
    selm = jnp.zeros((ng, ge, tm), F32)
    for _ in range(TOP_K):
        m = v.max(axis=0, keepdims=True).max(axis=1, keepdims=True)
        ix = jnp.where(v == m, eidx, float(N_EXPERTS)).min(axis=0, keepdims=True).min(axis=1, keepdims=True)
        oh = eidx == ix
        selm = jnp.where(oh, 1.0, selm)
        v = jnp.where(oh, neg, v)
    selsc = selm * sc
    ssum = selsc.sum(axis=0, keepdims=True).sum(axis=1, keepdims=True)
    return (selsc / ssum * ROUTED_SCALE).reshape(N_EXPERTS, tm)


def _post_kernel(x_ref, u_ref, yf_ref, yb_ref, ao_ref, mod_ref, d_ref, wglu_ref, wout_ref, n2_ref,
                 wrt_ref, rb_ref, x1_ref, h2_ref, gates_ref, *, tm):
    u = u_ref[...]
    y = u * d_ref[...] + yf_ref[0] + yb_ref[0]
    g = jax.nn.gelu(y)
    s5 = g * jax.nn.sigmoid(jnp.dot(g.astype(BF16), wglu_ref[...], preferred_element_type=F32))
    mixed = (jnp.dot(s5.astype(BF16), wout_ref[0:S5_WIDTH, :], preferred_element_type=F32)
             + jnp.dot(ao_ref[...], wout_ref[S5_WIDTH:, :], preferred_element_type=F32))
    mod = mod_ref[0]
    gate1 = mod[:, 2 * D_MODEL:3 * D_MODEL]
    shift2 = mod[:, 3 * D_MODEL:4 * D_MODEL]
    scale2 = mod[:, 4 * D_MODEL:5 * D_MODEL]
    x1 = x_ref[...] + gate1 * mixed
    x1_ref[...] = x1
    ms = jnp.mean(x1 * x1, axis=-1, keepdims=True)
    h2 = x1 * lax.rsqrt(ms + NORM_EPS) * n2_ref[...] * (1.0 + scale2) + shift2
    h2_ref[...] = h2.astype(BF16)
    logits_t = lax.dot_general(wrt_ref[...], h2, (((1,), (1,)), ((), ())),
                               preferred_element_type=F32, precision=lax.Precision.HIGHEST)
    gates_ref[...] = _route(logits_t, rb_ref[...], tm)


def _post_call(x2, u2, y4, ao2, mod3, ssm_d, wglu_bf, wout_bf, norm2_w, wr_t, rbias3, seq_len, tm=256):
    n = x2.shape[0]
    bpb = seq_len // tm
    y3 = y4.reshape(2, n, S5_WIDTH)
    row = lambda i: (i, 0)
    const2 = lambda i: (0, 0)
    return pl.pallas_call(
        functools.partial(_post_kernel, tm=tm),
        grid=(n // tm,),
        in_specs=[pl.BlockSpec((tm, D_MODEL), row),
                  pl.BlockSpec((tm, S5_WIDTH), row),
                  pl.BlockSpec((1, tm, S5_WIDTH), lambda i: (0, i, 0)),
                  pl.BlockSpec((1, tm, S5_WIDTH), lambda i: (1, i, 0)),
                  pl.BlockSpec((tm, ATTN_WIDTH), row),
                  pl.BlockSpec((1, 1, 6 * D_MODEL), lambda i: (i // bpb, 0, 0)),
                  pl.BlockSpec((1, S5_WIDTH), const2),
                  pl.BlockSpec((S5_WIDTH, S5_WIDTH), const2),
                  pl.BlockSpec((D_MODEL, D_MODEL), const2),
                  pl.BlockSpec((1, D_MODEL), const2),
                  pl.BlockSpec((N_EXPERTS, D_MODEL), const2),
                  pl.BlockSpec((N_EXPERT_GROUPS, N_EXPERTS // N_EXPERT_GROUPS, 1), lambda i: (0, 0, 0))],
        out_specs=[pl.BlockSpec((tm, D_MODEL), row),
                   pl.BlockSpec((tm, D_MODEL), row),
                   pl.BlockSpec((N_EXPERTS, tm), lambda i: (0, i))],
        out_shape=[jax.ShapeDtypeStruct((n, D_MODEL), F32),
                   jax.ShapeDtypeStruct((n, D_MODEL), BF16),
                   jax.ShapeDtypeStruct((N_EXPERTS, n), F32)],
        compiler_params=_cparams(("parallel",)),
        name="post_mix_route",
    )(x2, u2, y3, y3, ao2, mod3, ssm_d.reshape(1, S5_WIDTH), wglu_bf, wout_bf,
      norm2_w.reshape(1, D_MODEL), wr_t, rbias3)


def _swiglu_bf(t, wg, wu, wd):
    a = jnp.dot(t, wg, preferred_element_type=F32)
    b = jnp.dot(t, wu, preferred_element_type=F32)
    act = a * jax.nn.sigmoid(a) * b
    return jnp.dot(act.astype(BF16), wd, preferred_element_type=F32)


MOE_SB = 256
MOE_SEG = 16
MOE_TF = 512
MOE_XW = D_MODEL + 2 * N_EXPERTS
MOE_STATIC_CHUNKS = 3
MOE_CHUNK = 512
MOE_DMA_SIZES = (256, 128, 64, 32, 16)
MOE_RPAD = -(-(TOP_K * MOE_SB + N_EXPERTS * (MOE_SEG - 1) + MOE_STATIC_CHUNKS * MOE_SEG) // MOE_CHUNK) * MOE_CHUNK


def _seg_copies(cnt_ref, loc_ref, goff_ref, sb, make_copy, act):
    def body(e, carry):
        n = cnt_ref[sb * N_EXPERTS + e]
        off = loc_ref[sb * N_EXPERTS + e]
        g = goff_ref[sb * N_EXPERTS + e]
        for b in MOE_DMA_SIZES:
            @pl.when((n & b) != 0)
            def _(b=b):
                done = n & (-2 * b)
                act(make_copy(pl.multiple_of(off + done, MOE_SEG), pl.multiple_of(g + done, MOE_SEG), b))
        return carry
    lax.fori_loop(0, N_EXPERTS, body, 0)


MOE_WAIT_SIZES = tuple(2 ** k for k in range((MOE_RPAD - 1).bit_length() - 1, 3, -1))


def _sub_block_rows(cnt_ref, loc_ref, sb):
    last = sb * N_EXPERTS + N_EXPERTS - 1
    return loc_ref[last] + cnt_ref[last]


def _wait_rows(rows, make_copy):
    for b in MOE_WAIT_SIZES:
        @pl.when((rows & b) != 0)
        def _(b=b):
            make_copy(b).wait()


def _build_onehot(gt, p_ref, cnt_ref, loc_ref, sb):
    t = gt.shape[1]
    selm = gt > 0.0
    r = lax.broadcasted_iota(jnp.int32, (t, t), 0)
    c = lax.broadcasted_iota(jnp.int32, (t, t), 1)
    upper = jnp.where(r < c, 1.0, 0.0).astype(BF16)
    rank = jnp.dot(jnp.where(selm, 1.0, 0.0).astype(BF16), upper, preferred_element_type=F32)
    key = jnp.where(selm, rank, -1.0)
    j16 = lax.broadcasted_iota(jnp.int32, (MOE_SEG, t), 0).astype(F32)
    p_ref[...] = jnp.zeros(p_ref.shape, BF16)
    for e in range(N_EXPERTS):
        off = loc_ref[sb * N_EXPERTS + e]
        n = cnt_ref[sb * N_EXPERTS + e]
        row = key[e:e + 1, :]

        def put(cidx, row=row, off=off):
            first = cidx * MOE_SEG
            firstf = float(first) if isinstance(first, int) else first.astype(F32)
            tile = jnp.where(row == j16 + firstf, 1.0, 0.0).astype(BF16)
            p_ref[pl.ds(pl.multiple_of(off + first, MOE_SEG), MOE_SEG), :] = tile

        for cidx in range(MOE_STATIC_CHUNKS):
            put(cidx)

        def extra(cidx, carry, put=put):
            put(cidx)
            return carry
        lax.fori_loop(MOE_STATIC_CHUNKS, n // MOE_SEG, extra, 0)


def _dispatch_kernel(cnt_ref, loc_ref, goff_ref, padoff_ref, padn_ref, hp_ref, hs_ref, gt_ref, xe_ref,
                     p_ref, xg_ref, z_ref, sem, *, nsb_p, n_sb):
    s = pl.program_id(0)
    slot = s % 2

    def copies(sb, slot_, act):
        def mk(off, g, b):
            return pltpu.make_async_copy(xg_ref.at[slot_, pl.ds(off, b)], xe_ref.at[pl.ds(g, b)], sem.at[slot_])
        _seg_copies(cnt_ref, loc_ref, goff_ref, sb, mk, act)

    def wait_all(sb, slot_):
        _wait_rows(_sub_block_rows(cnt_ref, loc_ref, sb),
                   lambda b: pltpu.make_async_copy(xg_ref.at[slot_, pl.ds(0, b)], xe_ref.at[pl.ds(0, b)],
                                                   sem.at[slot_]))

    @pl.when(s >= 2)
    def _():
        wait_all(s - 2, slot)

    gt = gt_ref[...]
    _build_onehot(gt, p_ref, cnt_ref, loc_ref, s)
    x = jnp.where(s < nsb_p, hp_ref[...], hs_ref[...])
    gtt = gt.T
    ghi = gtt.astype(BF16)
    glo = (gtt - ghi.astype(F32)).astype(BF16)
    xext = jnp.concatenate([x, ghi, glo], axis=1)
    rows = _sub_block_rows(cnt_ref, loc_ref, s)
    for i in range(MOE_RPAD // MOE_CHUNK):
        @pl.when(i * MOE_CHUNK < rows)
        def _(i=i):
            xg_ref[slot, i * MOE_CHUNK:(i + 1) * MOE_CHUNK, :] = jnp.dot(
                p_ref[i * MOE_CHUNK:(i + 1) * MOE_CHUNK, :], xext, preferred_element_type=F32).astype(BF16)
    copies(s, slot, lambda cp: cp.start())

    @pl.when(s == n_sb - 1)
    def _():
        if n_sb >= 2:
            wait_all(s - 1, 1 - slot)
        wait_all(s, slot)
        z_ref[...] = jnp.zeros(z_ref.shape, BF16)

        def pads(act):
            def body(e, carry):
                n = padn_ref[e]
                off = padoff_ref[e]
                for b in MOE_DMA_SIZES:
                    @pl.when((n & b) != 0)
                    def _(b=b):
                        done = n & (-2 * b)
                        act(pltpu.make_async_copy(
                            z_ref.at[pl.ds(0, b)],
                            xe_ref.at[pl.ds(pl.multiple_of(off + done, MOE_SEG), b)], sem.at[0]))
                return carry
            lax.fori_loop(0, N_EXPERTS, body, 0)
        pads(lambda cp: cp.start())
        pads(lambda cp: cp.wait())


def _ffn_kernel(te_ref, tidx_ref, tvalid_ref, xe_ref, wg_ref, wu_ref, wd_ref, ye_ref):
    i = pl.program_id(0)

    @pl.when(tvalid_ref[i] != 0)
    def _():
        e = te_ref[i]
        g = xe_ref[:, D_MODEL:].astype(F32)
        lane = lax.broadcasted_iota(jnp.int32, g.shape, 1)
        w = jnp.sum(jnp.where((lane == e) | (lane == e + N_EXPERTS), g, 0.0), axis=1, keepdims=True)
        y = _swiglu_bf(xe_ref[:, :D_MODEL], wg_ref[0], wu_ref[0], wd_ref[0])
        ye_ref[...] = (w * y).astype(BF16)


def _combine_kernel(cnt_ref, loc_ref, goff_ref, ye_ref, gt_ref, h_ref, x1_ref, mod_ref, sg_ref, su_ref, sd_ref,
                    fn_ref, o_ref, p_ref, ys_ref, acc_ref, sem, *, sb0, n_steps):
    i = pl.program_id(0)
    slot = i % 2
    sb = sb0 + i

    def copies(sb_, slot_, act):
        def mk(off, g, b):
            return pltpu.make_async_copy(ye_ref.at[pl.ds(g, b)], ys_ref.at[slot_, pl.ds(off, b)], sem.at[slot_])
        _seg_copies(cnt_ref, loc_ref, goff_ref, sb_, mk, act)

    @pl.when(i == 0)
    def _():
        ys_ref[...] = jnp.zeros(ys_ref.shape, BF16)
        copies(sb, slot, lambda cp: cp.start())

    @pl.when(i + 1 < n_steps)
    def _():
        copies(sb + 1, 1 - slot, lambda cp: cp.start())

    acc_ref[...] = _swiglu_bf(h_ref[...], sg_ref[...], su_ref[...], sd_ref[...])
    _build_onehot(gt_ref[...], p_ref, cnt_ref, loc_ref, sb)
    rows = _sub_block_rows(cnt_ref, loc_ref, sb)
    _wait_rows(rows, lambda b: pltpu.make_async_copy(ye_ref.at[pl.ds(0, b)], ys_ref.at[slot, pl.ds(0, b)],
                                                     sem.at[slot]))
    for c in range(MOE_RPAD // MOE_CHUNK):
        @pl.when(c * MOE_CHUNK < rows)
        def _(c=c):
            acc_ref[...] += lax.dot_general(
                p_ref[c * MOE_CHUNK:(c + 1) * MOE_CHUNK, :], ys_ref[slot, c * MOE_CHUNK:(c + 1) * MOE_CHUNK, :],
                (((0,), (0,)), ((), ())), preferred_element_type=F32)
    gate2 = mod_ref[0][:, 5 * D_MODEL:6 * D_MODEL]
    x2 = x1_ref[...] + gate2 * acc_ref[...]
    ms = jnp.mean(x2 * x2, axis=-1, keepdims=True)
    o_ref[...] = x2 * lax.rsqrt(ms + NORM_EPS) * fn_ref[...]


def _moe_plan(gates_t):
    ne, n = gates_t.shape
    n_sb = n // MOE_SB
    cnt = jnp.sum((gates_t > 0.0).reshape(ne, n_sb, MOE_SB), axis=-1, dtype=jnp.int32).T
    cnt16 = (cnt + MOE_SEG - 1) // MOE_SEG * MOE_SEG
    loc = jnp.cumsum(cnt16, axis=1) - cnt16
    tot = jnp.sum(cnt16, axis=0)
    totp = (tot + MOE_TF - 1) // MOE_TF * MOE_TF
    ends = jnp.cumsum(totp)
    base = ends - totp
    goff = base[None, :] + jnp.cumsum(cnt16, axis=0) - cnt16
    rows_max = TOP_K * n + n_sb * ne * (MOE_SEG - 1) + ne * (MOE_TF - MOE_SEG)
    nt_max = -(-rows_max // MOE_TF)
    tiles = jnp.arange(nt_max, dtype=jnp.int32)
    used = ends[-1]
    valid = tiles * MOE_TF < used
    tidx = jnp.where(valid, tiles, used // MOE_TF - 1)
    te = jnp.sum(ends[None, :] <= (tidx * MOE_TF)[:, None], axis=1, dtype=jnp.int32)
    te = jnp.minimum(te, ne - 1)
    return dict(cnt=cnt16.reshape(-1), loc=loc.reshape(-1).astype(jnp.int32), goff=goff.reshape(-1).astype(jnp.int32),
                padoff=(base + tot).astype(jnp.int32), padn=(totp - tot).astype(jnp.int32),
                te=te, tidx=tidx.astype(jnp.int32), tvalid=valid.astype(jnp.int32), nt_max=nt_max)


def _dispatch_call(plan, h2_p, h2_s, gates_t):
    nsb_p = h2_p.shape[0] // MOE_SB
    n_sb = gates_t.shape[1] // MOE_SB
    grid_spec = pltpu.PrefetchScalarGridSpec(
        num_scalar_prefetch=5, grid=(n_sb,),
        in_specs=[pl.BlockSpec((MOE_SB, D_MODEL), lambda s, *_: (jnp.minimum(s, nsb_p - 1), 0)),
                  pl.BlockSpec((MOE_SB, D_MODEL), lambda s, *_: (jnp.maximum(s - nsb_p, 0), 0)),
                  pl.BlockSpec((N_EXPERTS, MOE_SB), lambda s, *_: (0, s))],
        out_specs=pl.BlockSpec(memory_space=pl.ANY),
        scratch_shapes=[pltpu.VMEM((MOE_RPAD, MOE_SB), BF16),
                        pltpu.VMEM((2, MOE_RPAD, MOE_XW), BF16),
                        pltpu.VMEM((MOE_DMA_SIZES[0], MOE_XW), BF16),
                        pltpu.SemaphoreType.DMA((2,))])
    return pl.pallas_call(
        functools.partial(_dispatch_kernel, nsb_p=nsb_p, n_sb=n_sb),
        grid_spec=grid_spec,
        out_shape=jax.ShapeDtypeStruct((plan['nt_max'] * MOE_TF, MOE_XW), BF16),
        compiler_params=_cparams(("arbitrary",)),
        name="moe_dispatch",
    )(plan['cnt'], plan['loc'], plan['goff'], plan['padoff'], plan['padn'], h2_p, h2_s, gates_t)


def _ffn_call(plan, xe, wg, wu, wd):
    grid_spec = pltpu.PrefetchScalarGridSpec(
        num_scalar_prefetch=3, grid=(plan['nt_max'],),
        in_specs=[pl.BlockSpec((MOE_TF, MOE_XW), lambda i, te, tidx, tv: (tidx[i], 0)),
                  pl.BlockSpec((1, D_MODEL, EXPERT_FF), lambda i, te, tidx, tv: (te[i], 0, 0)),
                  pl.BlockSpec((1, D_MODEL, EXPERT_FF), lambda i, te, tidx, tv: (te[i], 0, 0)),
                  pl.BlockSpec((1, EXPERT_FF, D_MODEL), lambda i, te, tidx, tv: (te[i], 0, 0))],
        out_specs=pl.BlockSpec((MOE_TF, D_MODEL), lambda i, te, tidx, tv: (tidx[i], 0)))
    return pl.pallas_call(
        _ffn_kernel,
        grid_spec=grid_spec,
        out_shape=jax.ShapeDtypeStruct((xe.shape[0], D_MODEL), BF16),
        compiler_params=_cparams(("arbitrary",)),
        name="moe_ffn",
    )(plan['te'], plan['tidx'], plan['tvalid'], xe, wg, wu, wd)


def _combine_call(plan, ye, gates_t, h2, x1, mod3, sg, su, sd, final_w, sb0, seq_len):
    n = h2.shape[0]
    n_steps = n // MOE_SB
    bpb = seq_len // MOE_SB
    row = lambda i, *_: (i, 0)
    const2 = lambda i, *_: (0, 0)
    grid_spec = pltpu.PrefetchScalarGridSpec(
        num_scalar_prefetch=3, grid=(n_steps,),
        in_specs=[pl.BlockSpec(memory_space=pl.ANY),
                  pl.BlockSpec((N_EXPERTS, MOE_SB), lambda i, *_: (0, sb0 + i)),
                  pl.BlockSpec((MOE_SB, D_MODEL), row),
                  pl.BlockSpec((MOE_SB, D_MODEL), row),
                  pl.BlockSpec((1, 1, 6 * D_MODEL), lambda i, *_: (i // bpb, 0, 0)),
                  pl.BlockSpec((D_MODEL, EXPERT_FF), const2),
                  pl.BlockSpec((D_MODEL, EXPERT_FF), const2),
                  pl.BlockSpec((EXPERT_FF, D_MODEL), const2),
                  pl.BlockSpec((1, D_MODEL), const2)],
        out_specs=pl.BlockSpec((MOE_SB, D_MODEL), row),
        scratch_shapes=[pltpu.VMEM((MOE_RPAD, MOE_SB), BF16),
                        pltpu.VMEM((2, MOE_RPAD, D_MODEL), BF16),
                        pltpu.VMEM((MOE_SB, D_MODEL), F32),
                        pltpu.SemaphoreType.DMA((2,))])
    return pl.pallas_call(
        functools.partial(_combine_kernel, sb0=sb0, n_steps=n_steps),
        grid_spec=grid_spec,
        out_shape=jax.ShapeDtypeStruct((n, D_MODEL), F32),
        compiler_params=_cparams(("arbitrary",)),
        name="moe_combine",
    )(plan['cnt'], plan['loc'], plan['goff'], ye, gates_t, h2, x1, mod3, sg, su, sd,
      final_w.reshape(1, D_MODEL))


def _moe_sparse(h2_p, h2_s, gt_p, gt_s, x1_p, x1_s, mod_p, mod_s, wg, wu, wd, sg, su, sd, final_w, dseq):
    gates_t = jnp.concatenate([gt_p, gt_s], axis=1)
    plan = _moe_plan(gates_t)
    xe = _dispatch_call(plan, h2_p, h2_s, gates_t)
    ye = _ffn_call(plan, xe, wg, wu, wd)
    n_p = h2_p.shape[0]
    y_p = _combine_call(plan, ye, gates_t, h2_p, x1_p, mod_p, sg, su, sd, final_w, 0, n_p)
    y_s = _combine_call(plan, ye, gates_t, h2_s, x1_s, mod_s, sg, su, sd, final_w, n_p // MOE_SB, dseq)
    return y_p, y_s


def _rope_tables(n_tokens):
    rows = n_tokens // GRID_W
    row = jnp.repeat(jnp.arange(rows, dtype=F32), GRID_W)
    col = jnp.tile(jnp.arange(GRID_W, dtype=F32), rows)
    freqs = ROPE_THETA ** (-jnp.arange(ROT_PAIRS, dtype=F32) / ROT_PAIRS)
    ar = row[:, None] * freqs
    ac = col[:, None] * freqs
    cos = jnp.concatenate([jnp.cos(ar), jnp.cos(ar), jnp.cos(ac), jnp.cos(ac)], axis=1)
    sin = jnp.concatenate([-jnp.sin(ar), jnp.sin(ar), -jnp.sin(ac), jnp.sin(ac)], axis=1)
    return jnp.tile(cos, (1, 2)), jnp.tile(sin, (1, 2))


def kernel(x_prompt, x_sample, c, cache_k, cache_v, state_ssm_re, state_ssm_im, c_ctx, w_ada, b_ada, norm1_w, w_in, ssm_lambda_re, ssm_lambda_im, ssm_log_dt, ssm_b_re, ssm_b_im, ssm_c_re, ssm_c_im, ssm_d, ssm_w_glu, diff_lambda_q, diff_lambda_k, diff_subln_w, w_out, norm2_w, w_router, router_bias, w_exp_gate, w_exp_up, w_exp_down, w_sh_gate, w_sh_up, w_sh_down, final_norm_w):
    depth = w_ada.shape[0]
    assert depth == 1
    l = 0
    lambda_init = 0.8 - 0.6 * math.exp(-0.3 * l)
    bsz, seq, _ = x_prompt.shape
    dbs, dseq, _ = x_sample.shape
    n_p, n_s = bsz * seq, dbs * dseq

    cond8 = jnp.zeros((SUBLANES, D_MODEL), F32).at[:dbs].set(c).at[dbs].set(c_ctx)
    mod = _ada_call(cond8, w_ada[l], b_ada[l])
    mod_s = mod[:dbs].reshape(dbs, 1, 6 * D_MODEL)
    mod_p = mod[dbs:dbs + 1].reshape(1, 1, 6 * D_MODEL)

    w_in_bf = w_in[l].astype(BF16)
    xp2 = x_prompt.reshape(n_p, D_MODEL)
    xs2 = x_sample.reshape(n_s, D_MODEL)
    u_p, q_p, k_p, v_p = _inproj_call(xp2, mod_p, norm1_w[l], w_in_bf, None, n_p, F32)
    u_s, q_s, k_s, v_s = _inproj_call(xs2, mod_s, norm1_w[l], w_in_bf, _rope_tables(dseq), dseq, BF16)

    bblk, cblk, lam_r, lam_i = _s5_weights(ssm_lambda_re[l], ssm_lambda_im[l], ssm_log_dt[l],
                                           ssm_b_re[l], ssm_b_im[l], ssm_c_re[l], ssm_c_im[l])
    lr_p = jnp.broadcast_to(lam_r[:, :, None, :], (2, N_COLBLK, SUBLANES, S5_BLK_STATES))
    li_p = jnp.broadcast_to(lam_i[:, :, None, :], (2, N_COLBLK, SUBLANES, S5_BLK_STATES))
    g_p = bsz // SUBLANES
    h0_p = jnp.zeros((2, g_p, N_COLBLK * 8, SUBLANES, LANES), F32)
    y_p, hfin = _s5_call(u_p.reshape(bsz, seq, S5_WIDTH), bblk, cblk, lr_p, li_p, h0_p, SUBLANES)

    def halves(x):
        x = x.reshape(2, 2, 2, 1, S5_BLK_STATES)
        x = jnp.broadcast_to(x, (2, 2, 2, dbs, S5_BLK_STATES))
        return x.transpose(0, 2, 1, 3, 4).reshape(2, 2, 2 * dbs, S5_BLK_STATES)

    def h0_tiles(s):
        s = s.astype(F32).reshape(dbs, 2, 2, 2, 4, LANES)
        return s.transpose(1, 3, 4, 2, 0, 5).reshape(2, 1, 2, 4, 2 * dbs, LANES)

    h0_s = jnp.concatenate([h0_tiles(state_ssm_re[:, l]), h0_tiles(state_ssm_im[:, l])], axis=3)
    h0_s = h0_s.reshape(2, 1, 16, SUBLANES, LANES)
    y_s, _ = _s5_call(u_s.reshape(dbs, dseq, S5_WIDTH), bblk, cblk, halves(lam_r), halves(lam_i), h0_s, dbs)

    lq = diff_lambda_q[l].astype(F32)
    lk = diff_lambda_k[l].astype(F32)
    lam = (jnp.exp(jnp.sum(lq[0] * lk[0])) - jnp.exp(jnp.sum(lq[1] * lk[1])) + lambda_init).reshape(1)
    subln = diff_subln_w[l].astype(F32).reshape(1, VALUE_DIM)
    ao_p = _attn_ctx_call(lam, q_p.reshape(bsz, seq, ATTN_WIDTH), k_p.reshape(bsz, seq, ATTN_WIDTH),
                          v_p.reshape(bsz, seq, ATTN_WIDTH), subln, lambda_init)
    past = cache_k.shape[2]
    ao_s = _attn_lat_call(lam, q_s.reshape(dbs, dseq, ATTN_WIDTH), k_s.reshape(dbs, dseq, ATTN_WIDTH),
                          v_s.reshape(dbs, dseq, ATTN_WIDTH),
                          cache_k[:, l].reshape(dbs, past, ATTN_WIDTH),
                          cache_v[:, l].reshape(dbs, past, ATTN_WIDTH), subln, lambda_init)

    wglu_bf = ssm_w_glu[l].astype(BF16)
    wout_bf = w_out[l].astype(BF16)
    wr_t = w_router[l].astype(F32).T
    rbias3 = router_bias[l].astype(F32).reshape(N_EXPERT_GROUPS, N_EXPERTS // N_EXPERT_GROUPS, 1)
    x1_p, h2_p, gates_p = _post_call(xp2, u_p, y_p, ao_p.reshape(n_p, ATTN_WIDTH), mod_p, ssm_d[l],
                                     wglu_bf, wout_bf, norm2_w[l], wr_t, rbias3, n_p)
    x1_s, h2_s, gates_s = _post_call(xs2, u_s, y_s, ao_s.reshape(n_s, ATTN_WIDTH), mod_s, ssm_d[l],
                                     wglu_bf, wout_bf, norm2_w[l], wr_t, rbias3, dseq)

    wg = w_exp_gate[l].astype(BF16)
    wu = w_exp_up[l].astype(BF16)
    wd = w_exp_down[l].astype(BF16)
    sg = w_sh_gate[l].astype(BF16)
    su = w_sh_up[l].astype(BF16)
    sd = w_sh_down[l].astype(BF16)
    y_prompt, y_sample = _moe_sparse(h2_p, h2_s, gates_p, gates_s, x1_p, x1_s, mod_p, mod_s,
                                     wg, wu, wd, sg, su, sd, final_norm_w, dseq)

    new_cache_k = k_p.reshape(bsz, 1, seq, N_DIFF_HEADS, VALUE_DIM)
    new_cache_v = v_p.reshape(bsz, 1, seq, N_DIFF_HEADS, VALUE_DIM)
    hf = hfin.reshape(2, g_p, N_COLBLK, 2, 4, SUBLANES, LANES)
    hf = hf.transpose(3, 1, 5, 0, 2, 4, 6).reshape(2, bsz, 1, 2, S5_GROUPS, S5_STATE)
    return (y_prompt.reshape(bsz, seq, D_MODEL), y_sample.reshape(dbs, dseq, D_MODEL),
            new_cache_k, new_cache_v, hf[0], hf[1])
```

```python
import functools
import math

import jax
import jax.numpy as jnp
from jax import lax
from jax.experimental import pallas as pl
from jax.experimental.pallas import tpu as pltpu

F32 = jnp.float32
BF16 = jnp.bfloat16

D_MODEL = 1024
GRID_W = 64
S5_WIDTH = 512
S5_GROUP = 16
S5_GROUPS = 32
S5_STATE = 64
ATTN_WIDTH = 512
DIFF_HEAD_DIM = 64
VALUE_DIM = 128
N_DIFF_HEADS = 4
IN_WIDTH = S5_WIDTH + 3 * ATTN_WIDTH
ROT_PAIRS = DIFF_HEAD_DIM // 4
ROPE_THETA = 10000.0
N_EXPERTS = 64
TOP_K = 8
N_EXPERT_GROUPS = 8
TOPK_GROUPS = 4
EXPERT_FF = 256
ROUTED_SCALE = 2.5
NORM_EPS = 1e-6

LANES = 128
SUBLANES = 8
S5_COLBLK = 8 * S5_GROUP
S5_BLK_STATES = 8 * S5_STATE
N_COLBLK = S5_WIDTH // S5_COLBLK
VMEM_LIMIT = 56 * 1024 * 1024


def _cparams(sem):
    return pltpu.CompilerParams(dimension_semantics=sem, vmem_limit_bytes=VMEM_LIMIT)


def _ada_kernel(cond_ref, w_ref, b_ref, o_ref):
    c = cond_ref[...]
    s = c * jax.nn.sigmoid(c)
    o_ref[...] = jnp.dot(s.astype(BF16), w_ref[...].astype(BF16),
                         preferred_element_type=F32) + b_ref[...]


def _ada_call(cond8, w_ada, b_ada):
    n = w_ada.shape[1]
    tn = 1536
    return pl.pallas_call(
        _ada_kernel,
        grid=(n // tn,),
        in_specs=[pl.BlockSpec((SUBLANES, D_MODEL), lambda j: (0, 0)),
                  pl.BlockSpec((D_MODEL, tn), lambda j: (0, j)),
                  pl.BlockSpec((1, tn), lambda j: (0, j))],
        out_specs=pl.BlockSpec((SUBLANES, tn), lambda j: (0, j)),
        out_shape=jax.ShapeDtypeStruct((SUBLANES, n), F32),
        compiler_params=_cparams(("arbitrary",)),
        name="adaln",
    )(cond8, w_ada, b_ada.reshape(1, n))


def _rope_apply(t, cos, sin):
    parts = []
    for cidx in range(ATTN_WIDTH // LANES):
        xc = t[:, cidx * LANES:(cidx + 1) * LANES]
        up = pltpu.roll(xc, LANES - ROT_PAIRS, 1)
        dn = pltpu.roll(xc, ROT_PAIRS, 1)
        lane = lax.broadcasted_iota(jnp.int32, xc.shape, 1)
        partner = jnp.where((lane % (2 * ROT_PAIRS)) < ROT_PAIRS, up, dn)
        parts.append(xc * cos + partner * sin)
    return jnp.concatenate(parts, axis=1)


def _inproj_kernel(*refs, rope, kv_dtype):
    if rope:
        x_ref, mod_ref, n1_ref, w_ref, cos_ref, sin_ref, u_ref, q_ref, k_ref, v_ref = refs
    else:
        x_ref, mod_ref, n1_ref, w_ref, u_ref, q_ref, k_ref, v_ref = refs
    x = x_ref[...]
    ms = jnp.mean(x * x, axis=-1, keepdims=True)
    xn = x * lax.rsqrt(ms + NORM_EPS) * n1_ref[...]
    mod = mod_ref[0]
    shift = mod[:, 0:D_MODEL]
    scale = mod[:, D_MODEL:2 * D_MODEL]
    h = xn * (1.0 + scale) + shift
    proj = jnp.dot(h.astype(BF16), w_ref[...], preferred_element_type=F32)
    u_ref[...] = proj[:, :S5_WIDTH]
    q = proj[:, S5_WIDTH:S5_WIDTH + ATTN_WIDTH]
    k = proj[:, S5_WIDTH + ATTN_WIDTH:S5_WIDTH + 2 * ATTN_WIDTH]
    v = proj[:, S5_WIDTH + 2 * ATTN_WIDTH:]
    if rope:
        cos = cos_ref[...]
        sin = sin_ref[...]
        q = _rope_apply(q, cos, sin)
        k = _rope_apply(k, cos, sin)
    q_ref[...] = (q * (DIFF_HEAD_DIM ** -0.5)).astype(BF16)
    k_ref[...] = k.astype(kv_dtype)
    v_ref[...] = v.astype(kv_dtype)


def _inproj_call(x2, mod3, norm1_w, w_in_bf, rope_tabs, seq_len, kv_dtype, tm=256):
    n = x2.shape[0]
    bpb = seq_len // tm
    rope = rope_tabs is not None
    in_specs = [pl.BlockSpec((tm, D_MODEL), lambda i: (i, 0)),
                pl.BlockSpec((1, 1, 6 * D_MODEL), lambda i: (i // bpb, 0, 0)),
                pl.BlockSpec((1, D_MODEL), lambda i: (0, 0)),
                pl.BlockSpec((D_MODEL, IN_WIDTH), lambda i: (0, 0))]
    args = [x2, mod3, norm1_w.reshape(1, D_MODEL), w_in_bf]
    if rope:
        in_specs += [pl.BlockSpec((tm, LANES), lambda i: (i % bpb, 0)),
                     pl.BlockSpec((tm, LANES), lambda i: (i % bpb, 0))]
        args += list(rope_tabs)
    ospec = pl.BlockSpec((tm, S5_WIDTH), lambda i: (i, 0))
    return pl.pallas_call(
        functools.partial(_inproj_kernel, rope=rope, kv_dtype=kv_dtype),
        grid=(n // tm,),
        in_specs=in_specs,
        out_specs=[ospec, ospec, ospec, ospec],
        out_shape=[jax.ShapeDtypeStruct((n, S5_WIDTH), F32),
                   jax.ShapeDtypeStruct((n, ATTN_WIDTH), BF16),
                   jax.ShapeDtypeStruct((n, ATTN_WIDTH), kv_dtype),
                   jax.ShapeDtypeStruct((n, ATTN_WIDTH), kv_dtype)],
        compiler_params=_cparams(("parallel",)),
        name="inproj_rope" if rope else "inproj",
    )(*args)


def _s5_slot(b, q, nseq):
    if nseq == SUBLANES:
        return b, q * 8
    half = q // 2
    return half * nseq + b, (q % 2) * 8


def _s5_kernel(u_ref, b_ref, c_ref, lr_ref, li_ref, h0_ref, y_ref, hfin_ref, buh, hs, hst,
               *, nseq, tb, stride, nqq):
    d = pl.program_id(1)
    c = pl.program_id(2)
    nc = pl.num_programs(2)

    @pl.when(c == 0)
    def _():
        hst[...] = h0_ref[0, 0]
        hs[...] = jnp.zeros(hs.shape, F32)

    u = u_ref[...].reshape(nseq * tb, S5_WIDTH).astype(BF16)
    for q in range(N_COLBLK):
        res = jnp.dot(u[:, q * S5_COLBLK:(q + 1) * S5_COLBLK], b_ref[0, q],
                      preferred_element_type=F32)
        for b in range(nseq):
            slot, ct0 = _s5_slot(b, q, nseq)
            for j in range(8):
                buh[ct0 + j, slot * stride:slot * stride + tb, :] = (
                    res[b * tb:(b + 1) * tb, j * LANES:(j + 1) * LANES])

    for qq in range(nqq):
        lr = [lr_ref[0, qq, :, i * LANES:(i + 1) * LANES] for i in range(4)]
        li = [li_ref[0, qq, :, i * LANES:(i + 1) * LANES] for i in range(4)]
        h_init = tuple(hst[qq * 8 + i] for i in range(8))

        def body(s, carry, qq=qq, lr=lr, li=li):
            t = jnp.where(d == 0, s, tb - 1 - s)
            idx = pl.ds(t, SUBLANES, stride=stride)
            new_r, new_i = [], []
            for i in range(4):
                hr, hi = carry[i], carry[4 + i]
                bur = buh[qq * 8 + i, idx, :]
                bui = buh[qq * 8 + 4 + i, idx, :]
                nr = lr[i] * hr - li[i] * hi + bur
                ni = lr[i] * hi + li[i] * hr + bui
                hs[qq * 8 + i, idx, :] = nr
                hs[qq * 8 + 4 + i, idx, :] = ni
                new_r.append(nr)
                new_i.append(ni)
            return tuple(new_r) + tuple(new_i)

        h_fin = lax.fori_loop(0, tb, body, h_init, unroll=4)
        for i in range(8):
            hst[qq * 8 + i] = h_fin[i]

    rows = nseq * stride
    for q in range(N_COLBLK):
        slot0, ct0 = _s5_slot(0, q, nseq)
        hmat = jnp.concatenate(
            [hs[ct0 + j, slot0 * stride:slot0 * stride + rows, :] for j in range(8)], axis=1)
        yq = jnp.dot(hmat.astype(BF16), c_ref[0, q], preferred_element_type=F32)
        for b in range(nseq):
            y_ref[0, b, :, q * S5_COLBLK:(q + 1) * S5_COLBLK] = yq[b * stride:b * stride + tb, :]

    @pl.when(c == nc - 1)
    def _():
        hfin_ref[0, 0] = hst[...]


def _s5_call(u3, bblk, cblk, lr, li, h0, nseq, tb=64):
    nseq_total, seq_len, _ = u3.shape
    g = nseq_total // nseq
    nc = seq_len // tb
    stride = tb + SUBLANES
    nqq = lr.shape[1]
    nct = nqq * 8

    def tmap(gi, d, c):
        return c + d * (nc - 1 - 2 * c)

    return pl.pallas_call(
        functools.partial(_s5_kernel, nseq=nseq, tb=tb, stride=stride, nqq=nqq),
        grid=(g, 2, nc),
        in_specs=[
            pl.BlockSpec((nseq, tb, S5_WIDTH), lambda gi, d, c: (gi, tmap(gi, d, c), 0)),
            pl.BlockSpec((1, N_COLBLK, S5_COLBLK, 2 * S5_BLK_STATES), lambda gi, d, c: (d, 0, 0, 0)),
            pl.BlockSpec((1, N_COLBLK, 2 * S5_BLK_STATES, S5_COLBLK), lambda gi, d, c: (d, 0, 0, 0)),
            pl.BlockSpec((1, nqq, SUBLANES, S5_BLK_STATES), lambda gi, d, c: (d, 0, 0, 0)),
            pl.BlockSpec((1, nqq, SUBLANES, S5_BLK_STATES), lambda gi, d, c: (d, 0, 0, 0)),
            pl.BlockSpec((1, 1, nct, SUBLANES, LANES), lambda gi, d, c: (d, gi, 0, 0, 0)),
        ],
        out_specs=[
            pl.BlockSpec((1, nseq, tb, S5_WIDTH), lambda gi, d, c: (d, gi, tmap(gi, d, c), 0)),
            pl.BlockSpec((1, 1, nct, SUBLANES, LANES), lambda gi, d, c: (d, gi, 0, 0, 0)),
        ],
        out_shape=[jax.ShapeDtypeStruct((2, nseq_total, seq_len, S5_WIDTH), F32),
                   jax.ShapeDtypeStruct((2, g, nct, SUBLANES, LANES), F32)],
        scratch_shapes=[pltpu.VMEM((nct, SUBLANES * stride, LANES), F32),
                        pltpu.VMEM((nct, SUBLANES * stride, LANES), F32),
                        pltpu.VMEM((nct, SUBLANES, LANES), F32)],
        compiler_params=_cparams(("parallel", "parallel", "arbitrary")),
        name="s5_scan_%d" % nseq,
    )(u3, bblk, cblk, lr, li, h0)


def _s5_weights(lam_re, lam_im, log_dt, b_re, b_im, c_re, c_im):
    dt = jnp.exp(log_dt.astype(F32))[..., None]
    lr = lam_re.astype(F32)
    li = lam_im.astype(F32)
    mag = jnp.exp(lr * dt)
    lbr = mag * jnp.cos(li * dt)
    lbi = mag * jnp.sin(li * dt)
    a = lbr - 1.0
    den = lr * lr + li * li
    cr = (a * lr + lbi * li) / den
    ci = (lbi * lr - a * li) / den
    bbr = cr[..., None] * b_re - ci[..., None] * b_im
    bbi = cr[..., None] * b_im + ci[..., None] * b_re
    eye = jnp.eye(8, dtype=F32)

    def pack_b(x):
        x = x.reshape(2, N_COLBLK, 8, S5_STATE, S5_GROUP).transpose(0, 1, 2, 4, 3)
        x = jnp.einsum('dqghn,gk->dqghkn', x, eye)
        return x.reshape(2, N_COLBLK, S5_COLBLK, S5_BLK_STATES)

    def pack_c(x):
        x = x.reshape(2, N_COLBLK, 8, S5_GROUP, S5_STATE).transpose(0, 1, 2, 4, 3)
        x = jnp.einsum('dqgnh,gk->dqgnkh', x, eye)
        return x.reshape(2, N_COLBLK, S5_BLK_STATES, S5_COLBLK)

    bblk = jnp.concatenate([pack_b(bbr), pack_b(bbi)], axis=-1).astype(BF16)
    cblk = jnp.concatenate([pack_c(c_re.astype(F32)), pack_c(-c_im.astype(F32))], axis=-2).astype(BF16)
    lam_r = lbr.reshape(2, N_COLBLK, S5_BLK_STATES)
    lam_i = lbi.reshape(2, N_COLBLK, S5_BLK_STATES)
    return bblk, cblk, lam_r, lam_i


ATTN_KC = 512


def _lane_fold(x, op):
    acc = x[:, 0:LANES]
    for j in range(1, x.shape[1] // LANES):
        acc = op(acc, x[:, j * LANES:(j + 1) * LANES])
    return acc


def _diff_attention(q, chunks, lam, subln, lambda_init, s_ref):
    lane = lax.broadcasted_iota(jnp.int32, q.shape, 1)
    dn = (((1,), (1,)), ((), ()))
    qms = [jnp.where(lane < DIFF_HEAD_DIM, q, jnp.zeros_like(q)),
           jnp.where(lane >= DIFF_HEAD_DIM, q, jnp.zeros_like(q))]
    offs = [0]
    for load_k, _ in chunks:
        offs.append(offs[-1] + load_k().shape[0])

    def score_chunk(m, ci, m128):
        s = lax.dot_general(qms[m], chunks[ci][0](), dn, preferred_element_type=F32)
        s_ref[m, :, offs[ci]:offs[ci + 1]] = s
        f = _lane_fold(s, jnp.maximum)
        return f if m128 is None else jnp.maximum(m128, f)

    def value_chunk(m, ci, mx, l128, o):
        p = jnp.exp(s_ref[m, :, offs[ci]:offs[ci + 1]] - mx)
        f = _lane_fold(p, jnp.add)
        t = jnp.dot(p.astype(BF16), chunks[ci][1](), preferred_element_type=F32)
        return (f if l128 is None else l128 + f), (t if o is None else o + t)

    nchunk = len(chunks)
    m128_0 = m128_1 = l0 = o0 = l1 = o1 = None
    for ci in range(nchunk):
        m128_0 = score_chunk(0, ci, m128_0)
    mx0 = m128_0.max(axis=-1, keepdims=True)
    for ci in range(nchunk):
        m128_1 = score_chunk(1, ci, m128_1)
        l0, o0 = value_chunk(0, ci, mx0, l0, o0)
    mx1 = m128_1.max(axis=-1, keepdims=True)
    for ci in range(nchunk):
        l1, o1 = value_chunk(1, ci, mx1, l1, o1)
    o = (o0 * (1.0 / l0.sum(axis=-1, keepdims=True))
         - lam * (o1 * (1.0 / l1.sum(axis=-1, keepdims=True))))
    ms = jnp.mean(o * o, axis=-1, keepdims=True)
    return o * lax.rsqrt(ms + NORM_EPS) * subln * (1.0 - lambda_init)


def _attn_ctx_kernel(lam_ref, q_ref, k_ref, v_ref, w_ref, o_ref, s_ref, *, lambda_init):
    chunks = [(lambda: k_ref[0].astype(BF16), lambda: v_ref[0].astype(BF16))]
    o = _diff_attention(q_ref[0], chunks, lam_ref[0], w_ref[...], lambda_init, s_ref)
    o_ref[0] = o.astype(BF16)


def _attn_lat_kernel(lam_ref, q_ref, k_ref, v_ref, ck_ref, cv_ref, w_ref, o_ref, s_ref, *, lambda_init):
    chunks = []
    for c in range(k_ref.shape[1] // ATTN_KC):
        chunks.append((lambda c=c: k_ref[0, c * ATTN_KC:(c + 1) * ATTN_KC, :],
                       lambda c=c: v_ref[0, c * ATTN_KC:(c + 1) * ATTN_KC, :]))
    for c in range(ck_ref.shape[1] // ATTN_KC):
        chunks.append((lambda c=c: ck_ref[0, c * ATTN_KC:(c + 1) * ATTN_KC, :],
                       lambda c=c: cv_ref[0, c * ATTN_KC:(c + 1) * ATTN_KC, :]))
    o = _diff_attention(q_ref[0], chunks, lam_ref[0], w_ref[...], lambda_init, s_ref)
    o_ref[0] = o.astype(BF16)


def _attn_ctx_call(lam, q3, k3, v3, subln, lambda_init):
    bsz, seq_len, _ = q3.shape
    spec = pl.BlockSpec((1, seq_len, VALUE_DIM), lambda b, h: (b, 0, h))
    return pl.pallas_call(
        functools.partial(_attn_ctx_kernel, lambda_init=lambda_init),
        grid=(bsz, N_DIFF_HEADS),
        in_specs=[pl.BlockSpec(memory_space=pltpu.SMEM), spec, spec, spec,
                  pl.BlockSpec((1, VALUE_DIM), lambda b, h: (0, 0))],
        out_specs=spec,
        out_shape=jax.ShapeDtypeStruct((bsz, seq_len, ATTN_WIDTH), BF16),
        scratch_shapes=[pltpu.VMEM((2, seq_len, seq_len), F32)],
        compiler_params=_cparams(("parallel", "parallel")),
        name="attn_ctx",
    )(lam, q3, k3, v3, subln)


def _attn_lat_call(lam, q3, k3, v3, ck3, cv3, subln, lambda_init, tq=256):
    bsz, seq_len, _ = q3.shape
    ctx_len = ck3.shape[1]
    qspec = pl.BlockSpec((1, tq, VALUE_DIM), lambda b, h, i: (b, i, h))
    kspec = pl.BlockSpec((1, seq_len, VALUE_DIM), lambda b, h, i: (b, 0, h))
    cspec = pl.BlockSpec((1, ctx_len, VALUE_DIM), lambda b, h, i: (b, 0, h))
    return pl.pallas_call(
        functools.partial(_attn_lat_kernel, lambda_init=lambda_init),
        grid=(bsz, N_DIFF_HEADS, seq_len // tq),
        in_specs=[pl.BlockSpec(memory_space=pltpu.SMEM), qspec, kspec, kspec, cspec, cspec,
                  pl.BlockSpec((1, VALUE_DIM), lambda b, h, i: (0, 0))],
        out_specs=qspec,
        out_shape=jax.ShapeDtypeStruct((bsz, seq_len, ATTN_WIDTH), BF16),
        scratch_shapes=[pltpu.VMEM((2, tq, seq_len + ctx_len), F32)],
        compiler_params=_cparams(("parallel", "parallel", "arbitrary")),
        name="attn_lat",
    )(lam, q3, k3, v3, ck3, cv3, subln)


def _route(logits_t, bias3, tm):
    ng, ge = N_EXPERT_GROUPS, N_EXPERTS // N_EXPERT_GROUPS
    neg = jnp.float32(-jnp.inf)
    sc = jax.nn.sigmoid(logits_t).reshape(ng, ge, tm)
    bi = sc + bias3
    eio = lax.broadcasted_iota(jnp.int32, (ng, ge, tm), 1).astype(F32)
    gio = lax.broadcasted_iota(jnp.int32, (ng, ge, tm), 0).astype(F32)
    m1 = bi.max(axis=1, keepdims=True)
    i1 = jnp.where(bi == m1, eio, float(ge)).min(axis=1, keepdims=True)
    m2 = jnp.where(eio == i1, neg, bi).max(axis=1, keepdims=True)
    gs = jnp.broadcast_to(m1 + m2, (ng, ge, tm))
    cnt = jnp.zeros((ng, ge, tm), F32)
    for g2 in range(ng):
        o = gs[g2:g2 + 1]
        better = (o > gs) | ((o == gs) & (gio > float(g2)))
        cnt = cnt + jnp.where(better, 1.0, 0.0)
    v = jnp.where(cnt < float(TOPK_GROUPS), bi, neg)
    eidx = gio * float(ge) + eio
    selm = jnp.zeros((ng, ge, tm), F32)
    for _ in range(TOP_K):
        m = v.max(axis=0, keepdims=True).max(axis=1, keepdims=True)
        ix = jnp.where(v == m, eidx, float(N_EXPERTS)).min(axis=0, keepdims=True).min(axis=1, keepdims=True)
        oh = eidx == ix
        selm = jnp.where(oh, 1.0, selm)
        v = jnp.where(oh, neg, v)
    selsc = selm * sc
    ssum = selsc.sum(axis=0, keepdims=True).sum(axis=1, keepdims=True)
    return (selsc / ssum * ROUTED_SCALE).reshape(N_EXPERTS, tm)


def _post_kernel(x_ref, u_ref, yf_ref, yb_ref, ao_ref, mod_ref, d_ref, wglu_ref, wout_ref, n2_ref,
                 wrt_ref, rb_ref, x1_ref, h2_ref, gates_ref, *, tm):
    u = u_ref[...]
    y = u * d_ref[...] + yf_ref[0] + yb_ref[0]
    g = jax.nn.gelu(y)
    s5 = g * jax.nn.sigmoid(jnp.dot(g.astype(BF16), wglu_ref[...], preferred_element_type=F32))
    mixed = (jnp.dot(s5.astype(BF16), wout_ref[0:S5_WIDTH, :], preferred_element_type=F32)
             + jnp.dot(ao_ref[...], wout_ref[S5_WIDTH:, :], preferred_element_type=F32))
    mod = mod_ref[0]
    gate1 = mod[:, 2 * D_MODEL:3 * D_MODEL]
    shift2 = mod[:, 3 * D_MODEL:4 * D_MODEL]
    scale2 = mod[:, 4 * D_MODEL:5 * D_MODEL]
    x1 = x_ref[...] + gate1 * mixed
    x1_ref[...] = x1
    ms = jnp.mean(x1 * x1, axis=-1, keepdims=True)
    h2 = x1 * lax.rsqrt(ms + NORM_EPS) * n2_ref[...] * (1.0 + scale2) + shift2
    h2_ref[...] = h2.astype(BF16)
    logits_t = lax.dot_general(wrt_ref[...], h2, (((1,), (1,)), ((), ())),
                               preferred_element_type=F32, precision=lax.Precision.HIGHEST)
    gates_ref[...] = _route(logits_t, rb_ref[...], tm)


def _post_call(x2, u2, y4, ao2, mod3, ssm_d, wglu_bf, wout_bf, norm2_w, wr_t, rbias3, seq_len, tm=256):
    n = x2.shape[0]
    bpb = seq_len // tm
    y3 = y4.reshape(2, n, S5_WIDTH)
    row = lambda i: (i, 0)
    const2 = lambda i: (0, 0)
    return pl.pallas_call(
        functools.partial(_post_kernel, tm=tm),
        grid=(n // tm,),
        in_specs=[pl.BlockSpec((tm, D_MODEL), row),
                  pl.BlockSpec((tm, S5_WIDTH), row),
                  pl.BlockSpec((1, tm, S5_WIDTH), lambda i: (0, i, 0)),
                  pl.BlockSpec((1, tm, S5_WIDTH), lambda i: (1, i, 0)),
                  pl.BlockSpec((tm, ATTN_WIDTH), row),
                  pl.BlockSpec((1, 1, 6 * D_MODEL), lambda i: (i // bpb, 0, 0)),
                  pl.BlockSpec((1, S5_WIDTH), const2),
                  pl.BlockSpec((S5_WIDTH, S5_WIDTH), const2),
                  pl.BlockSpec((D_MODEL, D_MODEL), const2),
                  pl.BlockSpec((1, D_MODEL), const2),
                  pl.BlockSpec((N_EXPERTS, D_MODEL), const2),
                  pl.BlockSpec((N_EXPERT_GROUPS, N_EXPERTS // N_EXPERT_GROUPS, 1), lambda i: (0, 0, 0))],
        out_specs=[pl.BlockSpec((tm, D_MODEL), row),
                   pl.BlockSpec((tm, D_MODEL), row),
                   pl.BlockSpec((N_EXPERTS, tm), lambda i: (0, i))],
        out_shape=[jax.ShapeDtypeStruct((n, D_MODEL), F32),
                   jax.ShapeDtypeStruct((n, D_MODEL), BF16),
                   jax.ShapeDtypeStruct((N_EXPERTS, n), F32)],
        compiler_params=_cparams(("parallel",)),
        name="post_mix_route",
    )(x2, u2, y3, y3, ao2, mod3, ssm_d.reshape(1, S5_WIDTH), wglu_bf, wout_bf,
      norm2_w.reshape(1, D_MODEL), wr_t, rbias3)


def _swiglu_bf(t, wg, wu, wd):
    a = jnp.dot(t, wg, preferred_element_type=F32)
    b = jnp.dot(t, wu, preferred_element_type=F32)
    act = a * jax.nn.sigmoid(a) * b
    return jnp.dot(act.astype(BF16), wd, preferred_element_type=F32)


MOE_SB = 256
MOE_SEG = 16
MOE_TF = 1024
MOE_XW = D_MODEL + 2 * N_EXPERTS
MOE_STATIC_CHUNKS = 3
MOE_CHUNK = 512
MOE_DMA_SIZES = tuple(2 ** k for k in range(MOE_TF.bit_length() - 2, 3, -1))
MOE_RPAD = -(-(TOP_K * MOE_SB + N_EXPERTS * (MOE_SEG - 1) + MOE_STATIC_CHUNKS * MOE_SEG) // MOE_CHUNK) * MOE_CHUNK


MOE_P32 = MOE_RPAD // (2 * MOE_SEG)
MOE_P16 = N_EXPERTS
MOE_PACK = 4096


def _seg_copies(pieces, sb, make_copy, act):
    p32_ref, n32_ref, p16_ref, n16_ref = pieces
    for p_ref, n_ref, width, rows in ((p32_ref, n32_ref, MOE_P32, 2 * MOE_SEG), (p16_ref, n16_ref, MOE_P16, MOE_SEG)):
        def body(k, carry, p_ref=p_ref, width=width, rows=rows):
            v = p_ref[sb * width + k]
            act(make_copy(pl.multiple_of(v & (MOE_PACK - 1), MOE_SEG),
                          pl.multiple_of(v >> (MOE_PACK.bit_length() - 1), MOE_SEG), rows))
            return carry
        lax.fori_loop(0, n_ref[sb], body, 0)


MOE_WAIT_SIZES = tuple(2 ** k for k in range((MOE_RPAD - 1).bit_length() - 1, 3, -1))


def _sub_block_rows(cnt_ref, loc_ref, sb):
    last = sb * N_EXPERTS + N_EXPERTS - 1
    return loc_ref[last] + cnt_ref[last]


def _wait_rows(rows, make_copy):
    for b in MOE_WAIT_SIZES:
        @pl.when((rows & b) != 0)
        def _(b=b):
            make_copy(b).wait()


def _build_onehot(gt, p_ref, cnt_ref, loc_ref, sb):
    t = gt.shape[1]
    selm = gt > 0.0
    r = lax.broadcasted_iota(jnp.int32, (t, t), 0)
    c = lax.broadcasted_iota(jnp.int32, (t, t), 1)
    upper = jnp.where(r < c, 1.0, 0.0).astype(BF16)
    rank = jnp.dot(jnp.where(selm, 1.0, 0.0).astype(BF16), upper, preferred_element_type=F32)
    key = jnp.where(selm, rank, -1.0)
    j16 = lax.broadcasted_iota(jnp.int32, (MOE_SEG, t), 0).astype(F32)
    p_ref[...] = jnp.zeros(p_ref.shape, BF16)
    for e in range(N_EXPERTS):
        off = loc_ref[sb * N_EXPERTS + e]
        n = cnt_ref[sb * N_EXPERTS + e]
        row = key[e:e + 1, :]

        def put(cidx, row=row, off=off):
            first = cidx * MOE_SEG
            firstf = float(first) if isinstance(first, int) else first.astype(F32)
            tile = jnp.where(row == j16 + firstf, 1.0, 0.0).astype(BF16)
            p_ref[pl.ds(pl.multiple_of(off + first, MOE_SEG), MOE_SEG), :] = tile

        for cidx in range(MOE_STATIC_CHUNKS):
            put(cidx)

        def extra(cidx, carry, put=put):
            put(cidx)
            return carry
        lax.fori_loop(MOE_STATIC_CHUNKS, n // MOE_SEG, extra, 0)


def _dispatch_kernel(cnt_ref, loc_ref, p32_ref, n32_ref, p16_ref, n16_ref, padoff_ref, padn_ref,
                     hp_ref, hs_ref, gt_ref, xe_ref, p_ref, xg_ref, z_ref, sem, *, nsb_p, n_sb):
    s = pl.program_id(0)
    slot = s % 2

    def copies(sb, slot_, act):
        def mk(off, g, b):
            return pltpu.make_async_copy(xg_ref.at[slot_, pl.ds(off, b)], xe_ref.at[pl.ds(g, b)], sem.at[slot_])
        _seg_copies((p32_ref, n32_ref, p16_ref, n16_ref), sb, mk, act)

    def wait_all(sb, slot_):
        _wait_rows(_sub_block_rows(cnt_ref, loc_ref, sb),
                   lambda b: pltpu.make_async_copy(xg_ref.at[slot_, pl.ds(0, b)], xe_ref.at[pl.ds(0, b)],
                                                   sem.at[slot_]))

    @pl.when(s >= 2)
    def _():
        wait_all(s - 2, slot)

    gt = gt_ref[...]
    _build_onehot(gt, p_ref, cnt_ref, loc_ref, s)
    x = jnp.where(s < nsb_p, hp_ref[...], hs_ref[...])
    gtt = gt.T
    ghi = gtt.astype(BF16)
    glo = (gtt - ghi.astype(F32)).astype(BF16)
    xext = jnp.concatenate([x, ghi, glo], axis=1)
    rows = _sub_block_rows(cnt_ref, loc_ref, s)
    for i in range(MOE_RPAD // MOE_CHUNK):
        @pl.when(i * MOE_CHUNK < rows)
        def _(i=i):
            xg_ref[slot, i * MOE_CHUNK:(i + 1) * MOE_CHUNK, :] = jnp.dot(
                p_ref[i * MOE_CHUNK:(i + 1) * MOE_CHUNK, :], xext, preferred_element_type=F32).astype(BF16)
    copies(s, slot, lambda cp: cp.start())

    @pl.when(s == n_sb - 1)
    def _():
        if n_sb >= 2:
            wait_all(s - 1, 1 - slot)
        wait_all(s, slot)
        z_ref[...] = jnp.zeros(z_ref.shape, BF16)

        def pads(act):
            def body(e, carry):
                n = padn_ref[e]
                off = padoff_ref[e]
                for b in MOE_DMA_SIZES:
                    @pl.when((n & b) != 0)
                    def _(b=b):
                        done = n & (-2 * b)
                        act(pltpu.make_async_copy(
                            z_ref.at[pl.ds(0, b)],
                            xe_ref.at[pl.ds(pl.multiple_of(off + done, MOE_SEG), b)], sem.at[0]))
                return carry
            lax.fori_loop(0, N_EXPERTS, body, 0)
        pads(lambda cp: cp.start())
        pads(lambda cp: cp.wait())


def _ffn_kernel(te_ref, tidx_ref, tvalid_ref, xe_ref, wg_ref, wu_ref, wd_ref, ye_ref):
    i = pl.program_id(0)

    @pl.when(tvalid_ref[i] != 0)
    def _():
        e = te_ref[i]
        g = xe_ref[:, D_MODEL:].astype(F32)
        lane = lax.broadcasted_iota(jnp.int32, g.shape, 1)
        w = jnp.sum(jnp.where((lane == e) | (lane == e + N_EXPERTS), g, 0.0), axis=1, keepdims=True)
        y = _swiglu_bf(xe_ref[:, :D_MODEL], wg_ref[0], wu_ref[0], wd_ref[0])
        ye_ref[...] = (w * y).astype(BF16)


def _combine_kernel(cnt_ref, loc_ref, p32_ref, n32_ref, p16_ref, n16_ref, ye_ref, gt_ref, h_ref, x1_ref, mod_ref,
                    sg_ref, su_ref, sd_ref, fn_ref, o_ref, p_ref, ys_ref, acc_ref, sem, *, sb0, n_steps):
    i = pl.program_id(0)
    slot = i % 2
    sb = sb0 + i

    def copies(sb_, slot_, act):
        def mk(off, g, b):
            return pltpu.make_async_copy(ye_ref.at[pl.ds(g, b)], ys_ref.at[slot_, pl.ds(off, b)], sem.at[slot_])
        _seg_copies((p32_ref, n32_ref, p16_ref, n16_ref), sb_, mk, act)

    @pl.when(i == 0)
    def _():
        ys_ref[...] = jnp.zeros(ys_ref.shape, BF16)
        copies(sb, slot, lambda cp: cp.start())

    @pl.when(i + 1 < n_steps)
    def _():
        copies(sb + 1, 1 - slot, lambda cp: cp.start())

    acc_ref[...] = _swiglu_bf(h_ref[...], sg_ref[...], su_ref[...], sd_ref[...])
    _build_onehot(gt_ref[...], p_ref, cnt_ref, loc_ref, sb)
    rows = _sub_block_rows(cnt_ref, loc_ref, sb)
    _wait_rows(rows, lambda b: pltpu.make_async_copy(ye_ref.at[pl.ds(0, b)], ys_ref.at[slot, pl.ds(0, b)],
                                                     sem.at[slot]))
    for c in range(MOE_RPAD // MOE_CHUNK):
        @pl.when(c * MOE_CHUNK < rows)
        def _(c=c):
            acc_ref[...] += lax.dot_general(
                p_ref[c * MOE_CHUNK:(c + 1) * MOE_CHUNK, :], ys_ref[slot, c * MOE_CHUNK:(c + 1) * MOE_CHUNK, :],
                (((0,), (0,)), ((), ())), preferred_element_type=F32)
    gate2 = mod_ref[0][:, 5 * D_MODEL:6 * D_MODEL]
    x2 = x1_ref[...] + gate2 * acc_ref[...]
    ms = jnp.mean(x2 * x2, axis=-1, keepdims=True)
    o_ref[...] = x2 * lax.rsqrt(ms + NORM_EPS) * fn_ref[...]


def _moe_plan(gates_t):
    ne, n = gates_t.shape
    n_sb = n // MOE_SB
    cnt = jnp.sum((gates_t > 0.0).reshape(ne, n_sb, MOE_SB), axis=-1, dtype=jnp.int32).T
    cnt16 = (cnt + MOE_SEG - 1) // MOE_SEG * MOE_SEG
    loc = jnp.cumsum(cnt16, axis=1) - cnt16
    tot = jnp.sum(cnt16, axis=0)
    totp = (tot + MOE_TF - 1) // MOE_TF * MOE_TF
    ends = jnp.cumsum(totp)
    base = ends - totp
    goff = base[None, :] + jnp.cumsum(cnt16, axis=0) - cnt16
    rows_max = TOP_K * n + n_sb * ne * (MOE_SEG - 1) + ne * (MOE_TF - MOE_SEG)
    nt_max = -(-rows_max // MOE_TF)
    tiles = jnp.arange(nt_max, dtype=jnp.int32)
    used = ends[-1]
    valid = tiles * MOE_TF < used
    tidx = jnp.where(valid, tiles, jnp.maximum(used // MOE_TF - 1, 0))
    te = jnp.sum(ends[None, :] <= (tidx * MOE_TF)[:, None], axis=1, dtype=jnp.int32)
    te = jnp.minimum(te, ne - 1)

    def piece_list(npieces, first_row, width, rows):
        cum = jnp.cumsum(npieces, axis=1)
        k = jnp.arange(width, dtype=jnp.int32)
        ek = jnp.minimum(jnp.sum(cum[:, None, :] <= k[None, :, None], axis=2, dtype=jnp.int32), ne - 1)
        onehot = ek[:, :, None] == jnp.arange(ne, dtype=jnp.int32)[None, None, :]
        pick = lambda a: jnp.sum(jnp.where(onehot, a[:, None, :], 0), axis=2, dtype=jnp.int32)
        row = pick(first_row) + rows * (k[None, :] - pick(cum - npieces))
        packed = (pick(goff) + row) * MOE_PACK + pick(loc) + row
        return packed.reshape(-1).astype(jnp.int32), cum[:, -1].astype(jnp.int32)

    n32 = cnt16 // (2 * MOE_SEG)
    p32, t32 = piece_list(n32, jnp.zeros_like(cnt16), MOE_P32, 2 * MOE_SEG)
    p16, t16 = piece_list(cnt16 // MOE_SEG % 2, n32 * (2 * MOE_SEG), MOE_P16, MOE_SEG)
    return dict(cnt=cnt16.reshape(-1), loc=loc.reshape(-1).astype(jnp.int32), pieces=(p32, t32, p16, t16),
                padoff=(base + tot).astype(jnp.int32), padn=(totp - tot).astype(jnp.int32),
                te=te, tidx=tidx.astype(jnp.int32), tvalid=valid.astype(jnp.int32), nt_max=nt_max)


def _dispatch_call(plan, h2_p, h2_s, gates_t):
    nsb_p = h2_p.shape[0] // MOE_SB
    n_sb = gates_t.shape[1] // MOE_SB
    grid_spec = pltpu.PrefetchScalarGridSpec(
        num_scalar_prefetch=8, grid=(n_sb,),
        in_specs=[pl.BlockSpec((MOE_SB, D_MODEL), lambda s, *_: (jnp.minimum(s, nsb_p - 1), 0)),
                  pl.BlockSpec((MOE_SB, D_MODEL), lambda s, *_: (jnp.maximum(s - nsb_p, 0), 0)),
                  pl.BlockSpec((N_EXPERTS, MOE_SB), lambda s, *_: (0, s))],
        out_specs=pl.BlockSpec(memory_space=pl.ANY),
        scratch_shapes=[pltpu.VMEM((MOE_RPAD, MOE_SB), BF16),
                        pltpu.VMEM((2, MOE_RPAD, MOE_XW), BF16),
                        pltpu.VMEM((MOE_DMA_SIZES[0], MOE_XW), BF16),
                        pltpu.SemaphoreType.DMA((2,))])
    return pl.pallas_call(
        functools.partial(_dispatch_kernel, nsb_p=nsb_p, n_sb=n_sb),
        grid_spec=grid_spec,
        out_shape=jax.ShapeDtypeStruct((plan['nt_max'] * MOE_TF, MOE_XW), BF16),
        compiler_params=_cparams(("arbitrary",)),
        name="moe_dispatch",
    )(plan['cnt'], plan['loc'], *plan['pieces'], plan['padoff'], plan['padn'], h2_p, h2_s, gates_t)


def _ffn_call(plan, xe, wg, wu, wd):
    grid_spec = pltpu.PrefetchScalarGridSpec(
        num_scalar_prefetch=3, grid=(plan['nt_max'],),
        in_specs=[pl.BlockSpec((MOE_TF, MOE_XW), lambda i, te, tidx, tv: (tidx[i], 0)),
                  pl.BlockSpec((1, D_MODEL, EXPERT_FF), lambda i, te, tidx, tv: (te[i], 0, 0)),
                  pl.BlockSpec((1, D_MODEL, EXPERT_FF), lambda i, te, tidx, tv: (te[i], 0, 0)),
                  pl.BlockSpec((1, EXPERT_FF, D_MODEL), lambda i, te, tidx, tv: (te[i], 0, 0))],
        out_specs=pl.BlockSpec((MOE_TF, D_MODEL), lambda i, te, tidx, tv: (tidx[i], 0)))
    return pl.pallas_call(
        _ffn_kernel,
        grid_spec=grid_spec,
        out_shape=jax.ShapeDtypeStruct((xe.shape[0], D_MODEL), BF16),
        compiler_params=_cparams(("arbitrary",)),
        name="moe_ffn",
    )(plan['te'], plan['tidx'], plan['tvalid'], xe, wg, wu, wd)


def _combine_call(plan, ye, gates_t, h2, x1, mod3, sg, su, sd, final_w, sb0, seq_len):
    n = h2.shape[0]
    n_steps = n // MOE_SB
    bpb = seq_len // MOE_SB
    row = lambda i, *_: (i, 0)
    const2 = lambda i, *_: (0, 0)
    grid_spec = pltpu.PrefetchScalarGridSpec(
        num_scalar_prefetch=6, grid=(n_steps,),
        in_specs=[pl.BlockSpec(memory_space=pl.ANY),
                  pl.BlockSpec((N_EXPERTS, MOE_SB), lambda i, *_: (0, sb0 + i)),
                  pl.BlockSpec((MOE_SB, D_MODEL), row),
                  pl.BlockSpec((MOE_SB, D_MODEL), row),
                  pl.BlockSpec((1, 1, 6 * D_MODEL), lambda i, *_: (i // bpb, 0, 0)),
                  pl.BlockSpec((D_MODEL, EXPERT_FF), const2),
                  pl.BlockSpec((D_MODEL, EXPERT_FF), const2),
                  pl.BlockSpec((EXPERT_FF, D_MODEL), const2),
                  pl.BlockSpec((1, D_MODEL), const2)],
        out_specs=pl.BlockSpec((MOE_SB, D_MODEL), row),
        scratch_shapes=[pltpu.VMEM((MOE_RPAD, MOE_SB), BF16),
                        pltpu.VMEM((2, MOE_RPAD, D_MODEL), BF16),
                        pltpu.VMEM((MOE_SB, D_MODEL), F32),
                        pltpu.SemaphoreType.DMA((2,))])
    return pl.pallas_call(
        functools.partial(_combine_kernel, sb0=sb0, n_steps=n_steps),
        grid_spec=grid_spec,
        out_shape=jax.ShapeDtypeStruct((n, D_MODEL), F32),
        compiler_params=_cparams(("arbitrary",)),
        name="moe_combine",
    )(plan['cnt'], plan['loc'], *plan['pieces'], ye, gates_t, h2, x1, mod3, sg, su, sd,
      final_w.reshape(1, D_MODEL))


def _moe_sparse(h2_p, h2_s, gt_p, gt_s, x1_p, x1_s, mod_p, mod_s, wg, wu, wd, sg, su, sd, final_w, dseq):
    gates_t = jnp.concatenate([gt_p, gt_s], axis=1)
    plan = _moe_plan(gates_t)
    xe = _dispatch_call(plan, h2_p, h2_s, gates_t)
    ye = _ffn_call(plan, xe, wg, wu, wd)
    n_p = h2_p.shape[0]
    y_p = _combine_call(plan, ye, gates_t, h2_p, x1_p, mod_p, sg, su, sd, final_w, 0, n_p)
    y_s = _combine_call(plan, ye, gates_t, h2_s, x1_s, mod_s, sg, su, sd, final_w, n_p // MOE_SB, dseq)
    return y_p, y_s


def _rope_tables(n_tokens):
    rows = n_tokens // GRID_W
    row = jnp.repeat(jnp.arange(rows, dtype=F32), GRID_W)
    col = jnp.tile(jnp.arange(GRID_W, dtype=F32), rows)
    freqs = ROPE_THETA ** (-jnp.arange(ROT_PAIRS, dtype=F32) / ROT_PAIRS)
    ar = row[:, None] * freqs
    ac = col[:, None] * freqs
    cos = jnp.concatenate([jnp.cos(ar), jnp.cos(ar), jnp.cos(ac), jnp.cos(ac)], axis=1)
    sin = jnp.concatenate([-jnp.sin(ar), jnp.sin(ar), -jnp.sin(ac), jnp.sin(ac)], axis=1)
    return jnp.tile(cos, (1, 2)), jnp.tile(sin, (1, 2))


def kernel(x_prompt, x_sample, c, cache_k, cache_v, state_ssm_re, state_ssm_im, c_ctx, w_ada, b_ada, norm1_w, w_in, ssm_lambda_re, ssm_lambda_im, ssm_log_dt, ssm_b_re, ssm_b_im, ssm_c_re, ssm_c_im, ssm_d, ssm_w_glu, diff_lambda_q, diff_lambda_k, diff_subln_w, w_out, norm2_w, w_router, router_bias, w_exp_gate, w_exp_up, w_exp_down, w_sh_gate, w_sh_up, w_sh_down, final_norm_w):
    depth = w_ada.shape[0]
    assert depth == 1
    l = 0
    lambda_init = 0.8 - 0.6 * math.exp(-0.3 * l)
    bsz, seq, _ = x_prompt.shape
    dbs, dseq, _ = x_sample.shape
    n_p, n_s = bsz * seq, dbs * dseq

    cond8 = jnp.zeros((SUBLANES, D_MODEL), F32).at[:dbs].set(c).at[dbs].set(c_ctx)
    mod = _ada_call(cond8, w_ada[l], b_ada[l])
    mod_s = mod[:dbs].reshape(dbs, 1, 6 * D_MODEL)
    mod_p = mod[dbs:dbs + 1].reshape(1, 1, 6 * D_MODEL)

    w_in_bf = w_in[l].astype(BF16)
    xp2 = x_prompt.reshape(n_p, D_MODEL)
    xs2 = x_sample.reshape(n_s, D_MODEL)
    u_p, q_p, k_p, v_p = _inproj_call(xp2, mod_p, norm1_w[l], w_in_bf, None, n_p, F32)
    u_s, q_s, k_s, v_s = _inproj_call(xs2, mod_s, norm1_w[l], w_in_bf, _rope_tables(dseq), dseq, BF16)

    bblk, cblk, lam_r, lam_i = _s5_weights(ssm_lambda_re[l], ssm_lambda_im[l], ssm_log_dt[l],
                                           ssm_b_re[l], ssm_b_im[l], ssm_c_re[l], ssm_c_im[l])
    lr_p = jnp.broadcast_to(lam_r[:, :, None, :], (2, N_COLBLK, SUBLANES, S5_BLK_STATES))
    li_p = jnp.broadcast_to(lam_i[:, :, None, :], (2, N_COLBLK, SUBLANES, S5_BLK_STATES))
    g_p = bsz // SUBLANES
    h0_p = jnp.zeros((2, g_p, N_COLBLK * 8, SUBLANES, LANES), F32)
    y_p, hfin = _s5_call(u_p.reshape(bsz, seq, S5_WIDTH), bblk, cblk, lr_p, li_p, h0_p, SUBLANES)

    def halves(x):
        x = x.reshape(2, 2, 2, 1, S5_BLK_STATES)
        x = jnp.broadcast_to(x, (2, 2, 2, dbs, S5_BLK_STATES))
        return x.transpose(0, 2, 1, 3, 4).reshape(2, 2, 2 * dbs, S5_BLK_STATES)

    def h0_tiles(s):
        s = s.astype(F32).reshape(dbs, 2, 2, 2, 4, LANES)
        return s.transpose(1, 3, 4, 2, 0, 5).reshape(2, 1, 2, 4, 2 * dbs, LANES)

    h0_s = jnp.concatenate([h0_tiles(state_ssm_re[:, l]), h0_tiles(state_ssm_im[:, l])], axis=3)
    h0_s = h0_s.reshape(2, 1, 16, SUBLANES, LANES)
    y_s, _ = _s5_call(u_s.reshape(dbs, dseq, S5_WIDTH), bblk, cblk, halves(lam_r), halves(lam_i), h0_s, dbs)

    lq = diff_lambda_q[l].astype(F32)
    lk = diff_lambda_k[l].astype(F32)
    lam = (jnp.exp(jnp.sum(lq[0] * lk[0])) - jnp.exp(jnp.sum(lq[1] * lk[1])) + lambda_init).reshape(1)
    subln = diff_subln_w[l].astype(F32).reshape(1, VALUE_DIM)
    ao_p = _attn_ctx_call(lam, q_p.reshape(bsz, seq, ATTN_WIDTH), k_p.reshape(bsz, seq, ATTN_WIDTH),
                          v_p.reshape(bsz, seq, ATTN_WIDTH), subln, lambda_init)
    past = cache_k.shape[2]
    ao_s = _attn_lat_call(lam, q_s.reshape(dbs, dseq, ATTN_WIDTH), k_s.reshape(dbs, dseq, ATTN_WIDTH),
                          v_s.reshape(dbs, dseq, ATTN_WIDTH),
                          cache_k[:, l].reshape(dbs, past, ATTN_WIDTH).astype(BF16),
                          cache_v[:, l].reshape(dbs, past, ATTN_WIDTH).astype(BF16), subln, lambda_init)

    wglu_bf = ssm_w_glu[l].astype(BF16)
    wout_bf = w_out[l].astype(BF16)
    wr_t = w_router[l].astype(F32).T
    rbias3 = router_bias[l].astype(F32).reshape(N_EXPERT_GROUPS, N_EXPERTS // N_EXPERT_GROUPS, 1)
    x1_p, h2_p, gates_p = _post_call(xp2, u_p, y_p, ao_p.reshape(n_p, ATTN_WIDTH), mod_p, ssm_d[l],
                                     wglu_bf, wout_bf, norm2_w[l], wr_t, rbias3, n_p)
    x1_s, h2_s, gates_s = _post_call(xs2, u_s, y_s, ao_s.reshape(n_s, ATTN_WIDTH), mod_s, ssm_d[l],
                                     wglu_bf, wout_bf, norm2_w[l], wr_t, rbias3, dseq)

    wg = w_exp_gate[l].astype(BF16)
    wu = w_exp_up[l].astype(BF16)
    wd = w_exp_down[l].astype(BF16)
    sg = w_sh_gate[l].astype(BF16)
    su = w_sh_up[l].astype(BF16)
    sd = w_sh_down[l].astype(BF16)
    y_prompt, y_sample = _moe_sparse(h2_p, h2_s, gates_p, gates_s, x1_p, x1_s, mod_p, mod_s,
                                     wg, wu, wd, sg, su, sd, final_norm_w, dseq)

    new_cache_k = k_p.reshape(bsz, 1, seq, N_DIFF_HEADS, VALUE_DIM)
    new_cache_v = v_p.reshape(bsz, 1, seq, N_DIFF_HEADS, VALUE_DIM)
    hf = hfin.reshape(2, g_p, N_COLBLK, 2, 4, SUBLANES, LANES)
    hf = hf.transpose(3, 1, 5, 0, 2, 4, 6).reshape(2, bsz, 1, 2, S5_GROUPS, S5_STATE)
    return (y_prompt.reshape(bsz, seq, D_MODEL), y_sample.reshape(dbs, dseq, D_MODEL),
            new_cache_k, new_cache_v, hf[0], hf[1])
```

```python
import functools
import math

import jax
import jax.numpy as jnp
from jax import lax
from jax.experimental import pallas as pl
from jax.experimental.pallas import tpu as pltpu

F32 = jnp.float32
BF16 = jnp.bfloat16

D_MODEL = 1024
GRID_W = 64
S5_WIDTH = 512
S5_GROUP = 16
S5_GROUPS = 32
S5_STATE = 64
ATTN_WIDTH = 512
DIFF_HEAD_DIM = 64
VALUE_DIM = 128
N_DIFF_HEADS = 4
IN_WIDTH = S5_WIDTH + 3 * ATTN_WIDTH
ROT_PAIRS = DIFF_HEAD_DIM // 4
ROPE_THETA = 10000.0
N_EXPERTS = 64
TOP_K = 8
N_EXPERT_GROUPS = 8
TOPK_GROUPS = 4
EXPERT_FF = 256
ROUTED_SCALE = 2.5
NORM_EPS = 1e-6

LANES = 128
SUBLANES = 8
S5_COLBLK = 8 * S5_GROUP
S5_BLK_STATES = 8 * S5_STATE
N_COLBLK = S5_WIDTH // S5_COLBLK
VMEM_LIMIT = 56 * 1024 * 1024


def _cparams(sem):
    return pltpu.CompilerParams(dimension_semantics=sem, vmem_limit_bytes=VMEM_LIMIT)


def _ada_kernel(cond_ref, w_ref, b_ref, o_ref):
    c = cond_ref[...]
    s = c * jax.nn.sigmoid(c)
    o_ref[...] = jnp.dot(s.astype(BF16), w_ref[...].astype(BF16),
                         preferred_element_type=F32) + b_ref[...]


def _ada_call(cond8, w_ada, b_ada):
    n = w_ada.shape[1]
    tn = 1536
    return pl.pallas_call(
        _ada_kernel,
        grid=(n // tn,),
        in_specs=[pl.BlockSpec((SUBLANES, D_MODEL), lambda j: (0, 0)),
                  pl.BlockSpec((D_MODEL, tn), lambda j: (0, j)),
                  pl.BlockSpec((1, tn), lambda j: (0, j))],
        out_specs=pl.BlockSpec((SUBLANES, tn), lambda j: (0, j)),
        out_shape=jax.ShapeDtypeStruct((SUBLANES, n), F32),
        compiler_params=_cparams(("arbitrary",)),
        name="adaln",
    )(cond8, w_ada, b_ada.reshape(1, n))


def _rope_apply(t, cos, sin):
    parts = []
    for cidx in range(ATTN_WIDTH // LANES):
        xc = t[:, cidx * LANES:(cidx + 1) * LANES]
        up = pltpu.roll(xc, LANES - ROT_PAIRS, 1)
        dn = pltpu.roll(xc, ROT_PAIRS, 1)
        lane = lax.broadcasted_iota(jnp.int32, xc.shape, 1)
        partner = jnp.where((lane % (2 * ROT_PAIRS)) < ROT_PAIRS, up, dn)
        parts.append(xc * cos + partner * sin)
    return jnp.concatenate(parts, axis=1)


def _inproj_kernel(*refs, rope, kv_dtype):
    if rope:
        x_ref, mod_ref, n1_ref, w_ref, cos_ref, sin_ref, u_ref, q_ref, k_ref, v_ref = refs
    else:
        x_ref, mod_ref, n1_ref, w_ref, u_ref, q_ref, k_ref, v_ref = refs
    x = x_ref[...]
    ms = jnp.mean(x * x, axis=-1, keepdims=True)
    xn = x * lax.rsqrt(ms + NORM_EPS) * n1_ref[...]
    mod = mod_ref[0]
    shift = mod[:, 0:D_MODEL]
    scale = mod[:, D_MODEL:2 * D_MODEL]
    h = xn * (1.0 + scale) + shift
    proj = jnp.dot(h.astype(BF16), w_ref[...], preferred_element_type=F32)
    u_ref[...] = proj[:, :S5_WIDTH]
    q = proj[:, S5_WIDTH:S5_WIDTH + ATTN_WIDTH]
    k = proj[:, S5_WIDTH + ATTN_WIDTH:S5_WIDTH + 2 * ATTN_WIDTH]
    v = proj[:, S5_WIDTH + 2 * ATTN_WIDTH:]
    if rope:
        cos = cos_ref[...]
        sin = sin_ref[...]
        q = _rope_apply(q, cos, sin)
        k = _rope_apply(k, cos, sin)
    q_ref[...] = (q * (DIFF_HEAD_DIM ** -0.5)).astype(BF16)
    k_ref[...] = k.astype(kv_dtype)
    v_ref[...] = v.astype(kv_dtype)


def _inproj_call(x2, mod3, norm1_w, w_in_bf, rope_tabs, seq_len, kv_dtype, tm=512):
    n = x2.shape[0]
    bpb = seq_len // tm
    rope = rope_tabs is not None
    in_specs = [pl.BlockSpec((tm, D_MODEL), lambda i: (i, 0)),
                pl.BlockSpec((1, 1, 6 * D_MODEL), lambda i: (i // bpb, 0, 0)),
                pl.BlockSpec((1, D_MODEL), lambda i: (0, 0)),
                pl.BlockSpec((D_MODEL, IN_WIDTH), lambda i: (0, 0))]
    args = [x2, mod3, norm1_w.reshape(1, D_MODEL), w_in_bf]
    if rope:
        in_specs += [pl.BlockSpec((tm, LANES), lambda i: (i % bpb, 0)),
                     pl.BlockSpec((tm, LANES), lambda i: (i % bpb, 0))]
        args += list(rope_tabs)
    ospec = pl.BlockSpec((tm, S5_WIDTH), lambda i: (i, 0))
    return pl.pallas_call(
        functools.partial(_inproj_kernel, rope=rope, kv_dtype=kv_dtype),
        grid=(n // tm,),
        in_specs=in_specs,
        out_specs=[ospec, ospec, ospec, ospec],
        out_shape=[jax.ShapeDtypeStruct((n, S5_WIDTH), F32),
                   jax.ShapeDtypeStruct((n, ATTN_WIDTH), BF16),
                   jax.ShapeDtypeStruct((n, ATTN_WIDTH), kv_dtype),
                   jax.ShapeDtypeStruct((n, ATTN_WIDTH), kv_dtype)],
        compiler_params=_cparams(("parallel",)),
        name="inproj_rope" if rope else "inproj",
    )(*args)


def _s5_slot(b, q, nseq):
    if nseq == SUBLANES:
        return b, q * 8
    half = q // 2
    return half * nseq + b, (q % 2) * 8


def _s5_kernel(u_ref, b_ref, c_ref, lr_ref, li_ref, h0_ref, y_ref, hfin_ref, buh, hs, hst,
               *, nseq, tb, stride, nqq):
    d = pl.program_id(1)
    c = pl.program_id(2)
    nc = pl.num_programs(2)

    @pl.when(c == 0)
    def _():
        hst[...] = h0_ref[0, 0]
        hs[...] = jnp.zeros(hs.shape, F32)

    u = u_ref[...].reshape(nseq * tb, S5_WIDTH).astype(BF16)
    for q in range(N_COLBLK):
        res = jnp.dot(u[:, q * S5_COLBLK:(q + 1) * S5_COLBLK], b_ref[0, q],
                      preferred_element_type=F32)
        for b in range(nseq):
            slot, ct0 = _s5_slot(b, q, nseq)
            for j in range(8):
                buh[ct0 + j, slot * stride:slot * stride + tb, :] = (
                    res[b * tb:(b + 1) * tb, j * LANES:(j + 1) * LANES])

    for qq in range(nqq):
        lr = [lr_ref[0, qq, :, i * LANES:(i + 1) * LANES] for i in range(4)]
        li = [li_ref[0, qq, :, i * LANES:(i + 1) * LANES] for i in range(4)]
        h_init = tuple(hst[qq * 8 + i] for i in range(8))

        def body(s, carry, qq=qq, lr=lr, li=li):
            t = jnp.where(d == 0, s, tb - 1 - s)
            idx = pl.ds(t, SUBLANES, stride=stride)
            new_r, new_i = [], []
            for i in range(4):
                hr, hi = carry[i], carry[4 + i]
                bur = buh[qq * 8 + i, idx, :]
                bui = buh[qq * 8 + 4 + i, idx, :]
                nr = lr[i] * hr - li[i] * hi + bur
                ni = lr[i] * hi + li[i] * hr + bui
                hs[qq * 8 + i, idx, :] = nr
                hs[qq * 8 + 4 + i, idx, :] = ni
                new_r.append(nr)
                new_i.append(ni)
            return tuple(new_r) + tuple(new_i)

        h_fin = lax.fori_loop(0, tb, body, h_init, unroll=4)
        for i in range(8):
            hst[qq * 8 + i] = h_fin[i]

    rows = nseq * stride
    for q in range(N_COLBLK):
        slot0, ct0 = _s5_slot(0, q, nseq)
        hmat = jnp.concatenate(
            [hs[ct0 + j, slot0 * stride:slot0 * stride + rows, :] for j in range(8)], axis=1)
        yq = jnp.dot(hmat.astype(BF16), c_ref[0, q], preferred_element_type=F32)
        for b in range(nseq):
            y_ref[0, b, :, q * S5_COLBLK:(q + 1) * S5_COLBLK] = yq[b * stride:b * stride + tb, :]

    @pl.when(c == nc - 1)
    def _():
        hfin_ref[0, 0] = hst[...]


def _s5_call(u3, bblk, cblk, lr, li, h0, nseq, tb=64):
    nseq_total, seq_len, _ = u3.shape
    g = nseq_total // nseq
    nc = seq_len // tb
    stride = tb + SUBLANES
    nqq = lr.shape[1]
    nct = nqq * 8

    def tmap(gi, d, c):
        return c + d * (nc - 1 - 2 * c)

    return pl.pallas_call(
        functools.partial(_s5_kernel, nseq=nseq, tb=tb, stride=stride, nqq=nqq),
        grid=(g, 2, nc),
        in_specs=[
            pl.BlockSpec((nseq, tb, S5_WIDTH), lambda gi, d, c: (gi, tmap(gi, d, c), 0)),
            pl.BlockSpec((1, N_COLBLK, S5_COLBLK, 2 * S5_BLK_STATES), lambda gi, d, c: (d, 0, 0, 0)),
            pl.BlockSpec((1, N_COLBLK, 2 * S5_BLK_STATES, S5_COLBLK), lambda gi, d, c: (d, 0, 0, 0)),
            pl.BlockSpec((1, nqq, SUBLANES, S5_BLK_STATES), lambda gi, d, c: (d, 0, 0, 0)),
            pl.BlockSpec((1, nqq, SUBLANES, S5_BLK_STATES), lambda gi, d, c: (d, 0, 0, 0)),
            pl.BlockSpec((1, 1, nct, SUBLANES, LANES), lambda gi, d, c: (d, gi, 0, 0, 0)),
        ],
        out_specs=[
            pl.BlockSpec((1, nseq, tb, S5_WIDTH), lambda gi, d, c: (d, gi, tmap(gi, d, c), 0)),
            pl.BlockSpec((1, 1, nct, SUBLANES, LANES), lambda gi, d, c: (d, gi, 0, 0, 0)),
        ],
        out_shape=[jax.ShapeDtypeStruct((2, nseq_total, seq_len, S5_WIDTH), F32),
                   jax.ShapeDtypeStruct((2, g, nct, SUBLANES, LANES), F32)],
        scratch_shapes=[pltpu.VMEM((nct, SUBLANES * stride, LANES), F32),
                        pltpu.VMEM((nct, SUBLANES * stride, LANES), F32),
                        pltpu.VMEM((nct, SUBLANES, LANES), F32)],
        compiler_params=_cparams(("parallel", "parallel", "arbitrary")),
        name="s5_scan_%d" % nseq,
    )(u3, bblk, cblk, lr, li, h0)


def _s5_weights(lam_re, lam_im, log_dt, b_re, b_im, c_re, c_im):
    dt = jnp.exp(log_dt.astype(F32))[..., None]
    lr = lam_re.astype(F32)
    li = lam_im.astype(F32)
    mag = jnp.exp(lr * dt)
    lbr = mag * jnp.cos(li * dt)
    lbi = mag * jnp.sin(li * dt)
    a = lbr - 1.0
    den = lr * lr + li * li
    cr = (a * lr + lbi * li) / den
    ci = (lbi * lr - a * li) / den
    bbr = cr[..., None] * b_re - ci[..., None] * b_im
    bbi = cr[..., None] * b_im + ci[..., None] * b_re
    eye = jnp.eye(8, dtype=F32)

    def pack_b(x):
        x = x.reshape(2, N_COLBLK, 8, S5_STATE, S5_GROUP).transpose(0, 1, 2, 4, 3)
        x = jnp.einsum('dqghn,gk->dqghkn', x, eye)
        return x.reshape(2, N_COLBLK, S5_COLBLK, S5_BLK_STATES)

    def pack_c(x):
        x = x.reshape(2, N_COLBLK, 8, S5_GROUP, S5_STATE).transpose(0, 1, 2, 4, 3)
        x = jnp.einsum('dqgnh,gk->dqgnkh', x, eye)
        return x.reshape(2, N_COLBLK, S5_BLK_STATES, S5_COLBLK)

    bblk = jnp.concatenate([pack_b(bbr), pack_b(bbi)], axis=-1).astype(BF16)
    cblk = jnp.concatenate([pack_c(c_re.astype(F32)), pack_c(-c_im.astype(F32))], axis=-2).astype(BF16)
    lam_r = lbr.reshape(2, N_COLBLK, S5_BLK_STATES)
    lam_i = lbi.reshape(2, N_COLBLK, S5_BLK_STATES)
    return bblk, cblk, lam_r, lam_i


ATTN_KC = 512


def _lane_fold(x, op):
    acc = x[:, 0:LANES]
    for j in range(1, x.shape[1] // LANES):
        acc = op(acc, x[:, j * LANES:(j + 1) * LANES])
    return acc


def _diff_attention(q, chunks, lam, subln, lambda_init, s_ref):
    lane = lax.broadcasted_iota(jnp.int32, q.shape, 1)
    dn = (((1,), (1,)), ((), ()))
    qms = [jnp.where(lane < DIFF_HEAD_DIM, q, jnp.zeros_like(q)),
           jnp.where(lane >= DIFF_HEAD_DIM, q, jnp.zeros_like(q))]
    offs = [0]
    for load_k, _ in chunks:
        offs.append(offs[-1] + load_k().shape[0])

    def score_chunk(m, ci, m128):
        s = lax.dot_general(qms[m], chunks[ci][0](), dn, preferred_element_type=F32)
        s_ref[m, :, offs[ci]:offs[ci + 1]] = s
        f = _lane_fold(s, jnp.maximum)
        return f if m128 is None else jnp.maximum(m128, f)

    def value_chunk(m, ci, mx, l128, o):
        p = jnp.exp(s_ref[m, :, offs[ci]:offs[ci + 1]] - mx)
        f = _lane_fold(p, jnp.add)
        t = jnp.dot(p.astype(BF16), chunks[ci][1](), preferred_element_type=F32)
        return (f if l128 is None else l128 + f), (t if o is None else o + t)

    nchunk = len(chunks)
    m128_0 = m128_1 = l0 = o0 = l1 = o1 = None
    for ci in range(nchunk):
        m128_0 = score_chunk(0, ci, m128_0)
    mx0 = m128_0.max(axis=-1, keepdims=True)
    for ci in range(nchunk):
        m128_1 = score_chunk(1, ci, m128_1)
        l0, o0 = value_chunk(0, ci, mx0, l0, o0)
    mx1 = m128_1.max(axis=-1, keepdims=True)
    for ci in range(nchunk):
        l1, o1 = value_chunk(1, ci, mx1, l1, o1)
    o = (o0 * (1.0 / l0.sum(axis=-1, keepdims=True))
         - lam * (o1 * (1.0 / l1.sum(axis=-1, keepdims=True))))
    ms = jnp.mean(o * o, axis=-1, keepdims=True)
    return o * lax.rsqrt(ms + NORM_EPS) * subln * (1.0 - lambda_init)


def _attn_ctx_kernel(lam_ref, q_ref, k_ref, v_ref, w_ref, o_ref, s_ref, *, lambda_init):
    chunks = [(lambda: k_ref[0].astype(BF16), lambda: v_ref[0].astype(BF16))]
    o = _diff_attention(q_ref[0], chunks, lam_ref[0], w_ref[...], lambda_init, s_ref)
    o_ref[0] = o.astype(BF16)


def _attn_lat_kernel(lam_ref, q_ref, k_ref, v_ref, ck_ref, cv_ref, w_ref, o_ref, s_ref, *, lambda_init):
    chunks = []
    for c in range(k_ref.shape[1] // ATTN_KC):
        chunks.append((lambda c=c: k_ref[0, c * ATTN_KC:(c + 1) * ATTN_KC, :],
                       lambda c=c: v_ref[0, c * ATTN_KC:(c + 1) * ATTN_KC, :]))
    for c in range(ck_ref.shape[1] // ATTN_KC):
        chunks.append((lambda c=c: ck_ref[0, c * ATTN_KC:(c + 1) * ATTN_KC, :],
                       lambda c=c: cv_ref[0, c * ATTN_KC:(c + 1) * ATTN_KC, :]))
    o = _diff_attention(q_ref[0], chunks, lam_ref[0], w_ref[...], lambda_init, s_ref)
    o_ref[0] = o.astype(BF16)


def _attn_ctx_call(lam, q3, k3, v3, subln, lambda_init):
    bsz, seq_len, _ = q3.shape
    spec = pl.BlockSpec((1, seq_len, VALUE_DIM), lambda b, h: (b, 0, h))
    return pl.pallas_call(
        functools.partial(_attn_ctx_kernel, lambda_init=lambda_init),
        grid=(bsz, N_DIFF_HEADS),
        in_specs=[pl.BlockSpec(memory_space=pltpu.SMEM), spec, spec, spec,
                  pl.BlockSpec((1, VALUE_DIM), lambda b, h: (0, 0))],
        out_specs=spec,
        out_shape=jax.ShapeDtypeStruct((bsz, seq_len, ATTN_WIDTH), BF16),
        scratch_shapes=[pltpu.VMEM((2, seq_len, seq_len), F32)],
        compiler_params=_cparams(("parallel", "parallel")),
        name="attn_ctx",
    )(lam, q3, k3, v3, subln)


def _attn_lat_call(lam, q3, k3, v3, ck3, cv3, subln, lambda_init, tq=256):
    bsz, seq_len, _ = q3.shape
    ctx_len = ck3.shape[1]
    qspec = pl.BlockSpec((1, tq, VALUE_DIM), lambda b, h, i: (b, i, h))
    kspec = pl.BlockSpec((1, seq_len, VALUE_DIM), lambda b, h, i: (b, 0, h))
    cspec = pl.BlockSpec((1, ctx_len, VALUE_DIM), lambda b, h, i: (b, 0, h))
    return pl.pallas_call(
        functools.partial(_attn_lat_kernel, lambda_init=lambda_init),
        grid=(bsz, N_DIFF_HEADS, seq_len // tq),
        in_specs=[pl.BlockSpec(memory_space=pltpu.SMEM), qspec, kspec, kspec, cspec, cspec,
                  pl.BlockSpec((1, VALUE_DIM), lambda b, h, i: (0, 0))],
        out_specs=qspec,
        out_shape=jax.ShapeDtypeStruct((bsz, seq_len, ATTN_WIDTH), BF16),
        scratch_shapes=[pltpu.VMEM((2, tq, seq_len + ctx_len), F32)],
        compiler_params=_cparams(("parallel", "parallel", "arbitrary")),
        name="attn_lat",
    )(lam, q3, k3, v3, ck3, cv3, subln)


def _route(logits_t, bias3, tm):
    ng, ge = N_EXPERT_GROUPS, N_EXPERTS // N_EXPERT_GROUPS
    neg = jnp.float32(-jnp.inf)
    sc = jax.nn.sigmoid(logits_t).reshape(ng, ge, tm)
    bi = sc + bias3
    eio = lax.broadcasted_iota(jnp.int32, (ng, ge, tm), 1).astype(F32)
    gio = lax.broadcasted_iota(jnp.int32, (ng, ge, tm), 0).astype(F32)
    m1 = bi.max(axis=1, keepdims=True)
    i1 = jnp.where(bi == m1, eio, float(ge)).min(axis=1, keepdims=True)
    m2 = jnp.where(eio == i1, neg, bi).max(axis=1, keepdims=True)
    gs = jnp.broadcast_to(m1 + m2, (ng, ge, tm))
    cnt = jnp.zeros((ng, ge, tm), F32)
    for g2 in range(ng):
        o = gs[g2:g2 + 1]
        better = (o > gs) | ((o == gs) & (gio > float(g2)))
        cnt = cnt + jnp.where(better, 1.0, 0.0)
    v = jnp.where(cnt < float(TOPK_GROUPS), bi, neg)
    eidx = gio * float(ge) + eio
    selm = jnp.zeros((ng, ge, tm), F32)
    for _ in range(TOP_K):
        m = v.max(axis=0, keepdims=True).max(axis=1, keepdims=True)
        ix = jnp.where(v == m, eidx, float(N_EXPERTS)).min(axis=0, keepdims=True).min(axis=1, keepdims=True)
        oh = eidx == ix
        selm = jnp.where(oh, 1.0, selm)
        v = jnp.where(oh, neg, v)
    selsc = selm * sc
    ssum = selsc.sum(axis=0, keepdims=True).sum(axis=1, keepdims=True)
    return (selsc / ssum * ROUTED_SCALE).reshape(N_EXPERTS, tm)


def _post_kernel(x_ref, u_ref, yf_ref, yb_ref, ao_ref, mod_ref, d_ref, wglu_ref, wout_ref, n2_ref,
                 wrt_ref, rb_ref, x1_ref, h2_ref, gates_ref, *, tm):
    u = u_ref[...]
    y = u * d_ref[...] + yf_ref[0] + yb_ref[0]
    g = jax.nn.gelu(y)
    s5 = g * jax.nn.sigmoid(jnp.dot(g.astype(BF16), wglu_ref[...], preferred_element_type=F32))
    mixed = (jnp.dot(s5.astype(BF16), wout_ref[0:S5_WIDTH, :], preferred_element_type=F32)
             + jnp.dot(ao_ref[...], wout_ref[S5_WIDTH:, :], preferred_element_type=F32))
    mod = mod_ref[0]
    gate1 = mod[:, 2 * D_MODEL:3 * D_MODEL]
    shift2 = mod[:, 3 * D_MODEL:4 * D_MODEL]
    scale2 = mod[:, 4 * D_MODEL:5 * D_MODEL]
    x1 = x_ref[...] + gate1 * mixed
    x1_ref[...] = x1
    ms = jnp.mean(x1 * x1, axis=-1, keepdims=True)
    h2 = x1 * lax.rsqrt(ms + NORM_EPS) * n2_ref[...] * (1.0 + scale2) + shift2
    h2_ref[...] = h2.astype(BF16)
    logits_t = lax.dot_general(wrt_ref[...], h2, (((1,), (1,)), ((), ())),
                               preferred_element_type=F32, precision=lax.Precision.HIGHEST)
    gates_ref[...] = _route(logits_t, rb_ref[...], tm)


def _post_call(x2, u2, y4, ao2, mod3, ssm_d, wglu_bf, wout_bf, norm2_w, wr_t, rbias3, seq_len, tm=512):
    n = x2.shape[0]
    bpb = seq_len // tm
    y3 = y4.reshape(2, n, S5_WIDTH)
    row = lambda i: (i, 0)
    const2 = lambda i: (0, 0)
    return pl.pallas_call(
        functools.partial(_post_kernel, tm=tm),
        grid=(n // tm,),
        in_specs=[pl.BlockSpec((tm, D_MODEL), row),
                  pl.BlockSpec((tm, S5_WIDTH), row),
                  pl.BlockSpec((1, tm, S5_WIDTH), lambda i: (0, i, 0)),
                  pl.BlockSpec((1, tm, S5_WIDTH), lambda i: (1, i, 0)),
                  pl.BlockSpec((tm, ATTN_WIDTH), row),
                  pl.BlockSpec((1, 1, 6 * D_MODEL), lambda i: (i // bpb, 0, 0)),
                  pl.BlockSpec((1, S5_WIDTH), const2),
                  pl.BlockSpec((S5_WIDTH, S5_WIDTH), const2),
                  pl.BlockSpec((D_MODEL, D_MODEL), const2),
                  pl.BlockSpec((1, D_MODEL), const2),
                  pl.BlockSpec((N_EXPERTS, D_MODEL), const2),
                  pl.BlockSpec((N_EXPERT_GROUPS, N_EXPERTS // N_EXPERT_GROUPS, 1), lambda i: (0, 0, 0))],
        out_specs=[pl.BlockSpec((tm, D_MODEL), row),
                   pl.BlockSpec((tm, D_MODEL), row),
                   pl.BlockSpec((N_EXPERTS, tm), lambda i: (0, i))],
        out_shape=[jax.ShapeDtypeStruct((n, D_MODEL), F32),
                   jax.ShapeDtypeStruct((n, D_MODEL), BF16),
                   jax.ShapeDtypeStruct((N_EXPERTS, n), F32)],
        compiler_params=_cparams(("parallel",)),
        name="post_mix_route",
    )(x2, u2, y3, y3, ao2, mod3, ssm_d.reshape(1, S5_WIDTH), wglu_bf, wout_bf,
      norm2_w.reshape(1, D_MODEL), wr_t, rbias3)


def _swiglu_bf(t, wg, wu, wd):
    a = jnp.dot(t, wg, preferred_element_type=F32)
    b = jnp.dot(t, wu, preferred_element_type=F32)
    act = a * jax.nn.sigmoid(a) * b
    return jnp.dot(act.astype(BF16), wd, preferred_element_type=F32)


MOE_SB = 256
MOE_SEG = 16
MOE_TF = 1024
MOE_XW = D_MODEL + 2 * N_EXPERTS
MOE_STATIC_CHUNKS = 3
MOE_CHUNK = 512
MOE_DMA_SIZES = tuple(2 ** k for k in range(MOE_TF.bit_length() - 2, 3, -1))
MOE_RPAD = -(-(TOP_K * MOE_SB + N_EXPERTS * (MOE_SEG - 1) + MOE_STATIC_CHUNKS * MOE_SEG) // MOE_CHUNK) * MOE_CHUNK


MOE_P32 = MOE_RPAD // (2 * MOE_SEG)
MOE_P16 = N_EXPERTS
MOE_PACK = 4096


def _seg_copies(pieces, sb, make_copy, act):
    p32_ref, n32_ref, p16_ref, n16_ref = pieces
    for p_ref, n_ref, width, rows in ((p32_ref, n32_ref, MOE_P32, 2 * MOE_SEG), (p16_ref, n16_ref, MOE_P16, MOE_SEG)):
        def body(k, carry, p_ref=p_ref, width=width, rows=rows):
            v = p_ref[sb * width + k]
            act(make_copy(pl.multiple_of(v & (MOE_PACK - 1), MOE_SEG),
                          pl.multiple_of(v >> (MOE_PACK.bit_length() - 1), MOE_SEG), rows))
            return carry
        lax.fori_loop(0, n_ref[sb], body, 0)


MOE_WAIT_SIZES = tuple(2 ** k for k in range((MOE_RPAD - 1).bit_length() - 1, 3, -1))


def _sub_block_rows(cnt_ref, loc_ref, sb):
    last = sb * N_EXPERTS + N_EXPERTS - 1
    return loc_ref[last] + cnt_ref[last]


def _wait_rows(rows, make_copy):
    for b in MOE_WAIT_SIZES:
        @pl.when((rows & b) != 0)
        def _(b=b):
            make_copy(b).wait()


def _build_onehot(gt, p_ref, cnt_ref, loc_ref, big_ref, sb):
    t = gt.shape[1]
    selm = gt > 0.0
    r = lax.broadcasted_iota(jnp.int32, (t, t), 0)
    c = lax.broadcasted_iota(jnp.int32, (t, t), 1)
    upper = jnp.where(r < c, 1.0, 0.0).astype(BF16)
    rank = jnp.dot(jnp.where(selm, 1.0, 0.0).astype(BF16), upper, preferred_element_type=F32)
    key = jnp.where(selm, rank, -1.0)
    j16 = lax.broadcasted_iota(jnp.int32, (MOE_SEG, t), 0).astype(F32)
    p_ref[...] = jnp.zeros(p_ref.shape, BF16)

    def put(e, cidx):
        first = cidx * MOE_SEG
        firstf = float(first) if isinstance(first, int) else first.astype(F32)
        tile = jnp.where(key[e:e + 1, :] == j16 + firstf, 1.0, 0.0).astype(BF16)
        p_ref[pl.ds(pl.multiple_of(loc_ref[sb * N_EXPERTS + e] + first, MOE_SEG), MOE_SEG), :] = tile

    for e in range(N_EXPERTS):
        for cidx in range(MOE_STATIC_CHUNKS):
            put(e, cidx)

    @pl.when(big_ref[sb] != 0)
    def _():
        for e in range(N_EXPERTS):
            def extra(cidx, carry, e=e):
                put(e, cidx)
                return carry
            lax.fori_loop(MOE_STATIC_CHUNKS, cnt_ref[sb * N_EXPERTS + e] // MOE_SEG, extra, 0)


def _dispatch_kernel(cnt_ref, loc_ref, big_ref, p32_ref, n32_ref, p16_ref, n16_ref, padoff_ref, padn_ref,
                     hp_ref, hs_ref, gt_ref, xe_ref, p_ref, xg_ref, z_ref, sem, *, nsb_p, n_sb):
    s = pl.program_id(0)
    slot = s % 2

    def copies(sb, slot_, act):
        def mk(off, g, b):
            return pltpu.make_async_copy(xg_ref.at[slot_, pl.ds(off, b)], xe_ref.at[pl.ds(g, b)], sem.at[slot_])
        _seg_copies((p32_ref, n32_ref, p16_ref, n16_ref), sb, mk, act)

    def wait_all(sb, slot_):
        _wait_rows(_sub_block_rows(cnt_ref, loc_ref, sb),
                   lambda b: pltpu.make_async_copy(xg_ref.at[slot_, pl.ds(0, b)], xe_ref.at[pl.ds(0, b)],
                                                   sem.at[slot_]))

    @pl.when(s >= 2)
    def _():
        wait_all(s - 2, slot)

    gt = gt_ref[...]
    _build_onehot(gt, p_ref, cnt_ref, loc_ref, big_ref, s)
    x = jnp.where(s < nsb_p, hp_ref[...], hs_ref[...])
    gtt = gt.T
    ghi = gtt.astype(BF16)
    glo = (gtt - ghi.astype(F32)).astype(BF16)
    xext = jnp.concatenate([x, ghi, glo], axis=1)
    rows = _sub_block_rows(cnt_ref, loc_ref, s)
    for i in range(MOE_RPAD // MOE_CHUNK):
        @pl.when(i * MOE_CHUNK < rows)
        def _(i=i):
            xg_ref[slot, i * MOE_CHUNK:(i + 1) * MOE_CHUNK, :] = jnp.dot(
                p_ref[i * MOE_CHUNK:(i + 1) * MOE_CHUNK, :], xext, preferred_element_type=F32).astype(BF16)
    copies(s, slot, lambda cp: cp.start())

    @pl.when(s == n_sb - 1)
    def _():
        if n_sb >= 2:
            wait_all(s - 1, 1 - slot)
        wait_all(s, slot)
        z_ref[...] = jnp.zeros(z_ref.shape, BF16)

        def pads(act):
            def body(e, carry):
                n = padn_ref[e]
                off = padoff_ref[e]
                for b in MOE_DMA_SIZES:
                    @pl.when((n & b) != 0)
                    def _(b=b):
                        done = n & (-2 * b)
                        act(pltpu.make_async_copy(
                            z_ref.at[pl.ds(0, b)],
                            xe_ref.at[pl.ds(pl.multiple_of(off + done, MOE_SEG), b)], sem.at[0]))
                return carry
            lax.fori_loop(0, N_EXPERTS, body, 0)
        pads(lambda cp: cp.start())
        pads(lambda cp: cp.wait())


def _ffn_kernel(te_ref, tidx_ref, tvalid_ref, xe_ref, wg_ref, wu_ref, wd_ref, ye_ref):
    i = pl.program_id(0)

    @pl.when(tvalid_ref[i] != 0)
    def _():
        e = te_ref[i]
        g = xe_ref[:, D_MODEL:].astype(F32)
        lane = lax.broadcasted_iota(jnp.int32, g.shape, 1)
        w = jnp.sum(jnp.where((lane == e) | (lane == e + N_EXPERTS), g, 0.0), axis=1, keepdims=True)
        y = _swiglu_bf(xe_ref[:, :D_MODEL], wg_ref[0], wu_ref[0], wd_ref[0])
        ye_ref[...] = (w * y).astype(BF16)


def _combine_kernel(cnt_ref, loc_ref, big_ref, p32_ref, n32_ref, p16_ref, n16_ref, ye_ref, gt_ref, h_ref, x1_ref, mod_ref,
                    sg_ref, su_ref, sd_ref, fn_ref, o_ref, p_ref, ys_ref, acc_ref, sem, *, sb0, n_steps):
    i = pl.program_id(0)
    slot = i % 2
    sb = sb0 + i

    def copies(sb_, slot_, act):
        def mk(off, g, b):
            return pltpu.make_async_copy(ye_ref.at[pl.ds(g, b)], ys_ref.at[slot_, pl.ds(off, b)], sem.at[slot_])
        _seg_copies((p32_ref, n32_ref, p16_ref, n16_ref), sb_, mk, act)

    @pl.when(i == 0)
    def _():
        ys_ref[...] = jnp.zeros(ys_ref.shape, BF16)
        copies(sb, slot, lambda cp: cp.start())

    @pl.when(i + 1 < n_steps)
    def _():
        copies(sb + 1, 1 - slot, lambda cp: cp.start())

    acc_ref[...] = _swiglu_bf(h_ref[...], sg_ref[...], su_ref[...], sd_ref[...])
    _build_onehot(gt_ref[...], p_ref, cnt_ref, loc_ref, big_ref, sb)
    rows = _sub_block_rows(cnt_ref, loc_ref, sb)
    _wait_rows(rows, lambda b: pltpu.make_async_copy(ye_ref.at[pl.ds(0, b)], ys_ref.at[slot, pl.ds(0, b)],
                                                     sem.at[slot]))
    for c in range(MOE_RPAD // MOE_CHUNK):
        @pl.when(c * MOE_CHUNK < rows)
        def _(c=c):
            acc_ref[...] += lax.dot_general(
                p_ref[c * MOE_CHUNK:(c + 1) * MOE_CHUNK, :], ys_ref[slot, c * MOE_CHUNK:(c + 1) * MOE_CHUNK, :],
                (((0,), (0,)), ((), ())), preferred_element_type=F32)
    gate2 = mod_ref[0][:, 5 * D_MODEL:6 * D_MODEL]
    x2 = x1_ref[...] + gate2 * acc_ref[...]
    ms = jnp.mean(x2 * x2, axis=-1, keepdims=True)
    o_ref[...] = x2 * lax.rsqrt(ms + NORM_EPS) * fn_ref[...]


def _moe_plan(gates_t):
    ne, n = gates_t.shape
    n_sb = n // MOE_SB
    cnt = jnp.sum((gates_t > 0.0).reshape(ne, n_sb, MOE_SB), axis=-1, dtype=jnp.int32).T
    cnt16 = (cnt + MOE_SEG - 1) // MOE_SEG * MOE_SEG
    loc = jnp.cumsum(cnt16, axis=1) - cnt16
    tot = jnp.sum(cnt16, axis=0)
    totp = (tot + MOE_TF - 1) // MOE_TF * MOE_TF
    ends = jnp.cumsum(totp)
    base = ends - totp
    goff = base[None, :] + jnp.cumsum(cnt16, axis=0) - cnt16
    rows_max = TOP_K * n + n_sb * ne * (MOE_SEG - 1) + ne * (MOE_TF - MOE_SEG)
    nt_max = -(-rows_max // MOE_TF)
    tiles = jnp.arange(nt_max, dtype=jnp.int32)
    used = ends[-1]
    valid = tiles * MOE_TF < used
    tidx = jnp.where(valid, tiles, jnp.maximum(used // MOE_TF - 1, 0))
    te = jnp.sum(ends[None, :] <= (tidx * MOE_TF)[:, None], axis=1, dtype=jnp.int32)
    te = jnp.minimum(te, ne - 1)

    def piece_list(npieces, first_row, width, rows):
        cum = jnp.cumsum(npieces, axis=1)
        k = jnp.arange(width, dtype=jnp.int32)
        ek = jnp.minimum(jnp.sum(cum[:, None, :] <= k[None, :, None], axis=2, dtype=jnp.int32), ne - 1)
        onehot = ek[:, :, None] == jnp.arange(ne, dtype=jnp.int32)[None, None, :]
        pick = lambda a: jnp.sum(jnp.where(onehot, a[:, None, :], 0), axis=2, dtype=jnp.int32)
        row = pick(first_row) + rows * (k[None, :] - pick(cum - npieces))
        packed = (pick(goff) + row) * MOE_PACK + pick(loc) + row
        return packed.reshape(-1).astype(jnp.int32), cum[:, -1].astype(jnp.int32)

    n32 = cnt16 // (2 * MOE_SEG)
    p32, t32 = piece_list(n32, jnp.zeros_like(cnt16), MOE_P32, 2 * MOE_SEG)
    p16, t16 = piece_list(cnt16 // MOE_SEG % 2, n32 * (2 * MOE_SEG), MOE_P16, MOE_SEG)
    big = (jnp.max(cnt16, axis=1) > MOE_STATIC_CHUNKS * MOE_SEG).astype(jnp.int32)
    return dict(cnt=cnt16.reshape(-1), loc=loc.reshape(-1).astype(jnp.int32), big=big, pieces=(p32, t32, p16, t16),
                padoff=(base + tot).astype(jnp.int32), padn=(totp - tot).astype(jnp.int32),
                te=te, tidx=tidx.astype(jnp.int32), tvalid=valid.astype(jnp.int32), nt_max=nt_max)


def _dispatch_call(plan, h2_p, h2_s, gates_t):
    nsb_p = h2_p.shape[0] // MOE_SB
    n_sb = gates_t.shape[1] // MOE_SB
    grid_spec = pltpu.PrefetchScalarGridSpec(
        num_scalar_prefetch=9, grid=(n_sb,),
        in_specs=[pl.BlockSpec((MOE_SB, D_MODEL), lambda s, *_: (jnp.minimum(s, nsb_p - 1), 0)),
                  pl.BlockSpec((MOE_SB, D_MODEL), lambda s, *_: (jnp.maximum(s - nsb_p, 0), 0)),
                  pl.BlockSpec((N_EXPERTS, MOE_SB), lambda s, *_: (0, s))],
        out_specs=pl.BlockSpec(memory_space=pl.ANY),
        scratch_shapes=[pltpu.VMEM((MOE_RPAD, MOE_SB), BF16),
                        pltpu.VMEM((2, MOE_RPAD, MOE_XW), BF16),
                        pltpu.VMEM((MOE_DMA_SIZES[0], MOE_XW), BF16),
                        pltpu.SemaphoreType.DMA((2,))])
    return pl.pallas_call(
        functools.partial(_dispatch_kernel, nsb_p=nsb_p, n_sb=n_sb),
        grid_spec=grid_spec,
        out_shape=jax.ShapeDtypeStruct((plan['nt_max'] * MOE_TF, MOE_XW), BF16),
        compiler_params=_cparams(("arbitrary",)),
        name="moe_dispatch",
    )(plan['cnt'], plan['loc'], plan['big'], *plan['pieces'], plan['padoff'], plan['padn'], h2_p, h2_s, gates_t)


def _ffn_call(plan, xe, wg, wu, wd):
    grid_spec = pltpu.PrefetchScalarGridSpec(
        num_scalar_prefetch=3, grid=(plan['nt_max'],),
        in_specs=[pl.BlockSpec((MOE_TF, MOE_XW), lambda i, te, tidx, tv: (tidx[i], 0)),
                  pl.BlockSpec((1, D_MODEL, EXPERT_FF), lambda i, te, tidx, tv: (te[i], 0, 0)),
                  pl.BlockSpec((1, D_MODEL, EXPERT_FF), lambda i, te, tidx, tv: (te[i], 0, 0)),
                  pl.BlockSpec((1, EXPERT_FF, D_MODEL), lambda i, te, tidx, tv: (te[i], 0, 0))],
        out_specs=pl.BlockSpec((MOE_TF, D_MODEL), lambda i, te, tidx, tv: (tidx[i], 0)))
    return pl.pallas_call(
        _ffn_kernel,
        grid_spec=grid_spec,
        out_shape=jax.ShapeDtypeStruct((xe.shape[0], D_MODEL), BF16),
        compiler_params=_cparams(("arbitrary",)),
        name="moe_ffn",
    )(plan['te'], plan['tidx'], plan['tvalid'], xe, wg, wu, wd)


def _combine_call(plan, ye, gates_t, h2, x1, mod3, sg, su, sd, final_w, sb0, seq_len):
    n = h2.shape[0]
    n_steps = n // MOE_SB
    bpb = seq_len // MOE_SB
    row = lambda i, *_: (i, 0)
    const2 = lambda i, *_: (0, 0)
    grid_spec = pltpu.PrefetchScalarGridSpec(
        num_scalar_prefetch=7, grid=(n_steps,),
        in_specs=[pl.BlockSpec(memory_space=pl.ANY),
                  pl.BlockSpec((N_EXPERTS, MOE_SB), lambda i, *_: (0, sb0 + i)),
                  pl.BlockSpec((MOE_SB, D_MODEL), row),
                  pl.BlockSpec((MOE_SB, D_MODEL), row),
                  pl.BlockSpec((1, 1, 6 * D_MODEL), lambda i, *_: (i // bpb, 0, 0)),
                  pl.BlockSpec((D_MODEL, EXPERT_FF), const2),
                  pl.BlockSpec((D_MODEL, EXPERT_FF), const2),
                  pl.BlockSpec((EXPERT_FF, D_MODEL), const2),
                  pl.BlockSpec((1, D_MODEL), const2)],
        out_specs=pl.BlockSpec((MOE_SB, D_MODEL), row),
        scratch_shapes=[pltpu.VMEM((MOE_RPAD, MOE_SB), BF16),
                        pltpu.VMEM((2, MOE_RPAD, D_MODEL), BF16),
                        pltpu.VMEM((MOE_SB, D_MODEL), F32),
                        pltpu.SemaphoreType.DMA((2,))])
    return pl.pallas_call(
        functools.partial(_combine_kernel, sb0=sb0, n_steps=n_steps),
        grid_spec=grid_spec,
        out_shape=jax.ShapeDtypeStruct((n, D_MODEL), F32),
        compiler_params=_cparams(("arbitrary",)),
        name="moe_combine",
    )(plan['cnt'], plan['loc'], plan['big'], *plan['pieces'], ye, gates_t, h2, x1, mod3, sg, su, sd,
      final_w.reshape(1, D_MODEL))


def _moe_sparse(h2_p, h2_s, gt_p, gt_s, x1_p, x1_s, mod_p, mod_s, wg, wu, wd, sg, su, sd, final_w, dseq):
    gates_t = jnp.concatenate([gt_p, gt_s], axis=1)
    plan = _moe_plan(gates_t)
    xe = _dispatch_call(plan, h2_p, h2_s, gates_t)
    ye = _ffn_call(plan, xe, wg, wu, wd)
    n_p = h2_p.shape[0]
    y_p = _combine_call(plan, ye, gates_t, h2_p, x1_p, mod_p, sg, su, sd, final_w, 0, n_p)
    y_s = _combine_call(plan, ye, gates_t, h2_s, x1_s, mod_s, sg, su, sd, final_w, n_p // MOE_SB, dseq)
    return y_p, y_s


def _rope_tables(n_tokens):
    rows = n_tokens // GRID_W
    row = jnp.repeat(jnp.arange(rows, dtype=F32), GRID_W)
    col = jnp.tile(jnp.arange(GRID_W, dtype=F32), rows)
    freqs = ROPE_THETA ** (-jnp.arange(ROT_PAIRS, dtype=F32) / ROT_PAIRS)
    ar = row[:, None] * freqs
    ac = col[:, None] * freqs
    cos = jnp.concatenate([jnp.cos(ar), jnp.cos(ar), jnp.cos(ac), jnp.cos(ac)], axis=1)
    sin = jnp.concatenate([-jnp.sin(ar), jnp.sin(ar), -jnp.sin(ac), jnp.sin(ac)], axis=1)
    return jnp.tile(cos, (1, 2)), jnp.tile(sin, (1, 2))


def kernel(x_prompt, x_sample, c, cache_k, cache_v, state_ssm_re, state_ssm_im, c_ctx, w_ada, b_ada, norm1_w, w_in, ssm_lambda_re, ssm_lambda_im, ssm_log_dt, ssm_b_re, ssm_b_im, ssm_c_re, ssm_c_im, ssm_d, ssm_w_glu, diff_lambda_q, diff_lambda_k, diff_subln_w, w_out, norm2_w, w_router, router_bias, w_exp_gate, w_exp_up, w_exp_down, w_sh_gate, w_sh_up, w_sh_down, final_norm_w):
    depth = w_ada.shape[0]
    assert depth == 1
    l = 0
    lambda_init = 0.8 - 0.6 * math.exp(-0.3 * l)
    bsz, seq, _ = x_prompt.shape
    dbs, dseq, _ = x_sample.shape
    n_p, n_s = bsz * seq, dbs * dseq

    cond8 = jnp.zeros((SUBLANES, D_MODEL), F32).at[:dbs].set(c).at[dbs].set(c_ctx)
    mod = _ada_call(cond8, w_ada[l], b_ada[l])
    mod_s = mod[:dbs].reshape(dbs, 1, 6 * D_MODEL)
    mod_p = mod[dbs:dbs + 1].reshape(1, 1, 6 * D_MODEL)

    w_in_bf = w_in[l].astype(BF16)
    xp2 = x_prompt.reshape(n_p, D_MODEL)
    xs2 = x_sample.reshape(n_s, D_MODEL)
    u_p, q_p, k_p, v_p = _inproj_call(xp2, mod_p, norm1_w[l], w_in_bf, None, n_p, F32)
    u_s, q_s, k_s, v_s = _inproj_call(xs2, mod_s, norm1_w[l], w_in_bf, _rope_tables(dseq), dseq, BF16)

    bblk, cblk, lam_r, lam_i = _s5_weights(ssm_lambda_re[l], ssm_lambda_im[l], ssm_log_dt[l],
                                           ssm_b_re[l], ssm_b_im[l], ssm_c_re[l], ssm_c_im[l])
    lr_p = jnp.broadcast_to(lam_r[:, :, None, :], (2, N_COLBLK, SUBLANES, S5_BLK_STATES))
    li_p = jnp.broadcast_to(lam_i[:, :, None, :], (2, N_COLBLK, SUBLANES, S5_BLK_STATES))
    g_p = bsz // SUBLANES
    h0_p = jnp.zeros((2, g_p, N_COLBLK * 8, SUBLANES, LANES), F32)
    y_p, hfin = _s5_call(u_p.reshape(bsz, seq, S5_WIDTH), bblk, cblk, lr_p, li_p, h0_p, SUBLANES)

    def halves(x):
        x = x.reshape(2, 2, 2, 1, S5_BLK_STATES)
        x = jnp.broadcast_to(x, (2, 2, 2, dbs, S5_BLK_STATES))
        return x.transpose(0, 2, 1, 3, 4).reshape(2, 2, 2 * dbs, S5_BLK_STATES)

    def h0_tiles(s):
        s = s.astype(F32).reshape(dbs, 2, 2, 2, 4, LANES)
        return s.transpose(1, 3, 4, 2, 0, 5).reshape(2, 1, 2, 4, 2 * dbs, LANES)

    h0_s = jnp.concatenate([h0_tiles(state_ssm_re[:, l]), h0_tiles(state_ssm_im[:, l])], axis=3)
    h0_s = h0_s.reshape(2, 1, 16, SUBLANES, LANES)
    y_s, _ = _s5_call(u_s.reshape(dbs, dseq, S5_WIDTH), bblk, cblk, halves(lam_r), halves(lam_i), h0_s, dbs)

    lq = diff_lambda_q[l].astype(F32)
    lk = diff_lambda_k[l].astype(F32)
    lam = (jnp.exp(jnp.sum(lq[0] * lk[0])) - jnp.exp(jnp.sum(lq[1] * lk[1])) + lambda_init).reshape(1)
    subln = diff_subln_w[l].astype(F32).reshape(1, VALUE_DIM)
    ao_p = _attn_ctx_call(lam, q_p.reshape(bsz, seq, ATTN_WIDTH), k_p.reshape(bsz, seq, ATTN_WIDTH),
                          v_p.reshape(bsz, seq, ATTN_WIDTH), subln, lambda_init)
    past = cache_k.shape[2]
    ao_s = _attn_lat_call(lam, q_s.reshape(dbs, dseq, ATTN_WIDTH), k_s.reshape(dbs, dseq, ATTN_WIDTH),
                          v_s.reshape(dbs, dseq, ATTN_WIDTH),
                          cache_k[:, l].reshape(dbs, past, ATTN_WIDTH).astype(BF16),
                          cache_v[:, l].reshape(dbs, past, ATTN_WIDTH).astype(BF16), subln, lambda_init)

    wglu_bf = ssm_w_glu[l].astype(BF16)
    wout_bf = w_out[l].astype(BF16)
    wr_t = w_router[l].astype(F32).T
    rbias3 = router_bias[l].astype(F32).reshape(N_EXPERT_GROUPS, N_EXPERTS // N_EXPERT_GROUPS, 1)
    x1_p, h2_p, gates_p = _post_call(xp2, u_p, y_p, ao_p.reshape(n_p, ATTN_WIDTH), mod_p, ssm_d[l],
                                     wglu_bf, wout_bf, norm2_w[l], wr_t, rbias3, n_p)
    x1_s, h2_s, gates_s = _post_call(xs2, u_s, y_s, ao_s.reshape(n_s, ATTN_WIDTH), mod_s, ssm_d[l],
                                     wglu_bf, wout_bf, norm2_w[l], wr_t, rbias3, dseq)

    wg = w_exp_gate[l].astype(BF16)
    wu = w_exp_up[l].astype(BF16)
    wd = w_exp_down[l].astype(BF16)
    sg = w_sh_gate[l].astype(BF16)
    su = w_sh_up[l].astype(BF16)
    sd = w_sh_down[l].astype(BF16)
    y_prompt, y_sample = _moe_sparse(h2_p, h2_s, gates_p, gates_s, x1_p, x1_s, mod_p, mod_s,
                                     wg, wu, wd, sg, su, sd, final_norm_w, dseq)

    new_cache_k = k_p.reshape(bsz, 1, seq, N_DIFF_HEADS, VALUE_DIM)
    new_cache_v = v_p.reshape(bsz, 1, seq, N_DIFF_HEADS, VALUE_DIM)
    hf = hfin.reshape(2, g_p, N_COLBLK, 2, 4, SUBLANES, LANES)
    hf = hf.transpose(3, 1, 5, 0, 2, 4, 6).reshape(2, bsz, 1, 2, S5_GROUPS, S5_STATE)
    return (y_prompt.reshape(bsz, seq, D_MODEL), y_sample.reshape(dbs, dseq, D_MODEL),
            new_cache_k, new_cache_v, hf[0], hf[1])
```

```python
import functools
import math

import jax
import jax.numpy as jnp
from jax import lax
from jax.experimental import pallas as pl
from jax.experimental.pallas import tpu as pltpu

F32 = jnp.float32
BF16 = jnp.bfloat16

D_MODEL = 1024
GRID_W = 64
S5_WIDTH = 512
S5_GROUP = 16
S5_GROUPS = 32
S5_STATE = 64
ATTN_WIDTH = 512
DIFF_HEAD_DIM = 64
VALUE_DIM = 128
N_DIFF_HEADS = 4
IN_WIDTH = S5_WIDTH + 3 * ATTN_WIDTH
ROT_PAIRS = DIFF_HEAD_DIM // 4
ROPE_THETA = 10000.0
N_EXPERTS = 64
TOP_K = 8
N_EXPERT_GROUPS = 8
TOPK_GROUPS = 4
EXPERT_FF = 256
ROUTED_SCALE = 2.5
NORM_EPS = 1e-6

LANES = 128
SUBLANES = 8
S5_COLBLK = 8 * S5_GROUP
S5_BLK_STATES = 8 * S5_STATE
N_COLBLK = S5_WIDTH // S5_COLBLK
VMEM_LIMIT = 56 * 1024 * 1024


def _cparams(sem):
    return pltpu.CompilerParams(dimension_semantics=sem, vmem_limit_bytes=VMEM_LIMIT)


def _ada_kernel(cond_ref, w_ref, b_ref, o_ref):
    c = cond_ref[...]
    s = c * jax.nn.sigmoid(c)
    o_ref[...] = jnp.dot(s.astype(BF16), w_ref[...].astype(BF16),
                         preferred_element_type=F32) + b_ref[...]


def _ada_call(cond8, w_ada, b_ada):
    n = w_ada.shape[1]
    tn = 1536
    return pl.pallas_call(
        _ada_kernel,
        grid=(n // tn,),
        in_specs=[pl.BlockSpec((SUBLANES, D_MODEL), lambda j: (0, 0)),
                  pl.BlockSpec((D_MODEL, tn), lambda j: (0, j)),
                  pl.BlockSpec((1, tn), lambda j: (0, j))],
        out_specs=pl.BlockSpec((SUBLANES, tn), lambda j: (0, j)),
        out_shape=jax.ShapeDtypeStruct((SUBLANES, n), F32),
        compiler_params=_cparams(("arbitrary",)),
        name="adaln",
    )(cond8, w_ada, b_ada.reshape(1, n))


def _rope_apply(t, cos, sin):
    parts = []
    for cidx in range(ATTN_WIDTH // LANES):
        xc = t[:, cidx * LANES:(cidx + 1) * LANES]
        up = pltpu.roll(xc, LANES - ROT_PAIRS, 1)
        dn = pltpu.roll(xc, ROT_PAIRS, 1)
        lane = lax.broadcasted_iota(jnp.int32, xc.shape, 1)
        partner = jnp.where((lane % (2 * ROT_PAIRS)) < ROT_PAIRS, up, dn)
        parts.append(xc * cos + partner * sin)
    return jnp.concatenate(parts, axis=1)


def _inproj_kernel(*refs, rope, kv_dtype):
    if rope:
        x_ref, mod_ref, n1_ref, w_ref, cos_ref, sin_ref, u_ref, q_ref, k_ref, v_ref = refs
    else:
        x_ref, mod_ref, n1_ref, w_ref, u_ref, q_ref, k_ref, v_ref = refs
    x = x_ref[...]
    ms = jnp.mean(x * x, axis=-1, keepdims=True)
    xn = x * lax.rsqrt(ms + NORM_EPS) * n1_ref[...]
    mod = mod_ref[0]
    shift = mod[:, 0:D_MODEL]
    scale = mod[:, D_MODEL:2 * D_MODEL]
    h = xn * (1.0 + scale) + shift
    proj = jnp.dot(h.astype(BF16), w_ref[...], preferred_element_type=F32)
    u_ref[...] = proj[:, :S5_WIDTH]
    q = proj[:, S5_WIDTH:S5_WIDTH + ATTN_WIDTH]
    k = proj[:, S5_WIDTH + ATTN_WIDTH:S5_WIDTH + 2 * ATTN_WIDTH]
    v = proj[:, S5_WIDTH + 2 * ATTN_WIDTH:]
    if rope:
        cos = cos_ref[...]
        sin = sin_ref[...]
        q = _rope_apply(q, cos, sin)
        k = _rope_apply(k, cos, sin)
    q_ref[...] = (q * (DIFF_HEAD_DIM ** -0.5)).astype(BF16)
    k_ref[...] = k.astype(kv_dtype)
    v_ref[...] = v.astype(kv_dtype)


def _inproj_call(x2, mod3, norm1_w, w_in_bf, rope_tabs, seq_len, kv_dtype, tm=512):
    n = x2.shape[0]
    bpb = seq_len // tm
    rope = rope_tabs is not None
    in_specs = [pl.BlockSpec((tm, D_MODEL), lambda i: (i, 0)),
                pl.BlockSpec((1, 1, 6 * D_MODEL), lambda i: (i // bpb, 0, 0)),
                pl.BlockSpec((1, D_MODEL), lambda i: (0, 0)),
                pl.BlockSpec((D_MODEL, IN_WIDTH), lambda i: (0, 0))]
    args = [x2, mod3, norm1_w.reshape(1, D_MODEL), w_in_bf]
    if rope:
        in_specs += [pl.BlockSpec((tm, LANES), lambda i: (i % bpb, 0)),
                     pl.BlockSpec((tm, LANES), lambda i: (i % bpb, 0))]
        args += list(rope_tabs)
    ospec = pl.BlockSpec((tm, S5_WIDTH), lambda i: (i, 0))
    return pl.pallas_call(
        functools.partial(_inproj_kernel, rope=rope, kv_dtype=kv_dtype),
        grid=(n // tm,),
        in_specs=in_specs,
        out_specs=[ospec, ospec, ospec, ospec],
        out_shape=[jax.ShapeDtypeStruct((n, S5_WIDTH), F32),
                   jax.ShapeDtypeStruct((n, ATTN_WIDTH), BF16),
                   jax.ShapeDtypeStruct((n, ATTN_WIDTH), kv_dtype),
                   jax.ShapeDtypeStruct((n, ATTN_WIDTH), kv_dtype)],
        compiler_params=_cparams(("parallel",)),
        name="inproj_rope" if rope else "inproj",
    )(*args)


def _s5_slot(b, q, nseq):
    if nseq == SUBLANES:
        return b, q * 8
    half = q // 2
    return half * nseq + b, (q % 2) * 8


def _s5_kernel(u_ref, b_ref, c_ref, lr_ref, li_ref, h0_ref, y_ref, hfin_ref, buh, hs, hst,
               *, nseq, tb, stride, nqq):
    d = pl.program_id(1)
    c = pl.program_id(2)
    nc = pl.num_programs(2)

    @pl.when(c == 0)
    def _():
        hst[...] = h0_ref[0, 0]
        hs[...] = jnp.zeros(hs.shape, F32)

    u = u_ref[...].reshape(nseq * tb, S5_WIDTH).astype(BF16)
    for q in range(N_COLBLK):
        res = jnp.dot(u[:, q * S5_COLBLK:(q + 1) * S5_COLBLK], b_ref[0, q],
                      preferred_element_type=F32)
        for b in range(nseq):
            slot, ct0 = _s5_slot(b, q, nseq)
            for j in range(8):
                buh[ct0 + j, slot * stride:slot * stride + tb, :] = (
                    res[b * tb:(b + 1) * tb, j * LANES:(j + 1) * LANES])

    for qq in range(nqq):
        lr = [lr_ref[0, qq, :, i * LANES:(i + 1) * LANES] for i in range(4)]
        li = [li_ref[0, qq, :, i * LANES:(i + 1) * LANES] for i in range(4)]
        h_init = tuple(hst[qq * 8 + i] for i in range(8))

        def body(s, carry, qq=qq, lr=lr, li=li):
            t = jnp.where(d == 0, s, tb - 1 - s)
            idx = pl.ds(t, SUBLANES, stride=stride)
            new_r, new_i = [], []
            for i in range(4):
                hr, hi = carry[i], carry[4 + i]
                bur = buh[qq * 8 + i, idx, :]
                bui = buh[qq * 8 + 4 + i, idx, :]
                nr = lr[i] * hr - li[i] * hi + bur
                ni = lr[i] * hi + li[i] * hr + bui
                hs[qq * 8 + i, idx, :] = nr
                hs[qq * 8 + 4 + i, idx, :] = ni
                new_r.append(nr)
                new_i.append(ni)
            return tuple(new_r) + tuple(new_i)

        h_fin = lax.fori_loop(0, tb, body, h_init, unroll=4)
        for i in range(8):
            hst[qq * 8 + i] = h_fin[i]

    rows = nseq * stride
    for q in range(N_COLBLK):
        slot0, ct0 = _s5_slot(0, q, nseq)
        hmat = jnp.concatenate(
            [hs[ct0 + j, slot0 * stride:slot0 * stride + rows, :] for j in range(8)], axis=1)
        yq = jnp.dot(hmat.astype(BF16), c_ref[0, q], preferred_element_type=F32)
        for b in range(nseq):
            y_ref[0, b, :, q * S5_COLBLK:(q + 1) * S5_COLBLK] = yq[b * stride:b * stride + tb, :]

    @pl.when(c == nc - 1)
    def _():
        hfin_ref[0, 0] = hst[...]


def _s5_call(u3, bblk, cblk, lr, li, h0, nseq, tb=64):
    nseq_total, seq_len, _ = u3.shape
    g = nseq_total // nseq
    nc = seq_len // tb
    stride = tb + SUBLANES
    nqq = lr.shape[1]
    nct = nqq * 8

    def tmap(gi, d, c):
        return c + d * (nc - 1 - 2 * c)

    return pl.pallas_call(
        functools.partial(_s5_kernel, nseq=nseq, tb=tb, stride=stride, nqq=nqq),
        grid=(g, 2, nc),
        in_specs=[
            pl.BlockSpec((nseq, tb, S5_WIDTH), lambda gi, d, c: (gi, tmap(gi, d, c), 0)),
            pl.BlockSpec((1, N_COLBLK, S5_COLBLK, 2 * S5_BLK_STATES), lambda gi, d, c: (d, 0, 0, 0)),
            pl.BlockSpec((1, N_COLBLK, 2 * S5_BLK_STATES, S5_COLBLK), lambda gi, d, c: (d, 0, 0, 0)),
            pl.BlockSpec((1, nqq, SUBLANES, S5_BLK_STATES), lambda gi, d, c: (d, 0, 0, 0)),
            pl.BlockSpec((1, nqq, SUBLANES, S5_BLK_STATES), lambda gi, d, c: (d, 0, 0, 0)),
            pl.BlockSpec((1, 1, nct, SUBLANES, LANES), lambda gi, d, c: (d, gi, 0, 0, 0)),
        ],
        out_specs=[
            pl.BlockSpec((1, nseq, tb, S5_WIDTH), lambda gi, d, c: (d, gi, tmap(gi, d, c), 0)),
            pl.BlockSpec((1, 1, nct, SUBLANES, LANES), lambda gi, d, c: (d, gi, 0, 0, 0)),
        ],
        out_shape=[jax.ShapeDtypeStruct((2, nseq_total, seq_len, S5_WIDTH), F32),
                   jax.ShapeDtypeStruct((2, g, nct, SUBLANES, LANES), F32)],
        scratch_shapes=[pltpu.VMEM((nct, SUBLANES * stride, LANES), F32),
                        pltpu.VMEM((nct, SUBLANES * stride, LANES), F32),
                        pltpu.VMEM((nct, SUBLANES, LANES), F32)],
        compiler_params=_cparams(("parallel", "parallel", "arbitrary")),
        name="s5_scan_%d" % nseq,
    )(u3, bblk, cblk, lr, li, h0)


def _s5_weights(lam_re, lam_im, log_dt, b_re, b_im, c_re, c_im):
    dt = jnp.exp(log_dt.astype(F32))[..., None]
    lr = lam_re.astype(F32)
    li = lam_im.astype(F32)
    mag = jnp.exp(lr * dt)
    lbr = mag * jnp.cos(li * dt)
    lbi = mag * jnp.sin(li * dt)
    a = lbr - 1.0
    den = lr * lr + li * li
    cr = (a * lr + lbi * li) / den
    ci = (lbi * lr - a * li) / den
    bbr = cr[..., None] * b_re - ci[..., None] * b_im
    bbi = cr[..., None] * b_im + ci[..., None] * b_re
    eye = jnp.eye(8, dtype=F32)

    def pack_b(x):
        x = x.reshape(2, N_COLBLK, 8, S5_STATE, S5_GROUP).transpose(0, 1, 2, 4, 3)
        x = jnp.einsum('dqghn,gk->dqghkn', x, eye)
        return x.reshape(2, N_COLBLK, S5_COLBLK, S5_BLK_STATES)

    def pack_c(x):
        x = x.reshape(2, N_COLBLK, 8, S5_GROUP, S5_STATE).transpose(0, 1, 2, 4, 3)
        x = jnp.einsum('dqgnh,gk->dqgnkh', x, eye)
        return x.reshape(2, N_COLBLK, S5_BLK_STATES, S5_COLBLK)

    bblk = jnp.concatenate([pack_b(bbr), pack_b(bbi)], axis=-1).astype(BF16)
    cblk = jnp.concatenate([pack_c(c_re.astype(F32)), pack_c(-c_im.astype(F32))], axis=-2).astype(BF16)
    lam_r = lbr.reshape(2, N_COLBLK, S5_BLK_STATES)
    lam_i = lbi.reshape(2, N_COLBLK, S5_BLK_STATES)
    return bblk, cblk, lam_r, lam_i


ATTN_KC = 512


def _lane_fold(x, op):
    acc = x[:, 0:LANES]
    for j in range(1, x.shape[1] // LANES):
        acc = op(acc, x[:, j * LANES:(j + 1) * LANES])
    return acc


def _attn_ops(q, chunks, s_ref):
    lane = lax.broadcasted_iota(jnp.int32, q.shape, 1)
    dn = (((1,), (1,)), ((), ()))
    qms = [jnp.where(lane < DIFF_HEAD_DIM, q, jnp.zeros_like(q)),
           jnp.where(lane >= DIFF_HEAD_DIM, q, jnp.zeros_like(q))]
    offs = [0]
    for load_k, _ in chunks:
        offs.append(offs[-1] + load_k().shape[0])

    def score_chunk(m, ci, m128):
        s = lax.dot_general(qms[m], chunks[ci][0](), dn, preferred_element_type=F32)
        s_ref[m, :, offs[ci]:offs[ci + 1]] = s
        f = _lane_fold(s, jnp.maximum)
        return f if m128 is None else jnp.maximum(m128, f)

    def value_chunk(m, ci, mx, l128, o):
        p = jnp.exp(s_ref[m, :, offs[ci]:offs[ci + 1]] - mx)
        f = _lane_fold(p, jnp.add)
        t = jnp.dot(p.astype(BF16), chunks[ci][1](), preferred_element_type=F32)
        return (f if l128 is None else l128 + f), (t if o is None else o + t)

    return score_chunk, value_chunk


def _row_max(m128):
    return m128.max(axis=-1, keepdims=True)


def _normalized(l128, o):
    return o * (1.0 / l128.sum(axis=-1, keepdims=True))


def _sub_ln(o, subln, lambda_init):
    ms = jnp.mean(o * o, axis=-1, keepdims=True)
    return o * lax.rsqrt(ms + NORM_EPS) * subln * (1.0 - lambda_init)


def _attn_ctx_kernel(lam_ref, q_ref, k_ref, v_ref, w_ref, o_ref, s_ref, *, lambda_init):
    chunks = [(lambda: k_ref[0].astype(BF16), lambda: v_ref[0].astype(BF16))]
    score_chunk, value_chunk = _attn_ops(q_ref[0], chunks, s_ref)
    outs = []
    for m in range(2):
        l, o = value_chunk(m, 0, _row_max(score_chunk(m, 0, None)), None, None)
        outs.append(_normalized(l, o))
    o_ref[0] = _sub_ln(outs[0] - lam_ref[0] * outs[1], w_ref[...], lambda_init).astype(BF16)


def _attn_lat_kernel(lam_ref, q_ref, k_ref, v_ref, ck_ref, cv_ref, w_ref, o_ref, s_ref, *, lambda_init):
    chunks = []
    for c in range(k_ref.shape[1] // ATTN_KC):
        chunks.append((lambda c=c: k_ref[0, c * ATTN_KC:(c + 1) * ATTN_KC, :],
                       lambda c=c: v_ref[0, c * ATTN_KC:(c + 1) * ATTN_KC, :]))
    for c in range(ck_ref.shape[1] // ATTN_KC):
        chunks.append((lambda c=c: ck_ref[0, c * ATTN_KC:(c + 1) * ATTN_KC, :],
                       lambda c=c: cv_ref[0, c * ATTN_KC:(c + 1) * ATTN_KC, :]))
    nchunk = len(chunks)
    score_chunk, value_chunk = _attn_ops(q_ref[0], chunks, s_ref)

    m0 = m1 = l0 = o0 = l1 = o1 = None
    for ci in range(nchunk):
        m0 = score_chunk(0, ci, m0)
    mx0 = _row_max(m0)
    for ci in range(nchunk):
        m1 = score_chunk(1, ci, m1)
        l0, o0 = value_chunk(0, ci, mx0, l0, o0)
    mx1 = _row_max(m1)
    for ci in range(nchunk):
        l1, o1 = value_chunk(1, ci, mx1, l1, o1)
    o = _normalized(l0, o0) - lam_ref[0] * _normalized(l1, o1)
    o_ref[0] = _sub_ln(o, w_ref[...], lambda_init).astype(BF16)


def _attn_ctx_call(lam, q3, k3, v3, subln, lambda_init):
    bsz, seq_len, _ = q3.shape
    spec = pl.BlockSpec((1, seq_len, VALUE_DIM), lambda b, h: (b, 0, h))
    return pl.pallas_call(
        functools.partial(_attn_ctx_kernel, lambda_init=lambda_init),
        grid=(bsz, N_DIFF_HEADS),
        in_specs=[pl.BlockSpec(memory_space=pltpu.SMEM), spec, spec, spec,
                  pl.BlockSpec((1, VALUE_DIM), lambda b, h: (0, 0))],
        out_specs=spec,
        out_shape=jax.ShapeDtypeStruct((bsz, seq_len, ATTN_WIDTH), BF16),
        scratch_shapes=[pltpu.VMEM((2, seq_len, seq_len), F32)],
        compiler_params=_cparams(("parallel", "parallel")),
        name="attn_ctx",
    )(lam, q3, k3, v3, subln)


def _attn_lat_call(lam, q3, k3, v3, ck3, cv3, subln, lambda_init, tq=256):
    bsz, seq_len, _ = q3.shape
    ctx_len = ck3.shape[1]
    qspec = pl.BlockSpec((1, tq, VALUE_DIM), lambda b, h, i: (b, i, h))
    kspec = pl.BlockSpec((1, seq_len, VALUE_DIM), lambda b, h, i: (b, 0, h))
    cspec = pl.BlockSpec((1, ctx_len, VALUE_DIM), lambda b, h, i: (b, 0, h))
    return pl.pallas_call(
        functools.partial(_attn_lat_kernel, lambda_init=lambda_init),
        grid=(bsz, N_DIFF_HEADS, seq_len // tq),
        in_specs=[pl.BlockSpec(memory_space=pltpu.SMEM), qspec, kspec, kspec, cspec, cspec,
                  pl.BlockSpec((1, VALUE_DIM), lambda b, h, i: (0, 0))],
        out_specs=qspec,
        out_shape=jax.ShapeDtypeStruct((bsz, seq_len, ATTN_WIDTH), BF16),
        scratch_shapes=[pltpu.VMEM((2, tq, seq_len + ctx_len), F32)],
        compiler_params=_cparams(("parallel", "parallel", "arbitrary")),
        name="attn_lat",
    )(lam, q3, k3, v3, ck3, cv3, subln)


def _route(logits_t, bias3, tm):
    ng, ge = N_EXPERT_GROUPS, N_EXPERTS // N_EXPERT_GROUPS
    neg = jnp.float32(-jnp.inf)
    sc = jax.nn.sigmoid(logits_t).reshape(ng, ge, tm)
    bi = sc + bias3
    eio = lax.broadcasted_iota(jnp.int32, (ng, ge, tm), 1).astype(F32)
    gio = lax.broadcasted_iota(jnp.int32, (ng, ge, tm), 0).astype(F32)
    m1 = bi.max(axis=1, keepdims=True)
    i1 = jnp.where(bi == m1, eio, float(ge)).min(axis=1, keepdims=True)
    m2 = jnp.where(eio == i1, neg, bi).max(axis=1, keepdims=True)
    gs = jnp.broadcast_to(m1 + m2, (ng, ge, tm))
    cnt = jnp.zeros((ng, ge, tm), F32)
    for g2 in range(ng):
        o = gs[g2:g2 + 1]
        better = (o > gs) | ((o == gs) & (gio > float(g2)))
        cnt = cnt + jnp.where(better, 1.0, 0.0)
    v = jnp.where(cnt < float(TOPK_GROUPS), bi, neg)
    eidx = gio * float(ge) + eio
    selm = jnp.zeros((ng, ge, tm), F32)
    for _ in range(TOP_K):
        m = v.max(axis=0, keepdims=True).max(axis=1, keepdims=True)
        ix = jnp.where(v == m, eidx, float(N_EXPERTS)).min(axis=0, keepdims=True).min(axis=1, keepdims=True)
        oh = eidx == ix
        selm = jnp.where(oh, 1.0, selm)
        v = jnp.where(oh, neg, v)
    selsc = selm * sc
    ssum = selsc.sum(axis=0, keepdims=True).sum(axis=1, keepdims=True)
    return (selsc / ssum * ROUTED_SCALE).reshape(N_EXPERTS, tm)


def _post_kernel(x_ref, u_ref, yf_ref, yb_ref, ao_ref, mod_ref, d_ref, wglu_ref, wout_ref, n2_ref,
                 wrt_ref, rb_ref, x1_ref, h2_ref, gates_ref, *, tm):
    u = u_ref[...]
    y = u * d_ref[...] + yf_ref[0] + yb_ref[0]
    g = jax.nn.gelu(y)
    s5 = g * jax.nn.sigmoid(jnp.dot(g.astype(BF16), wglu_ref[...], preferred_element_type=F32))
    mixed = (jnp.dot(s5.astype(BF16), wout_ref[0:S5_WIDTH, :], preferred_element_type=F32)
             + jnp.dot(ao_ref[...], wout_ref[S5_WIDTH:, :], preferred_element_type=F32))
    mod = mod_ref[0]
    gate1 = mod[:, 2 * D_MODEL:3 * D_MODEL]
    shift2 = mod[:, 3 * D_MODEL:4 * D_MODEL]
    scale2 = mod[:, 4 * D_MODEL:5 * D_MODEL]
    x1 = x_ref[...] + gate1 * mixed
    x1_ref[...] = x1
    ms = jnp.mean(x1 * x1, axis=-1, keepdims=True)
    h2 = x1 * lax.rsqrt(ms + NORM_EPS) * n2_ref[...] * (1.0 + scale2) + shift2
    h2_ref[...] = h2.astype(BF16)
    logits_t = lax.dot_general(wrt_ref[...], h2, (((1,), (1,)), ((), ())),
                               preferred_element_type=F32, precision=lax.Precision.HIGHEST)
    gates_ref[...] = _route(logits_t, rb_ref[...], tm)


def _post_call(x2, u2, y4, ao2, mod3, ssm_d, wglu_bf, wout_bf, norm2_w, wr_t, rbias3, seq_len, tm=512):
    n = x2.shape[0]
    bpb = seq_len // tm
    y3 = y4.reshape(2, n, S5_WIDTH)
    row = lambda i: (i, 0)
    const2 = lambda i: (0, 0)
    return pl.pallas_call(
        functools.partial(_post_kernel, tm=tm),
        grid=(n // tm,),
        in_specs=[pl.BlockSpec((tm, D_MODEL), row),
                  pl.BlockSpec((tm, S5_WIDTH), row),
                  pl.BlockSpec((1, tm, S5_WIDTH), lambda i: (0, i, 0)),
                  pl.BlockSpec((1, tm, S5_WIDTH), lambda i: (1, i, 0)),
                  pl.BlockSpec((tm, ATTN_WIDTH), row),
                  pl.BlockSpec((1, 1, 6 * D_MODEL), lambda i: (i // bpb, 0, 0)),
                  pl.BlockSpec((1, S5_WIDTH), const2),
                  pl.BlockSpec((S5_WIDTH, S5_WIDTH), const2),
                  pl.BlockSpec((D_MODEL, D_MODEL), const2),
                  pl.BlockSpec((1, D_MODEL), const2),
                  pl.BlockSpec((N_EXPERTS, D_MODEL), const2),
                  pl.BlockSpec((N_EXPERT_GROUPS, N_EXPERTS // N_EXPERT_GROUPS, 1), lambda i: (0, 0, 0))],
        out_specs=[pl.BlockSpec((tm, D_MODEL), row),
                   pl.BlockSpec((tm, D_MODEL), row),
                   pl.BlockSpec((N_EXPERTS, tm), lambda i: (0, i))],
        out_shape=[jax.ShapeDtypeStruct((n, D_MODEL), F32),
                   jax.ShapeDtypeStruct((n, D_MODEL), BF16),
                   jax.ShapeDtypeStruct((N_EXPERTS, n), F32)],
        compiler_params=_cparams(("parallel",)),
        name="post_mix_route",
    )(x2, u2, y3, y3, ao2, mod3, ssm_d.reshape(1, S5_WIDTH), wglu_bf, wout_bf,
      norm2_w.reshape(1, D_MODEL), wr_t, rbias3)


def _swiglu_bf(t, wg, wu, wd):
    a = jnp.dot(t, wg, preferred_element_type=F32)
    b = jnp.dot(t, wu, preferred_element_type=F32)
    act = a * jax.nn.sigmoid(a) * b
    return jnp.dot(act.astype(BF16), wd, preferred_element_type=F32)


MOE_SB = 256
MOE_SEG = 16
MOE_TF = 1024
MOE_XW = D_MODEL + 2 * N_EXPERTS
MOE_STATIC_CHUNKS = 3
MOE_CHUNK = 512
MOE_DMA_SIZES = tuple(2 ** k for k in range(MOE_TF.bit_length() - 2, 3, -1))
MOE_RPAD = -(-(TOP_K * MOE_SB + N_EXPERTS * (MOE_SEG - 1) + MOE_STATIC_CHUNKS * MOE_SEG) // MOE_CHUNK) * MOE_CHUNK


MOE_P32 = MOE_RPAD // (2 * MOE_SEG)
MOE_P16 = N_EXPERTS
MOE_PACK = 4096


def _seg_copies(pieces, sb, make_copy, act):
    p32_ref, n32_ref, p16_ref, n16_ref = pieces
    for p_ref, n_ref, width, rows in ((p32_ref, n32_ref, MOE_P32, 2 * MOE_SEG), (p16_ref, n16_ref, MOE_P16, MOE_SEG)):
        def body(k, carry, p_ref=p_ref, width=width, rows=rows):
            v = p_ref[sb * width + k]
            act(make_copy(pl.multiple_of(v & (MOE_PACK - 1), MOE_SEG),
                          pl.multiple_of(v >> (MOE_PACK.bit_length() - 1), MOE_SEG), rows))
            return carry
        lax.fori_loop(0, n_ref[sb], body, 0)


MOE_WAIT_SIZES = tuple(2 ** k for k in range((MOE_RPAD - 1).bit_length() - 1, 3, -1))


def _sub_block_rows(cnt_ref, loc_ref, sb):
    last = sb * N_EXPERTS + N_EXPERTS - 1
    return loc_ref[last] + cnt_ref[last]


def _wait_rows(rows, make_copy):
    for b in MOE_WAIT_SIZES:
        @pl.when((rows & b) != 0)
        def _(b=b):
            make_copy(b).wait()


def _build_onehot(gt, p_ref, cnt_ref, loc_ref, big_ref, sb):
    t = gt.shape[1]
    selm = gt > 0.0
    r = lax.broadcasted_iota(jnp.int32, (t, t), 0)
    c = lax.broadcasted_iota(jnp.int32, (t, t), 1)
    upper = jnp.where(r < c, 1.0, 0.0).astype(BF16)
    rank = jnp.dot(jnp.where(selm, 1.0, 0.0).astype(BF16), upper, preferred_element_type=F32)
    key = jnp.where(selm, rank, -1.0)
    j16 = lax.broadcasted_iota(jnp.int32, (MOE_SEG, t), 0).astype(F32)
    p_ref[...] = jnp.zeros(p_ref.shape, BF16)

    def put(row, off, cidx):
        first = cidx * MOE_SEG
        firstf = float(first) if isinstance(first, int) else first.astype(F32)
        tile = jnp.where(row == j16 + firstf, 1.0, 0.0).astype(BF16)
        p_ref[pl.ds(pl.multiple_of(off + first, MOE_SEG), MOE_SEG), :] = tile

    for e in range(N_EXPERTS):
        row = key[e:e + 1, :]
        off = loc_ref[sb * N_EXPERTS + e]
        for cidx in range(MOE_STATIC_CHUNKS):
            put(row, off, cidx)

    @pl.when(big_ref[sb] != 0)
    def _():
        for e in range(N_EXPERTS):
            def extra(cidx, carry, e=e):
                put(key[e:e + 1, :], loc_ref[sb * N_EXPERTS + e], cidx)
                return carry
            lax.fori_loop(MOE_STATIC_CHUNKS, cnt_ref[sb * N_EXPERTS + e] // MOE_SEG, extra, 0)


def _dispatch_kernel(cnt_ref, loc_ref, big_ref, p32_ref, n32_ref, p16_ref, n16_ref, padoff_ref, padn_ref,
                     hp_ref, hs_ref, gt_ref, xe_ref, p_ref, xg_ref, z_ref, sem, *, nsb_p, n_sb):
    s = pl.program_id(0)
    slot = s % 2

    def copies(sb, slot_, act):
        def mk(off, g, b):
            return pltpu.make_async_copy(xg_ref.at[slot_, pl.ds(off, b)], xe_ref.at[pl.ds(g, b)], sem.at[slot_])
        _seg_copies((p32_ref, n32_ref, p16_ref, n16_ref), sb, mk, act)

    def wait_all(sb, slot_):
        _wait_rows(_sub_block_rows(cnt_ref, loc_ref, sb),
                   lambda b: pltpu.make_async_copy(xg_ref.at[slot_, pl.ds(0, b)], xe_ref.at[pl.ds(0, b)],
                                                   sem.at[slot_]))

    @pl.when(s >= 2)
    def _():
        wait_all(s - 2, slot)

    gt = gt_ref[...]
    _build_onehot(gt, p_ref, cnt_ref, loc_ref, big_ref, s)
    x = jnp.where(s < nsb_p, hp_ref[...], hs_ref[...])
    gtt = gt.T
    ghi = gtt.astype(BF16)
    glo = (gtt - ghi.astype(F32)).astype(BF16)
    xext = jnp.concatenate([x, ghi, glo], axis=1)
    rows = _sub_block_rows(cnt_ref, loc_ref, s)
    for i in range(MOE_RPAD // MOE_CHUNK):
        @pl.when(i * MOE_CHUNK < rows)
        def _(i=i):
            xg_ref[slot, i * MOE_CHUNK:(i + 1) * MOE_CHUNK, :] = jnp.dot(
                p_ref[i * MOE_CHUNK:(i + 1) * MOE_CHUNK, :], xext, preferred_element_type=F32).astype(BF16)
    copies(s, slot, lambda cp: cp.start())

    @pl.when(s == n_sb - 1)
    def _():
        if n_sb >= 2:
            wait_all(s - 1, 1 - slot)
        wait_all(s, slot)
        z_ref[...] = jnp.zeros(z_ref.shape, BF16)

        def pads(act):
            def body(e, carry):
                n = padn_ref[e]
                off = padoff_ref[e]
                for b in MOE_DMA_SIZES:
                    @pl.when((n & b) != 0)
                    def _(b=b):
                        done = n & (-2 * b)
                        act(pltpu.make_async_copy(
                            z_ref.at[pl.ds(0, b)],
                            xe_ref.at[pl.ds(pl.multiple_of(off + done, MOE_SEG), b)], sem.at[0]))
                return carry
            lax.fori_loop(0, N_EXPERTS, body, 0)
        pads(lambda cp: cp.start())
        pads(lambda cp: cp.wait())


def _ffn_kernel(te_ref, tidx_ref, tcode_ref, xe_ref, wg_ref, wu_ref, wd_ref, ye_ref, wgb, wub, wdb):
    i = pl.program_id(0)
    code = tcode_ref[i]

    @pl.when(code == 2)
    def _():
        wgb[...] = wg_ref[0].astype(BF16)
        wub[...] = wu_ref[0].astype(BF16)
        wdb[...] = wd_ref[0].astype(BF16)

    @pl.when(code != 0)
    def _():
        e = te_ref[i]
        g = xe_ref[:, D_MODEL:].astype(F32)
        lane = lax.broadcasted_iota(jnp.int32, g.shape, 1)
        w = jnp.sum(jnp.where((lane == e) | (lane == e + N_EXPERTS), g, 0.0), axis=1, keepdims=True)
        y = _swiglu_bf(xe_ref[:, :D_MODEL], wgb[...], wub[...], wdb[...])
        ye_ref[...] = (w * y).astype(BF16)


def _combine_kernel(cnt_ref, loc_ref, big_ref, p32_ref, n32_ref, p16_ref, n16_ref, ye_ref, gt_ref, h_ref, x1_ref, mod_ref,
                    sg_ref, su_ref, sd_ref, fn_ref, o_ref, p_ref, ys_ref, acc_ref, sem, *, sb0, n_steps):
    i = pl.program_id(0)
    slot = i % 2
    sb = sb0 + i

    def copies(sb_, slot_, act):
        def mk(off, g, b):
            return pltpu.make_async_copy(ye_ref.at[pl.ds(g, b)], ys_ref.at[slot_, pl.ds(off, b)], sem.at[slot_])
        _seg_copies((p32_ref, n32_ref, p16_ref, n16_ref), sb_, mk, act)

    @pl.when(i == 0)
    def _():
        ys_ref[...] = jnp.zeros(ys_ref.shape, BF16)
        copies(sb, slot, lambda cp: cp.start())

    @pl.when(i + 1 < n_steps)
    def _():
        copies(sb + 1, 1 - slot, lambda cp: cp.start())

    acc_ref[...] = _swiglu_bf(h_ref[...], sg_ref[...], su_ref[...], sd_ref[...])
    _build_onehot(gt_ref[...], p_ref, cnt_ref, loc_ref, big_ref, sb)
    rows = _sub_block_rows(cnt_ref, loc_ref, sb)
    _wait_rows(rows, lambda b: pltpu.make_async_copy(ye_ref.at[pl.ds(0, b)], ys_ref.at[slot, pl.ds(0, b)],
                                                     sem.at[slot]))
    for c in range(MOE_RPAD // MOE_CHUNK):
        @pl.when(c * MOE_CHUNK < rows)
        def _(c=c):
            acc_ref[...] += lax.dot_general(
                p_ref[c * MOE_CHUNK:(c + 1) * MOE_CHUNK, :], ys_ref[slot, c * MOE_CHUNK:(c + 1) * MOE_CHUNK, :],
                (((0,), (0,)), ((), ())), preferred_element_type=F32)
    gate2 = mod_ref[0][:, 5 * D_MODEL:6 * D_MODEL]
    x2 = x1_ref[...] + gate2 * acc_ref[...]
    ms = jnp.mean(x2 * x2, axis=-1, keepdims=True)
    o_ref[...] = x2 * lax.rsqrt(ms + NORM_EPS) * fn_ref[...]


def _moe_plan(gates_t):
    ne, n = gates_t.shape
    n_sb = n // MOE_SB
    cnt = jnp.sum((gates_t > 0.0).reshape(ne, n_sb, MOE_SB), axis=-1, dtype=jnp.int32).T
    cnt16 = (cnt + MOE_SEG - 1) // MOE_SEG * MOE_SEG
    loc = jnp.cumsum(cnt16, axis=1) - cnt16
    tot = jnp.sum(cnt16, axis=0)
    totp = (tot + MOE_TF - 1) // MOE_TF * MOE_TF
    ends = jnp.cumsum(totp)
    base = ends - totp
    goff = base[None, :] + jnp.cumsum(cnt16, axis=0) - cnt16
    rows_max = TOP_K * n + n_sb * ne * (MOE_SEG - 1) + ne * (MOE_TF - MOE_SEG)
    nt_max = -(-rows_max // MOE_TF)
    tiles = jnp.arange(nt_max, dtype=jnp.int32)
    used = ends[-1]
    valid = tiles * MOE_TF < used
    tidx = jnp.where(valid, tiles, jnp.maximum(used // MOE_TF - 1, 0))
    te = jnp.sum(ends[None, :] <= (tidx * MOE_TF)[:, None], axis=1, dtype=jnp.int32)
    te = jnp.minimum(te, ne - 1)
    first = jnp.concatenate([jnp.ones((1,), jnp.bool_), te[1:] != te[:-1]])
    tcode = jnp.where(valid, 1 + first.astype(jnp.int32), 0)

    def piece_list(npieces, first_row, width, rows):
        cum = jnp.cumsum(npieces, axis=1)
        k = jnp.arange(width, dtype=jnp.int32)
        ek = jnp.minimum(jnp.sum(cum[:, None, :] <= k[None, :, None], axis=2, dtype=jnp.int32), ne - 1)
        onehot = ek[:, :, None] == jnp.arange(ne, dtype=jnp.int32)[None, None, :]
        pick = lambda a: jnp.sum(jnp.where(onehot, a[:, None, :], 0), axis=2, dtype=jnp.int32)
        row = pick(first_row) + rows * (k[None, :] - pick(cum - npieces))
        packed = (pick(goff) + row) * MOE_PACK + pick(loc) + row
        return packed.reshape(-1).astype(jnp.int32), cum[:, -1].astype(jnp.int32)

    n32 = cnt16 // (2 * MOE_SEG)
    p32, t32 = piece_list(n32, jnp.zeros_like(cnt16), MOE_P32, 2 * MOE_SEG)
    p16, t16 = piece_list(cnt16 // MOE_SEG % 2, n32 * (2 * MOE_SEG), MOE_P16, MOE_SEG)
    big = (jnp.max(cnt16, axis=1) > MOE_STATIC_CHUNKS * MOE_SEG).astype(jnp.int32)
    return dict(cnt=cnt16.reshape(-1), loc=loc.reshape(-1).astype(jnp.int32), big=big, pieces=(p32, t32, p16, t16),
                padoff=(base + tot).astype(jnp.int32), padn=(totp - tot).astype(jnp.int32),
                te=te, tidx=tidx.astype(jnp.int32), tcode=tcode, nt_max=nt_max)


def _dispatch_call(plan, h2_p, h2_s, gates_t):
    nsb_p = h2_p.shape[0] // MOE_SB
    n_sb = gates_t.shape[1] // MOE_SB
    grid_spec = pltpu.PrefetchScalarGridSpec(
        num_scalar_prefetch=9, grid=(n_sb,),
        in_specs=[pl.BlockSpec((MOE_SB, D_MODEL), lambda s, *_: (jnp.minimum(s, nsb_p - 1), 0)),
                  pl.BlockSpec((MOE_SB, D_MODEL), lambda s, *_: (jnp.maximum(s - nsb_p, 0), 0)),
                  pl.BlockSpec((N_EXPERTS, MOE_SB), lambda s, *_: (0, s))],
        out_specs=pl.BlockSpec(memory_space=pl.ANY),
        scratch_shapes=[pltpu.VMEM((MOE_RPAD, MOE_SB), BF16),
                        pltpu.VMEM((2, MOE_RPAD, MOE_XW), BF16),
                        pltpu.VMEM((MOE_DMA_SIZES[0], MOE_XW), BF16),
                        pltpu.SemaphoreType.DMA((2,))])
    return pl.pallas_call(
        functools.partial(_dispatch_kernel, nsb_p=nsb_p, n_sb=n_sb),
        grid_spec=grid_spec,
        out_shape=jax.ShapeDtypeStruct((plan['nt_max'] * MOE_TF, MOE_XW), BF16),
        compiler_params=_cparams(("arbitrary",)),
        name="moe_dispatch",
    )(plan['cnt'], plan['loc'], plan['big'], *plan['pieces'], plan['padoff'], plan['padn'], h2_p, h2_s, gates_t)


def _ffn_call(plan, xe, wg, wu, wd):
    grid_spec = pltpu.PrefetchScalarGridSpec(
        num_scalar_prefetch=3, grid=(plan['nt_max'],),
        in_specs=[pl.BlockSpec((MOE_TF, MOE_XW), lambda i, te, tidx, tv: (tidx[i], 0)),
                  pl.BlockSpec((1, D_MODEL, EXPERT_FF), lambda i, te, tidx, tv: (te[i], 0, 0)),
                  pl.BlockSpec((1, D_MODEL, EXPERT_FF), lambda i, te, tidx, tv: (te[i], 0, 0)),
                  pl.BlockSpec((1, EXPERT_FF, D_MODEL), lambda i, te, tidx, tv: (te[i], 0, 0))],
        out_specs=pl.BlockSpec((MOE_TF, D_MODEL), lambda i, te, tidx, tv: (tidx[i], 0)),
        scratch_shapes=[pltpu.VMEM((D_MODEL, EXPERT_FF), BF16),
                        pltpu.VMEM((D_MODEL, EXPERT_FF), BF16),
                        pltpu.VMEM((EXPERT_FF, D_MODEL), BF16)])
    return pl.pallas_call(
        _ffn_kernel,
        grid_spec=grid_spec,
        out_shape=jax.ShapeDtypeStruct((xe.shape[0], D_MODEL), BF16),
        compiler_params=_cparams(("arbitrary",)),
        name="moe_ffn",
    )(plan['te'], plan['tidx'], plan['tcode'], xe, wg, wu, wd)


def _combine_call(plan, ye, gates_t, h2, x1, mod3, sg, su, sd, final_w, sb0, seq_len):
    n = h2.shape[0]
    n_steps = n // MOE_SB
    bpb = seq_len // MOE_SB
    row = lambda i, *_: (i, 0)
    const2 = lambda i, *_: (0, 0)
    grid_spec = pltpu.PrefetchScalarGridSpec(
        num_scalar_prefetch=7, grid=(n_steps,),
        in_specs=[pl.BlockSpec(memory_space=pl.ANY),
                  pl.BlockSpec((N_EXPERTS, MOE_SB), lambda i, *_: (0, sb0 + i)),
                  pl.BlockSpec((MOE_SB, D_MODEL), row),
                  pl.BlockSpec((MOE_SB, D_MODEL), row),
                  pl.BlockSpec((1, 1, 6 * D_MODEL), lambda i, *_: (i // bpb, 0, 0)),
                  pl.BlockSpec((D_MODEL, EXPERT_FF), const2),
                  pl.BlockSpec((D_MODEL, EXPERT_FF), const2),
                  pl.BlockSpec((EXPERT_FF, D_MODEL), const2),
                  pl.BlockSpec((1, D_MODEL), const2)],
        out_specs=pl.BlockSpec((MOE_SB, D_MODEL), row),
        scratch_shapes=[pltpu.VMEM((MOE_RPAD, MOE_SB), BF16),
                        pltpu.VMEM((2, MOE_RPAD, D_MODEL), BF16),
                        pltpu.VMEM((MOE_SB, D_MODEL), F32),
                        pltpu.SemaphoreType.DMA((2,))])
    return pl.pallas_call(
        functools.partial(_combine_kernel, sb0=sb0, n_steps=n_steps),
        grid_spec=grid_spec,
        out_shape=jax.ShapeDtypeStruct((n, D_MODEL), F32),
        compiler_params=_cparams(("arbitrary",)),
        name="moe_combine",
    )(plan['cnt'], plan['loc'], plan['big'], *plan['pieces'], ye, gates_t, h2, x1, mod3, sg, su, sd,
      final_w.reshape(1, D_MODEL))


def _moe_sparse(h2_p, h2_s, gt_p, gt_s, x1_p, x1_s, mod_p, mod_s, wg, wu, wd, sg, su, sd, final_w, dseq):
    gates_t = jnp.concatenate([gt_p, gt_s], axis=1)
    plan = _moe_plan(gates_t)
    xe = _dispatch_call(plan, h2_p, h2_s, gates_t)
    ye = _ffn_call(plan, xe, wg, wu, wd)
    n_p = h2_p.shape[0]
    y_p = _combine_call(plan, ye, gates_t, h2_p, x1_p, mod_p, sg, su, sd, final_w, 0, n_p)
    y_s = _combine_call(plan, ye, gates_t, h2_s, x1_s, mod_s, sg, su, sd, final_w, n_p // MOE_SB, dseq)
    return y_p, y_s


def _rope_tables(n_tokens):
    rows = n_tokens // GRID_W
    row = jnp.repeat(jnp.arange(rows, dtype=F32), GRID_W)
    col = jnp.tile(jnp.arange(GRID_W, dtype=F32), rows)
    freqs = ROPE_THETA ** (-jnp.arange(ROT_PAIRS, dtype=F32) / ROT_PAIRS)
    ar = row[:, None] * freqs
    ac = col[:, None] * freqs
    cos = jnp.concatenate([jnp.cos(ar), jnp.cos(ar), jnp.cos(ac), jnp.cos(ac)], axis=1)
    sin = jnp.concatenate([-jnp.sin(ar), jnp.sin(ar), -jnp.sin(ac), jnp.sin(ac)], axis=1)
    return jnp.tile(cos, (1, 2)), jnp.tile(sin, (1, 2))


def kernel(x_prompt, x_sample, c, cache_k, cache_v, state_ssm_re, state_ssm_im, c_ctx, w_ada, b_ada, norm1_w, w_in, ssm_lambda_re, ssm_lambda_im, ssm_log_dt, ssm_b_re, ssm_b_im, ssm_c_re, ssm_c_im, ssm_d, ssm_w_glu, diff_lambda_q, diff_lambda_k, diff_subln_w, w_out, norm2_w, w_router, router_bias, w_exp_gate, w_exp_up, w_exp_down, w_sh_gate, w_sh_up, w_sh_down, final_norm_w):
    depth = w_ada.shape[0]
    assert depth == 1
    l = 0
    lambda_init = 0.8 - 0.6 * math.exp(-0.3 * l)
    bsz, seq, _ = x_prompt.shape
    dbs, dseq, _ = x_sample.shape
    n_p, n_s = bsz * seq, dbs * dseq

    cond8 = jnp.zeros((SUBLANES, D_MODEL), F32).at[:dbs].set(c).at[dbs].set(c_ctx)
    mod = _ada_call(cond8, w_ada[l], b_ada[l])
    mod_s = mod[:dbs].reshape(dbs, 1, 6 * D_MODEL)
    mod_p = mod[dbs:dbs + 1].reshape(1, 1, 6 * D_MODEL)

    w_in_bf = w_in[l].astype(BF16)
    xp2 = x_prompt.reshape(n_p, D_MODEL)
    xs2 = x_sample.reshape(n_s, D_MODEL)
    u_p, q_p, k_p, v_p = _inproj_call(xp2, mod_p, norm1_w[l], w_in_bf, None, n_p, F32)
    u_s, q_s, k_s, v_s = _inproj_call(xs2, mod_s, norm1_w[l], w_in_bf, _rope_tables(dseq), dseq, BF16)

    bblk, cblk, lam_r, lam_i = _s5_weights(ssm_lambda_re[l], ssm_lambda_im[l], ssm_log_dt[l],
                                           ssm_b_re[l], ssm_b_im[l], ssm_c_re[l], ssm_c_im[l])
    lr_p = jnp.broadcast_to(lam_r[:, :, None, :], (2, N_COLBLK, SUBLANES, S5_BLK_STATES))
    li_p = jnp.broadcast_to(lam_i[:, :, None, :], (2, N_COLBLK, SUBLANES, S5_BLK_STATES))
    g_p = bsz // SUBLANES
    h0_p = jnp.zeros((2, g_p, N_COLBLK * 8, SUBLANES, LANES), F32)
    y_p, hfin = _s5_call(u_p.reshape(bsz, seq, S5_WIDTH), bblk, cblk, lr_p, li_p, h0_p, SUBLANES)

    def halves(x):
        x = x.reshape(2, 2, 2, 1, S5_BLK_STATES)
        x = jnp.broadcast_to(x, (2, 2, 2, dbs, S5_BLK_STATES))
        return x.transpose(0, 2, 1, 3, 4).reshape(2, 2, 2 * dbs, S5_BLK_STATES)

    def h0_tiles(s):
        s = s.astype(F32).reshape(dbs, 2, 2, 2, 4, LANES)
        return s.transpose(1, 3, 4, 2, 0, 5).reshape(2, 1, 2, 4, 2 * dbs, LANES)

    h0_s = jnp.concatenate([h0_tiles(state_ssm_re[:, l]), h0_tiles(state_ssm_im[:, l])], axis=3)
    h0_s = h0_s.reshape(2, 1, 16, SUBLANES, LANES)
    y_s, _ = _s5_call(u_s.reshape(dbs, dseq, S5_WIDTH), bblk, cblk, halves(lam_r), halves(lam_i), h0_s, dbs,
                      tb=128)

    lq = diff_lambda_q[l].astype(F32)
    lk = diff_lambda_k[l].astype(F32)
    lam = (jnp.exp(jnp.sum(lq[0] * lk[0])) - jnp.exp(jnp.sum(lq[1] * lk[1])) + lambda_init).reshape(1)
    subln = diff_subln_w[l].astype(F32).reshape(1, VALUE_DIM)
    ao_p = _attn_ctx_call(lam, q_p.reshape(bsz, seq, ATTN_WIDTH), k_p.reshape(bsz, seq, ATTN_WIDTH),
                          v_p.reshape(bsz, seq, ATTN_WIDTH), subln, lambda_init)
    past = cache_k.shape[2]
    ao_s = _attn_lat_call(lam, q_s.reshape(dbs, dseq, ATTN_WIDTH), k_s.reshape(dbs, dseq, ATTN_WIDTH),
                          v_s.reshape(dbs, dseq, ATTN_WIDTH),
                          cache_k[:, l].reshape(dbs, past, ATTN_WIDTH).astype(BF16),
                          cache_v[:, l].reshape(dbs, past, ATTN_WIDTH).astype(BF16), subln, lambda_init)

    wglu_bf = ssm_w_glu[l].astype(BF16)
    wout_bf = w_out[l].astype(BF16)
    wr_t = w_router[l].astype(F32).T
    rbias3 = router_bias[l].astype(F32).reshape(N_EXPERT_GROUPS, N_EXPERTS // N_EXPERT_GROUPS, 1)
    x1_p, h2_p, gates_p = _post_call(xp2, u_p, y_p, ao_p.reshape(n_p, ATTN_WIDTH), mod_p, ssm_d[l],
                                     wglu_bf, wout_bf, norm2_w[l], wr_t, rbias3, n_p)
    x1_s, h2_s, gates_s = _post_call(xs2, u_s, y_s, ao_s.reshape(n_s, ATTN_WIDTH), mod_s, ssm_d[l],
                                     wglu_bf, wout_bf, norm2_w[l], wr_t, rbias3, dseq)

    wg = w_exp_gate[l]
    wu = w_exp_up[l]
    wd = w_exp_down[l]
    sg = w_sh_gate[l].astype(BF16)
    su = w_sh_up[l].astype(BF16)
    sd = w_sh_down[l].astype(BF16)
    y_prompt, y_sample = _moe_sparse(h2_p, h2_s, gates_p, gates_s, x1_p, x1_s, mod_p, mod_s,
                                     wg, wu, wd, sg, su, sd, final_norm_w, dseq)

    new_cache_k = k_p.reshape(bsz, 1, seq, N_DIFF_HEADS, VALUE_DIM)
    new_cache_v = v_p.reshape(bsz, 1, seq, N_DIFF_HEADS, VALUE_DIM)
    hf = hfin.reshape(2, g_p, N_COLBLK, 2, 4, SUBLANES, LANES)
    hf = hf.transpose(3, 1, 5, 0, 2, 4, 6).reshape(2, bsz, 1, 2, S5_GROUPS, S5_STATE)
    return (y_prompt.reshape(bsz, seq, D_MODEL), y_sample.reshape(dbs, dseq, D_MODEL),
            new_cache_k, new_cache_v, hf[0], hf[1])
```

```python
import functools
import math

import jax
import jax.numpy as jnp
from jax import lax
from jax.experimental import pallas as pl
from jax.experimental.pallas import tpu as pltpu

F32 = jnp.float32
BF16 = jnp.bfloat16

D_MODEL = 1024
GRID_W = 64
S5_WIDTH = 512
S5_GROUP = 16
S5_GROUPS = 32
S5_STATE = 64
ATTN_WIDTH = 512
DIFF_HEAD_DIM = 64
VALUE_DIM = 128
N_DIFF_HEADS = 4
IN_WIDTH = S5_WIDTH + 3 * ATTN_WIDTH
ROT_PAIRS = DIFF_HEAD_DIM // 4
ROPE_THETA = 10000.0
N_EXPERTS = 64
TOP_K = 8
N_EXPERT_GROUPS = 8
TOPK_GROUPS = 4
EXPERT_FF = 256
ROUTED_SCALE = 2.5
NORM_EPS = 1e-6

LANES = 128
SUBLANES = 8
S5_COLBLK = 8 * S5_GROUP
S5_BLK_STATES = 8 * S5_STATE
N_COLBLK = S5_WIDTH // S5_COLBLK
VMEM_LIMIT = 56 * 1024 * 1024


def _cparams(sem):
    return pltpu.CompilerParams(dimension_semantics=sem, vmem_limit_bytes=VMEM_LIMIT)


def _ada_kernel(cond_ref, w_ref, b_ref, o_ref):
    c = cond_ref[...]
    s = c * jax.nn.sigmoid(c)
    o_ref[...] = jnp.dot(s.astype(BF16), w_ref[...].astype(BF16),
                         preferred_element_type=F32) + b_ref[...]


def _ada_call(cond8, w_ada, b_ada):
    n = w_ada.shape[1]
    tn = 1536
    return pl.pallas_call(
        _ada_kernel,
        grid=(n // tn,),
        in_specs=[pl.BlockSpec((SUBLANES, D_MODEL), lambda j: (0, 0)),
                  pl.BlockSpec((D_MODEL, tn), lambda j: (0, j)),
                  pl.BlockSpec((1, tn), lambda j: (0, j))],
        out_specs=pl.BlockSpec((SUBLANES, tn), lambda j: (0, j)),
        out_shape=jax.ShapeDtypeStruct((SUBLANES, n), F32),
        compiler_params=_cparams(("arbitrary",)),
        name="adaln",
    )(cond8, w_ada, b_ada.reshape(1, n))


def _rope_apply(t, cos, sin):
    parts = []
    for cidx in range(ATTN_WIDTH // LANES):
        xc = t[:, cidx * LANES:(cidx + 1) * LANES]
        up = pltpu.roll(xc, LANES - ROT_PAIRS, 1)
        dn = pltpu.roll(xc, ROT_PAIRS, 1)
        lane = lax.broadcasted_iota(jnp.int32, xc.shape, 1)
        partner = jnp.where((lane % (2 * ROT_PAIRS)) < ROT_PAIRS, up, dn)
        parts.append(xc * cos + partner * sin)
    return jnp.concatenate(parts, axis=1)


def _inproj_kernel(*refs, rope, kv_dtype):
    if rope:
        x_ref, mod_ref, n1_ref, w_ref, cos_ref, sin_ref, u_ref, q_ref, k_ref, v_ref = refs
    else:
        x_ref, mod_ref, n1_ref, w_ref, u_ref, q_ref, k_ref, v_ref = refs
    x = x_ref[...]
    ms = jnp.mean(x * x, axis=-1, keepdims=True)
    xn = x * lax.rsqrt(ms + NORM_EPS) * n1_ref[...]
    mod = mod_ref[0]
    shift = mod[:, 0:D_MODEL]
    scale = mod[:, D_MODEL:2 * D_MODEL]
    h = xn * (1.0 + scale) + shift
    proj = jnp.dot(h.astype(BF16), w_ref[...], preferred_element_type=F32)
    u_ref[...] = proj[:, :S5_WIDTH]
    q = proj[:, S5_WIDTH:S5_WIDTH + ATTN_WIDTH]
    k = proj[:, S5_WIDTH + ATTN_WIDTH:S5_WIDTH + 2 * ATTN_WIDTH]
    v = proj[:, S5_WIDTH + 2 * ATTN_WIDTH:]
    if rope:
        cos = cos_ref[...]
        sin = sin_ref[...]
        q = _rope_apply(q, cos, sin)
        k = _rope_apply(k, cos, sin)
    q_ref[...] = (q * (DIFF_HEAD_DIM ** -0.5)).astype(BF16)
    k_ref[...] = k.astype(kv_dtype)
    v_ref[...] = v.astype(kv_dtype)


def _inproj_call(x2, mod3, norm1_w, w_in_bf, rope_tabs, seq_len, kv_dtype, tm=512):
    n = x2.shape[0]
    bpb = seq_len // tm
    rope = rope_tabs is not None
    in_specs = [pl.BlockSpec((tm, D_MODEL), lambda i: (i, 0)),
                pl.BlockSpec((1, 1, 6 * D_MODEL), lambda i: (i // bpb, 0, 0)),
                pl.BlockSpec((1, D_MODEL), lambda i: (0, 0)),
                pl.BlockSpec((D_MODEL, IN_WIDTH), lambda i: (0, 0))]
    args = [x2, mod3, norm1_w.reshape(1, D_MODEL), w_in_bf]
    if rope:
        in_specs += [pl.BlockSpec((tm, LANES), lambda i: (i % bpb, 0)),
                     pl.BlockSpec((tm, LANES), lambda i: (i % bpb, 0))]
        args += list(rope_tabs)
    ospec = pl.BlockSpec((tm, S5_WIDTH), lambda i: (i, 0))
    return pl.pallas_call(
        functools.partial(_inproj_kernel, rope=rope, kv_dtype=kv_dtype),
        grid=(n // tm,),
        in_specs=in_specs,
        out_specs=[ospec, ospec, ospec, ospec],
        out_shape=[jax.ShapeDtypeStruct((n, S5_WIDTH), F32),
                   jax.ShapeDtypeStruct((n, ATTN_WIDTH), BF16),
                   jax.ShapeDtypeStruct((n, ATTN_WIDTH), kv_dtype),
                   jax.ShapeDtypeStruct((n, ATTN_WIDTH), kv_dtype)],
        compiler_params=_cparams(("parallel",)),
        name="inproj_rope" if rope else "inproj",
    )(*args)


def _s5_slot(b, q, nseq):
    if nseq == SUBLANES:
        return b, q * 8
    half = q // 2
    return half * nseq + b, (q % 2) * 8


def _s5_kernel(u_ref, b_ref, c_ref, lr_ref, li_ref, h0_ref, y_ref, hfin_ref, buh, hs, hst,
               *, nseq, tb, stride, nqq):
    d = pl.program_id(1)
    c = pl.program_id(2)
    nc = pl.num_programs(2)

    @pl.when(c == 0)
    def _():
        hst[...] = h0_ref[0, 0]
        hs[...] = jnp.zeros(hs.shape, F32)

    u = u_ref[...].reshape(nseq * tb, S5_WIDTH).astype(BF16)
    for q in range(N_COLBLK):
        res = jnp.dot(u[:, q * S5_COLBLK:(q + 1) * S5_COLBLK], b_ref[0, q],
                      preferred_element_type=F32)
        for b in range(nseq):
            slot, ct0 = _s5_slot(b, q, nseq)
            for j in range(8):
                buh[ct0 + j, slot * stride:slot * stride + tb, :] = (
                    res[b * tb:(b + 1) * tb, j * LANES:(j + 1) * LANES])

    for qq in range(nqq):
        lr = [lr_ref[0, qq, :, i * LANES:(i + 1) * LANES] for i in range(4)]
        li = [li_ref[0, qq, :, i * LANES:(i + 1) * LANES] for i in range(4)]
        h_init = tuple(hst[qq * 8 + i] for i in range(8))

        def body(s, carry, qq=qq, lr=lr, li=li):
            t = jnp.where(d == 0, s, tb - 1 - s)
            idx = pl.ds(t, SUBLANES, stride=stride)
            new_r, new_i = [], []
            for i in range(4):
                hr, hi = carry[i], carry[4 + i]
                bur = buh[qq * 8 + i, idx, :]
                bui = buh[qq * 8 + 4 + i, idx, :]
                nr = lr[i] * hr - li[i] * hi + bur
                ni = lr[i] * hi + li[i] * hr + bui
                hs[qq * 8 + i, idx, :] = nr
                hs[qq * 8 + 4 + i, idx, :] = ni
                new_r.append(nr)
                new_i.append(ni)
            return tuple(new_r) + tuple(new_i)

        h_fin = lax.fori_loop(0, tb, body, h_init, unroll=4)
        for i in range(8):
            hst[qq * 8 + i] = h_fin[i]

    rows = nseq * stride
    for q in range(N_COLBLK):
        slot0, ct0 = _s5_slot(0, q, nseq)
        hmat = jnp.concatenate(
            [hs[ct0 + j, slot0 * stride:slot0 * stride + rows, :] for j in range(8)], axis=1)
        yq = jnp.dot(hmat.astype(BF16), c_ref[0, q], preferred_element_type=F32)
        for b in range(nseq):
            y_ref[0, b, :, q * S5_COLBLK:(q + 1) * S5_COLBLK] = yq[b * stride:b * stride + tb, :]

    @pl.when(c == nc - 1)
    def _():
        hfin_ref[0, 0] = hst[...]


def _s5_call(u3, bblk, cblk, lr, li, h0, nseq, tb=64):
    nseq_total, seq_len, _ = u3.shape
    g = nseq_total // nseq
    nc = seq_len // tb
    stride = tb + SUBLANES
    nqq = lr.shape[1]
    nct = nqq * 8

    def tmap(gi, d, c):
        return c + d * (nc - 1 - 2 * c)

    return pl.pallas_call(
        functools.partial(_s5_kernel, nseq=nseq, tb=tb, stride=stride, nqq=nqq),
        grid=(g, 2, nc),
        in_specs=[
            pl.BlockSpec((nseq, tb, S5_WIDTH), lambda gi, d, c: (gi, tmap(gi, d, c), 0)),
            pl.BlockSpec((1, N_COLBLK, S5_COLBLK, 2 * S5_BLK_STATES), lambda gi, d, c: (d, 0, 0, 0)),
            pl.BlockSpec((1, N_COLBLK, 2 * S5_BLK_STATES, S5_COLBLK), lambda gi, d, c: (d, 0, 0, 0)),
            pl.BlockSpec((1, nqq, SUBLANES, S5_BLK_STATES), lambda gi, d, c: (d, 0, 0, 0)),
            pl.BlockSpec((1, nqq, SUBLANES, S5_BLK_STATES), lambda gi, d, c: (d, 0, 0, 0)),
            pl.BlockSpec((1, 1, nct, SUBLANES, LANES), lambda gi, d, c: (d, gi, 0, 0, 0)),
        ],
        out_specs=[
            pl.BlockSpec((1, nseq, tb, S5_WIDTH), lambda gi, d, c: (d, gi, tmap(gi, d, c), 0)),
            pl.BlockSpec((1, 1, nct, SUBLANES, LANES), lambda gi, d, c: (d, gi, 0, 0, 0)),
        ],
        out_shape=[jax.ShapeDtypeStruct((2, nseq_total, seq_len, S5_WIDTH), F32),
                   jax.ShapeDtypeStruct((2, g, nct, SUBLANES, LANES), F32)],
        scratch_shapes=[pltpu.VMEM((nct, SUBLANES * stride, LANES), F32),
                        pltpu.VMEM((nct, SUBLANES * stride, LANES), F32),
                        pltpu.VMEM((nct, SUBLANES, LANES), F32)],
        compiler_params=_cparams(("parallel", "parallel", "arbitrary")),
        name="s5_scan_%d" % nseq,
    )(u3, bblk, cblk, lr, li, h0)


def _s5_weights(lam_re, lam_im, log_dt, b_re, b_im, c_re, c_im):
    dt = jnp.exp(log_dt.astype(F32))[..., None]
    lr = lam_re.astype(F32)
    li = lam_im.astype(F32)
    mag = jnp.exp(lr * dt)
    lbr = mag * jnp.cos(li * dt)
    lbi = mag * jnp.sin(li * dt)
    a = lbr - 1.0
    den = lr * lr + li * li
    cr = (a * lr + lbi * li) / den
    ci = (lbi * lr - a * li) / den
    bbr = cr[..., None] * b_re - ci[..., None] * b_im
    bbi = cr[..., None] * b_im + ci[..., None] * b_re
    eye = jnp.eye(8, dtype=F32)

    def pack_b(x):
        x = x.reshape(2, N_COLBLK, 8, S5_STATE, S5_GROUP).transpose(0, 1, 2, 4, 3)
        x = jnp.einsum('dqghn,gk->dqghkn', x, eye)
        return x.reshape(2, N_COLBLK, S5_COLBLK, S5_BLK_STATES)

    def pack_c(x):
        x = x.reshape(2, N_COLBLK, 8, S5_GROUP, S5_STATE).transpose(0, 1, 2, 4, 3)
        x = jnp.einsum('dqgnh,gk->dqgnkh', x, eye)
        return x.reshape(2, N_COLBLK, S5_BLK_STATES, S5_COLBLK)

    bblk = jnp.concatenate([pack_b(bbr), pack_b(bbi)], axis=-1).astype(BF16)
    cblk = jnp.concatenate([pack_c(c_re.astype(F32)), pack_c(-c_im.astype(F32))], axis=-2).astype(BF16)
    lam_r = lbr.reshape(2, N_COLBLK, S5_BLK_STATES)
    lam_i = lbi.reshape(2, N_COLBLK, S5_BLK_STATES)
    return bblk, cblk, lam_r, lam_i


ATTN_KC = 512


def _lane_fold(x, op):
    acc = x[:, 0:LANES]
    for j in range(1, x.shape[1] // LANES):
        acc = op(acc, x[:, j * LANES:(j + 1) * LANES])
    return acc


def _attn_ops(q, chunks, s_ref):
    lane = lax.broadcasted_iota(jnp.int32, q.shape, 1)
    dn = (((1,), (1,)), ((), ()))
    qms = [jnp.where(lane < DIFF_HEAD_DIM, q, jnp.zeros_like(q)),
           jnp.where(lane >= DIFF_HEAD_DIM, q, jnp.zeros_like(q))]
    offs = [0]
    for load_k, _ in chunks:
        offs.append(offs[-1] + load_k().shape[0])

    def score_chunk(m, ci, m128):
        s = lax.dot_general(qms[m], chunks[ci][0](), dn, preferred_element_type=F32)
        s_ref[m, :, offs[ci]:offs[ci + 1]] = s
        f = _lane_fold(s, jnp.maximum)
        return f if m128 is None else jnp.maximum(m128, f)

    def value_chunk(m, ci, mx, l128, o):
        p = jnp.exp(s_ref[m, :, offs[ci]:offs[ci + 1]] - mx)
        f = _lane_fold(p, jnp.add)
        t = jnp.dot(p.astype(BF16), chunks[ci][1](), preferred_element_type=F32)
        return (f if l128 is None else l128 + f), (t if o is None else o + t)

    return score_chunk, value_chunk


def _row_max(m128):
    return m128.max(axis=-1, keepdims=True)


def _normalized(l128, o):
    return o * (1.0 / l128.sum(axis=-1, keepdims=True))


def _sub_ln(o, subln, lambda_init):
    ms = jnp.mean(o * o, axis=-1, keepdims=True)
    return o * lax.rsqrt(ms + NORM_EPS) * subln * (1.0 - lambda_init)


def _attn_ctx_kernel(lam_ref, q_ref, k_ref, v_ref, w_ref, o_ref, s_ref, *, lambda_init):
    chunks = [(lambda: k_ref[0].astype(BF16), lambda: v_ref[0].astype(BF16))]
    score_chunk, value_chunk = _attn_ops(q_ref[0], chunks, s_ref)
    mx0 = _row_max(score_chunk(0, 0, None))
    m1 = score_chunk(1, 0, None)
    l0, o0 = value_chunk(0, 0, mx0, None, None)
    l1, o1 = value_chunk(1, 0, _row_max(m1), None, None)
    o = _normalized(l0, o0) - lam_ref[0] * _normalized(l1, o1)
    o_ref[0] = _sub_ln(o, w_ref[...], lambda_init).astype(BF16)


def _attn_lat_kernel(lam_ref, q_ref, k_ref, v_ref, ck_ref, cv_ref, w_ref, o_ref, s_ref, *, lambda_init):
    chunks = []
    for c in range(k_ref.shape[1] // ATTN_KC):
        chunks.append((lambda c=c: k_ref[0, c * ATTN_KC:(c + 1) * ATTN_KC, :],
                       lambda c=c: v_ref[0, c * ATTN_KC:(c + 1) * ATTN_KC, :]))
    for c in range(ck_ref.shape[1] // ATTN_KC):
        chunks.append((lambda c=c: ck_ref[0, c * ATTN_KC:(c + 1) * ATTN_KC, :],
                       lambda c=c: cv_ref[0, c * ATTN_KC:(c + 1) * ATTN_KC, :]))
    nchunk = len(chunks)
    score_chunk, value_chunk = _attn_ops(q_ref[0], chunks, s_ref)

    m0 = m1 = l0 = o0 = l1 = o1 = None
    for ci in range(nchunk):
        m0 = score_chunk(0, ci, m0)
    mx0 = _row_max(m0)
    for ci in range(nchunk):
        m1 = score_chunk(1, ci, m1)
        l0, o0 = value_chunk(0, ci, mx0, l0, o0)
    mx1 = _row_max(m1)
    for ci in range(nchunk):
        l1, o1 = value_chunk(1, ci, mx1, l1, o1)
    o = _normalized(l0, o0) - lam_ref[0] * _normalized(l1, o1)
    o_ref[0] = _sub_ln(o, w_ref[...], lambda_init).astype(BF16)


def _attn_ctx_call(lam, q3, k3, v3, subln, lambda_init):
    bsz, seq_len, _ = q3.shape
    spec = pl.BlockSpec((1, seq_len, VALUE_DIM), lambda b, h: (b, 0, h))
    return pl.pallas_call(
        functools.partial(_attn_ctx_kernel, lambda_init=lambda_init),
        grid=(bsz, N_DIFF_HEADS),
        in_specs=[pl.BlockSpec(memory_space=pltpu.SMEM), spec, spec, spec,
                  pl.BlockSpec((1, VALUE_DIM), lambda b, h: (0, 0))],
        out_specs=spec,
        out_shape=jax.ShapeDtypeStruct((bsz, seq_len, ATTN_WIDTH), BF16),
        scratch_shapes=[pltpu.VMEM((2, seq_len, seq_len), F32)],
        compiler_params=_cparams(("parallel", "parallel")),
        name="attn_ctx",
    )(lam, q3, k3, v3, subln)


def _attn_lat_call(lam, q3, k3, v3, ck3, cv3, subln, lambda_init, tq=256):
    bsz, seq_len, _ = q3.shape
    ctx_len = ck3.shape[1]
    qspec = pl.BlockSpec((1, tq, VALUE_DIM), lambda b, h, i: (b, i, h))
    kspec = pl.BlockSpec((1, seq_len, VALUE_DIM), lambda b, h, i: (b, 0, h))
    cspec = pl.BlockSpec((1, ctx_len, VALUE_DIM), lambda b, h, i: (b, 0, h))
    return pl.pallas_call(
        functools.partial(_attn_lat_kernel, lambda_init=lambda_init),
        grid=(bsz, N_DIFF_HEADS, seq_len // tq),
        in_specs=[pl.BlockSpec(memory_space=pltpu.SMEM), qspec, kspec, kspec, cspec, cspec,
                  pl.BlockSpec((1, VALUE_DIM), lambda b, h, i: (0, 0))],
        out_specs=qspec,
        out_shape=jax.ShapeDtypeStruct((bsz, seq_len, ATTN_WIDTH), BF16),
        scratch_shapes=[pltpu.VMEM((2, tq, seq_len + ctx_len), F32)],
        compiler_params=_cparams(("parallel", "parallel", "arbitrary")),
        name="attn_lat",
    )(lam, q3, k3, v3, ck3, cv3, subln)


def _route(logits_t, bias3, tm):
    ng, ge = N_EXPERT_GROUPS, N_EXPERTS // N_EXPERT_GROUPS
    neg = jnp.float32(-jnp.inf)
    sc = jax.nn.sigmoid(logits_t).reshape(ng, ge, tm)
    bi = sc + bias3
    eio = lax.broadcasted_iota(jnp.int32, (ng, ge, tm), 1).astype(F32)
    gio = lax.broadcasted_iota(jnp.int32, (ng, ge, tm), 0).astype(F32)
    m1 = bi.max(axis=1, keepdims=True)
    i1 = jnp.where(bi == m1, eio, float(ge)).min(axis=1, keepdims=True)
    m2 = jnp.where(eio == i1, neg, bi).max(axis=1, keepdims=True)
    gs = jnp.broadcast_to(m1 + m2, (ng, ge, tm))
    cnt = jnp.zeros((ng, ge, tm), F32)
    for g2 in range(ng):
        o = gs[g2:g2 + 1]
        better = (o > gs) | ((o == gs) & (gio > float(g2)))
        cnt = cnt + jnp.where(better, 1.0, 0.0)
    v = jnp.where(cnt < float(TOPK_GROUPS), bi, neg)
    eidx = gio * float(ge) + eio
    selm = jnp.zeros((ng, ge, tm), F32)
    for _ in range(TOP_K):
        m = v.max(axis=0, keepdims=True).max(axis=1, keepdims=True)
        ix = jnp.where(v == m, eidx, float(N_EXPERTS)).min(axis=0, keepdims=True).min(axis=1, keepdims=True)
        oh = eidx == ix
        selm = jnp.where(oh, 1.0, selm)
        v = jnp.where(oh, neg, v)
    selsc = selm * sc
    ssum = selsc.sum(axis=0, keepdims=True).sum(axis=1, keepdims=True)
    return (selsc / ssum * ROUTED_SCALE).reshape(N_EXPERTS, tm)


def _post_kernel(x_ref, u_ref, yf_ref, yb_ref, ao_ref, mod_ref, d_ref, wglu_ref, wout_ref, n2_ref,
                 wrt_ref, rb_ref, x1_ref, h2_ref, gates_ref, *, tm):
    u = u_ref[...]
    y = u * d_ref[...] + yf_ref[0] + yb_ref[0]
    g = jax.nn.gelu(y)
    s5 = g * jax.nn.sigmoid(jnp.dot(g.astype(BF16), wglu_ref[...], preferred_element_type=F32))
    mixed = (jnp.dot(s5.astype(BF16), wout_ref[0:S5_WIDTH, :], preferred_element_type=F32)
             + jnp.dot(ao_ref[...], wout_ref[S5_WIDTH:, :], preferred_element_type=F32))
    mod = mod_ref[0]
    gate1 = mod[:, 2 * D_MODEL:3 * D_MODEL]
    shift2 = mod[:, 3 * D_MODEL:4 * D_MODEL]
    scale2 = mod[:, 4 * D_MODEL:5 * D_MODEL]
    x1 = x_ref[...] + gate1 * mixed
    x1_ref[...] = x1
    ms = jnp.mean(x1 * x1, axis=-1, keepdims=True)
    h2 = x1 * lax.rsqrt(ms + NORM_EPS) * n2_ref[...] * (1.0 + scale2) + shift2
    h2_ref[...] = h2.astype(BF16)
    logits_t = lax.dot_general(wrt_ref[...], h2, (((1,), (1,)), ((), ())),
                               preferred_element_type=F32, precision=lax.Precision.HIGHEST)
    gates_ref[...] = _route(logits_t, rb_ref[...], tm)


def _post_call(x2, u2, y4, ao2, mod3, ssm_d, wglu_bf, wout_bf, norm2_w, wr_t, rbias3, seq_len, tm=512):
    n = x2.shape[0]
    bpb = seq_len // tm
    y3 = y4.reshape(2, n, S5_WIDTH)
    row = lambda i: (i, 0)
    const2 = lambda i: (0, 0)
    return pl.pallas_call(
        functools.partial(_post_kernel, tm=tm),
        grid=(n // tm,),
        in_specs=[pl.BlockSpec((tm, D_MODEL), row),
                  pl.BlockSpec((tm, S5_WIDTH), row),
                  pl.BlockSpec((1, tm, S5_WIDTH), lambda i: (0, i, 0)),
                  pl.BlockSpec((1, tm, S5_WIDTH), lambda i: (1, i, 0)),
                  pl.BlockSpec((tm, ATTN_WIDTH), row),
                  pl.BlockSpec((1, 1, 6 * D_MODEL), lambda i: (i // bpb, 0, 0)),
                  pl.BlockSpec((1, S5_WIDTH), const2),
                  pl.BlockSpec((S5_WIDTH, S5_WIDTH), const2),
                  pl.BlockSpec((D_MODEL, D_MODEL), const2),
                  pl.BlockSpec((1, D_MODEL), const2),
                  pl.BlockSpec((N_EXPERTS, D_MODEL), const2),
                  pl.BlockSpec((N_EXPERT_GROUPS, N_EXPERTS // N_EXPERT_GROUPS, 1), lambda i: (0, 0, 0))],
        out_specs=[pl.BlockSpec((tm, D_MODEL), row),
                   pl.BlockSpec((tm, D_MODEL), row),
                   pl.BlockSpec((N_EXPERTS, tm), lambda i: (0, i))],
        out_shape=[jax.ShapeDtypeStruct((n, D_MODEL), F32),
                   jax.ShapeDtypeStruct((n, D_MODEL), BF16),
                   jax.ShapeDtypeStruct((N_EXPERTS, n), F32)],
        compiler_params=_cparams(("parallel",)),
        name="post_mix_route",
    )(x2, u2, y3, y3, ao2, mod3, ssm_d.reshape(1, S5_WIDTH), wglu_bf, wout_bf,
      norm2_w.reshape(1, D_MODEL), wr_t, rbias3)


def _swiglu_bf(t, wg, wu, wd):
    a = jnp.dot(t, wg, preferred_element_type=F32)
    b = jnp.dot(t, wu, preferred_element_type=F32)
    act = a * jax.nn.sigmoid(a) * b
    return jnp.dot(act.astype(BF16), wd, preferred_element_type=F32)


MOE_SB = 256
MOE_SEG = 16
MOE_TF = 1024
MOE_XW = D_MODEL + 2 * N_EXPERTS
MOE_STATIC_CHUNKS = 3
MOE_CHUNK = 512
MOE_DMA_SIZES = tuple(2 ** k for k in range(MOE_TF.bit_length() - 2, 3, -1))
MOE_RPAD = -(-(TOP_K * MOE_SB + N_EXPERTS * (MOE_SEG - 1) + MOE_STATIC_CHUNKS * MOE_SEG) // MOE_CHUNK) * MOE_CHUNK


MOE_P32 = MOE_RPAD // (2 * MOE_SEG)
MOE_P16 = N_EXPERTS
MOE_PACK = 4096


def _seg_copies(pieces, sb, make_copy, act):
    p32_ref, n32_ref, p16_ref, n16_ref = pieces
    for p_ref, n_ref, width, rows in ((p32_ref, n32_ref, MOE_P32, 2 * MOE_SEG), (p16_ref, n16_ref, MOE_P16, MOE_SEG)):
        def body(k, carry, p_ref=p_ref, width=width, rows=rows):
            v = p_ref[sb * width + k]
            act(make_copy(pl.multiple_of(v & (MOE_PACK - 1), MOE_SEG),
                          pl.multiple_of(v >> (MOE_PACK.bit_length() - 1), MOE_SEG), rows))
            return carry
        lax.fori_loop(0, n_ref[sb], body, 0)


MOE_WAIT_SIZES = tuple(2 ** k for k in range((MOE_RPAD - 1).bit_length() - 1, 3, -1))


def _sub_block_rows(cnt_ref, loc_ref, sb):
    last = sb * N_EXPERTS + N_EXPERTS - 1
    return loc_ref[last] + cnt_ref[last]


def _wait_rows(rows, make_copy):
    for b in MOE_WAIT_SIZES:
        @pl.when((rows & b) != 0)
        def _(b=b):
            make_copy(b).wait()


def _build_onehot(gt, p_ref, cnt_ref, loc_ref, sb):
    t = gt.shape[1]
    selm = gt > 0.0
    r = lax.broadcasted_iota(jnp.int32, (t, t), 0)
    c = lax.broadcasted_iota(jnp.int32, (t, t), 1)
    upper = jnp.where(r < c, 1.0, 0.0).astype(BF16)
    rank = jnp.dot(jnp.where(selm, 1.0, 0.0).astype(BF16), upper, preferred_element_type=F32)
    key = jnp.where(selm, rank, -1.0)
    j16 = lax.broadcasted_iota(jnp.int32, (MOE_SEG, t), 0).astype(F32)
    p_ref[...] = jnp.zeros(p_ref.shape, BF16)

    for e in range(N_EXPERTS):
        off = loc_ref[sb * N_EXPERTS + e]
        n = cnt_ref[sb * N_EXPERTS + e]
        row = key[e:e + 1, :]

        def put(cidx, row=row, off=off):
            first = cidx * MOE_SEG
            firstf = float(first) if isinstance(first, int) else first.astype(F32)
            tile = jnp.where(row == j16 + firstf, 1.0, 0.0).astype(BF16)
            p_ref[pl.ds(pl.multiple_of(off + first, MOE_SEG), MOE_SEG), :] = tile

        for cidx in range(MOE_STATIC_CHUNKS):
            put(cidx)

        def extra(cidx, carry, put=put):
            put(cidx)
            return carry
        lax.fori_loop(MOE_STATIC_CHUNKS, n // MOE_SEG, extra, 0)


def _dispatch_kernel(cnt_ref, loc_ref, p32_ref, n32_ref, p16_ref, n16_ref, padoff_ref, padn_ref,
                     hp_ref, hs_ref, gt_ref, xe_ref, p_ref, xg_ref, z_ref, sem, *, nsb_p, n_sb):
    s = pl.program_id(0)
    slot = s % 2

    def copies(sb, slot_, act):
        def mk(off, g, b):
            return pltpu.make_async_copy(xg_ref.at[slot_, pl.ds(off, b)], xe_ref.at[pl.ds(g, b)], sem.at[slot_])
        _seg_copies((p32_ref, n32_ref, p16_ref, n16_ref), sb, mk, act)

    def wait_all(sb, slot_):
        _wait_rows(_sub_block_rows(cnt_ref, loc_ref, sb),
                   lambda b: pltpu.make_async_copy(xg_ref.at[slot_, pl.ds(0, b)], xe_ref.at[pl.ds(0, b)],
                                                   sem.at[slot_]))

    @pl.when(s >= 2)
    def _():
        wait_all(s - 2, slot)

    gt = gt_ref[...]
    _build_onehot(gt, p_ref, cnt_ref, loc_ref, s)
    x = jnp.where(s < nsb_p, hp_ref[...], hs_ref[...])
    gtt = gt.T
    ghi = gtt.astype(BF16)
    glo = (gtt - ghi.astype(F32)).astype(BF16)
    xext = jnp.concatenate([x, ghi, glo], axis=1)
    rows = _sub_block_rows(cnt_ref, loc_ref, s)
    for i in range(MOE_RPAD // MOE_CHUNK):
        @pl.when(i * MOE_CHUNK < rows)
        def _(i=i):
            xg_ref[slot, i * MOE_CHUNK:(i + 1) * MOE_CHUNK, :] = jnp.dot(
                p_ref[i * MOE_CHUNK:(i + 1) * MOE_CHUNK, :], xext, preferred_element_type=F32).astype(BF16)
    copies(s, slot, lambda cp: cp.start())

    @pl.when(s == n_sb - 1)
    def _():
        if n_sb >= 2:
            wait_all(s - 1, 1 - slot)
        wait_all(s, slot)
        z_ref[...] = jnp.zeros(z_ref.shape, BF16)

        def pads(act):
            def body(e, carry):
                n = padn_ref[e]
                off = padoff_ref[e]
                for b in MOE_DMA_SIZES:
                    @pl.when((n & b) != 0)
                    def _(b=b):
                        done = n & (-2 * b)
                        act(pltpu.make_async_copy(
                            z_ref.at[pl.ds(0, b)],
                            xe_ref.at[pl.ds(pl.multiple_of(off + done, MOE_SEG), b)], sem.at[0]))
                return carry
            lax.fori_loop(0, N_EXPERTS, body, 0)
        pads(lambda cp: cp.start())
        pads(lambda cp: cp.wait())


def _ffn_kernel(te_ref, tidx_ref, tcode_ref, xe_ref, wg_ref, wu_ref, wd_ref, ye_ref, wgub, wdb):
    i = pl.program_id(0)
    code = tcode_ref[i]

    @pl.when(code == 2)
    def _():
        wgub[:, :EXPERT_FF] = wg_ref[0].astype(BF16)
        wgub[:, EXPERT_FF:] = wu_ref[0].astype(BF16)
        wdb[...] = wd_ref[0].astype(BF16)

    @pl.when(code != 0)
    def _():
        e = te_ref[i]
        g = xe_ref[:, D_MODEL:].astype(F32)
        lane = lax.broadcasted_iota(jnp.int32, g.shape, 1)
        w = jnp.sum(jnp.where((lane == e) | (lane == e + N_EXPERTS), g, 0.0), axis=1, keepdims=True)
        ab = jnp.dot(xe_ref[:, :D_MODEL], wgub[...], preferred_element_type=F32)
        a = ab[:, :EXPERT_FF]
        act = a * jax.nn.sigmoid(a) * ab[:, EXPERT_FF:]
        y = jnp.dot(act.astype(BF16), wdb[...], preferred_element_type=F32)
        ye_ref[...] = (w * y).astype(BF16)


def _combine_kernel(cnt_ref, loc_ref, p32_ref, n32_ref, p16_ref, n16_ref, ye_ref, gt_ref, h_ref, x1_ref, mod_ref,
                    sg_ref, su_ref, sd_ref, fn_ref, o_ref, p_ref, ys_ref, acc_ref, sem, *, sb0, n_steps):
    i = pl.program_id(0)
    slot = i % 2
    sb = sb0 + i

    def copies(sb_, slot_, act):
        def mk(off, g, b):
            return pltpu.make_async_copy(ye_ref.at[pl.ds(g, b)], ys_ref.at[slot_, pl.ds(off, b)], sem.at[slot_])
        _seg_copies((p32_ref, n32_ref, p16_ref, n16_ref), sb_, mk, act)

    @pl.when(i == 0)
    def _():
        ys_ref[...] = jnp.zeros(ys_ref.shape, BF16)
        copies(sb, slot, lambda cp: cp.start())

    @pl.when(i + 1 < n_steps)
    def _():
        copies(sb + 1, 1 - slot, lambda cp: cp.start())

    acc_ref[...] = _swiglu_bf(h_ref[...], sg_ref[...], su_ref[...], sd_ref[...])
    _build_onehot(gt_ref[...], p_ref, cnt_ref, loc_ref, sb)
    rows = _sub_block_rows(cnt_ref, loc_ref, sb)
    _wait_rows(rows, lambda b: pltpu.make_async_copy(ye_ref.at[pl.ds(0, b)], ys_ref.at[slot, pl.ds(0, b)],
                                                     sem.at[slot]))
    for c in range(MOE_RPAD // MOE_CHUNK):
        @pl.when(c * MOE_CHUNK < rows)
        def _(c=c):
            acc_ref[...] += lax.dot_general(
                p_ref[c * MOE_CHUNK:(c + 1) * MOE_CHUNK, :], ys_ref[slot, c * MOE_CHUNK:(c + 1) * MOE_CHUNK, :],
                (((0,), (0,)), ((), ())), preferred_element_type=F32)
    gate2 = mod_ref[0][:, 5 * D_MODEL:6 * D_MODEL]
    x2 = x1_ref[...] + gate2 * acc_ref[...]
    ms = jnp.mean(x2 * x2, axis=-1, keepdims=True)
    o_ref[...] = x2 * lax.rsqrt(ms + NORM_EPS) * fn_ref[...]


def _moe_plan(gates_t):
    ne, n = gates_t.shape
    n_sb = n // MOE_SB
    cnt = jnp.sum((gates_t > 0.0).reshape(ne, n_sb, MOE_SB), axis=-1, dtype=jnp.int32).T
    cnt16 = (cnt + MOE_SEG - 1) // MOE_SEG * MOE_SEG
    loc = jnp.cumsum(cnt16, axis=1) - cnt16
    tot = jnp.sum(cnt16, axis=0)
    totp = (tot + MOE_TF - 1) // MOE_TF * MOE_TF
    ends = jnp.cumsum(totp)
    base = ends - totp
    goff = base[None, :] + jnp.cumsum(cnt16, axis=0) - cnt16
    rows_max = TOP_K * n + n_sb * ne * (MOE_SEG - 1) + ne * (MOE_TF - MOE_SEG)
    nt_max = -(-rows_max // MOE_TF)
    tiles = jnp.arange(nt_max, dtype=jnp.int32)
    used = ends[-1]
    valid = tiles * MOE_TF < used
    tidx = jnp.where(valid, tiles, jnp.maximum(used // MOE_TF - 1, 0))
    te = jnp.sum(ends[None, :] <= (tidx * MOE_TF)[:, None], axis=1, dtype=jnp.int32)
    te = jnp.minimum(te, ne - 1)
    first = jnp.concatenate([jnp.ones((1,), jnp.bool_), te[1:] != te[:-1]])
    tcode = jnp.where(valid, 1 + first.astype(jnp.int32), 0)

    def piece_list(npieces, first_row, width, rows):
        cum = jnp.cumsum(npieces, axis=1)
        k = jnp.arange(width, dtype=jnp.int32)
        ek = jnp.minimum(jnp.sum(cum[:, None, :] <= k[None, :, None], axis=2, dtype=jnp.int32), ne - 1)
        onehot = ek[:, :, None] == jnp.arange(ne, dtype=jnp.int32)[None, None, :]
        pick = lambda a: jnp.sum(jnp.where(onehot, a[:, None, :], 0), axis=2, dtype=jnp.int32)
        row = pick(first_row) + rows * (k[None, :] - pick(cum - npieces))
        packed = (pick(goff) + row) * MOE_PACK + pick(loc) + row
        return packed.reshape(-1).astype(jnp.int32), cum[:, -1].astype(jnp.int32)

    n32 = cnt16 // (2 * MOE_SEG)
    p32, t32 = piece_list(n32, jnp.zeros_like(cnt16), MOE_P32, 2 * MOE_SEG)
    p16, t16 = piece_list(cnt16 // MOE_SEG % 2, n32 * (2 * MOE_SEG), MOE_P16, MOE_SEG)
    return dict(cnt=cnt16.reshape(-1), loc=loc.reshape(-1).astype(jnp.int32), pieces=(p32, t32, p16, t16),
                padoff=(base + tot).astype(jnp.int32), padn=(totp - tot).astype(jnp.int32),
                te=te, tidx=tidx.astype(jnp.int32), tcode=tcode, nt_max=nt_max)


def _dispatch_call(plan, h2_p, h2_s, gates_t):
    nsb_p = h2_p.shape[0] // MOE_SB
    n_sb = gates_t.shape[1] // MOE_SB
    grid_spec = pltpu.PrefetchScalarGridSpec(
        num_scalar_prefetch=8, grid=(n_sb,),
        in_specs=[pl.BlockSpec((MOE_SB, D_MODEL), lambda s, *_: (jnp.minimum(s, nsb_p - 1), 0)),
                  pl.BlockSpec((MOE_SB, D_MODEL), lambda s, *_: (jnp.maximum(s - nsb_p, 0), 0)),
                  pl.BlockSpec((N_EXPERTS, MOE_SB), lambda s, *_: (0, s))],
        out_specs=pl.BlockSpec(memory_space=pl.ANY),
        scratch_shapes=[pltpu.VMEM((MOE_RPAD, MOE_SB), BF16),
                        pltpu.VMEM((2, MOE_RPAD, MOE_XW), BF16),
                        pltpu.VMEM((MOE_DMA_SIZES[0], MOE_XW), BF16),
                        pltpu.SemaphoreType.DMA((2,))])
    return pl.pallas_call(
        functools.partial(_dispatch_kernel, nsb_p=nsb_p, n_sb=n_sb),
        grid_spec=grid_spec,
        out_shape=jax.ShapeDtypeStruct((plan['nt_max'] * MOE_TF, MOE_XW), BF16),
        compiler_params=_cparams(("arbitrary",)),
        name="moe_dispatch",
    )(plan['cnt'], plan['loc'], *plan['pieces'], plan['padoff'], plan['padn'], h2_p, h2_s, gates_t)


def _ffn_call(plan, xe, wg, wu, wd):
    grid_spec = pltpu.PrefetchScalarGridSpec(
        num_scalar_prefetch=3, grid=(plan['nt_max'],),
        in_specs=[pl.BlockSpec((MOE_TF, MOE_XW), lambda i, te, tidx, tv: (tidx[i], 0)),
                  pl.BlockSpec((1, D_MODEL, EXPERT_FF), lambda i, te, tidx, tv: (te[i], 0, 0)),
                  pl.BlockSpec((1, D_MODEL, EXPERT_FF), lambda i, te, tidx, tv: (te[i], 0, 0)),
                  pl.BlockSpec((1, EXPERT_FF, D_MODEL), lambda i, te, tidx, tv: (te[i], 0, 0))],
        out_specs=pl.BlockSpec((MOE_TF, D_MODEL), lambda i, te, tidx, tv: (tidx[i], 0)),
        scratch_shapes=[pltpu.VMEM((D_MODEL, 2 * EXPERT_FF), BF16),
                        pltpu.VMEM((EXPERT_FF, D_MODEL), BF16)])
    return pl.pallas_call(
        _ffn_kernel,
        grid_spec=grid_spec,
        out_shape=jax.ShapeDtypeStruct((xe.shape[0], D_MODEL), BF16),
        compiler_params=_cparams(("arbitrary",)),
        name="moe_ffn",
    )(plan['te'], plan['tidx'], plan['tcode'], xe, wg, wu, wd)


def _combine_call(plan, ye, gates_t, h2, x1, mod3, sg, su, sd, final_w, sb0, seq_len):
    n = h2.shape[0]
    n_steps = n // MOE_SB
    bpb = seq_len // MOE_SB
    row = lambda i, *_: (i, 0)
    const2 = lambda i, *_: (0, 0)
    grid_spec = pltpu.PrefetchScalarGridSpec(
        num_scalar_prefetch=6, grid=(n_steps,),
        in_specs=[pl.BlockSpec(memory_space=pl.ANY),
                  pl.BlockSpec((N_EXPERTS, MOE_SB), lambda i, *_: (0, sb0 + i)),
                  pl.BlockSpec((MOE_SB, D_MODEL), row),
                  pl.BlockSpec((MOE_SB, D_MODEL), row),
                  pl.BlockSpec((1, 1, 6 * D_MODEL), lambda i, *_: (i // bpb, 0, 0)),
                  pl.BlockSpec((D_MODEL, EXPERT_FF), const2),
                  pl.BlockSpec((D_MODEL, EXPERT_FF), const2),
                  pl.BlockSpec((EXPERT_FF, D_MODEL), const2),
                  pl.BlockSpec((1, D_MODEL), const2)],
        out_specs=pl.BlockSpec((MOE_SB, D_MODEL), row),
        scratch_shapes=[pltpu.VMEM((MOE_RPAD, MOE_SB), BF16),
                        pltpu.VMEM((2, MOE_RPAD, D_MODEL), BF16),
                        pltpu.VMEM((MOE_SB, D_MODEL), F32),
                        pltpu.SemaphoreType.DMA((2,))])
    return pl.pallas_call(
        functools.partial(_combine_kernel, sb0=sb0, n_steps=n_steps),
        grid_spec=grid_spec,
        out_shape=jax.ShapeDtypeStruct((n, D_MODEL), F32),
        compiler_params=_cparams(("arbitrary",)),
        name="moe_combine",
    )(plan['cnt'], plan['loc'], *plan['pieces'], ye, gates_t, h2, x1, mod3, sg, su, sd,
      final_w.reshape(1, D_MODEL))


def _moe_sparse(h2_p, h2_s, gt_p, gt_s, x1_p, x1_s, mod_p, mod_s, wg, wu, wd, sg, su, sd, final_w, dseq):
    gates_t = jnp.concatenate([gt_p, gt_s], axis=1)
    plan = _moe_plan(gates_t)
    xe = _dispatch_call(plan, h2_p, h2_s, gates_t)
    ye = _ffn_call(plan, xe, wg, wu, wd)
    n_p = h2_p.shape[0]
    y_p = _combine_call(plan, ye, gates_t, h2_p, x1_p, mod_p, sg, su, sd, final_w, 0, n_p)
    y_s = _combine_call(plan, ye, gates_t, h2_s, x1_s, mod_s, sg, su, sd, final_w, n_p // MOE_SB, dseq)
    return y_p, y_s


def _rope_tables(n_tokens):
    rows = n_tokens // GRID_W
    row = jnp.repeat(jnp.arange(rows, dtype=F32), GRID_W)
    col = jnp.tile(jnp.arange(GRID_W, dtype=F32), rows)
    freqs = ROPE_THETA ** (-jnp.arange(ROT_PAIRS, dtype=F32) / ROT_PAIRS)
    ar = row[:, None] * freqs
    ac = col[:, None] * freqs
    cos = jnp.concatenate([jnp.cos(ar), jnp.cos(ar), jnp.cos(ac), jnp.cos(ac)], axis=1)
    sin = jnp.concatenate([-jnp.sin(ar), jnp.sin(ar), -jnp.sin(ac), jnp.sin(ac)], axis=1)
    return jnp.tile(cos, (1, 2)), jnp.tile(sin, (1, 2))


def kernel(x_prompt, x_sample, c, cache_k, cache_v, state_ssm_re, state_ssm_im, c_ctx, w_ada, b_ada, norm1_w, w_in, ssm_lambda_re, ssm_lambda_im, ssm_log_dt, ssm_b_re, ssm_b_im, ssm_c_re, ssm_c_im, ssm_d, ssm_w_glu, diff_lambda_q, diff_lambda_k, diff_subln_w, w_out, norm2_w, w_router, router_bias, w_exp_gate, w_exp_up, w_exp_down, w_sh_gate, w_sh_up, w_sh_down, final_norm_w):
    depth = w_ada.shape[0]
    assert depth == 1
    l = 0
    lambda_init = 0.8 - 0.6 * math.exp(-0.3 * l)
    bsz, seq, _ = x_prompt.shape
    dbs, dseq, _ = x_sample.shape
    n_p, n_s = bsz * seq, dbs * dseq

    cond8 = jnp.zeros((SUBLANES, D_MODEL), F32).at[:dbs].set(c).at[dbs].set(c_ctx)
    mod = _ada_call(cond8, w_ada[l], b_ada[l])
    mod_s = mod[:dbs].reshape(dbs, 1, 6 * D_MODEL)
    mod_p = mod[dbs:dbs + 1].reshape(1, 1, 6 * D_MODEL)

    w_in_bf = w_in[l].astype(BF16)
    xp2 = x_prompt.reshape(n_p, D_MODEL)
    xs2 = x_sample.reshape(n_s, D_MODEL)
    u_p, q_p, k_p, v_p = _inproj_call(xp2, mod_p, norm1_w[l], w_in_bf, None, n_p, F32)
    u_s, q_s, k_s, v_s = _inproj_call(xs2, mod_s, norm1_w[l], w_in_bf, _rope_tables(dseq), dseq, BF16)

    bblk, cblk, lam_r, lam_i = _s5_weights(ssm_lambda_re[l], ssm_lambda_im[l], ssm_log_dt[l],
                                           ssm_b_re[l], ssm_b_im[l], ssm_c_re[l], ssm_c_im[l])
    lr_p = jnp.broadcast_to(lam_r[:, :, None, :], (2, N_COLBLK, SUBLANES, S5_BLK_STATES))
    li_p = jnp.broadcast_to(lam_i[:, :, None, :], (2, N_COLBLK, SUBLANES, S5_BLK_STATES))
    g_p = bsz // SUBLANES
    h0_p = jnp.zeros((2, g_p, N_COLBLK * 8, SUBLANES, LANES), F32)
    y_p, hfin = _s5_call(u_p.reshape(bsz, seq, S5_WIDTH), bblk, cblk, lr_p, li_p, h0_p, SUBLANES)

    def halves(x):
        x = x.reshape(2, 2, 2, 1, S5_BLK_STATES)
        x = jnp.broadcast_to(x, (2, 2, 2, dbs, S5_BLK_STATES))
        return x.transpose(0, 2, 1, 3, 4).reshape(2, 2, 2 * dbs, S5_BLK_STATES)

    def h0_tiles(s):
        s = s.astype(F32).reshape(dbs, 2, 2, 2, 4, LANES)
        return s.transpose(1, 3, 4, 2, 0, 5).reshape(2, 1, 2, 4, 2 * dbs, LANES)

    h0_s = jnp.concatenate([h0_tiles(state_ssm_re[:, l]), h0_tiles(state_ssm_im[:, l])], axis=3)
    h0_s = h0_s.reshape(2, 1, 16, SUBLANES, LANES)
    y_s, _ = _s5_call(u_s.reshape(dbs, dseq, S5_WIDTH), bblk, cblk, halves(lam_r), halves(lam_i), h0_s, dbs,
                      tb=128)

    lq = diff_lambda_q[l].astype(F32)
    lk = diff_lambda_k[l].astype(F32)
    lam = (jnp.exp(jnp.sum(lq[0] * lk[0])) - jnp.exp(jnp.sum(lq[1] * lk[1])) + lambda_init).reshape(1)
    subln = diff_subln_w[l].astype(F32).reshape(1, VALUE_DIM)
    ao_p = _attn_ctx_call(lam, q_p.reshape(bsz, seq, ATTN_WIDTH), k_p.reshape(bsz, seq, ATTN_WIDTH),
                          v_p.reshape(bsz, seq, ATTN_WIDTH), subln, lambda_init)
    past = cache_k.shape[2]
    ao_s = _attn_lat_call(lam, q_s.reshape(dbs, dseq, ATTN_WIDTH), k_s.reshape(dbs, dseq, ATTN_WIDTH),
                          v_s.reshape(dbs, dseq, ATTN_WIDTH),
                          cache_k[:, l].reshape(dbs, past, ATTN_WIDTH).astype(BF16),
                          cache_v[:, l].reshape(dbs, past, ATTN_WIDTH).astype(BF16), subln, lambda_init)

    wglu_bf = ssm_w_glu[l].astype(BF16)
    wout_bf = w_out[l].astype(BF16)
    wr_t = w_router[l].astype(F32).T
    rbias3 = router_bias[l].astype(F32).reshape(N_EXPERT_GROUPS, N_EXPERTS // N_EXPERT_GROUPS, 1)
    x1_p, h2_p, gates_p = _post_call(xp2, u_p, y_p, ao_p.reshape(n_p, ATTN_WIDTH), mod_p, ssm_d[l],
                                     wglu_bf, wout_bf, norm2_w[l], wr_t, rbias3, n_p)
    x1_s, h2_s, gates_s = _post_call(xs2, u_s, y_s, ao_s.reshape(n_s, ATTN_WIDTH), mod_s, ssm_d[l],
                                     wglu_bf, wout_bf, norm2_w[l], wr_t, rbias3, dseq)

    wg = w_exp_gate[l]
    wu = w_exp_up[l]
    wd = w_exp_down[l]
    sg = w_sh_gate[l].astype(BF16)
    su = w_sh_up[l].astype(BF16)
    sd = w_sh_down[l].astype(BF16)
    y_prompt, y_sample = _moe_sparse(h2_p, h2_s, gates_p, gates_s, x1_p, x1_s, mod_p, mod_s,
                                     wg, wu, wd, sg, su, sd, final_norm_w, dseq)

    new_cache_k = k_p.reshape(bsz, 1, seq, N_DIFF_HEADS, VALUE_DIM)
    new_cache_v = v_p.reshape(bsz, 1, seq, N_DIFF_HEADS, VALUE_DIM)
    hf = hfin.reshape(2, g_p, N_COLBLK, 2, 4, SUBLANES, LANES)
    hf = hf.transpose(3, 1, 5, 0, 2, 4, 6).reshape(2, bsz, 1, 2, S5_GROUPS, S5_STATE)
    return (y_prompt.reshape(bsz, seq, D_MODEL), y_sample.reshape(dbs, dseq, D_MODEL),
            new_cache_k, new_cache_v, hf[0], hf[1])
```

```python
import functools
import math

import jax
import jax.numpy as jnp
from jax import lax
from jax.experimental import pallas as pl
from jax.experimental.pallas import tpu as pltpu

F32 = jnp.float32
BF16 = jnp.bfloat16

D_MODEL = 1024
GRID_W = 64
S5_WIDTH = 512
S5_GROUP = 16
S5_GROUPS = 32
S5_STATE = 64
ATTN_WIDTH = 512
DIFF_HEAD_DIM = 64
VALUE_DIM = 128
N_DIFF_HEADS = 4
IN_WIDTH = S5_WIDTH + 3 * ATTN_WIDTH
ROT_PAIRS = DIFF_HEAD_DIM // 4
ROPE_THETA = 10000.0
N_EXPERTS = 64
TOP_K = 8
N_EXPERT_GROUPS = 8
TOPK_GROUPS = 4
EXPERT_FF = 256
ROUTED_SCALE = 2.5
NORM_EPS = 1e-6

LANES = 128
SUBLANES = 8
S5_COLBLK = 8 * S5_GROUP
S5_BLK_STATES = 8 * S5_STATE
N_COLBLK = S5_WIDTH // S5_COLBLK
VMEM_LIMIT = 56 * 1024 * 1024


def _cparams(sem):
    return pltpu.CompilerParams(dimension_semantics=sem, vmem_limit_bytes=VMEM_LIMIT)


def _ada_kernel(cond_ref, w_ref, b_ref, o_ref):
    c = cond_ref[...]
    s = c * jax.nn.sigmoid(c)
    o_ref[...] = jnp.dot(s.astype(BF16), w_ref[...].astype(BF16),
                         preferred_element_type=F32) + b_ref[...]


def _ada_call(cond8, w_ada, b_ada):
    n = w_ada.shape[1]
    tn = 1536
    return pl.pallas_call(
        _ada_kernel,
        grid=(n // tn,),
        in_specs=[pl.BlockSpec((SUBLANES, D_MODEL), lambda j: (0, 0)),
                  pl.BlockSpec((D_MODEL, tn), lambda j: (0, j)),
                  pl.BlockSpec((1, tn), lambda j: (0, j))],
        out_specs=pl.BlockSpec((SUBLANES, tn), lambda j: (0, j)),
        out_shape=jax.ShapeDtypeStruct((SUBLANES, n), F32),
        compiler_params=_cparams(("arbitrary",)),
        name="adaln",
    )(cond8, w_ada, b_ada.reshape(1, n))


def _rope_apply(t, cos, sin):
    parts = []
    for cidx in range(ATTN_WIDTH // LANES):
        xc = t[:, cidx * LANES:(cidx + 1) * LANES]
        up = pltpu.roll(xc, LANES - ROT_PAIRS, 1)
        dn = pltpu.roll(xc, ROT_PAIRS, 1)
        lane = lax.broadcasted_iota(jnp.int32, xc.shape, 1)
        partner = jnp.where((lane % (2 * ROT_PAIRS)) < ROT_PAIRS, up, dn)
        parts.append(xc * cos + partner * sin)
    return jnp.concatenate(parts, axis=1)


def _inproj_kernel(*refs, rope, kv_dtype):
    if rope:
        x_ref, mod_ref, n1_ref, w_ref, cos_ref, sin_ref, u_ref, q_ref, k_ref, v_ref = refs
    else:
        x_ref, mod_ref, n1_ref, w_ref, u_ref, q_ref, k_ref, v_ref = refs
    x = x_ref[...]
    ms = jnp.mean(x * x, axis=-1, keepdims=True)
    xn = x * lax.rsqrt(ms + NORM_EPS) * n1_ref[...]
    mod = mod_ref[0]
    shift = mod[:, 0:D_MODEL]
    scale = mod[:, D_MODEL:2 * D_MODEL]
    h = xn * (1.0 + scale) + shift
    proj = jnp.dot(h.astype(BF16), w_ref[...], preferred_element_type=F32)
    u_ref[...] = proj[:, :S5_WIDTH]
    q = proj[:, S5_WIDTH:S5_WIDTH + ATTN_WIDTH]
    k = proj[:, S5_WIDTH + ATTN_WIDTH:S5_WIDTH + 2 * ATTN_WIDTH]
    v = proj[:, S5_WIDTH + 2 * ATTN_WIDTH:]
    if rope:
        cos = cos_ref[...]
        sin = sin_ref[...]
        q = _rope_apply(q, cos, sin)
        k = _rope_apply(k, cos, sin)
    q_ref[...] = (q * (DIFF_HEAD_DIM ** -0.5)).astype(BF16)
    k_ref[...] = k.astype(kv_dtype)
    v_ref[...] = v.astype(kv_dtype)


def _inproj_call(x2, mod3, norm1_w, w_in_bf, rope_tabs, seq_len, kv_dtype, tm=512):
    n = x2.shape[0]
    bpb = seq_len // tm
    rope = rope_tabs is not None
    in_specs = [pl.BlockSpec((tm, D_MODEL), lambda i: (i, 0)),
                pl.BlockSpec((1, 1, 6 * D_MODEL), lambda i: (i // bpb, 0, 0)),
                pl.BlockSpec((1, D_MODEL), lambda i: (0, 0)),
                pl.BlockSpec((D_MODEL, IN_WIDTH), lambda i: (0, 0))]
    args = [x2, mod3, norm1_w.reshape(1, D_MODEL), w_in_bf]
    if rope:
        in_specs += [pl.BlockSpec((tm, LANES), lambda i: (i % bpb, 0)),
                     pl.BlockSpec((tm, LANES), lambda i: (i % bpb, 0))]
        args += list(rope_tabs)
    ospec = pl.BlockSpec((tm, S5_WIDTH), lambda i: (i, 0))
    return pl.pallas_call(
        functools.partial(_inproj_kernel, rope=rope, kv_dtype=kv_dtype),
        grid=(n // tm,),
        in_specs=in_specs,
        out_specs=[ospec, ospec, ospec, ospec],
        out_shape=[jax.ShapeDtypeStruct((n, S5_WIDTH), F32),
                   jax.ShapeDtypeStruct((n, ATTN_WIDTH), BF16),
                   jax.ShapeDtypeStruct((n, ATTN_WIDTH), kv_dtype),
                   jax.ShapeDtypeStruct((n, ATTN_WIDTH), kv_dtype)],
        compiler_params=_cparams(("parallel",)),
        name="inproj_rope" if rope else "inproj",
    )(*args)


def _s5_slot(b, q, nseq):
    if nseq == SUBLANES:
        return b, q * 8
    half = q // 2
    return half * nseq + b, (q % 2) * 8


def _s5_kernel(u_ref, b_ref, c_ref, lr_ref, li_ref, h0_ref, y_ref, hfin_ref, buh, hs, hst,
               *, nseq, tb, stride, nqq):
    d = pl.program_id(1)
    c = pl.program_id(2)
    nc = pl.num_programs(2)

    @pl.when(c == 0)
    def _():
        hst[...] = h0_ref[0, 0]
        hs[...] = jnp.zeros(hs.shape, F32)

    u = u_ref[...].reshape(nseq * tb, S5_WIDTH).astype(BF16)
    for q in range(N_COLBLK):
        res = jnp.dot(u[:, q * S5_COLBLK:(q + 1) * S5_COLBLK], b_ref[0, q],
                      preferred_element_type=F32)
        for b in range(nseq):
            slot, ct0 = _s5_slot(b, q, nseq)
            for j in range(8):
                buh[ct0 + j, slot * stride:slot * stride + tb, :] = (
                    res[b * tb:(b + 1) * tb, j * LANES:(j + 1) * LANES])

    for qq in range(nqq):
        lr = [lr_ref[0, qq, :, i * LANES:(i + 1) * LANES] for i in range(4)]
        li = [li_ref[0, qq, :, i * LANES:(i + 1) * LANES] for i in range(4)]
        h_init = tuple(hst[qq * 8 + i] for i in range(8))

        def body(s, carry, qq=qq, lr=lr, li=li):
            t = jnp.where(d == 0, s, tb - 1 - s)
            idx = pl.ds(t, SUBLANES, stride=stride)
            new_r, new_i = [], []
            for i in range(4):
                hr, hi = carry[i], carry[4 + i]
                bur = buh[qq * 8 + i, idx, :]
                bui = buh[qq * 8 + 4 + i, idx, :]
                nr = lr[i] * hr - li[i] * hi + bur
                ni = lr[i] * hi + li[i] * hr + bui
                hs[qq * 8 + i, idx, :] = nr
                hs[qq * 8 + 4 + i, idx, :] = ni
                new_r.append(nr)
                new_i.append(ni)
            return tuple(new_r) + tuple(new_i)

        h_fin = lax.fori_loop(0, tb, body, h_init, unroll=4)
        for i in range(8):
            hst[qq * 8 + i] = h_fin[i]

    rows = nseq * stride
    for q in range(N_COLBLK):
        slot0, ct0 = _s5_slot(0, q, nseq)
        hmat = jnp.concatenate(
            [hs[ct0 + j, slot0 * stride:slot0 * stride + rows, :] for j in range(8)], axis=1)
        yq = jnp.dot(hmat.astype(BF16), c_ref[0, q], preferred_element_type=F32)
        for b in range(nseq):
            y_ref[0, b, :, q * S5_COLBLK:(q + 1) * S5_COLBLK] = yq[b * stride:b * stride + tb, :]

    @pl.when(c == nc - 1)
    def _():
        hfin_ref[0, 0] = hst[...]


def _s5_call(u3, bblk, cblk, lr, li, h0, nseq, tb=64):
    nseq_total, seq_len, _ = u3.shape
    g = nseq_total // nseq
    nc = seq_len // tb
    stride = tb + SUBLANES
    nqq = lr.shape[1]
    nct = nqq * 8

    def tmap(gi, d, c):
        return c + d * (nc - 1 - 2 * c)

    return pl.pallas_call(
        functools.partial(_s5_kernel, nseq=nseq, tb=tb, stride=stride, nqq=nqq),
        grid=(g, 2, nc),
        in_specs=[
            pl.BlockSpec((nseq, tb, S5_WIDTH), lambda gi, d, c: (gi, tmap(gi, d, c), 0)),
            pl.BlockSpec((1, N_COLBLK, S5_COLBLK, 2 * S5_BLK_STATES), lambda gi, d, c: (d, 0, 0, 0)),
            pl.BlockSpec((1, N_COLBLK, 2 * S5_BLK_STATES, S5_COLBLK), lambda gi, d, c: (d, 0, 0, 0)),
            pl.BlockSpec((1, nqq, SUBLANES, S5_BLK_STATES), lambda gi, d, c: (d, 0, 0, 0)),
            pl.BlockSpec((1, nqq, SUBLANES, S5_BLK_STATES), lambda gi, d, c: (d, 0, 0, 0)),
            pl.BlockSpec((1, 1, nct, SUBLANES, LANES), lambda gi, d, c: (d, gi, 0, 0, 0)),
        ],
        out_specs=[
            pl.BlockSpec((1, nseq, tb, S5_WIDTH), lambda gi, d, c: (d, gi, tmap(gi, d, c), 0)),
            pl.BlockSpec((1, 1, nct, SUBLANES, LANES), lambda gi, d, c: (d, gi, 0, 0, 0)),
        ],
        out_shape=[jax.ShapeDtypeStruct((2, nseq_total, seq_len, S5_WIDTH), F32),
                   jax.ShapeDtypeStruct((2, g, nct, SUBLANES, LANES), F32)],
        scratch_shapes=[pltpu.VMEM((nct, SUBLANES * stride, LANES), F32),
                        pltpu.VMEM((nct, SUBLANES * stride, LANES), F32),
                        pltpu.VMEM((nct, SUBLANES, LANES), F32)],
        compiler_params=_cparams(("parallel", "parallel", "arbitrary")),
        name="s5_scan_%d" % nseq,
    )(u3, bblk, cblk, lr, li, h0)


def _s5_weights(lam_re, lam_im, log_dt, b_re, b_im, c_re, c_im):
    dt = jnp.exp(log_dt.astype(F32))[..., None]
    lr = lam_re.astype(F32)
    li = lam_im.astype(F32)
    mag = jnp.exp(lr * dt)
    lbr = mag * jnp.cos(li * dt)
    lbi = mag * jnp.sin(li * dt)
    a = lbr - 1.0
    den = lr * lr + li * li
    cr = (a * lr + lbi * li) / den
    ci = (lbi * lr - a * li) / den
    bbr = cr[..., None] * b_re - ci[..., None] * b_im
    bbi = cr[..., None] * b_im + ci[..., None] * b_re
    eye = jnp.eye(8, dtype=F32)

    def pack_b(x):
        x = x.reshape(2, N_COLBLK, 8, S5_STATE, S5_GROUP).transpose(0, 1, 2, 4, 3)
        x = jnp.einsum('dqghn,gk->dqghkn', x, eye)
        return x.reshape(2, N_COLBLK, S5_COLBLK, S5_BLK_STATES)

    def pack_c(x):
        x = x.reshape(2, N_COLBLK, 8, S5_GROUP, S5_STATE).transpose(0, 1, 2, 4, 3)
        x = jnp.einsum('dqgnh,gk->dqgnkh', x, eye)
        return x.reshape(2, N_COLBLK, S5_BLK_STATES, S5_COLBLK)

    bblk = jnp.concatenate([pack_b(bbr), pack_b(bbi)], axis=-1).astype(BF16)
    cblk = jnp.concatenate([pack_c(c_re.astype(F32)), pack_c(-c_im.astype(F32))], axis=-2).astype(BF16)
    lam_r = lbr.reshape(2, N_COLBLK, S5_BLK_STATES)
    lam_i = lbi.reshape(2, N_COLBLK, S5_BLK_STATES)
    return bblk, cblk, lam_r, lam_i


ATTN_KC = 512


def _lane_fold(x, op):
    acc = x[:, 0:LANES]
    for j in range(1, x.shape[1] // LANES):
        acc = op(acc, x[:, j * LANES:(j + 1) * LANES])
    return acc


def _attn_ops(q, chunks, s_ref):
    lane = lax.broadcasted_iota(jnp.int32, q.shape, 1)
    dn = (((1,), (1,)), ((), ()))
    qms = [jnp.where(lane < DIFF_HEAD_DIM, q, jnp.zeros_like(q)),
           jnp.where(lane >= DIFF_HEAD_DIM, q, jnp.zeros_like(q))]
    offs = [0]
    for load_k, _ in chunks:
        offs.append(offs[-1] + load_k().shape[0])

    def score_chunk(m, ci, m128):
        s = lax.dot_general(qms[m], chunks[ci][0](), dn, preferred_element_type=F32)
        s_ref[m, :, offs[ci]:offs[ci + 1]] = s
        f = _lane_fold(s, jnp.maximum)
        return f if m128 is None else jnp.maximum(m128, f)

    def value_chunk(m, ci, mx, l128, o):
        p = jnp.exp(s_ref[m, :, offs[ci]:offs[ci + 1]] - mx)
        f = _lane_fold(p, jnp.add)
        t = jnp.dot(p.astype(BF16), chunks[ci][1](), preferred_element_type=F32)
        return (f if l128 is None else l128 + f), (t if o is None else o + t)

    return score_chunk, value_chunk


def _row_max(m128):
    return m128.max(axis=-1, keepdims=True)


def _normalized(l128, o):
    return o * (1.0 / l128.sum(axis=-1, keepdims=True))


def _sub_ln(o, subln, lambda_init):
    ms = jnp.mean(o * o, axis=-1, keepdims=True)
    return o * lax.rsqrt(ms + NORM_EPS) * subln * (1.0 - lambda_init)


def _attn_ctx_kernel(lam_ref, q_ref, k_ref, v_ref, w_ref, o_ref, s_ref, *, lambda_init):
    chunks = [(lambda: k_ref[0].astype(BF16), lambda: v_ref[0].astype(BF16))]
    score_chunk, value_chunk = _attn_ops(q_ref[0], chunks, s_ref)
    mx0 = _row_max(score_chunk(0, 0, None))
    m1 = score_chunk(1, 0, None)
    l0, o0 = value_chunk(0, 0, mx0, None, None)
    l1, o1 = value_chunk(1, 0, _row_max(m1), None, None)
    o = _normalized(l0, o0) - lam_ref[0] * _normalized(l1, o1)
    o_ref[0] = _sub_ln(o, w_ref[...], lambda_init).astype(BF16)


def _attn_lat_kernel(lam_ref, q_ref, k_ref, v_ref, ck_ref, cv_ref, w_ref, o_ref, s_ref, *, lambda_init):
    chunks = []
    for c in range(k_ref.shape[1] // ATTN_KC):
        chunks.append((lambda c=c: k_ref[0, c * ATTN_KC:(c + 1) * ATTN_KC, :],
                       lambda c=c: v_ref[0, c * ATTN_KC:(c + 1) * ATTN_KC, :]))
    for c in range(ck_ref.shape[1] // ATTN_KC):
        chunks.append((lambda c=c: ck_ref[0, c * ATTN_KC:(c + 1) * ATTN_KC, :],
                       lambda c=c: cv_ref[0, c * ATTN_KC:(c + 1) * ATTN_KC, :]))
    nchunk = len(chunks)
    score_chunk, value_chunk = _attn_ops(q_ref[0], chunks, s_ref)

    m0 = m1 = l0 = o0 = l1 = o1 = None
    for ci in range(nchunk):
        m0 = score_chunk(0, ci, m0)
    mx0 = _row_max(m0)
    for ci in range(nchunk):
        m1 = score_chunk(1, ci, m1)
        l0, o0 = value_chunk(0, ci, mx0, l0, o0)
    mx1 = _row_max(m1)
    for ci in range(nchunk):
        l1, o1 = value_chunk(1, ci, mx1, l1, o1)
    o = _normalized(l0, o0) - lam_ref[0] * _normalized(l1, o1)
    o_ref[0] = _sub_ln(o, w_ref[...], lambda_init).astype(BF16)


def _attn_ctx_call(lam, q3, k3, v3, subln, lambda_init):
    bsz, seq_len, _ = q3.shape
    spec = pl.BlockSpec((1, seq_len, VALUE_DIM), lambda b, h: (b, 0, h))
    return pl.pallas_call(
        functools.partial(_attn_ctx_kernel, lambda_init=lambda_init),
        grid=(bsz, N_DIFF_HEADS),
        in_specs=[pl.BlockSpec(memory_space=pltpu.SMEM), spec, spec, spec,
                  pl.BlockSpec((1, VALUE_DIM), lambda b, h: (0, 0))],
        out_specs=spec,
        out_shape=jax.ShapeDtypeStruct((bsz, seq_len, ATTN_WIDTH), BF16),
        scratch_shapes=[pltpu.VMEM((2, seq_len, seq_len), F32)],
        compiler_params=_cparams(("parallel", "parallel")),
        name="attn_ctx",
    )(lam, q3, k3, v3, subln)


def _attn_lat_call(lam, q3, k3, v3, ck3, cv3, subln, lambda_init, tq=256):
    bsz, seq_len, _ = q3.shape
    ctx_len = ck3.shape[1]
    qspec = pl.BlockSpec((1, tq, VALUE_DIM), lambda b, h, i: (b, i, h))
    kspec = pl.BlockSpec((1, seq_len, VALUE_DIM), lambda b, h, i: (b, 0, h))
    cspec = pl.BlockSpec((1, ctx_len, VALUE_DIM), lambda b, h, i: (b, 0, h))
    return pl.pallas_call(
        functools.partial(_attn_lat_kernel, lambda_init=lambda_init),
        grid=(bsz, N_DIFF_HEADS, seq_len // tq),
        in_specs=[pl.BlockSpec(memory_space=pltpu.SMEM), qspec, kspec, kspec, cspec, cspec,
                  pl.BlockSpec((1, VALUE_DIM), lambda b, h, i: (0, 0))],
        out_specs=qspec,
        out_shape=jax.ShapeDtypeStruct((bsz, seq_len, ATTN_WIDTH), BF16),
        scratch_shapes=[pltpu.VMEM((2, tq, seq_len + ctx_len), F32)],
        compiler_params=_cparams(("parallel", "parallel", "arbitrary")),
        name="attn_lat",
    )(lam, q3, k3, v3, ck3, cv3, subln)


def _route(logits_t, bias3, tm):
    ng, ge = N_EXPERT_GROUPS, N_EXPERTS // N_EXPERT_GROUPS
    neg = jnp.float32(-jnp.inf)
    sc = jax.nn.sigmoid(logits_t).reshape(ng, ge, tm)
    bi = sc + bias3
    eio = lax.broadcasted_iota(jnp.int32, (ng, ge, tm), 1).astype(F32)
    gio = lax.broadcasted_iota(jnp.int32, (ng, ge, tm), 0).astype(F32)
    m1 = bi.max(axis=1, keepdims=True)
    i1 = jnp.where(bi == m1, eio, float(ge)).min(axis=1, keepdims=True)
    m2 = jnp.where(eio == i1, neg, bi).max(axis=1, keepdims=True)
    gs = jnp.broadcast_to(m1 + m2, (ng, ge, tm))
    cnt = jnp.zeros((ng, ge, tm), F32)
    for g2 in range(ng):
        o = gs[g2:g2 + 1]
        better = (o > gs) | ((o == gs) & (gio > float(g2)))
        cnt = cnt + jnp.where(better, 1.0, 0.0)
    v = jnp.where(cnt < float(TOPK_GROUPS), bi, neg)
    eidx = gio * float(ge) + eio
    selm = jnp.zeros((ng, ge, tm), F32)
    for _ in range(TOP_K):
        m = v.max(axis=0, keepdims=True).max(axis=1, keepdims=True)
        ix = jnp.where(v == m, eidx, float(N_EXPERTS)).min(axis=0, keepdims=True).min(axis=1, keepdims=True)
        oh = eidx == ix
        selm = jnp.where(oh, 1.0, selm)
        v = jnp.where(oh, neg, v)
    selsc = selm * sc
    ssum = selsc.sum(axis=0, keepdims=True).sum(axis=1, keepdims=True)
    return (selsc / ssum * ROUTED_SCALE).reshape(N_EXPERTS, tm)


def _post_kernel(x_ref, u_ref, yf_ref, yb_ref, ao_ref, mod_ref, d_ref, wglu_ref, wout_ref, n2_ref,
                 wrt_ref, rb_ref, x1_ref, h2_ref, gates_ref, *, tm):
    u = u_ref[...]
    y = u * d_ref[...] + yf_ref[0] + yb_ref[0]
    g = jax.nn.gelu(y)
    s5 = g * jax.nn.sigmoid(jnp.dot(g.astype(BF16), wglu_ref[...], preferred_element_type=F32))
    mixed = (jnp.dot(s5.astype(BF16), wout_ref[0:S5_WIDTH, :], preferred_element_type=F32)
             + jnp.dot(ao_ref[...], wout_ref[S5_WIDTH:, :], preferred_element_type=F32))
    mod = mod_ref[0]
    gate1 = mod[:, 2 * D_MODEL:3 * D_MODEL]
    shift2 = mod[:, 3 * D_MODEL:4 * D_MODEL]
    scale2 = mod[:, 4 * D_MODEL:5 * D_MODEL]
    x1 = x_ref[...] + gate1 * mixed
    x1_ref[...] = x1
    ms = jnp.mean(x1 * x1, axis=-1, keepdims=True)
    h2 = x1 * lax.rsqrt(ms + NORM_EPS) * n2_ref[...] * (1.0 + scale2) + shift2
    h2_ref[...] = h2.astype(BF16)
    logits_t = lax.dot_general(wrt_ref[...], h2, (((1,), (1,)), ((), ())),
                               preferred_element_type=F32, precision=lax.Precision.HIGHEST)
    gates_ref[...] = _route(logits_t, rb_ref[...], tm)


def _post_call(x2, u2, y4, ao2, mod3, ssm_d, wglu_bf, wout_bf, norm2_w, wr_t, rbias3, seq_len, tm=512):
    n = x2.shape[0]
    bpb = seq_len // tm
    y3 = y4.reshape(2, n, S5_WIDTH)
    row = lambda i: (i, 0)
    const2 = lambda i: (0, 0)
    return pl.pallas_call(
        functools.partial(_post_kernel, tm=tm),
        grid=(n // tm,),
        in_specs=[pl.BlockSpec((tm, D_MODEL), row),
                  pl.BlockSpec((tm, S5_WIDTH), row),
                  pl.BlockSpec((1, tm, S5_WIDTH), lambda i: (0, i, 0)),
                  pl.BlockSpec((1, tm, S5_WIDTH), lambda i: (1, i, 0)),
                  pl.BlockSpec((tm, ATTN_WIDTH), row),
                  pl.BlockSpec((1, 1, 6 * D_MODEL), lambda i: (i // bpb, 0, 0)),
                  pl.BlockSpec((1, S5_WIDTH), const2),
                  pl.BlockSpec((S5_WIDTH, S5_WIDTH), const2),
                  pl.BlockSpec((D_MODEL, D_MODEL), const2),
                  pl.BlockSpec((1, D_MODEL), const2),
                  pl.BlockSpec((N_EXPERTS, D_MODEL), const2),
                  pl.BlockSpec((N_EXPERT_GROUPS, N_EXPERTS // N_EXPERT_GROUPS, 1), lambda i: (0, 0, 0))],
        out_specs=[pl.BlockSpec((tm, D_MODEL), row),
                   pl.BlockSpec((tm, D_MODEL), row),
                   pl.BlockSpec((N_EXPERTS, tm), lambda i: (0, i))],
        out_shape=[jax.ShapeDtypeStruct((n, D_MODEL), F32),
                   jax.ShapeDtypeStruct((n, D_MODEL), BF16),
                   jax.ShapeDtypeStruct((N_EXPERTS, n), F32)],
        compiler_params=_cparams(("parallel",)),
        name="post_mix_route",
    )(x2, u2, y3, y3, ao2, mod3, ssm_d.reshape(1, S5_WIDTH), wglu_bf, wout_bf,
      norm2_w.reshape(1, D_MODEL), wr_t, rbias3)


def _swiglu_bf(t, wg, wu, wd):
    a = jnp.dot(t, wg, preferred_element_type=F32)
    b = jnp.dot(t, wu, preferred_element_type=F32)
    act = a * jax.nn.sigmoid(a) * b
    return jnp.dot(act.astype(BF16), wd, preferred_element_type=F32)


MOE_SB = 256
MOE_SEG = 16
MOE_TF = 1024
MOE_XW = D_MODEL
MOE_STATIC_CHUNKS = 3
MOE_CHUNK = 512
MOE_DMA_SIZES = tuple(2 ** k for k in range(MOE_TF.bit_length() - 2, 3, -1))
MOE_RPAD = -(-(TOP_K * MOE_SB + N_EXPERTS * (MOE_SEG - 1) + MOE_STATIC_CHUNKS * MOE_SEG) // MOE_CHUNK) * MOE_CHUNK


MOE_P32 = MOE_RPAD // (2 * MOE_SEG)
MOE_P16 = N_EXPERTS
MOE_PACK = 4096


def _seg_copies(pieces, sb, make_copy, act):
    p32_ref, n32_ref, p16_ref, n16_ref = pieces
    for p_ref, n_ref, width, rows in ((p32_ref, n32_ref, MOE_P32, 2 * MOE_SEG), (p16_ref, n16_ref, MOE_P16, MOE_SEG)):
        def body(k, carry, p_ref=p_ref, width=width, rows=rows):
            v = p_ref[sb * width + k]
            act(make_copy(pl.multiple_of(v & (MOE_PACK - 1), MOE_SEG),
                          pl.multiple_of(v >> (MOE_PACK.bit_length() - 1), MOE_SEG), rows))
            return carry
        lax.fori_loop(0, n_ref[sb], body, 0)


MOE_WAIT_SIZES = tuple(2 ** k for k in range((MOE_RPAD - 1).bit_length() - 1, 3, -1))


def _sub_block_rows(cnt_ref, loc_ref, sb):
    last = sb * N_EXPERTS + N_EXPERTS - 1
    return loc_ref[last] + cnt_ref[last]


def _wait_rows(rows, make_copy):
    for b in MOE_WAIT_SIZES:
        @pl.when((rows & b) != 0)
        def _(b=b):
            make_copy(b).wait()


def _build_onehot(gt, p_ref, cnt_ref, loc_ref, sb, weighted):
    t = gt.shape[1]
    selm = gt > 0.0
    r = lax.broadcasted_iota(jnp.int32, (t, t), 0)
    c = lax.broadcasted_iota(jnp.int32, (t, t), 1)
    upper = jnp.where(r < c, 1.0, 0.0).astype(BF16)
    rank = jnp.dot(jnp.where(selm, 1.0, 0.0).astype(BF16), upper, preferred_element_type=F32)
    key = jnp.where(selm, rank, -1.0)
    j16 = lax.broadcasted_iota(jnp.int32, (MOE_SEG, t), 0).astype(F32)
    p_ref[...] = jnp.zeros(p_ref.shape, BF16)

    for e in range(N_EXPERTS):
        off = loc_ref[sb * N_EXPERTS + e]
        n = cnt_ref[sb * N_EXPERTS + e]
        row = key[e:e + 1, :]
        val = gt[e:e + 1, :] if weighted else 1.0

        def put(cidx, row=row, off=off, val=val):
            first = cidx * MOE_SEG
            firstf = float(first) if isinstance(first, int) else first.astype(F32)
            tile = jnp.where(row == j16 + firstf, val, 0.0).astype(BF16)
            p_ref[pl.ds(pl.multiple_of(off + first, MOE_SEG), MOE_SEG), :] = tile

        for cidx in range(MOE_STATIC_CHUNKS):
            put(cidx)

        def extra(cidx, carry, put=put):
            put(cidx)
            return carry
        lax.fori_loop(MOE_STATIC_CHUNKS, n // MOE_SEG, extra, 0)


def _dispatch_kernel(cnt_ref, loc_ref, p32_ref, n32_ref, p16_ref, n16_ref, padoff_ref, padn_ref,
                     hp_ref, hs_ref, gt_ref, xe_ref, p_ref, xg_ref, z_ref, sem, *, nsb_p, n_sb):
    s = pl.program_id(0)
    slot = s % 2

    def copies(sb, slot_, act):
        def mk(off, g, b):
            return pltpu.make_async_copy(xg_ref.at[slot_, pl.ds(off, b)], xe_ref.at[pl.ds(g, b)], sem.at[slot_])
        _seg_copies((p32_ref, n32_ref, p16_ref, n16_ref), sb, mk, act)

    def wait_all(sb, slot_):
        _wait_rows(_sub_block_rows(cnt_ref, loc_ref, sb),
                   lambda b: pltpu.make_async_copy(xg_ref.at[slot_, pl.ds(0, b)], xe_ref.at[pl.ds(0, b)],
                                                   sem.at[slot_]))

    @pl.when(s >= 2)
    def _():
        wait_all(s - 2, slot)

    gt = gt_ref[...]
    _build_onehot(gt, p_ref, cnt_ref, loc_ref, s, weighted=False)
    xext = jnp.where(s < nsb_p, hp_ref[...], hs_ref[...])
    rows = _sub_block_rows(cnt_ref, loc_ref, s)
    for i in range(MOE_RPAD // MOE_CHUNK):
        @pl.when(i * MOE_CHUNK < rows)
        def _(i=i):
            xg_ref[slot, i * MOE_CHUNK:(i + 1) * MOE_CHUNK, :] = jnp.dot(
                p_ref[i * MOE_CHUNK:(i + 1) * MOE_CHUNK, :], xext, preferred_element_type=F32).astype(BF16)
    copies(s, slot, lambda cp: cp.start())

    @pl.when(s == n_sb - 1)
    def _():
        if n_sb >= 2:
            wait_all(s - 1, 1 - slot)
        wait_all(s, slot)
        z_ref[...] = jnp.zeros(z_ref.shape, BF16)

        def pads(act):
            def body(e, carry):
                n = padn_ref[e]
                off = padoff_ref[e]
                for b in MOE_DMA_SIZES:
                    @pl.when((n & b) != 0)
                    def _(b=b):
                        done = n & (-2 * b)
                        act(pltpu.make_async_copy(
                            z_ref.at[pl.ds(0, b)],
                            xe_ref.at[pl.ds(pl.multiple_of(off + done, MOE_SEG), b)], sem.at[0]))
                return carry
            lax.fori_loop(0, N_EXPERTS, body, 0)
        pads(lambda cp: cp.start())
        pads(lambda cp: cp.wait())


def _ffn_kernel(te_ref, tidx_ref, tcode_ref, xe_ref, wg_ref, wu_ref, wd_ref, ye_ref, wgub, wdb):
    i = pl.program_id(0)
    code = tcode_ref[i]

    @pl.when(code == 2)
    def _():
        wgub[:, :EXPERT_FF] = wg_ref[0].astype(BF16)
        wgub[:, EXPERT_FF:] = wu_ref[0].astype(BF16)
        wdb[...] = wd_ref[0].astype(BF16)

    @pl.when(code != 0)
    def _():
        ab = jnp.dot(xe_ref[...], wgub[...], preferred_element_type=F32)
        a = ab[:, :EXPERT_FF]
        act = a * jax.nn.sigmoid(a) * ab[:, EXPERT_FF:]
        ye_ref[...] = jnp.dot(act.astype(BF16), wdb[...], preferred_element_type=F32).astype(BF16)


def _combine_kernel(cnt_ref, loc_ref, p32_ref, n32_ref, p16_ref, n16_ref, ye_ref, gt_ref, h_ref, x1_ref, mod_ref,
                    sg_ref, su_ref, sd_ref, fn_ref, o_ref, p_ref, ys_ref, acc_ref, sem, *, sb0, n_steps):
    i = pl.program_id(0)
    slot = i % 2
    sb = sb0 + i

    def copies(sb_, slot_, act):
        def mk(off, g, b):
            return pltpu.make_async_copy(ye_ref.at[pl.ds(g, b)], ys_ref.at[slot_, pl.ds(off, b)], sem.at[slot_])
        _seg_copies((p32_ref, n32_ref, p16_ref, n16_ref), sb_, mk, act)

    @pl.when(i == 0)
    def _():
        ys_ref[...] = jnp.zeros(ys_ref.shape, BF16)
        copies(sb, slot, lambda cp: cp.start())

    @pl.when(i + 1 < n_steps)
    def _():
        copies(sb + 1, 1 - slot, lambda cp: cp.start())

    acc_ref[...] = _swiglu_bf(h_ref[...], sg_ref[...], su_ref[...], sd_ref[...])
    _build_onehot(gt_ref[...], p_ref, cnt_ref, loc_ref, sb, weighted=True)
    rows = _sub_block_rows(cnt_ref, loc_ref, sb)
    _wait_rows(rows, lambda b: pltpu.make_async_copy(ye_ref.at[pl.ds(0, b)], ys_ref.at[slot, pl.ds(0, b)],
                                                     sem.at[slot]))
    for c in range(MOE_RPAD // MOE_CHUNK):
        @pl.when(c * MOE_CHUNK < rows)
        def _(c=c):
            acc_ref[...] += lax.dot_general(
                p_ref[c * MOE_CHUNK:(c + 1) * MOE_CHUNK, :], ys_ref[slot, c * MOE_CHUNK:(c + 1) * MOE_CHUNK, :],
                (((0,), (0,)), ((), ())), preferred_element_type=F32)
    gate2 = mod_ref[0][:, 5 * D_MODEL:6 * D_MODEL]
    x2 = x1_ref[...] + gate2 * acc_ref[...]
    ms = jnp.mean(x2 * x2, axis=-1, keepdims=True)
    o_ref[...] = x2 * lax.rsqrt(ms + NORM_EPS) * fn_ref[...]


def _moe_plan(gates_t):
    ne, n = gates_t.shape
    n_sb = n // MOE_SB
    cnt = jnp.sum((gates_t > 0.0).reshape(ne, n_sb, MOE_SB), axis=-1, dtype=jnp.int32).T
    cnt16 = (cnt + MOE_SEG - 1) // MOE_SEG * MOE_SEG
    loc = jnp.cumsum(cnt16, axis=1) - cnt16
    tot = jnp.sum(cnt16, axis=0)
    totp = (tot + MOE_TF - 1) // MOE_TF * MOE_TF
    ends = jnp.cumsum(totp)
    base = ends - totp
    goff = base[None, :] + jnp.cumsum(cnt16, axis=0) - cnt16
    rows_max = TOP_K * n + n_sb * ne * (MOE_SEG - 1) + ne * (MOE_TF - MOE_SEG)
    nt_max = -(-rows_max // MOE_TF)
    tiles = jnp.arange(nt_max, dtype=jnp.int32)
    used = ends[-1]
    valid = tiles * MOE_TF < used
    tidx = jnp.where(valid, tiles, jnp.maximum(used // MOE_TF - 1, 0))
    te = jnp.sum(ends[None, :] <= (tidx * MOE_TF)[:, None], axis=1, dtype=jnp.int32)
    te = jnp.minimum(te, ne - 1)
    first = jnp.concatenate([jnp.ones((1,), jnp.bool_), te[1:] != te[:-1]])
    tcode = jnp.where(valid, 1 + first.astype(jnp.int32), 0)

    def piece_list(npieces, first_row, width, rows):
        cum = jnp.cumsum(npieces, axis=1)
        k = jnp.arange(width, dtype=jnp.int32)
        ek = jnp.minimum(jnp.sum(cum[:, None, :] <= k[None, :, None], axis=2, dtype=jnp.int32), ne - 1)
        onehot = ek[:, :, None] == jnp.arange(ne, dtype=jnp.int32)[None, None, :]
        pick = lambda a: jnp.sum(jnp.where(onehot, a[:, None, :], 0), axis=2, dtype=jnp.int32)
        row = pick(first_row) + rows * (k[None, :] - pick(cum - npieces))
        packed = (pick(goff) + row) * MOE_PACK + pick(loc) + row
        return packed.reshape(-1).astype(jnp.int32), cum[:, -1].astype(jnp.int32)

    n32 = cnt16 // (2 * MOE_SEG)
    p32, t32 = piece_list(n32, jnp.zeros_like(cnt16), MOE_P32, 2 * MOE_SEG)
    p16, t16 = piece_list(cnt16 // MOE_SEG % 2, n32 * (2 * MOE_SEG), MOE_P16, MOE_SEG)
    return dict(cnt=cnt16.reshape(-1), loc=loc.reshape(-1).astype(jnp.int32), pieces=(p32, t32, p16, t16),
                padoff=(base + tot).astype(jnp.int32), padn=(totp - tot).astype(jnp.int32),
                te=te, tidx=tidx.astype(jnp.int32), tcode=tcode, nt_max=nt_max)


def _dispatch_call(plan, h2_p, h2_s, gates_t):
    nsb_p = h2_p.shape[0] // MOE_SB
    n_sb = gates_t.shape[1] // MOE_SB
    grid_spec = pltpu.PrefetchScalarGridSpec(
        num_scalar_prefetch=8, grid=(n_sb,),
        in_specs=[pl.BlockSpec((MOE_SB, D_MODEL), lambda s, *_: (jnp.minimum(s, nsb_p - 1), 0)),
                  pl.BlockSpec((MOE_SB, D_MODEL), lambda s, *_: (jnp.maximum(s - nsb_p, 0), 0)),
                  pl.BlockSpec((N_EXPERTS, MOE_SB), lambda s, *_: (0, s))],
        out_specs=pl.BlockSpec(memory_space=pl.ANY),
        scratch_shapes=[pltpu.VMEM((MOE_RPAD, MOE_SB), BF16),
                        pltpu.VMEM((2, MOE_RPAD, MOE_XW), BF16),
                        pltpu.VMEM((MOE_DMA_SIZES[0], MOE_XW), BF16),
                        pltpu.SemaphoreType.DMA((2,))])
    return pl.pallas_call(
        functools.partial(_dispatch_kernel, nsb_p=nsb_p, n_sb=n_sb),
        grid_spec=grid_spec,
        out_shape=jax.ShapeDtypeStruct((plan['nt_max'] * MOE_TF, MOE_XW), BF16),
        compiler_params=_cparams(("arbitrary",)),
        name="moe_dispatch",
    )(plan['cnt'], plan['loc'], *plan['pieces'], plan['padoff'], plan['padn'], h2_p, h2_s, gates_t)


def _ffn_call(plan, xe, wg, wu, wd):
    grid_spec = pltpu.PrefetchScalarGridSpec(
        num_scalar_prefetch=3, grid=(plan['nt_max'],),
        in_specs=[pl.BlockSpec((MOE_TF, MOE_XW), lambda i, te, tidx, tv: (tidx[i], 0)),
                  pl.BlockSpec((1, D_MODEL, EXPERT_FF), lambda i, te, tidx, tv: (te[i], 0, 0)),
                  pl.BlockSpec((1, D_MODEL, EXPERT_FF), lambda i, te, tidx, tv: (te[i], 0, 0)),
                  pl.BlockSpec((1, EXPERT_FF, D_MODEL), lambda i, te, tidx, tv: (te[i], 0, 0))],
        out_specs=pl.BlockSpec((MOE_TF, D_MODEL), lambda i, te, tidx, tv: (tidx[i], 0)),
        scratch_shapes=[pltpu.VMEM((D_MODEL, 2 * EXPERT_FF), BF16),
                        pltpu.VMEM((EXPERT_FF, D_MODEL), BF16)])
    return pl.pallas_call(
        _ffn_kernel,
        grid_spec=grid_spec,
        out_shape=jax.ShapeDtypeStruct((xe.shape[0], D_MODEL), BF16),
        compiler_params=_cparams(("arbitrary",)),
        name="moe_ffn",
    )(plan['te'], plan['tidx'], plan['tcode'], xe, wg, wu, wd)


def _combine_call(plan, ye, gates_t, h2, x1, mod3, sg, su, sd, final_w, sb0, seq_len):
    n = h2.shape[0]
    n_steps = n // MOE_SB
    bpb = seq_len // MOE_SB
    row = lambda i, *_: (i, 0)
    const2 = lambda i, *_: (0, 0)
    grid_spec = pltpu.PrefetchScalarGridSpec(
        num_scalar_prefetch=6, grid=(n_steps,),
        in_specs=[pl.BlockSpec(memory_space=pl.ANY),
                  pl.BlockSpec((N_EXPERTS, MOE_SB), lambda i, *_: (0, sb0 + i)),
                  pl.BlockSpec((MOE_SB, D_MODEL), row),
                  pl.BlockSpec((MOE_SB, D_MODEL), row),
                  pl.BlockSpec((1, 1, 6 * D_MODEL), lambda i, *_: (i // bpb, 0, 0)),
                  pl.BlockSpec((D_MODEL, EXPERT_FF), const2),
                  pl.BlockSpec((D_MODEL, EXPERT_FF), const2),
                  pl.BlockSpec((EXPERT_FF, D_MODEL), const2),
                  pl.BlockSpec((1, D_MODEL), const2)],
        out_specs=pl.BlockSpec((MOE_SB, D_MODEL), row),
        scratch_shapes=[pltpu.VMEM((MOE_RPAD, MOE_SB), BF16),
                        pltpu.VMEM((2, MOE_RPAD, D_MODEL), BF16),
                        pltpu.VMEM((MOE_SB, D_MODEL), F32),
                        pltpu.SemaphoreType.DMA((2,))])
    return pl.pallas_call(
        functools.partial(_combine_kernel, sb0=sb0, n_steps=n_steps),
        grid_spec=grid_spec,
        out_shape=jax.ShapeDtypeStruct((n, D_MODEL), F32),
        compiler_params=_cparams(("arbitrary",)),
        name="moe_combine",
    )(plan['cnt'], plan['loc'], *plan['pieces'], ye, gates_t, h2, x1, mod3, sg, su, sd,
      final_w.reshape(1, D_MODEL))


def _moe_sparse(h2_p, h2_s, gt_p, gt_s, x1_p, x1_s, mod_p, mod_s, wg, wu, wd, sg, su, sd, final_w, dseq):
    gates_t = jnp.concatenate([gt_p, gt_s], axis=1)
    plan = _moe_plan(gates_t)
    xe = _dispatch_call(plan, h2_p, h2_s, gates_t)
    ye = _ffn_call(plan, xe, wg, wu, wd)
    n_p = h2_p.shape[0]
    y_p = _combine_call(plan, ye, gates_t, h2_p, x1_p, mod_p, sg, su, sd, final_w, 0, n_p)
    y_s = _combine_call(plan, ye, gates_t, h2_s, x1_s, mod_s, sg, su, sd, final_w, n_p // MOE_SB, dseq)
    return y_p, y_s


def _rope_tables(n_tokens):
    rows = n_tokens // GRID_W
    row = jnp.repeat(jnp.arange(rows, dtype=F32), GRID_W)
    col = jnp.tile(jnp.arange(GRID_W, dtype=F32), rows)
    freqs = ROPE_THETA ** (-jnp.arange(ROT_PAIRS, dtype=F32) / ROT_PAIRS)
    ar = row[:, None] * freqs
    ac = col[:, None] * freqs
    cos = jnp.concatenate([jnp.cos(ar), jnp.cos(ar), jnp.cos(ac), jnp.cos(ac)], axis=1)
    sin = jnp.concatenate([-jnp.sin(ar), jnp.sin(ar), -jnp.sin(ac), jnp.sin(ac)], axis=1)
    return jnp.tile(cos, (1, 2)), jnp.tile(sin, (1, 2))


def kernel(x_prompt, x_sample, c, cache_k, cache_v, state_ssm_re, state_ssm_im, c_ctx, w_ada, b_ada, norm1_w, w_in, ssm_lambda_re, ssm_lambda_im, ssm_log_dt, ssm_b_re, ssm_b_im, ssm_c_re, ssm_c_im, ssm_d, ssm_w_glu, diff_lambda_q, diff_lambda_k, diff_subln_w, w_out, norm2_w, w_router, router_bias, w_exp_gate, w_exp_up, w_exp_down, w_sh_gate, w_sh_up, w_sh_down, final_norm_w):
    depth = w_ada.shape[0]
    assert depth == 1
    l = 0
    lambda_init = 0.8 - 0.6 * math.exp(-0.3 * l)
    bsz, seq, _ = x_prompt.shape
    dbs, dseq, _ = x_sample.shape
    n_p, n_s = bsz * seq, dbs * dseq

    cond8 = jnp.zeros((SUBLANES, D_MODEL), F32).at[:dbs].set(c).at[dbs].set(c_ctx)
    mod = _ada_call(cond8, w_ada[l], b_ada[l])
    mod_s = mod[:dbs].reshape(dbs, 1, 6 * D_MODEL)
    mod_p = mod[dbs:dbs + 1].reshape(1, 1, 6 * D_MODEL)

    w_in_bf = w_in[l].astype(BF16)
    xp2 = x_prompt.reshape(n_p, D_MODEL)
    xs2 = x_sample.reshape(n_s, D_MODEL)
    u_p, q_p, k_p, v_p = _inproj_call(xp2, mod_p, norm1_w[l], w_in_bf, None, n_p, F32)
    u_s, q_s, k_s, v_s = _inproj_call(xs2, mod_s, norm1_w[l], w_in_bf, _rope_tables(dseq), dseq, BF16)

    bblk, cblk, lam_r, lam_i = _s5_weights(ssm_lambda_re[l], ssm_lambda_im[l], ssm_log_dt[l],
                                           ssm_b_re[l], ssm_b_im[l], ssm_c_re[l], ssm_c_im[l])
    lr_p = jnp.broadcast_to(lam_r[:, :, None, :], (2, N_COLBLK, SUBLANES, S5_BLK_STATES))
    li_p = jnp.broadcast_to(lam_i[:, :, None, :], (2, N_COLBLK, SUBLANES, S5_BLK_STATES))
    g_p = bsz // SUBLANES
    h0_p = jnp.zeros((2, g_p, N_COLBLK * 8, SUBLANES, LANES), F32)
    y_p, hfin = _s5_call(u_p.reshape(bsz, seq, S5_WIDTH), bblk, cblk, lr_p, li_p, h0_p, SUBLANES)

    def halves(x):
        x = x.reshape(2, 2, 2, 1, S5_BLK_STATES)
        x = jnp.broadcast_to(x, (2, 2, 2, dbs, S5_BLK_STATES))
        return x.transpose(0, 2, 1, 3, 4).reshape(2, 2, 2 * dbs, S5_BLK_STATES)

    def h0_tiles(s):
        s = s.astype(F32).reshape(dbs, 2, 2, 2, 4, LANES)
        return s.transpose(1, 3, 4, 2, 0, 5).reshape(2, 1, 2, 4, 2 * dbs, LANES)

    h0_s = jnp.concatenate([h0_tiles(state_ssm_re[:, l]), h0_tiles(state_ssm_im[:, l])], axis=3)
    h0_s = h0_s.reshape(2, 1, 16, SUBLANES, LANES)
    y_s, _ = _s5_call(u_s.reshape(dbs, dseq, S5_WIDTH), bblk, cblk, halves(lam_r), halves(lam_i), h0_s, dbs,
                      tb=128)

    lq = diff_lambda_q[l].astype(F32)
    lk = diff_lambda_k[l].astype(F32)
    lam = (jnp.exp(jnp.sum(lq[0] * lk[0])) - jnp.exp(jnp.sum(lq[1] * lk[1])) + lambda_init).reshape(1)
    subln = diff_subln_w[l].astype(F32).reshape(1, VALUE_DIM)
    ao_p = _attn_ctx_call(lam, q_p.reshape(bsz, seq, ATTN_WIDTH), k_p.reshape(bsz, seq, ATTN_WIDTH),
                          v_p.reshape(bsz, seq, ATTN_WIDTH), subln, lambda_init)
    past = cache_k.shape[2]
    ao_s = _attn_lat_call(lam, q_s.reshape(dbs, dseq, ATTN_WIDTH), k_s.reshape(dbs, dseq, ATTN_WIDTH),
                          v_s.reshape(dbs, dseq, ATTN_WIDTH),
                          cache_k[:, l].reshape(dbs, past, ATTN_WIDTH).astype(BF16),
                          cache_v[:, l].reshape(dbs, past, ATTN_WIDTH).astype(BF16), subln, lambda_init)

    wglu_bf = ssm_w_glu[l].astype(BF16)
    wout_bf = w_out[l].astype(BF16)
    wr_t = w_router[l].astype(F32).T
    rbias3 = router_bias[l].astype(F32).reshape(N_EXPERT_GROUPS, N_EXPERTS // N_EXPERT_GROUPS, 1)
    x1_p, h2_p, gates_p = _post_call(xp2, u_p, y_p, ao_p.reshape(n_p, ATTN_WIDTH), mod_p, ssm_d[l],
                                     wglu_bf, wout_bf, norm2_w[l], wr_t, rbias3, n_p)
    x1_s, h2_s, gates_s = _post_call(xs2, u_s, y_s, ao_s.reshape(n_s, ATTN_WIDTH), mod_s, ssm_d[l],
                                     wglu_bf, wout_bf, norm2_w[l], wr_t, rbias3, dseq)

    wg = w_exp_gate[l]
    wu = w_exp_up[l]
    wd = w_exp_down[l]
    sg = w_sh_gate[l].astype(BF16)
    su = w_sh_up[l].astype(BF16)
    sd = w_sh_down[l].astype(BF16)
    y_prompt, y_sample = _moe_sparse(h2_p, h2_s, gates_p, gates_s, x1_p, x1_s, mod_p, mod_s,
                                     wg, wu, wd, sg, su, sd, final_norm_w, dseq)

    new_cache_k = k_p.reshape(bsz, 1, seq, N_DIFF_HEADS, VALUE_DIM)
    new_cache_v = v_p.reshape(bsz, 1, seq, N_DIFF_HEADS, VALUE_DIM)
    hf = hfin.reshape(2, g_p, N_COLBLK, 2, 4, SUBLANES, LANES)
    hf = hf.transpose(3, 1, 5, 0, 2, 4, 6).reshape(2, bsz, 1, 2, S5_GROUPS, S5_STATE)
    return (y_prompt.reshape(bsz, seq, D_MODEL), y_sample.reshape(dbs, dseq, D_MODEL),
            new_cache_k, new_cache_v, hf[0], hf[1])
```

```python
import functools
import math

import jax
import jax.numpy as jnp
from jax import lax
from jax.experimental import pallas as pl
from jax.experimental.pallas import tpu as pltpu

F32 = jnp.float32
BF16 = jnp.bfloat16

D_MODEL = 1024
GRID_W = 64
S5_WIDTH = 512
S5_GROUP = 16
S5_GROUPS = 32
S5_STATE = 64
ATTN_WIDTH = 512
DIFF_HEAD_DIM = 64
VALUE_DIM = 128
N_DIFF_HEADS = 4
IN_WIDTH = S5_WIDTH + 3 * ATTN_WIDTH
ROT_PAIRS = DIFF_HEAD_DIM // 4
ROPE_THETA = 10000.0
N_EXPERTS = 64
TOP_K = 8
N_EXPERT_GROUPS = 8
TOPK_GROUPS = 4
EXPERT_FF = 256
ROUTED_SCALE = 2.5
NORM_EPS = 1e-6

LANES = 128
SUBLANES = 8
S5_COLBLK = 8 * S5_GROUP
S5_BLK_STATES = 8 * S5_STATE
N_COLBLK = S5_WIDTH // S5_COLBLK
VMEM_LIMIT = 56 * 1024 * 1024


def _cparams(sem):
    return pltpu.CompilerParams(dimension_semantics=sem, vmem_limit_bytes=VMEM_LIMIT)


def _ada_kernel(cond_ref, w_ref, b_ref, o_ref):
    c = cond_ref[...]
    s = c * jax.nn.sigmoid(c)
    o_ref[...] = jnp.dot(s.astype(BF16), w_ref[...].astype(BF16),
                         preferred_element_type=F32) + b_ref[...]


def _ada_call(cond8, w_ada, b_ada):
    n = w_ada.shape[1]
    tn = 1536
    return pl.pallas_call(
        _ada_kernel,
        grid=(n // tn,),
        in_specs=[pl.BlockSpec((SUBLANES, D_MODEL), lambda j: (0, 0)),
                  pl.BlockSpec((D_MODEL, tn), lambda j: (0, j)),
                  pl.BlockSpec((1, tn), lambda j: (0, j))],
        out_specs=pl.BlockSpec((SUBLANES, tn), lambda j: (0, j)),
        out_shape=jax.ShapeDtypeStruct((SUBLANES, n), F32),
        compiler_params=_cparams(("arbitrary",)),
        name="adaln",
    )(cond8, w_ada, b_ada.reshape(1, n))


def _rope_apply(t, cos, sin):
    parts = []
    for cidx in range(ATTN_WIDTH // LANES):
        xc = t[:, cidx * LANES:(cidx + 1) * LANES]
        up = pltpu.roll(xc, LANES - ROT_PAIRS, 1)
        dn = pltpu.roll(xc, ROT_PAIRS, 1)
        lane = lax.broadcasted_iota(jnp.int32, xc.shape, 1)
        partner = jnp.where((lane % (2 * ROT_PAIRS)) < ROT_PAIRS, up, dn)
        parts.append(xc * cos + partner * sin)
    return jnp.concatenate(parts, axis=1)


def _inproj_kernel(*refs, rope, kv_dtype):
    if rope:
        x_ref, mod_ref, n1_ref, w_ref, cos_ref, sin_ref, u_ref, q_ref, k_ref, v_ref = refs
    else:
        x_ref, mod_ref, n1_ref, w_ref, u_ref, q_ref, k_ref, v_ref = refs
    x = x_ref[...]
    ms = jnp.mean(x * x, axis=-1, keepdims=True)
    xn = x * lax.rsqrt(ms + NORM_EPS) * n1_ref[...]
    mod = mod_ref[0]
    shift = mod[:, 0:D_MODEL]
    scale = mod[:, D_MODEL:2 * D_MODEL]
    h = xn * (1.0 + scale) + shift
    proj = jnp.dot(h.astype(BF16), w_ref[...], preferred_element_type=F32)
    u_ref[...] = proj[:, :S5_WIDTH]
    q = proj[:, S5_WIDTH:S5_WIDTH + ATTN_WIDTH]
    k = proj[:, S5_WIDTH + ATTN_WIDTH:S5_WIDTH + 2 * ATTN_WIDTH]
    v = proj[:, S5_WIDTH + 2 * ATTN_WIDTH:]
    if rope:
        cos = cos_ref[...]
        sin = sin_ref[...]
        q = _rope_apply(q, cos, sin)
        k = _rope_apply(k, cos, sin)
    q_ref[...] = (q * (DIFF_HEAD_DIM ** -0.5)).astype(BF16)
    k_ref[...] = k.astype(kv_dtype)
    v_ref[...] = v.astype(kv_dtype)


def _inproj_call(x2, mod3, norm1_w, w_in_bf, rope_tabs, seq_len, kv_dtype, tm=512):
    n = x2.shape[0]
    bpb = seq_len // tm
    rope = rope_tabs is not None
    in_specs = [pl.BlockSpec((tm, D_MODEL), lambda i: (i, 0)),
                pl.BlockSpec((1, 1, 6 * D_MODEL), lambda i: (i // bpb, 0, 0)),
                pl.BlockSpec((1, D_MODEL), lambda i: (0, 0)),
                pl.BlockSpec((D_MODEL, IN_WIDTH), lambda i: (0, 0))]
    args = [x2, mod3, norm1_w.reshape(1, D_MODEL), w_in_bf]
    if rope:
        in_specs += [pl.BlockSpec((tm, LANES), lambda i: (i % bpb, 0)),
                     pl.BlockSpec((tm, LANES), lambda i: (i % bpb, 0))]
        args += list(rope_tabs)
    ospec = pl.BlockSpec((tm, S5_WIDTH), lambda i: (i, 0))
    return pl.pallas_call(
        functools.partial(_inproj_kernel, rope=rope, kv_dtype=kv_dtype),
        grid=(n // tm,),
        in_specs=in_specs,
        out_specs=[ospec, ospec, ospec, ospec],
        out_shape=[jax.ShapeDtypeStruct((n, S5_WIDTH), F32),
                   jax.ShapeDtypeStruct((n, ATTN_WIDTH), BF16),
                   jax.ShapeDtypeStruct((n, ATTN_WIDTH), kv_dtype),
                   jax.ShapeDtypeStruct((n, ATTN_WIDTH), kv_dtype)],
        compiler_params=_cparams(("parallel",)),
        name="inproj_rope" if rope else "inproj",
    )(*args)


def _s5_slot(b, q, nseq):
    if nseq == SUBLANES:
        return b, q * 8
    half = q // 2
    return half * nseq + b, (q % 2) * 8


def _s5_kernel(u_ref, b_ref, c_ref, lr_ref, li_ref, h0_ref, y_ref, hfin_ref, buh, hs, hst,
               *, nseq, tb, stride, nqq):
    d = pl.program_id(1)
    c = pl.program_id(2)
    nc = pl.num_programs(2)

    @pl.when(c == 0)
    def _():
        hst[...] = h0_ref[0, 0]
        hs[...] = jnp.zeros(hs.shape, F32)

    u = u_ref[...].reshape(nseq * tb, S5_WIDTH).astype(BF16)
    for q in range(N_COLBLK):
        res = jnp.dot(u[:, q * S5_COLBLK:(q + 1) * S5_COLBLK], b_ref[0, q],
                      preferred_element_type=F32)
        for b in range(nseq):
            slot, ct0 = _s5_slot(b, q, nseq)
            for j in range(8):
                buh[ct0 + j, slot * stride:slot * stride + tb, :] = (
                    res[b * tb:(b + 1) * tb, j * LANES:(j + 1) * LANES])

    for qq in range(nqq):
        lr = [lr_ref[0, qq, :, i * LANES:(i + 1) * LANES] for i in range(4)]
        li = [li_ref[0, qq, :, i * LANES:(i + 1) * LANES] for i in range(4)]
        h_init = tuple(hst[qq * 8 + i] for i in range(8))

        def body(s, carry, qq=qq, lr=lr, li=li):
            t = jnp.where(d == 0, s, tb - 1 - s)
            idx = pl.ds(t, SUBLANES, stride=stride)
            new_r, new_i = [], []
            for i in range(4):
                hr, hi = carry[i], carry[4 + i]
                bur = buh[qq * 8 + i, idx, :]
                bui = buh[qq * 8 + 4 + i, idx, :]
                nr = lr[i] * hr - li[i] * hi + bur
                ni = lr[i] * hi + li[i] * hr + bui
                hs[qq * 8 + i, idx, :] = nr
                hs[qq * 8 + 4 + i, idx, :] = ni
                new_r.append(nr)
                new_i.append(ni)
            return tuple(new_r) + tuple(new_i)

        h_fin = lax.fori_loop(0, tb, body, h_init, unroll=4)
        for i in range(8):
            hst[qq * 8 + i] = h_fin[i]

    rows = nseq * stride
    for q in range(N_COLBLK):
        slot0, ct0 = _s5_slot(0, q, nseq)
        hmat = jnp.concatenate(
            [hs[ct0 + j, slot0 * stride:slot0 * stride + rows, :] for j in range(8)], axis=1)
        yq = jnp.dot(hmat.astype(BF16), c_ref[0, q], preferred_element_type=F32)
        for b in range(nseq):
            y_ref[0, b, :, q * S5_COLBLK:(q + 1) * S5_COLBLK] = yq[b * stride:b * stride + tb, :]

    @pl.when(c == nc - 1)
    def _():
        hfin_ref[0, 0] = hst[...]


def _s5_call(u3, bblk, cblk, lr, li, h0, nseq, tb=64):
    nseq_total, seq_len, _ = u3.shape
    g = nseq_total // nseq
    nc = seq_len // tb
    stride = tb + SUBLANES // 2
    nqq = lr.shape[1]
    nct = nqq * 8

    def tmap(gi, d, c):
        return c + d * (nc - 1 - 2 * c)

    return pl.pallas_call(
        functools.partial(_s5_kernel, nseq=nseq, tb=tb, stride=stride, nqq=nqq),
        grid=(g, 2, nc),
        in_specs=[
            pl.BlockSpec((nseq, tb, S5_WIDTH), lambda gi, d, c: (gi, tmap(gi, d, c), 0)),
            pl.BlockSpec((1, N_COLBLK, S5_COLBLK, 2 * S5_BLK_STATES), lambda gi, d, c: (d, 0, 0, 0)),
            pl.BlockSpec((1, N_COLBLK, 2 * S5_BLK_STATES, S5_COLBLK), lambda gi, d, c: (d, 0, 0, 0)),
            pl.BlockSpec((1, nqq, SUBLANES, S5_BLK_STATES), lambda gi, d, c: (d, 0, 0, 0)),
            pl.BlockSpec((1, nqq, SUBLANES, S5_BLK_STATES), lambda gi, d, c: (d, 0, 0, 0)),
            pl.BlockSpec((1, 1, nct, SUBLANES, LANES), lambda gi, d, c: (d, gi, 0, 0, 0)),
        ],
        out_specs=[
            pl.BlockSpec((1, nseq, tb, S5_WIDTH), lambda gi, d, c: (d, gi, tmap(gi, d, c), 0)),
            pl.BlockSpec((1, 1, nct, SUBLANES, LANES), lambda gi, d, c: (d, gi, 0, 0, 0)),
        ],
        out_shape=[jax.ShapeDtypeStruct((2, nseq_total, seq_len, S5_WIDTH), F32),
                   jax.ShapeDtypeStruct((2, g, nct, SUBLANES, LANES), F32)],
        scratch_shapes=[pltpu.VMEM((nct, SUBLANES * stride, LANES), F32),
                        pltpu.VMEM((nct, SUBLANES * stride, LANES), F32),
                        pltpu.VMEM((nct, SUBLANES, LANES), F32)],
        compiler_params=_cparams(("parallel", "parallel", "arbitrary")),
        name="s5_scan_%d" % nseq,
    )(u3, bblk, cblk, lr, li, h0)


def _s5_weights(lam_re, lam_im, log_dt, b_re, b_im, c_re, c_im):
    dt = jnp.exp(log_dt.astype(F32))[..., None]
    lr = lam_re.astype(F32)
    li = lam_im.astype(F32)
    mag = jnp.exp(lr * dt)
    lbr = mag * jnp.cos(li * dt)
    lbi = mag * jnp.sin(li * dt)
    a = lbr - 1.0
    den = lr * lr + li * li
    cr = (a * lr + lbi * li) / den
    ci = (lbi * lr - a * li) / den
    bbr = cr[..., None] * b_re - ci[..., None] * b_im
    bbi = cr[..., None] * b_im + ci[..., None] * b_re
    eye = jnp.eye(8, dtype=F32)

    def pack_b(x):
        x = x.reshape(2, N_COLBLK, 8, S5_STATE, S5_GROUP).transpose(0, 1, 2, 4, 3)
        x = jnp.einsum('dqghn,gk->dqghkn', x, eye)
        return x.reshape(2, N_COLBLK, S5_COLBLK, S5_BLK_STATES)

    def pack_c(x):
        x = x.reshape(2, N_COLBLK, 8, S5_GROUP, S5_STATE).transpose(0, 1, 2, 4, 3)
        x = jnp.einsum('dqgnh,gk->dqgnkh', x, eye)
        return x.reshape(2, N_COLBLK, S5_BLK_STATES, S5_COLBLK)

    bblk = jnp.concatenate([pack_b(bbr), pack_b(bbi)], axis=-1).astype(BF16)
    cblk = jnp.concatenate([pack_c(c_re.astype(F32)), pack_c(-c_im.astype(F32))], axis=-2).astype(BF16)
    lam_r = lbr.reshape(2, N_COLBLK, S5_BLK_STATES)
    lam_i = lbi.reshape(2, N_COLBLK, S5_BLK_STATES)
    return bblk, cblk, lam_r, lam_i


ATTN_KC = 512


def _lane_fold(x, op):
    acc = x[:, 0:LANES]
    for j in range(1, x.shape[1] // LANES):
        acc = op(acc, x[:, j * LANES:(j + 1) * LANES])
    return acc


def _attn_ops(q, chunks, s_ref):
    lane = lax.broadcasted_iota(jnp.int32, q.shape, 1)
    dn = (((1,), (1,)), ((), ()))
    qms = [jnp.where(lane < DIFF_HEAD_DIM, q, jnp.zeros_like(q)),
           jnp.where(lane >= DIFF_HEAD_DIM, q, jnp.zeros_like(q))]
    offs = [0]
    for load_k, _ in chunks:
        offs.append(offs[-1] + load_k().shape[0])

    def score_chunk(m, ci, m128):
        s = lax.dot_general(qms[m], chunks[ci][0](), dn, preferred_element_type=F32)
        s_ref[m, :, offs[ci]:offs[ci + 1]] = s
        f = _lane_fold(s, jnp.maximum)
        return f if m128 is None else jnp.maximum(m128, f)

    def value_chunk(m, ci, mx, l128, o):
        p = jnp.exp(s_ref[m, :, offs[ci]:offs[ci + 1]] - mx)
        f = _lane_fold(p, jnp.add)
        t = jnp.dot(p.astype(BF16), chunks[ci][1](), preferred_element_type=F32)
        return (f if l128 is None else l128 + f), (t if o is None else o + t)

    return score_chunk, value_chunk


def _row_max(m128):
    return m128.max(axis=-1, keepdims=True)


def _normalized(l128, o):
    return o * (1.0 / l128.sum(axis=-1, keepdims=True))


def _sub_ln(o, subln, lambda_init):
    ms = jnp.mean(o * o, axis=-1, keepdims=True)
    return o * lax.rsqrt(ms + NORM_EPS) * subln * (1.0 - lambda_init)


def _attn_ctx_kernel(lam_ref, q_ref, k_ref, v_ref, w_ref, o_ref, s_ref, *, lambda_init):
    chunks = [(lambda: k_ref[0].astype(BF16), lambda: v_ref[0].astype(BF16))]
    score_chunk, value_chunk = _attn_ops(q_ref[0], chunks, s_ref)
    mx0 = _row_max(score_chunk(0, 0, None))
    m1 = score_chunk(1, 0, None)
    l0, o0 = value_chunk(0, 0, mx0, None, None)
    l1, o1 = value_chunk(1, 0, _row_max(m1), None, None)
    o = _normalized(l0, o0) - lam_ref[0] * _normalized(l1, o1)
    o_ref[0] = _sub_ln(o, w_ref[...], lambda_init).astype(BF16)


def _attn_lat_kernel(lam_ref, q_ref, k_ref, v_ref, ck_ref, cv_ref, w_ref, o_ref, s_ref, *, lambda_init):
    chunks = []
    for c in range(k_ref.shape[1] // ATTN_KC):
        chunks.append((lambda c=c: k_ref[0, c * ATTN_KC:(c + 1) * ATTN_KC, :],
                       lambda c=c: v_ref[0, c * ATTN_KC:(c + 1) * ATTN_KC, :]))
    for c in range(ck_ref.shape[1] // ATTN_KC):
        chunks.append((lambda c=c: ck_ref[0, c * ATTN_KC:(c + 1) * ATTN_KC, :],
                       lambda c=c: cv_ref[0, c * ATTN_KC:(c + 1) * ATTN_KC, :]))
    nchunk = len(chunks)
    score_chunk, value_chunk = _attn_ops(q_ref[0], chunks, s_ref)

    m0 = m1 = l0 = o0 = l1 = o1 = None
    for ci in range(nchunk):
        m0 = score_chunk(0, ci, m0)
    mx0 = _row_max(m0)
    for ci in range(nchunk):
        m1 = score_chunk(1, ci, m1)
        l0, o0 = value_chunk(0, ci, mx0, l0, o0)
    mx1 = _row_max(m1)
    for ci in range(nchunk):
        l1, o1 = value_chunk(1, ci, mx1, l1, o1)
    o = _normalized(l0, o0) - lam_ref[0] * _normalized(l1, o1)
    o_ref[0] = _sub_ln(o, w_ref[...], lambda_init).astype(BF16)


def _attn_ctx_call(lam, q3, k3, v3, subln, lambda_init):
    bsz, seq_len, _ = q3.shape
    spec = pl.BlockSpec((1, seq_len, VALUE_DIM), lambda b, h: (b, 0, h))
    return pl.pallas_call(
        functools.partial(_attn_ctx_kernel, lambda_init=lambda_init),
        grid=(bsz, N_DIFF_HEADS),
        in_specs=[pl.BlockSpec(memory_space=pltpu.SMEM), spec, spec, spec,
                  pl.BlockSpec((1, VALUE_DIM), lambda b, h: (0, 0))],
        out_specs=spec,
        out_shape=jax.ShapeDtypeStruct((bsz, seq_len, ATTN_WIDTH), BF16),
        scratch_shapes=[pltpu.VMEM((2, seq_len, seq_len), F32)],
        compiler_params=_cparams(("parallel", "parallel")),
        name="attn_ctx",
    )(lam, q3, k3, v3, subln)


def _attn_lat_call(lam, q3, k3, v3, ck3, cv3, subln, lambda_init, tq=256):
    bsz, seq_len, _ = q3.shape
    ctx_len = ck3.shape[1]
    qspec = pl.BlockSpec((1, tq, VALUE_DIM), lambda b, h, i: (b, i, h))
    kspec = pl.BlockSpec((1, seq_len, VALUE_DIM), lambda b, h, i: (b, 0, h))
    cspec = pl.BlockSpec((1, ctx_len, VALUE_DIM), lambda b, h, i: (b, 0, h))
    return pl.pallas_call(
        functools.partial(_attn_lat_kernel, lambda_init=lambda_init),
        grid=(bsz, N_DIFF_HEADS, seq_len // tq),
        in_specs=[pl.BlockSpec(memory_space=pltpu.SMEM), qspec, kspec, kspec, cspec, cspec,
                  pl.BlockSpec((1, VALUE_DIM), lambda b, h, i: (0, 0))],
        out_specs=qspec,
        out_shape=jax.ShapeDtypeStruct((bsz, seq_len, ATTN_WIDTH), BF16),
        scratch_shapes=[pltpu.VMEM((2, tq, seq_len + ctx_len), F32)],
        compiler_params=_cparams(("parallel", "parallel", "arbitrary")),
        name="attn_lat",
    )(lam, q3, k3, v3, ck3, cv3, subln)


def _route(logits_t, bias3, tm):
    ng, ge = N_EXPERT_GROUPS, N_EXPERTS // N_EXPERT_GROUPS
    neg = jnp.float32(-jnp.inf)
    sc = jax.nn.sigmoid(logits_t).reshape(ng, ge, tm)
    bi = sc + bias3
    eio = lax.broadcasted_iota(jnp.int32, (ng, ge, tm), 1).astype(F32)
    gio = lax.broadcasted_iota(jnp.int32, (ng, ge, tm), 0).astype(F32)
    m1 = bi.max(axis=1, keepdims=True)
    i1 = jnp.where(bi == m1, eio, float(ge)).min(axis=1, keepdims=True)
    m2 = jnp.where(eio == i1, neg, bi).max(axis=1, keepdims=True)
    gs = jnp.broadcast_to(m1 + m2, (ng, ge, tm))
    cnt = jnp.zeros((ng, ge, tm), F32)
    for g2 in range(ng):
        o = gs[g2:g2 + 1]
        better = (o > gs) | ((o == gs) & (gio > float(g2)))
        cnt = cnt + jnp.where(better, 1.0, 0.0)
    v = jnp.where(cnt < float(TOPK_GROUPS), bi, neg)
    eidx = gio * float(ge) + eio
    selm = jnp.zeros((ng, ge, tm), F32)
    for _ in range(TOP_K):
        m = v.max(axis=0, keepdims=True).max(axis=1, keepdims=True)
        ix = jnp.where(v == m, eidx, float(N_EXPERTS)).min(axis=0, keepdims=True).min(axis=1, keepdims=True)
        oh = eidx == ix
        selm = jnp.where(oh, 1.0, selm)
        v = jnp.where(oh, neg, v)
    selsc = selm * sc
    ssum = selsc.sum(axis=0, keepdims=True).sum(axis=1, keepdims=True)
    return (selsc / ssum * ROUTED_SCALE).reshape(N_EXPERTS, tm)


def _post_kernel(x_ref, u_ref, yf_ref, yb_ref, ao_ref, mod_ref, d_ref, wglu_ref, wout_ref, n2_ref,
                 wrt_ref, rb_ref, x1_ref, h2_ref, gates_ref, *, tm):
    u = u_ref[...]
    y = u * d_ref[...] + yf_ref[0] + yb_ref[0]
    g = jax.nn.gelu(y)
    s5 = g * jax.nn.sigmoid(jnp.dot(g.astype(BF16), wglu_ref[...], preferred_element_type=F32))
    mixed = (jnp.dot(s5.astype(BF16), wout_ref[0:S5_WIDTH, :], preferred_element_type=F32)
             + jnp.dot(ao_ref[...], wout_ref[S5_WIDTH:, :], preferred_element_type=F32))
    mod = mod_ref[0]
    gate1 = mod[:, 2 * D_MODEL:3 * D_MODEL]
    shift2 = mod[:, 3 * D_MODEL:4 * D_MODEL]
    scale2 = mod[:, 4 * D_MODEL:5 * D_MODEL]
    x1 = x_ref[...] + gate1 * mixed
    x1_ref[...] = x1
    ms = jnp.mean(x1 * x1, axis=-1, keepdims=True)
    h2 = x1 * lax.rsqrt(ms + NORM_EPS) * n2_ref[...] * (1.0 + scale2) + shift2
    h2_ref[...] = h2.astype(BF16)
    logits_t = lax.dot_general(wrt_ref[...], h2, (((1,), (1,)), ((), ())),
                               preferred_element_type=F32, precision=lax.Precision.HIGHEST)
    gates_ref[...] = _route(logits_t, rb_ref[...], tm)


def _post_call(x2, u2, y4, ao2, mod3, ssm_d, wglu_bf, wout_bf, norm2_w, wr_t, rbias3, seq_len, tm=512):
    n = x2.shape[0]
    bpb = seq_len // tm
    y3 = y4.reshape(2, n, S5_WIDTH)
    row = lambda i: (i, 0)
    const2 = lambda i: (0, 0)
    return pl.pallas_call(
        functools.partial(_post_kernel, tm=tm),
        grid=(n // tm,),
        in_specs=[pl.BlockSpec((tm, D_MODEL), row),
                  pl.BlockSpec((tm, S5_WIDTH), row),
                  pl.BlockSpec((1, tm, S5_WIDTH), lambda i: (0, i, 0)),
                  pl.BlockSpec((1, tm, S5_WIDTH), lambda i: (1, i, 0)),
                  pl.BlockSpec((tm, ATTN_WIDTH), row),
                  pl.BlockSpec((1, 1, 6 * D_MODEL), lambda i: (i // bpb, 0, 0)),
                  pl.BlockSpec((1, S5_WIDTH), const2),
                  pl.BlockSpec((S5_WIDTH, S5_WIDTH), const2),
                  pl.BlockSpec((D_MODEL, D_MODEL), const2),
                  pl.BlockSpec((1, D_MODEL), const2),
                  pl.BlockSpec((N_EXPERTS, D_MODEL), const2),
                  pl.BlockSpec((N_EXPERT_GROUPS, N_EXPERTS // N_EXPERT_GROUPS, 1), lambda i: (0, 0, 0))],
        out_specs=[pl.BlockSpec((tm, D_MODEL), row),
                   pl.BlockSpec((tm, D_MODEL), row),
                   pl.BlockSpec((N_EXPERTS, tm), lambda i: (0, i))],
        out_shape=[jax.ShapeDtypeStruct((n, D_MODEL), F32),
                   jax.ShapeDtypeStruct((n, D_MODEL), BF16),
                   jax.ShapeDtypeStruct((N_EXPERTS, n), F32)],
        compiler_params=_cparams(("parallel",)),
        name="post_mix_route",
    )(x2, u2, y3, y3, ao2, mod3, ssm_d.reshape(1, S5_WIDTH), wglu_bf, wout_bf,
      norm2_w.reshape(1, D_MODEL), wr_t, rbias3)


def _swiglu_bf(t, wg, wu, wd):
    a = jnp.dot(t, wg, preferred_element_type=F32)
    b = jnp.dot(t, wu, preferred_element_type=F32)
    act = a * jax.nn.sigmoid(a) * b
    return jnp.dot(act.astype(BF16), wd, preferred_element_type=F32)


MOE_SB = 256
MOE_SEG = 16
MOE_TF = 1024
MOE_XW = D_MODEL
MOE_STATIC_CHUNKS = 3
MOE_CHUNK = 512
MOE_DMA_SIZES = tuple(2 ** k for k in range(MOE_TF.bit_length() - 2, 3, -1))
MOE_RPAD = -(-(TOP_K * MOE_SB + N_EXPERTS * (MOE_SEG - 1) + MOE_STATIC_CHUNKS * MOE_SEG) // MOE_CHUNK) * MOE_CHUNK


MOE_PIECE_ROWS = (4 * MOE_SEG, 3 * MOE_SEG, 2 * MOE_SEG, MOE_SEG)
MOE_PIECE_WIDTH = (MOE_RPAD // (4 * MOE_SEG), N_EXPERTS, N_EXPERTS, N_EXPERTS)
MOE_NPIECE_REFS = 2 * len(MOE_PIECE_ROWS)
MOE_PACK = 4096


def _seg_copies(pieces, sb, make_copy, act):
    for ci, (width, rows) in enumerate(zip(MOE_PIECE_WIDTH, MOE_PIECE_ROWS)):
        p_ref, n_ref = pieces[2 * ci], pieces[2 * ci + 1]

        def body(k, carry, p_ref=p_ref, width=width, rows=rows):
            v = p_ref[sb * width + k]
            act(make_copy(pl.multiple_of(v & (MOE_PACK - 1), MOE_SEG),
                          pl.multiple_of(v >> (MOE_PACK.bit_length() - 1), MOE_SEG), rows))
            return carry
        lax.fori_loop(0, n_ref[sb], body, 0)


MOE_WAIT_SIZES = tuple(2 ** k for k in range((MOE_RPAD - 1).bit_length() - 1, 3, -1))


def _sub_block_rows(cnt_ref, loc_ref, sb):
    last = sb * N_EXPERTS + N_EXPERTS - 1
    return loc_ref[last] + cnt_ref[last]


def _wait_rows(rows, make_copy):
    for b in MOE_WAIT_SIZES:
        @pl.when((rows & b) != 0)
        def _(b=b):
            make_copy(b).wait()


def _build_onehot(gt, p_ref, cnt_ref, loc_ref, sb, weighted):
    t = gt.shape[1]
    selm = gt > 0.0
    r = lax.broadcasted_iota(jnp.int32, (t, t), 0)
    c = lax.broadcasted_iota(jnp.int32, (t, t), 1)
    upper = jnp.where(r < c, 1.0, 0.0).astype(BF16)
    rank = jnp.dot(jnp.where(selm, 1.0, 0.0).astype(BF16), upper, preferred_element_type=F32)
    key = jnp.where(selm, rank, -1.0)
    j16 = lax.broadcasted_iota(jnp.int32, (MOE_SEG, t), 0).astype(F32)
    p_ref[...] = jnp.zeros(p_ref.shape, BF16)

    for e in range(N_EXPERTS):
        off = loc_ref[sb * N_EXPERTS + e]
        n = cnt_ref[sb * N_EXPERTS + e]
        row = key[e:e + 1, :]
        val = gt[e:e + 1, :] if weighted else 1.0

        def put(cidx, row=row, off=off, val=val):
            first = cidx * MOE_SEG
            firstf = float(first) if isinstance(first, int) else first.astype(F32)
            tile = jnp.where(row == j16 + firstf, val, 0.0).astype(BF16)
            p_ref[pl.ds(pl.multiple_of(off + first, MOE_SEG), MOE_SEG), :] = tile

        for cidx in range(MOE_STATIC_CHUNKS):
            put(cidx)

        def extra(cidx, carry, put=put):
            put(cidx)
            return carry
        lax.fori_loop(MOE_STATIC_CHUNKS, n // MOE_SEG, extra, 0)


def _dispatch_kernel(cnt_ref, loc_ref, *refs, nsb_p, n_sb):
    pieces = refs[:MOE_NPIECE_REFS]
    padoff_ref, padn_ref, hp_ref, hs_ref, gt_ref, xe_ref, p_ref, xg_ref, z_ref, sem = refs[MOE_NPIECE_REFS:]
    s = pl.program_id(0)
    slot = s % 2

    def copies(sb, slot_, act):
        def mk(off, g, b):
            return pltpu.make_async_copy(xg_ref.at[slot_, pl.ds(off, b)], xe_ref.at[pl.ds(g, b)], sem.at[slot_])
        _seg_copies(pieces, sb, mk, act)

    def wait_all(sb, slot_):
        _wait_rows(_sub_block_rows(cnt_ref, loc_ref, sb),
                   lambda b: pltpu.make_async_copy(xg_ref.at[slot_, pl.ds(0, b)], xe_ref.at[pl.ds(0, b)],
                                                   sem.at[slot_]))

    @pl.when(s >= 2)
    def _():
        wait_all(s - 2, slot)

    gt = gt_ref[...]
    _build_onehot(gt, p_ref, cnt_ref, loc_ref, s, weighted=False)
    xext = jnp.where(s < nsb_p, hp_ref[...], hs_ref[...])
    rows = _sub_block_rows(cnt_ref, loc_ref, s)
    for i in range(MOE_RPAD // MOE_CHUNK):
        @pl.when(i * MOE_CHUNK < rows)
        def _(i=i):
            xg_ref[slot, i * MOE_CHUNK:(i + 1) * MOE_CHUNK, :] = jnp.dot(
                p_ref[i * MOE_CHUNK:(i + 1) * MOE_CHUNK, :], xext, preferred_element_type=F32).astype(BF16)
    copies(s, slot, lambda cp: cp.start())

    @pl.when(s == n_sb - 1)
    def _():
        if n_sb >= 2:
            wait_all(s - 1, 1 - slot)
        wait_all(s, slot)
        z_ref[...] = jnp.zeros(z_ref.shape, BF16)

        def pads(act):
            def body(e, carry):
                n = padn_ref[e]
                off = padoff_ref[e]
                for b in MOE_DMA_SIZES:
                    @pl.when((n & b) != 0)
                    def _(b=b):
                        done = n & (-2 * b)
                        act(pltpu.make_async_copy(
                            z_ref.at[pl.ds(0, b)],
                            xe_ref.at[pl.ds(pl.multiple_of(off + done, MOE_SEG), b)], sem.at[0]))
                return carry
            lax.fori_loop(0, N_EXPERTS, body, 0)
        pads(lambda cp: cp.start())
        pads(lambda cp: cp.wait())


def _ffn_kernel(te_ref, tidx_ref, tcode_ref, xe_ref, wg_ref, wu_ref, wd_ref, ye_ref, wgub, wdb):
    i = pl.program_id(0)
    code = tcode_ref[i]

    @pl.when(code == 2)
    def _():
        wgub[:, :EXPERT_FF] = wg_ref[0].astype(BF16)
        wgub[:, EXPERT_FF:] = wu_ref[0].astype(BF16)
        wdb[...] = wd_ref[0].astype(BF16)

    @pl.when(code != 0)
    def _():
        ab = jnp.dot(xe_ref[...], wgub[...], preferred_element_type=F32)
        a = ab[:, :EXPERT_FF]
        act = a * jax.nn.sigmoid(a) * ab[:, EXPERT_FF:]
        ye_ref[...] = jnp.dot(act.astype(BF16), wdb[...], preferred_element_type=F32).astype(BF16)


def _combine_kernel(cnt_ref, loc_ref, *refs, sb0, n_steps):
    pieces = refs[:MOE_NPIECE_REFS]
    (ye_ref, gt_ref, h_ref, x1_ref, mod_ref, sg_ref, su_ref, sd_ref, fn_ref, o_ref,
     p_ref, ys_ref, acc_ref, sem) = refs[MOE_NPIECE_REFS:]
    i = pl.program_id(0)
    slot = i % 2
    sb = sb0 + i

    def copies(sb_, slot_, act):
        def mk(off, g, b):
            return pltpu.make_async_copy(ye_ref.at[pl.ds(g, b)], ys_ref.at[slot_, pl.ds(off, b)], sem.at[slot_])
        _seg_copies(pieces, sb_, mk, act)

    @pl.when(i == 0)
    def _():
        ys_ref[...] = jnp.zeros(ys_ref.shape, BF16)
        copies(sb, slot, lambda cp: cp.start())

    @pl.when(i + 1 < n_steps)
    def _():
        copies(sb + 1, 1 - slot, lambda cp: cp.start())

    acc_ref[...] = _swiglu_bf(h_ref[...], sg_ref[...], su_ref[...], sd_ref[...])
    _build_onehot(gt_ref[...], p_ref, cnt_ref, loc_ref, sb, weighted=True)
    rows = _sub_block_rows(cnt_ref, loc_ref, sb)
    _wait_rows(rows, lambda b: pltpu.make_async_copy(ye_ref.at[pl.ds(0, b)], ys_ref.at[slot, pl.ds(0, b)],
                                                     sem.at[slot]))
    for c in range(MOE_RPAD // MOE_CHUNK):
        @pl.when(c * MOE_CHUNK < rows)
        def _(c=c):
            acc_ref[...] += lax.dot_general(
                p_ref[c * MOE_CHUNK:(c + 1) * MOE_CHUNK, :], ys_ref[slot, c * MOE_CHUNK:(c + 1) * MOE_CHUNK, :],
                (((0,), (0,)), ((), ())), preferred_element_type=F32)
    gate2 = mod_ref[0][:, 5 * D_MODEL:6 * D_MODEL]
    x2 = x1_ref[...] + gate2 * acc_ref[...]
    ms = jnp.mean(x2 * x2, axis=-1, keepdims=True)
    o_ref[...] = x2 * lax.rsqrt(ms + NORM_EPS) * fn_ref[...]


def _moe_plan(gates_t):
    ne, n = gates_t.shape
    n_sb = n // MOE_SB
    cnt = jnp.sum((gates_t > 0.0).reshape(ne, n_sb, MOE_SB), axis=-1, dtype=jnp.int32).T
    cnt16 = (cnt + MOE_SEG - 1) // MOE_SEG * MOE_SEG
    loc = jnp.cumsum(cnt16, axis=1) - cnt16
    tot = jnp.sum(cnt16, axis=0)
    totp = (tot + MOE_TF - 1) // MOE_TF * MOE_TF
    ends = jnp.cumsum(totp)
    base = ends - totp
    goff = base[None, :] + jnp.cumsum(cnt16, axis=0) - cnt16
    rows_max = TOP_K * n + n_sb * ne * (MOE_SEG - 1) + ne * (MOE_TF - MOE_SEG)
    nt_max = -(-rows_max // MOE_TF)
    tiles = jnp.arange(nt_max, dtype=jnp.int32)
    used = ends[-1]
    valid = tiles * MOE_TF < used
    tidx = jnp.where(valid, tiles, jnp.maximum(used // MOE_TF - 1, 0))
    te = jnp.sum(ends[None, :] <= (tidx * MOE_TF)[:, None], axis=1, dtype=jnp.int32)
    te = jnp.minimum(te, ne - 1)
    first = jnp.concatenate([jnp.ones((1,), jnp.bool_), te[1:] != te[:-1]])
    tcode = jnp.where(valid, 1 + first.astype(jnp.int32), 0)

    def piece_list(npieces, first_row, width, rows):
        cum = jnp.cumsum(npieces, axis=1)
        k = jnp.arange(width, dtype=jnp.int32)
        ek = jnp.minimum(jnp.sum(cum[:, None, :] <= k[None, :, None], axis=2, dtype=jnp.int32), ne - 1)
        onehot = ek[:, :, None] == jnp.arange(ne, dtype=jnp.int32)[None, None, :]
        pick = lambda a: jnp.sum(jnp.where(onehot, a[:, None, :], 0), axis=2, dtype=jnp.int32)
        row = pick(first_row) + rows * (k[None, :] - pick(cum - npieces))
        packed = (pick(goff) + row) * MOE_PACK + pick(loc) + row
        return packed.reshape(-1).astype(jnp.int32), cum[:, -1].astype(jnp.int32)

    big = MOE_PIECE_ROWS[0]
    nbig = cnt16 // big
    pieces = list(piece_list(nbig, jnp.zeros_like(cnt16), MOE_PIECE_WIDTH[0], big))
    for width, rows in zip(MOE_PIECE_WIDTH[1:], MOE_PIECE_ROWS[1:]):
        pieces += list(piece_list((cnt16 % big == rows).astype(jnp.int32), nbig * big, width, rows))
    return dict(cnt=cnt16.reshape(-1), loc=loc.reshape(-1).astype(jnp.int32), pieces=tuple(pieces),
                padoff=(base + tot).astype(jnp.int32), padn=(totp - tot).astype(jnp.int32),
                te=te, tidx=tidx.astype(jnp.int32), tcode=tcode, nt_max=nt_max)


def _dispatch_call(plan, h2_p, h2_s, gates_t):
    nsb_p = h2_p.shape[0] // MOE_SB
    n_sb = gates_t.shape[1] // MOE_SB
    grid_spec = pltpu.PrefetchScalarGridSpec(
        num_scalar_prefetch=4 + MOE_NPIECE_REFS, grid=(n_sb,),
        in_specs=[pl.BlockSpec((MOE_SB, D_MODEL), lambda s, *_: (jnp.minimum(s, nsb_p - 1), 0)),
                  pl.BlockSpec((MOE_SB, D_MODEL), lambda s, *_: (jnp.maximum(s - nsb_p, 0), 0)),
                  pl.BlockSpec((N_EXPERTS, MOE_SB), lambda s, *_: (0, s))],
        out_specs=pl.BlockSpec(memory_space=pl.ANY),
        scratch_shapes=[pltpu.VMEM((MOE_RPAD, MOE_SB), BF16),
                        pltpu.VMEM((2, MOE_RPAD, MOE_XW), BF16),
                        pltpu.VMEM((MOE_DMA_SIZES[0], MOE_XW), BF16),
                        pltpu.SemaphoreType.DMA((2,))])
    return pl.pallas_call(
        functools.partial(_dispatch_kernel, nsb_p=nsb_p, n_sb=n_sb),
        grid_spec=grid_spec,
        out_shape=jax.ShapeDtypeStruct((plan['nt_max'] * MOE_TF, MOE_XW), BF16),
        compiler_params=_cparams(("arbitrary",)),
        name="moe_dispatch",
    )(plan['cnt'], plan['loc'], *plan['pieces'], plan['padoff'], plan['padn'], h2_p, h2_s, gates_t)


def _ffn_call(plan, xe, wg, wu, wd):
    grid_spec = pltpu.PrefetchScalarGridSpec(
        num_scalar_prefetch=3, grid=(plan['nt_max'],),
        in_specs=[pl.BlockSpec((MOE_TF, MOE_XW), lambda i, te, tidx, tv: (tidx[i], 0)),
                  pl.BlockSpec((1, D_MODEL, EXPERT_FF), lambda i, te, tidx, tv: (te[i], 0, 0)),
                  pl.BlockSpec((1, D_MODEL, EXPERT_FF), lambda i, te, tidx, tv: (te[i], 0, 0)),
                  pl.BlockSpec((1, EXPERT_FF, D_MODEL), lambda i, te, tidx, tv: (te[i], 0, 0))],
        out_specs=pl.BlockSpec((MOE_TF, D_MODEL), lambda i, te, tidx, tv: (tidx[i], 0)),
        scratch_shapes=[pltpu.VMEM((D_MODEL, 2 * EXPERT_FF), BF16),
                        pltpu.VMEM((EXPERT_FF, D_MODEL), BF16)])
    return pl.pallas_call(
        _ffn_kernel,
        grid_spec=grid_spec,
        out_shape=jax.ShapeDtypeStruct((xe.shape[0], D_MODEL), BF16),
        compiler_params=_cparams(("arbitrary",)),
        name="moe_ffn",
    )(plan['te'], plan['tidx'], plan['tcode'], xe, wg, wu, wd)


def _combine_call(plan, ye, gates_t, h2, x1, mod3, sg, su, sd, final_w, sb0, seq_len):
    n = h2.shape[0]
    n_steps = n // MOE_SB
    bpb = seq_len // MOE_SB
    row = lambda i, *_: (i, 0)
    const2 = lambda i, *_: (0, 0)
    grid_spec = pltpu.PrefetchScalarGridSpec(
        num_scalar_prefetch=2 + MOE_NPIECE_REFS, grid=(n_steps,),
        in_specs=[pl.BlockSpec(memory_space=pl.ANY),
                  pl.BlockSpec((N_EXPERTS, MOE_SB), lambda i, *_: (0, sb0 + i)),
                  pl.BlockSpec((MOE_SB, D_MODEL), row),
                  pl.BlockSpec((MOE_SB, D_MODEL), row),
                  pl.BlockSpec((1, 1, 6 * D_MODEL), lambda i, *_: (i // bpb, 0, 0)),
                  pl.BlockSpec((D_MODEL, EXPERT_FF), const2),
                  pl.BlockSpec((D_MODEL, EXPERT_FF), const2),
                  pl.BlockSpec((EXPERT_FF, D_MODEL), const2),
                  pl.BlockSpec((1, D_MODEL), const2)],
        out_specs=pl.BlockSpec((MOE_SB, D_MODEL), row),
        scratch_shapes=[pltpu.VMEM((MOE_RPAD, MOE_SB), BF16),
                        pltpu.VMEM((2, MOE_RPAD, D_MODEL), BF16),
                        pltpu.VMEM((MOE_SB, D_MODEL), F32),
                        pltpu.SemaphoreType.DMA((2,))])
    return pl.pallas_call(
        functools.partial(_combine_kernel, sb0=sb0, n_steps=n_steps),
        grid_spec=grid_spec,
        out_shape=jax.ShapeDtypeStruct((n, D_MODEL), F32),
        compiler_params=_cparams(("arbitrary",)),
        name="moe_combine",
    )(plan['cnt'], plan['loc'], *plan['pieces'], ye, gates_t, h2, x1, mod3, sg, su, sd,
      final_w.reshape(1, D_MODEL))


def _moe_sparse(h2_p, h2_s, gt_p, gt_s, x1_p, x1_s, mod_p, mod_s, wg, wu, wd, sg, su, sd, final_w, dseq):
    gates_t = jnp.concatenate([gt_p, gt_s], axis=1)
    plan = _moe_plan(gates_t)
    xe = _dispatch_call(plan, h2_p, h2_s, gates_t)
    ye = _ffn_call(plan, xe, wg, wu, wd)
    n_p = h2_p.shape[0]
    y_p = _combine_call(plan, ye, gates_t, h2_p, x1_p, mod_p, sg, su, sd, final_w, 0, n_p)
    y_s = _combine_call(plan, ye, gates_t, h2_s, x1_s, mod_s, sg, su, sd, final_w, n_p // MOE_SB, dseq)
    return y_p, y_s


def _rope_tables(n_tokens):
    rows = n_tokens // GRID_W
    row = jnp.repeat(jnp.arange(rows, dtype=F32), GRID_W)
    col = jnp.tile(jnp.arange(GRID_W, dtype=F32), rows)
    freqs = ROPE_THETA ** (-jnp.arange(ROT_PAIRS, dtype=F32) / ROT_PAIRS)
    ar = row[:, None] * freqs
    ac = col[:, None] * freqs
    cos = jnp.concatenate([jnp.cos(ar), jnp.cos(ar), jnp.cos(ac), jnp.cos(ac)], axis=1)
    sin = jnp.concatenate([-jnp.sin(ar), jnp.sin(ar), -jnp.sin(ac), jnp.sin(ac)], axis=1)
    return jnp.tile(cos, (1, 2)), jnp.tile(sin, (1, 2))


def kernel(x_prompt, x_sample, c, cache_k, cache_v, state_ssm_re, state_ssm_im, c_ctx, w_ada, b_ada, norm1_w, w_in, ssm_lambda_re, ssm_lambda_im, ssm_log_dt, ssm_b_re, ssm_b_im, ssm_c_re, ssm_c_im, ssm_d, ssm_w_glu, diff_lambda_q, diff_lambda_k, diff_subln_w, w_out, norm2_w, w_router, router_bias, w_exp_gate, w_exp_up, w_exp_down, w_sh_gate, w_sh_up, w_sh_down, final_norm_w):
    depth = w_ada.shape[0]
    assert depth == 1
    l = 0
    lambda_init = 0.8 - 0.6 * math.exp(-0.3 * l)
    bsz, seq, _ = x_prompt.shape
    dbs, dseq, _ = x_sample.shape
    n_p, n_s = bsz * seq, dbs * dseq

    cond8 = jnp.zeros((SUBLANES, D_MODEL), F32).at[:dbs].set(c).at[dbs].set(c_ctx)
    mod = _ada_call(cond8, w_ada[l], b_ada[l])
    mod_s = mod[:dbs].reshape(dbs, 1, 6 * D_MODEL)
    mod_p = mod[dbs:dbs + 1].reshape(1, 1, 6 * D_MODEL)

    w_in_bf = w_in[l].astype(BF16)
    xp2 = x_prompt.reshape(n_p, D_MODEL)
    xs2 = x_sample.reshape(n_s, D_MODEL)
    u_p, q_p, k_p, v_p = _inproj_call(xp2, mod_p, norm1_w[l], w_in_bf, None, n_p, F32)
    u_s, q_s, k_s, v_s = _inproj_call(xs2, mod_s, norm1_w[l], w_in_bf, _rope_tables(dseq), dseq, BF16)

    bblk, cblk, lam_r, lam_i = _s5_weights(ssm_lambda_re[l], ssm_lambda_im[l], ssm_log_dt[l],
                                           ssm_b_re[l], ssm_b_im[l], ssm_c_re[l], ssm_c_im[l])
    lr_p = jnp.broadcast_to(lam_r[:, :, None, :], (2, N_COLBLK, SUBLANES, S5_BLK_STATES))
    li_p = jnp.broadcast_to(lam_i[:, :, None, :], (2, N_COLBLK, SUBLANES, S5_BLK_STATES))
    g_p = bsz // SUBLANES
    h0_p = jnp.zeros((2, g_p, N_COLBLK * 8, SUBLANES, LANES), F32)
    y_p, hfin = _s5_call(u_p.reshape(bsz, seq, S5_WIDTH), bblk, cblk, lr_p, li_p, h0_p, SUBLANES)

    def halves(x):
        x = x.reshape(2, 2, 2, 1, S5_BLK_STATES)
        x = jnp.broadcast_to(x, (2, 2, 2, dbs, S5_BLK_STATES))
        return x.transpose(0, 2, 1, 3, 4).reshape(2, 2, 2 * dbs, S5_BLK_STATES)

    def h0_tiles(s):
        s = s.astype(F32).reshape(dbs, 2, 2, 2, 4, LANES)
        return s.transpose(1, 3, 4, 2, 0, 5).reshape(2, 1, 2, 4, 2 * dbs, LANES)

    h0_s = jnp.concatenate([h0_tiles(state_ssm_re[:, l]), h0_tiles(state_ssm_im[:, l])], axis=3)
    h0_s = h0_s.reshape(2, 1, 16, SUBLANES, LANES)
    y_s, _ = _s5_call(u_s.reshape(dbs, dseq, S5_WIDTH), bblk, cblk, halves(lam_r), halves(lam_i), h0_s, dbs,
                      tb=128)

    lq = diff_lambda_q[l].astype(F32)
    lk = diff_lambda_k[l].astype(F32)
    lam = (jnp.exp(jnp.sum(lq[0] * lk[0])) - jnp.exp(jnp.sum(lq[1] * lk[1])) + lambda_init).reshape(1)
    subln = diff_subln_w[l].astype(F32).reshape(1, VALUE_DIM)
    ao_p = _attn_ctx_call(lam, q_p.reshape(bsz, seq, ATTN_WIDTH), k_p.reshape(bsz, seq, ATTN_WIDTH),
                          v_p.reshape(bsz, seq, ATTN_WIDTH), subln, lambda_init)
    past = cache_k.shape[2]
    ao_s = _attn_lat_call(lam, q_s.reshape(dbs, dseq, ATTN_WIDTH), k_s.reshape(dbs, dseq, ATTN_WIDTH),
                          v_s.reshape(dbs, dseq, ATTN_WIDTH),
                          cache_k[:, l].reshape(dbs, past, ATTN_WIDTH).astype(BF16),
                          cache_v[:, l].reshape(dbs, past, ATTN_WIDTH).astype(BF16), subln, lambda_init)

    wglu_bf = ssm_w_glu[l].astype(BF16)
    wout_bf = w_out[l].astype(BF16)
    wr_t = w_router[l].astype(F32).T
    rbias3 = router_bias[l].astype(F32).reshape(N_EXPERT_GROUPS, N_EXPERTS // N_EXPERT_GROUPS, 1)
    x1_p, h2_p, gates_p = _post_call(xp2, u_p, y_p, ao_p.reshape(n_p, ATTN_WIDTH), mod_p, ssm_d[l],
                                     wglu_bf, wout_bf, norm2_w[l], wr_t, rbias3, n_p)
    x1_s, h2_s, gates_s = _post_call(xs2, u_s, y_s, ao_s.reshape(n_s, ATTN_WIDTH), mod_s, ssm_d[l],
                                     wglu_bf, wout_bf, norm2_w[l], wr_t, rbias3, dseq)

    wg = w_exp_gate[l]
    wu = w_exp_up[l]
    wd = w_exp_down[l]
    sg = w_sh_gate[l].astype(BF16)
    su = w_sh_up[l].astype(BF16)
    sd = w_sh_down[l].astype(BF16)
    y_prompt, y_sample = _moe_sparse(h2_p, h2_s, gates_p, gates_s, x1_p, x1_s, mod_p, mod_s,
                                     wg, wu, wd, sg, su, sd, final_norm_w, dseq)

    new_cache_k = k_p.reshape(bsz, 1, seq, N_DIFF_HEADS, VALUE_DIM)
    new_cache_v = v_p.reshape(bsz, 1, seq, N_DIFF_HEADS, VALUE_DIM)
    hf = hfin.reshape(2, g_p, N_COLBLK, 2, 4, SUBLANES, LANES)
    hf = hf.transpose(3, 1, 5, 0, 2, 4, 6).reshape(2, bsz, 1, 2, S5_GROUPS, S5_STATE)
    return (y_prompt.reshape(bsz, seq, D_MODEL), y_sample.reshape(dbs, dseq, D_MODEL),
            new_cache_k, new_cache_v, hf[0], hf[1])
```

```python
import functools
import math

import jax
import jax.numpy as jnp
from jax import lax
from jax.experimental import pallas as pl
from jax.experimental.pallas import tpu as pltpu

F32 = jnp.float32
BF16 = jnp.bfloat16

D_MODEL = 1024
GRID_W = 64
S5_WIDTH = 512
S5_GROUP = 16
S5_GROUPS = 32
S5_STATE = 64
ATTN_WIDTH = 512
DIFF_HEAD_DIM = 64
VALUE_DIM = 128
N_DIFF_HEADS = 4
IN_WIDTH = S5_WIDTH + 3 * ATTN_WIDTH
ROT_PAIRS = DIFF_HEAD_DIM // 4
ROPE_THETA = 10000.0
N_EXPERTS = 64
TOP_K = 8
N_EXPERT_GROUPS = 8
TOPK_GROUPS = 4
EXPERT_FF = 256
ROUTED_SCALE = 2.5
NORM_EPS = 1e-6

LANES = 128
SUBLANES = 8
S5_COLBLK = 8 * S5_GROUP
S5_BLK_STATES = 8 * S5_STATE
N_COLBLK = S5_WIDTH // S5_COLBLK
VMEM_LIMIT = 56 * 1024 * 1024


def _cparams(sem):
    return pltpu.CompilerParams(dimension_semantics=sem, vmem_limit_bytes=VMEM_LIMIT)


def _ada_kernel(cond_ref, w_ref, b_ref, o_ref):
    c = cond_ref[...]
    s = c * jax.nn.sigmoid(c)
    o_ref[...] = jnp.dot(s.astype(BF16), w_ref[...].astype(BF16),
                         preferred_element_type=F32) + b_ref[...]


def _ada_call(cond8, w_ada, b_ada):
    n = w_ada.shape[1]
    tn = 1536
    return pl.pallas_call(
        _ada_kernel,
        grid=(n // tn,),
        in_specs=[pl.BlockSpec((SUBLANES, D_MODEL), lambda j: (0, 0)),
                  pl.BlockSpec((D_MODEL, tn), lambda j: (0, j)),
                  pl.BlockSpec((1, tn), lambda j: (0, j))],
        out_specs=pl.BlockSpec((SUBLANES, tn), lambda j: (0, j)),
        out_shape=jax.ShapeDtypeStruct((SUBLANES, n), F32),
        compiler_params=_cparams(("arbitrary",)),
        name="adaln",
    )(cond8, w_ada, b_ada.reshape(1, n))


def _rope_apply(t, cos, sin):
    parts = []
    for cidx in range(ATTN_WIDTH // LANES):
        xc = t[:, cidx * LANES:(cidx + 1) * LANES]
        up = pltpu.roll(xc, LANES - ROT_PAIRS, 1)
        dn = pltpu.roll(xc, ROT_PAIRS, 1)
        lane = lax.broadcasted_iota(jnp.int32, xc.shape, 1)
        partner = jnp.where((lane % (2 * ROT_PAIRS)) < ROT_PAIRS, up, dn)
        parts.append(xc * cos + partner * sin)
    return jnp.concatenate(parts, axis=1)


def _inproj_kernel(*refs, rope, kv_dtype):
    if rope:
        x_ref, mod_ref, n1_ref, w_ref, cos_ref, sin_ref, u_ref, q_ref, k_ref, v_ref = refs
    else:
        x_ref, mod_ref, n1_ref, w_ref, u_ref, q_ref, k_ref, v_ref = refs
    x = x_ref[...]
    ms = jnp.mean(x * x, axis=-1, keepdims=True)
    xn = x * lax.rsqrt(ms + NORM_EPS) * n1_ref[...]
    mod = mod_ref[0]
    shift = mod[:, 0:D_MODEL]
    scale = mod[:, D_MODEL:2 * D_MODEL]
    h = xn * (1.0 + scale) + shift
    proj = jnp.dot(h.astype(BF16), w_ref[...], preferred_element_type=F32)
    u_ref[...] = proj[:, :S5_WIDTH]
    q = proj[:, S5_WIDTH:S5_WIDTH + ATTN_WIDTH]
    k = proj[:, S5_WIDTH + ATTN_WIDTH:S5_WIDTH + 2 * ATTN_WIDTH]
    v = proj[:, S5_WIDTH + 2 * ATTN_WIDTH:]
    if rope:
        cos = cos_ref[...]
        sin = sin_ref[...]
        q = _rope_apply(q, cos, sin)
        k = _rope_apply(k, cos, sin)
    q_ref[...] = (q * (DIFF_HEAD_DIM ** -0.5)).astype(BF16)
    k_ref[...] = k.astype(kv_dtype)
    v_ref[...] = v.astype(kv_dtype)


def _inproj_call(x2, mod3, norm1_w, w_in_bf, rope_tabs, seq_len, kv_dtype, tm=512):
    n = x2.shape[0]
    bpb = seq_len // tm
    rope = rope_tabs is not None
    in_specs = [pl.BlockSpec((tm, D_MODEL), lambda i: (i, 0)),
                pl.BlockSpec((1, 1, 6 * D_MODEL), lambda i: (i // bpb, 0, 0)),
                pl.BlockSpec((1, D_MODEL), lambda i: (0, 0)),
                pl.BlockSpec((D_MODEL, IN_WIDTH), lambda i: (0, 0))]
    args = [x2, mod3, norm1_w.reshape(1, D_MODEL), w_in_bf]
    if rope:
        in_specs += [pl.BlockSpec((tm, LANES), lambda i: (i % bpb, 0)),
                     pl.BlockSpec((tm, LANES), lambda i: (i % bpb, 0))]
        args += list(rope_tabs)
    ospec = pl.BlockSpec((tm, S5_WIDTH), lambda i: (i, 0))
    return pl.pallas_call(
        functools.partial(_inproj_kernel, rope=rope, kv_dtype=kv_dtype),
        grid=(n // tm,),
        in_specs=in_specs,
        out_specs=[ospec, ospec, ospec, ospec],
        out_shape=[jax.ShapeDtypeStruct((n, S5_WIDTH), F32),
                   jax.ShapeDtypeStruct((n, ATTN_WIDTH), BF16),
                   jax.ShapeDtypeStruct((n, ATTN_WIDTH), kv_dtype),
                   jax.ShapeDtypeStruct((n, ATTN_WIDTH), kv_dtype)],
        compiler_params=_cparams(("parallel",)),
        name="inproj_rope" if rope else "inproj",
    )(*args)


def _s5_slot(b, q, nseq):
    if nseq == SUBLANES:
        return b, q * 8
    half = q // 2
    return half * nseq + b, (q % 2) * 8


def _s5_kernel(u_ref, b_ref, c_ref, lr_ref, li_ref, h0_ref, y_ref, hfin_ref, buh, hs, hst,
               *, nseq, tb, stride, nqq):
    d = pl.program_id(1)
    c = pl.program_id(2)
    nc = pl.num_programs(2)

    @pl.when(c == 0)
    def _():
        hst[...] = h0_ref[0, 0]
        hs[...] = jnp.zeros(hs.shape, F32)

    u = u_ref[...].reshape(nseq * tb, S5_WIDTH).astype(BF16)
    for q in range(N_COLBLK):
        res = jnp.dot(u[:, q * S5_COLBLK:(q + 1) * S5_COLBLK], b_ref[0, q],
                      preferred_element_type=F32)
        for b in range(nseq):
            slot, ct0 = _s5_slot(b, q, nseq)
            for j in range(8):
                buh[ct0 + j, slot * stride:slot * stride + tb, :] = (
                    res[b * tb:(b + 1) * tb, j * LANES:(j + 1) * LANES])

    for qq in range(nqq):
        lr = [lr_ref[0, qq, :, i * LANES:(i + 1) * LANES] for i in range(4)]
        li = [li_ref[0, qq, :, i * LANES:(i + 1) * LANES] for i in range(4)]
        h_init = tuple(hst[qq * 8 + i] for i in range(8))

        def body(s, carry, qq=qq, lr=lr, li=li):
            t = jnp.where(d == 0, s, tb - 1 - s)
            idx = pl.ds(t, SUBLANES, stride=stride)
            new_r, new_i = [], []
            for i in range(4):
                hr, hi = carry[i], carry[4 + i]
                bur = buh[qq * 8 + i, idx, :]
                bui = buh[qq * 8 + 4 + i, idx, :]
                nr = lr[i] * hr - li[i] * hi + bur
                ni = lr[i] * hi + li[i] * hr + bui
                hs[qq * 8 + i, idx, :] = nr
                hs[qq * 8 + 4 + i, idx, :] = ni
                new_r.append(nr)
                new_i.append(ni)
            return tuple(new_r) + tuple(new_i)

        h_fin = lax.fori_loop(0, tb, body, h_init, unroll=4)
        for i in range(8):
            hst[qq * 8 + i] = h_fin[i]

    rows = nseq * stride
    for q in range(N_COLBLK):
        slot0, ct0 = _s5_slot(0, q, nseq)
        hmat = jnp.concatenate(
            [hs[ct0 + j, slot0 * stride:slot0 * stride + rows, :] for j in range(8)], axis=1)
        yq = jnp.dot(hmat.astype(BF16), c_ref[0, q], preferred_element_type=F32)
        for b in range(nseq):
            y_ref[0, b, :, q * S5_COLBLK:(q + 1) * S5_COLBLK] = yq[b * stride:b * stride + tb, :]

    @pl.when(c == nc - 1)
    def _():
        hfin_ref[0, 0] = hst[...]


def _s5_call(u3, bblk, cblk, lr, li, h0, nseq, tb=64):
    nseq_total, seq_len, _ = u3.shape
    g = nseq_total // nseq
    nc = seq_len // tb
    stride = tb + SUBLANES // 2
    nqq = lr.shape[1]
    nct = nqq * 8

    def tmap(gi, d, c):
        return c + d * (nc - 1 - 2 * c)

    return pl.pallas_call(
        functools.partial(_s5_kernel, nseq=nseq, tb=tb, stride=stride, nqq=nqq),
        grid=(g, 2, nc),
        in_specs=[
            pl.BlockSpec((nseq, tb, S5_WIDTH), lambda gi, d, c: (gi, tmap(gi, d, c), 0)),
            pl.BlockSpec((1, N_COLBLK, S5_COLBLK, 2 * S5_BLK_STATES), lambda gi, d, c: (d, 0, 0, 0)),
            pl.BlockSpec((1, N_COLBLK, 2 * S5_BLK_STATES, S5_COLBLK), lambda gi, d, c: (d, 0, 0, 0)),
            pl.BlockSpec((1, nqq, SUBLANES, S5_BLK_STATES), lambda gi, d, c: (d, 0, 0, 0)),
            pl.BlockSpec((1, nqq, SUBLANES, S5_BLK_STATES), lambda gi, d, c: (d, 0, 0, 0)),
            pl.BlockSpec((1, 1, nct, SUBLANES, LANES), lambda gi, d, c: (d, gi, 0, 0, 0)),
        ],
        out_specs=[
            pl.BlockSpec((1, nseq, tb, S5_WIDTH), lambda gi, d, c: (d, gi, tmap(gi, d, c), 0)),
            pl.BlockSpec((1, 1, nct, SUBLANES, LANES), lambda gi, d, c: (d, gi, 0, 0, 0)),
        ],
        out_shape=[jax.ShapeDtypeStruct((2, nseq_total, seq_len, S5_WIDTH), F32),
                   jax.ShapeDtypeStruct((2, g, nct, SUBLANES, LANES), F32)],
        scratch_shapes=[pltpu.VMEM((nct, SUBLANES * stride, LANES), F32),
                        pltpu.VMEM((nct, SUBLANES * stride, LANES), F32),
                        pltpu.VMEM((nct, SUBLANES, LANES), F32)],
        compiler_params=_cparams(("parallel", "parallel", "arbitrary")),
        name="s5_scan_%d" % nseq,
    )(u3, bblk, cblk, lr, li, h0)


def _s5_weights(lam_re, lam_im, log_dt, b_re, b_im, c_re, c_im):
    dt = jnp.exp(log_dt.astype(F32))[..., None]
    lr = lam_re.astype(F32)
    li = lam_im.astype(F32)
    mag = jnp.exp(lr * dt)
    lbr = mag * jnp.cos(li * dt)
    lbi = mag * jnp.sin(li * dt)
    a = lbr - 1.0
    den = lr * lr + li * li
    cr = (a * lr + lbi * li) / den
    ci = (lbi * lr - a * li) / den
    bbr = cr[..., None] * b_re - ci[..., None] * b_im
    bbi = cr[..., None] * b_im + ci[..., None] * b_re
    eye = jnp.eye(8, dtype=F32)

    def pack_b(x):
        x = x.reshape(2, N_COLBLK, 8, S5_STATE, S5_GROUP).transpose(0, 1, 2, 4, 3)
        x = jnp.einsum('dqghn,gk->dqghkn', x, eye)
        return x.reshape(2, N_COLBLK, S5_COLBLK, S5_BLK_STATES)

    def pack_c(x):
        x = x.reshape(2, N_COLBLK, 8, S5_GROUP, S5_STATE).transpose(0, 1, 2, 4, 3)
        x = jnp.einsum('dqgnh,gk->dqgnkh', x, eye)
        return x.reshape(2, N_COLBLK, S5_BLK_STATES, S5_COLBLK)

    bblk = jnp.concatenate([pack_b(bbr), pack_b(bbi)], axis=-1).astype(BF16)
    cblk = jnp.concatenate([pack_c(c_re.astype(F32)), pack_c(-c_im.astype(F32))], axis=-2).astype(BF16)
    lam_r = lbr.reshape(2, N_COLBLK, S5_BLK_STATES)
    lam_i = lbi.reshape(2, N_COLBLK, S5_BLK_STATES)
    return bblk, cblk, lam_r, lam_i


ATTN_KC = 512


def _lane_fold(x, op):
    acc = x[:, 0:LANES]
    for j in range(1, x.shape[1] // LANES):
        acc = op(acc, x[:, j * LANES:(j + 1) * LANES])
    return acc


def _attn_ops(q, chunks, s_ref):
    lane = lax.broadcasted_iota(jnp.int32, q.shape, 1)
    dn = (((1,), (1,)), ((), ()))
    qms = [jnp.where(lane < DIFF_HEAD_DIM, q, jnp.zeros_like(q)),
           jnp.where(lane >= DIFF_HEAD_DIM, q, jnp.zeros_like(q))]
    offs = [0]
    for load_k, _ in chunks:
        offs.append(offs[-1] + load_k().shape[0])

    def score_chunk(m, ci, m128):
        s = lax.dot_general(qms[m], chunks[ci][0](), dn, preferred_element_type=F32)
        s_ref[m, :, offs[ci]:offs[ci + 1]] = s
        f = _lane_fold(s, jnp.maximum)
        return f if m128 is None else jnp.maximum(m128, f)

    def value_chunk(m, ci, mx, acc):
        p = jnp.exp(s_ref[m, :, offs[ci]:offs[ci + 1]] - mx)
        v = chunks[ci][1]()
        t = jnp.dot(p.astype(BF16), jnp.concatenate([v, jnp.ones_like(v)], axis=1), preferred_element_type=F32)
        return t if acc is None else acc + t

    return score_chunk, value_chunk


def _row_max(m128):
    return m128.max(axis=-1, keepdims=True)


def _normalized(acc):
    return acc[:, :VALUE_DIM] * (1.0 / acc[:, VALUE_DIM:VALUE_DIM + 1])


def _sub_ln(o, subln, lambda_init):
    ms = jnp.mean(o * o, axis=-1, keepdims=True)
    return o * lax.rsqrt(ms + NORM_EPS) * subln * (1.0 - lambda_init)


def _attn_ctx_kernel(lam_ref, q_ref, k_ref, v_ref, w_ref, o_ref, s_ref, *, lambda_init):
    chunks = [(lambda: k_ref[0].astype(BF16), lambda: v_ref[0].astype(BF16))]
    score_chunk, value_chunk = _attn_ops(q_ref[0], chunks, s_ref)
    mx0 = _row_max(score_chunk(0, 0, None))
    m1 = score_chunk(1, 0, None)
    a0 = value_chunk(0, 0, mx0, None)
    a1 = value_chunk(1, 0, _row_max(m1), None)
    o = _normalized(a0) - lam_ref[0] * _normalized(a1)
    o_ref[0] = _sub_ln(o, w_ref[...], lambda_init).astype(BF16)


def _attn_lat_kernel(lam_ref, q_ref, k_ref, v_ref, ck_ref, cv_ref, w_ref, o_ref, s_ref, *, lambda_init):
    chunks = []
    for c in range(k_ref.shape[1] // ATTN_KC):
        chunks.append((lambda c=c: k_ref[0, c * ATTN_KC:(c + 1) * ATTN_KC, :],
                       lambda c=c: v_ref[0, c * ATTN_KC:(c + 1) * ATTN_KC, :]))
    for c in range(ck_ref.shape[1] // ATTN_KC):
        chunks.append((lambda c=c: ck_ref[0, c * ATTN_KC:(c + 1) * ATTN_KC, :],
                       lambda c=c: cv_ref[0, c * ATTN_KC:(c + 1) * ATTN_KC, :]))
    nchunk = len(chunks)
    score_chunk, value_chunk = _attn_ops(q_ref[0], chunks, s_ref)

    m0 = m1 = a0 = a1 = None
    for ci in range(nchunk):
        m0 = score_chunk(0, ci, m0)
    mx0 = _row_max(m0)
    for ci in range(nchunk):
        m1 = score_chunk(1, ci, m1)
        a0 = value_chunk(0, ci, mx0, a0)
    mx1 = _row_max(m1)
    for ci in range(nchunk):
        a1 = value_chunk(1, ci, mx1, a1)
    o = _normalized(a0) - lam_ref[0] * _normalized(a1)
    o_ref[0] = _sub_ln(o, w_ref[...], lambda_init).astype(BF16)


def _attn_ctx_call(lam, q3, k3, v3, subln, lambda_init):
    bsz, seq_len, _ = q3.shape
    spec = pl.BlockSpec((1, seq_len, VALUE_DIM), lambda b, h: (b, 0, h))
    return pl.pallas_call(
        functools.partial(_attn_ctx_kernel, lambda_init=lambda_init),
        grid=(bsz, N_DIFF_HEADS),
        in_specs=[pl.BlockSpec(memory_space=pltpu.SMEM), spec, spec, spec,
                  pl.BlockSpec((1, VALUE_DIM), lambda b, h: (0, 0))],
        out_specs=spec,
        out_shape=jax.ShapeDtypeStruct((bsz, seq_len, ATTN_WIDTH), BF16),
        scratch_shapes=[pltpu.VMEM((2, seq_len, seq_len), F32)],
        compiler_params=_cparams(("parallel", "parallel")),
        name="attn_ctx",
    )(lam, q3, k3, v3, subln)


def _attn_lat_call(lam, q3, k3, v3, ck3, cv3, subln, lambda_init, tq=256):
    bsz, seq_len, _ = q3.shape
    ctx_len = ck3.shape[1]
    qspec = pl.BlockSpec((1, tq, VALUE_DIM), lambda b, h, i: (b, i, h))
    kspec = pl.BlockSpec((1, seq_len, VALUE_DIM), lambda b, h, i: (b, 0, h))
    cspec = pl.BlockSpec((1, ctx_len, VALUE_DIM), lambda b, h, i: (b, 0, h))
    return pl.pallas_call(
        functools.partial(_attn_lat_kernel, lambda_init=lambda_init),
        grid=(bsz, N_DIFF_HEADS, seq_len // tq),
        in_specs=[pl.BlockSpec(memory_space=pltpu.SMEM), qspec, kspec, kspec, cspec, cspec,
                  pl.BlockSpec((1, VALUE_DIM), lambda b, h, i: (0, 0))],
        out_specs=qspec,
        out_shape=jax.ShapeDtypeStruct((bsz, seq_len, ATTN_WIDTH), BF16),
        scratch_shapes=[pltpu.VMEM((2, tq, seq_len + ctx_len), F32)],
        compiler_params=_cparams(("parallel", "parallel", "arbitrary")),
        name="attn_lat",
    )(lam, q3, k3, v3, ck3, cv3, subln)


def _route(logits_t, bias3, tm):
    ng, ge = N_EXPERT_GROUPS, N_EXPERTS // N_EXPERT_GROUPS
    neg = jnp.float32(-jnp.inf)
    sc = jax.nn.sigmoid(logits_t).reshape(ng, ge, tm)
    bi = sc + bias3
    eio = lax.broadcasted_iota(jnp.int32, (ng, ge, tm), 1).astype(F32)
    gio = lax.broadcasted_iota(jnp.int32, (ng, ge, tm), 0).astype(F32)
    m1 = bi.max(axis=1, keepdims=True)
    i1 = jnp.where(bi == m1, eio, float(ge)).min(axis=1, keepdims=True)
    m2 = jnp.where(eio == i1, neg, bi).max(axis=1, keepdims=True)
    gs = jnp.broadcast_to(m1 + m2, (ng, ge, tm))
    cnt = jnp.zeros((ng, ge, tm), F32)
    for g2 in range(ng):
        o = gs[g2:g2 + 1]
        better = (o > gs) | ((o == gs) & (gio > float(g2)))
        cnt = cnt + jnp.where(better, 1.0, 0.0)
    v = jnp.where(cnt < float(TOPK_GROUPS), bi, neg)
    eidx = gio * float(ge) + eio
    selm = jnp.zeros((ng, ge, tm), F32)
    for _ in range(TOP_K):
        m = v.max(axis=0, keepdims=True).max(axis=1, keepdims=True)
        ix = jnp.where(v == m, eidx, float(N_EXPERTS)).min(axis=0, keepdims=True).min(axis=1, keepdims=True)
        oh = eidx == ix
        selm = jnp.where(oh, 1.0, selm)
        v = jnp.where(oh, neg, v)
    selsc = selm * sc
    ssum = selsc.sum(axis=0, keepdims=True).sum(axis=1, keepdims=True)
    return (selsc / ssum * ROUTED_SCALE).reshape(N_EXPERTS, tm)


def _post_kernel(x_ref, u_ref, yf_ref, yb_ref, ao_ref, mod_ref, d_ref, wglu_ref, wout_ref, n2_ref,
                 wrt_ref, rb_ref, x1_ref, h2_ref, gates_ref, *, tm):
    u = u_ref[...]
    y = u * d_ref[...] + yf_ref[0] + yb_ref[0]
    g = jax.nn.gelu(y)
    s5 = g * jax.nn.sigmoid(jnp.dot(g.astype(BF16), wglu_ref[...], preferred_element_type=F32))
    mixed = (jnp.dot(s5.astype(BF16), wout_ref[0:S5_WIDTH, :], preferred_element_type=F32)
             + jnp.dot(ao_ref[...], wout_ref[S5_WIDTH:, :], preferred_element_type=F32))
    mod = mod_ref[0]
    gate1 = mod[:, 2 * D_MODEL:3 * D_MODEL]
    shift2 = mod[:, 3 * D_MODEL:4 * D_MODEL]
    scale2 = mod[:, 4 * D_MODEL:5 * D_MODEL]
    x1 = x_ref[...] + gate1 * mixed
    x1_ref[...] = x1
    ms = jnp.mean(x1 * x1, axis=-1, keepdims=True)
    h2 = x1 * lax.rsqrt(ms + NORM_EPS) * n2_ref[...] * (1.0 + scale2) + shift2
    h2_ref[...] = h2.astype(BF16)
    logits_t = lax.dot_general(wrt_ref[...], h2, (((1,), (1,)), ((), ())),
                               preferred_element_type=F32, precision=lax.Precision.HIGHEST)
    gates_ref[...] = _route(logits_t, rb_ref[...], tm)


def _post_call(x2, u2, y4, ao2, mod3, ssm_d, wglu_bf, wout_bf, norm2_w, wr_t, rbias3, seq_len, tm=512):
    n = x2.shape[0]
    bpb = seq_len // tm
    y3 = y4.reshape(2, n, S5_WIDTH)
    row = lambda i: (i, 0)
    const2 = lambda i: (0, 0)
    return pl.pallas_call(
        functools.partial(_post_kernel, tm=tm),
        grid=(n // tm,),
        in_specs=[pl.BlockSpec((tm, D_MODEL), row),
                  pl.BlockSpec((tm, S5_WIDTH), row),
                  pl.BlockSpec((1, tm, S5_WIDTH), lambda i: (0, i, 0)),
                  pl.BlockSpec((1, tm, S5_WIDTH), lambda i: (1, i, 0)),
                  pl.BlockSpec((tm, ATTN_WIDTH), row),
                  pl.BlockSpec((1, 1, 6 * D_MODEL), lambda i: (i // bpb, 0, 0)),
                  pl.BlockSpec((1, S5_WIDTH), const2),
                  pl.BlockSpec((S5_WIDTH, S5_WIDTH), const2),
                  pl.BlockSpec((D_MODEL, D_MODEL), const2),
                  pl.BlockSpec((1, D_MODEL), const2),
                  pl.BlockSpec((N_EXPERTS, D_MODEL), const2),
                  pl.BlockSpec((N_EXPERT_GROUPS, N_EXPERTS // N_EXPERT_GROUPS, 1), lambda i: (0, 0, 0))],
        out_specs=[pl.BlockSpec((tm, D_MODEL), row),
                   pl.BlockSpec((tm, D_MODEL), row),
                   pl.BlockSpec((N_EXPERTS, tm), lambda i: (0, i))],
        out_shape=[jax.ShapeDtypeStruct((n, D_MODEL), F32),
                   jax.ShapeDtypeStruct((n, D_MODEL), BF16),
                   jax.ShapeDtypeStruct((N_EXPERTS, n), F32)],
        compiler_params=_cparams(("parallel",)),
        name="post_mix_route",
    )(x2, u2, y3, y3, ao2, mod3, ssm_d.reshape(1, S5_WIDTH), wglu_bf, wout_bf,
      norm2_w.reshape(1, D_MODEL), wr_t, rbias3)


def _swiglu_bf(t, wg, wu, wd):
    a = jnp.dot(t, wg, preferred_element_type=F32)
    b = jnp.dot(t, wu, preferred_element_type=F32)
    act = a * jax.nn.sigmoid(a) * b
    return jnp.dot(act.astype(BF16), wd, preferred_element_type=F32)


MOE_SB = 256
MOE_SEG = 16
MOE_TF = 1024
MOE_XW = D_MODEL
MOE_STATIC_CHUNKS = 3
MOE_CHUNK = 512
MOE_DMA_SIZES = tuple(2 ** k for k in range(MOE_TF.bit_length() - 2, 3, -1))
MOE_RPAD = -(-(TOP_K * MOE_SB + N_EXPERTS * (MOE_SEG - 1) + MOE_STATIC_CHUNKS * MOE_SEG) // MOE_CHUNK) * MOE_CHUNK


MOE_PIECE_ROWS = (4 * MOE_SEG, 3 * MOE_SEG, 2 * MOE_SEG, MOE_SEG)
MOE_PIECE_WIDTH = (MOE_RPAD // (4 * MOE_SEG), N_EXPERTS, N_EXPERTS, N_EXPERTS)
MOE_NPIECE_REFS = 2 * len(MOE_PIECE_ROWS)
MOE_PACK = 4096


def _seg_copies(pieces, sb, make_copy, act):
    for ci, (width, rows) in enumerate(zip(MOE_PIECE_WIDTH, MOE_PIECE_ROWS)):
        p_ref, n_ref = pieces[2 * ci], pieces[2 * ci + 1]

        def body(k, carry, p_ref=p_ref, width=width, rows=rows):
            v = p_ref[sb * width + k]
            act(make_copy(pl.multiple_of(v & (MOE_PACK - 1), MOE_SEG),
                          pl.multiple_of(v >> (MOE_PACK.bit_length() - 1), MOE_SEG), rows))
            return carry
        lax.fori_loop(0, n_ref[sb], body, 0)


MOE_WAIT_SIZES = tuple(2 ** k for k in range((MOE_RPAD - 1).bit_length() - 1, 3, -1))


def _sub_block_rows(cnt_ref, loc_ref, sb):
    last = sb * N_EXPERTS + N_EXPERTS - 1
    return loc_ref[last] + cnt_ref[last]


def _wait_rows(rows, make_copy):
    for b in MOE_WAIT_SIZES:
        @pl.when((rows & b) != 0)
        def _(b=b):
            make_copy(b).wait()


def _build_onehot(gt, p_ref, cnt_ref, loc_ref, sb, weighted):
    t = gt.shape[1]
    selm = gt > 0.0
    r = lax.broadcasted_iota(jnp.int32, (t, t), 0)
    c = lax.broadcasted_iota(jnp.int32, (t, t), 1)
    upper = jnp.where(r < c, 1.0, 0.0).astype(BF16)
    rank = jnp.dot(jnp.where(selm, 1.0, 0.0).astype(BF16), upper, preferred_element_type=F32)
    key = jnp.where(selm, rank, -1.0)
    j16 = lax.broadcasted_iota(jnp.int32, (MOE_SEG, t), 0).astype(F32)
    p_ref[...] = jnp.zeros(p_ref.shape, BF16)

    for e in range(N_EXPERTS):
        off = loc_ref[sb * N_EXPERTS + e]
        n = cnt_ref[sb * N_EXPERTS + e]
        row = key[e:e + 1, :]
        val = gt[e:e + 1, :] if weighted else 1.0

        def put(cidx, row=row, off=off, val=val):
            first = cidx * MOE_SEG
            firstf = float(first) if isinstance(first, int) else first.astype(F32)
            tile = jnp.where(row == j16 + firstf, val, 0.0).astype(BF16)
            p_ref[pl.ds(pl.multiple_of(off + first, MOE_SEG), MOE_SEG), :] = tile

        for cidx in range(MOE_STATIC_CHUNKS):
            put(cidx)

        def extra(cidx, carry, put=put):
            put(cidx)
            return carry
        lax.fori_loop(MOE_STATIC_CHUNKS, n // MOE_SEG, extra, 0)


def _dispatch_kernel(cnt_ref, loc_ref, *refs, nsb_p, n_sb):
    pieces = refs[:MOE_NPIECE_REFS]
    padoff_ref, padn_ref, hp_ref, hs_ref, gt_ref, xe_ref, p_ref, xg_ref, z_ref, sem = refs[MOE_NPIECE_REFS:]
    s = pl.program_id(0)
    slot = s % 2

    def copies(sb, slot_, act):
        def mk(off, g, b):
            return pltpu.make_async_copy(xg_ref.at[slot_, pl.ds(off, b)], xe_ref.at[pl.ds(g, b)], sem.at[slot_])
        _seg_copies(pieces, sb, mk, act)

    def wait_all(sb, slot_):
        _wait_rows(_sub_block_rows(cnt_ref, loc_ref, sb),
                   lambda b: pltpu.make_async_copy(xg_ref.at[slot_, pl.ds(0, b)], xe_ref.at[pl.ds(0, b)],
                                                   sem.at[slot_]))

    @pl.when(s >= 2)
    def _():
        wait_all(s - 2, slot)

    gt = gt_ref[...]
    _build_onehot(gt, p_ref, cnt_ref, loc_ref, s, weighted=False)
    xext = jnp.where(s < nsb_p, hp_ref[...], hs_ref[...])
    rows = _sub_block_rows(cnt_ref, loc_ref, s)
    for i in range(MOE_RPAD // MOE_CHUNK):
        @pl.when(i * MOE_CHUNK < rows)
        def _(i=i):
            xg_ref[slot, i * MOE_CHUNK:(i + 1) * MOE_CHUNK, :] = jnp.dot(
                p_ref[i * MOE_CHUNK:(i + 1) * MOE_CHUNK, :], xext, preferred_element_type=F32).astype(BF16)
    copies(s, slot, lambda cp: cp.start())

    @pl.when(s == n_sb - 1)
    def _():
        if n_sb >= 2:
            wait_all(s - 1, 1 - slot)
        wait_all(s, slot)
        z_ref[...] = jnp.zeros(z_ref.shape, BF16)

        def pads(act):
            def body(e, carry):
                n = padn_ref[e]
                off = padoff_ref[e]
                for b in MOE_DMA_SIZES:
                    @pl.when((n & b) != 0)
                    def _(b=b):
                        done = n & (-2 * b)
                        act(pltpu.make_async_copy(
                            z_ref.at[pl.ds(0, b)],
                            xe_ref.at[pl.ds(pl.multiple_of(off + done, MOE_SEG), b)], sem.at[0]))
                return carry
            lax.fori_loop(0, N_EXPERTS, body, 0)
        pads(lambda cp: cp.start())
        pads(lambda cp: cp.wait())


def _ffn_kernel(te_ref, tidx_ref, tcode_ref, xe_ref, wg_ref, wu_ref, wd_ref, ye_ref, wgub, wdb):
    i = pl.program_id(0)
    code = tcode_ref[i]

    @pl.when(code == 2)
    def _():
        wgub[:, :EXPERT_FF] = wg_ref[0].astype(BF16)
        wgub[:, EXPERT_FF:] = wu_ref[0].astype(BF16)
        wdb[...] = wd_ref[0].astype(BF16)

    @pl.when(code != 0)
    def _():
        ab = jnp.dot(xe_ref[...], wgub[...], preferred_element_type=F32)
        a = ab[:, :EXPERT_FF]
        act = a * jax.nn.sigmoid(a) * ab[:, EXPERT_FF:]
        ye_ref[...] = jnp.dot(act.astype(BF16), wdb[...], preferred_element_type=F32).astype(BF16)


def _combine_kernel(cnt_ref, loc_ref, *refs, sb0, n_steps):
    pieces = refs[:MOE_NPIECE_REFS]
    (ye_ref, gt_ref, h_ref, x1_ref, mod_ref, sg_ref, su_ref, sd_ref, fn_ref, o_ref,
     p_ref, ys_ref, acc_ref, sem) = refs[MOE_NPIECE_REFS:]
    i = pl.program_id(0)
    slot = i % 2
    sb = sb0 + i

    def copies(sb_, slot_, act):
        def mk(off, g, b):
            return pltpu.make_async_copy(ye_ref.at[pl.ds(g, b)], ys_ref.at[slot_, pl.ds(off, b)], sem.at[slot_])
        _seg_copies(pieces, sb_, mk, act)

    @pl.when(i == 0)
    def _():
        ys_ref[...] = jnp.zeros(ys_ref.shape, BF16)
        copies(sb, slot, lambda cp: cp.start())

    @pl.when(i + 1 < n_steps)
    def _():
        copies(sb + 1, 1 - slot, lambda cp: cp.start())

    acc_ref[...] = _swiglu_bf(h_ref[...], sg_ref[...], su_ref[...], sd_ref[...])
    _build_onehot(gt_ref[...], p_ref, cnt_ref, loc_ref, sb, weighted=True)
    rows = _sub_block_rows(cnt_ref, loc_ref, sb)
    _wait_rows(rows, lambda b: pltpu.make_async_copy(ye_ref.at[pl.ds(0, b)], ys_ref.at[slot, pl.ds(0, b)],
                                                     sem.at[slot]))
    for c in range(MOE_RPAD // MOE_CHUNK):
        @pl.when(c * MOE_CHUNK < rows)
        def _(c=c):
            acc_ref[...] += lax.dot_general(
                p_ref[c * MOE_CHUNK:(c + 1) * MOE_CHUNK, :], ys_ref[slot, c * MOE_CHUNK:(c + 1) * MOE_CHUNK, :],
                (((0,), (0,)), ((), ())), preferred_element_type=F32)
    gate2 = mod_ref[0][:, 5 * D_MODEL:6 * D_MODEL]
    x2 = x1_ref[...] + gate2 * acc_ref[...]
    ms = jnp.mean(x2 * x2, axis=-1, keepdims=True)
    o_ref[...] = x2 * lax.rsqrt(ms + NORM_EPS) * fn_ref[...]


def _moe_plan(gates_t):
    ne, n = gates_t.shape
    n_sb = n // MOE_SB
    cnt = jnp.sum((gates_t > 0.0).reshape(ne, n_sb, MOE_SB), axis=-1, dtype=jnp.int32).T
    cnt16 = (cnt + MOE_SEG - 1) // MOE_SEG * MOE_SEG
    loc = jnp.cumsum(cnt16, axis=1) - cnt16
    tot = jnp.sum(cnt16, axis=0)
    totp = (tot + MOE_TF - 1) // MOE_TF * MOE_TF
    ends = jnp.cumsum(totp)
    base = ends - totp
    goff = base[None, :] + jnp.cumsum(cnt16, axis=0) - cnt16
    rows_max = TOP_K * n + n_sb * ne * (MOE_SEG - 1) + ne * (MOE_TF - MOE_SEG)
    nt_max = -(-rows_max // MOE_TF)
    tiles = jnp.arange(nt_max, dtype=jnp.int32)
    used = ends[-1]
    valid = tiles * MOE_TF < used
    tidx = jnp.where(valid, tiles, jnp.maximum(used // MOE_TF - 1, 0))
    te = jnp.sum(ends[None, :] <= (tidx * MOE_TF)[:, None], axis=1, dtype=jnp.int32)
    te = jnp.minimum(te, ne - 1)
    first = jnp.concatenate([jnp.ones((1,), jnp.bool_), te[1:] != te[:-1]])
    tcode = jnp.where(valid, 1 + first.astype(jnp.int32), 0)

    def piece_list(npieces, first_row, width, rows):
        cum = jnp.cumsum(npieces, axis=1)
        k = jnp.arange(width, dtype=jnp.int32)
        ek = jnp.minimum(jnp.sum(cum[:, None, :] <= k[None, :, None], axis=2, dtype=jnp.int32), ne - 1)
        onehot = ek[:, :, None] == jnp.arange(ne, dtype=jnp.int32)[None, None, :]
        pick = lambda a: jnp.sum(jnp.where(onehot, a[:, None, :], 0), axis=2, dtype=jnp.int32)
        row = pick(first_row) + rows * (k[None, :] - pick(cum - npieces))
        packed = (pick(goff) + row) * MOE_PACK + pick(loc) + row
        return packed.reshape(-1).astype(jnp.int32), cum[:, -1].astype(jnp.int32)

    big = MOE_PIECE_ROWS[0]
    nbig = cnt16 // big
    pieces = list(piece_list(nbig, jnp.zeros_like(cnt16), MOE_PIECE_WIDTH[0], big))
    for width, rows in zip(MOE_PIECE_WIDTH[1:], MOE_PIECE_ROWS[1:]):
        pieces += list(piece_list((cnt16 % big == rows).astype(jnp.int32), nbig * big, width, rows))
    return dict(cnt=cnt16.reshape(-1), loc=loc.reshape(-1).astype(jnp.int32), pieces=tuple(pieces),
                padoff=(base + tot).astype(jnp.int32), padn=(totp - tot).astype(jnp.int32),
                te=te, tidx=tidx.astype(jnp.int32), tcode=tcode, nt_max=nt_max)


def _dispatch_call(plan, h2_p, h2_s, gates_t):
    nsb_p = h2_p.shape[0] // MOE_SB
    n_sb = gates_t.shape[1] // MOE_SB
    grid_spec = pltpu.PrefetchScalarGridSpec(
        num_scalar_prefetch=4 + MOE_NPIECE_REFS, grid=(n_sb,),
        in_specs=[pl.BlockSpec((MOE_SB, D_MODEL), lambda s, *_: (jnp.minimum(s, nsb_p - 1), 0)),
                  pl.BlockSpec((MOE_SB, D_MODEL), lambda s, *_: (jnp.maximum(s - nsb_p, 0), 0)),
                  pl.BlockSpec((N_EXPERTS, MOE_SB), lambda s, *_: (0, s))],
        out_specs=pl.BlockSpec(memory_space=pl.ANY),
        scratch_shapes=[pltpu.VMEM((MOE_RPAD, MOE_SB), BF16),
                        pltpu.VMEM((2, MOE_RPAD, MOE_XW), BF16),
                        pltpu.VMEM((MOE_DMA_SIZES[0], MOE_XW), BF16),
                        pltpu.SemaphoreType.DMA((2,))])
    return pl.pallas_call(
        functools.partial(_dispatch_kernel, nsb_p=nsb_p, n_sb=n_sb),
        grid_spec=grid_spec,
        out_shape=jax.ShapeDtypeStruct((plan['nt_max'] * MOE_TF, MOE_XW), BF16),
        compiler_params=_cparams(("arbitrary",)),
        name="moe_dispatch",
    )(plan['cnt'], plan['loc'], *plan['pieces'], plan['padoff'], plan['padn'], h2_p, h2_s, gates_t)


def _ffn_call(plan, xe, wg, wu, wd):
    grid_spec = pltpu.PrefetchScalarGridSpec(
        num_scalar_prefetch=3, grid=(plan['nt_max'],),
        in_specs=[pl.BlockSpec((MOE_TF, MOE_XW), lambda i, te, tidx, tv: (tidx[i], 0)),
                  pl.BlockSpec((1, D_MODEL, EXPERT_FF), lambda i, te, tidx, tv: (te[i], 0, 0)),
                  pl.BlockSpec((1, D_MODEL, EXPERT_FF), lambda i, te, tidx, tv: (te[i], 0, 0)),
                  pl.BlockSpec((1, EXPERT_FF, D_MODEL), lambda i, te, tidx, tv: (te[i], 0, 0))],
        out_specs=pl.BlockSpec((MOE_TF, D_MODEL), lambda i, te, tidx, tv: (tidx[i], 0)),
        scratch_shapes=[pltpu.VMEM((D_MODEL, 2 * EXPERT_FF), BF16),
                        pltpu.VMEM((EXPERT_FF, D_MODEL), BF16)])
    return pl.pallas_call(
        _ffn_kernel,
        grid_spec=grid_spec,
        out_shape=jax.ShapeDtypeStruct((xe.shape[0], D_MODEL), BF16),
        compiler_params=_cparams(("arbitrary",)),
        name="moe_ffn",
    )(plan['te'], plan['tidx'], plan['tcode'], xe, wg, wu, wd)


def _combine_call(plan, ye, gates_t, h2, x1, mod3, sg, su, sd, final_w, sb0, seq_len):
    n = h2.shape[0]
    n_steps = n // MOE_SB
    bpb = seq_len // MOE_SB
    row = lambda i, *_: (i, 0)
    const2 = lambda i, *_: (0, 0)
    grid_spec = pltpu.PrefetchScalarGridSpec(
        num_scalar_prefetch=2 + MOE_NPIECE_REFS, grid=(n_steps,),
        in_specs=[pl.BlockSpec(memory_space=pl.ANY),
                  pl.BlockSpec((N_EXPERTS, MOE_SB), lambda i, *_: (0, sb0 + i)),
                  pl.BlockSpec((MOE_SB, D_MODEL), row),
                  pl.BlockSpec((MOE_SB, D_MODEL), row),
                  pl.BlockSpec((1, 1, 6 * D_MODEL), lambda i, *_: (i // bpb, 0, 0)),
                  pl.BlockSpec((D_MODEL, EXPERT_FF), const2),
                  pl.BlockSpec((D_MODEL, EXPERT_FF), const2),
                  pl.BlockSpec((EXPERT_FF, D_MODEL), const2),
                  pl.BlockSpec((1, D_MODEL), const2)],
        out_specs=pl.BlockSpec((MOE_SB, D_MODEL), row),
        scratch_shapes=[pltpu.VMEM((MOE_RPAD, MOE_SB), BF16),
                        pltpu.VMEM((2, MOE_RPAD, D_MODEL), BF16),
                        pltpu.VMEM((MOE_SB, D_MODEL), F32),
                        pltpu.SemaphoreType.DMA((2,))])
    return pl.pallas_call(
        functools.partial(_combine_kernel, sb0=sb0, n_steps=n_steps),
        grid_spec=grid_spec,
        out_shape=jax.ShapeDtypeStruct((n, D_MODEL), F32),
        compiler_params=_cparams(("arbitrary",)),
        name="moe_combine",
    )(plan['cnt'], plan['loc'], *plan['pieces'], ye, gates_t, h2, x1, mod3, sg, su, sd,
      final_w.reshape(1, D_MODEL))


def _moe_sparse(h2_p, h2_s, gt_p, gt_s, x1_p, x1_s, mod_p, mod_s, wg, wu, wd, sg, su, sd, final_w, dseq):
    gates_t = jnp.concatenate([gt_p, gt_s], axis=1)
    plan = _moe_plan(gates_t)
    xe = _dispatch_call(plan, h2_p, h2_s, gates_t)
    ye = _ffn_call(plan, xe, wg, wu, wd)
    n_p = h2_p.shape[0]
    y_p = _combine_call(plan, ye, gates_t, h2_p, x1_p, mod_p, sg, su, sd, final_w, 0, n_p)
    y_s = _combine_call(plan, ye, gates_t, h2_s, x1_s, mod_s, sg, su, sd, final_w, n_p // MOE_SB, dseq)
    return y_p, y_s


def _rope_tables(n_tokens):
    rows = n_tokens // GRID_W
    row = jnp.repeat(jnp.arange(rows, dtype=F32), GRID_W)
    col = jnp.tile(jnp.arange(GRID_W, dtype=F32), rows)
    freqs = ROPE_THETA ** (-jnp.arange(ROT_PAIRS, dtype=F32) / ROT_PAIRS)
    ar = row[:, None] * freqs
    ac = col[:, None] * freqs
    cos = jnp.concatenate([jnp.cos(ar), jnp.cos(ar), jnp.cos(ac), jnp.cos(ac)], axis=1)
    sin = jnp.concatenate([-jnp.sin(ar), jnp.sin(ar), -jnp.sin(ac), jnp.sin(ac)], axis=1)
    return jnp.tile(cos, (1, 2)), jnp.tile(sin, (1, 2))


def kernel(x_prompt, x_sample, c, cache_k, cache_v, state_ssm_re, state_ssm_im, c_ctx, w_ada, b_ada, norm1_w, w_in, ssm_lambda_re, ssm_lambda_im, ssm_log_dt, ssm_b_re, ssm_b_im, ssm_c_re, ssm_c_im, ssm_d, ssm_w_glu, diff_lambda_q, diff_lambda_k, diff_subln_w, w_out, norm2_w, w_router, router_bias, w_exp_gate, w_exp_up, w_exp_down, w_sh_gate, w_sh_up, w_sh_down, final_norm_w):
    depth = w_ada.shape[0]
    assert depth == 1
    l = 0
    lambda_init = 0.8 - 0.6 * math.exp(-0.3 * l)
    bsz, seq, _ = x_prompt.shape
    dbs, dseq, _ = x_sample.shape
    n_p, n_s = bsz * seq, dbs * dseq

    cond8 = jnp.zeros((SUBLANES, D_MODEL), F32).at[:dbs].set(c).at[dbs].set(c_ctx)
    mod = _ada_call(cond8, w_ada[l], b_ada[l])
    mod_s = mod[:dbs].reshape(dbs, 1, 6 * D_MODEL)
    mod_p = mod[dbs:dbs + 1].reshape(1, 1, 6 * D_MODEL)

    w_in_bf = w_in[l].astype(BF16)
    xp2 = x_prompt.reshape(n_p, D_MODEL)
    xs2 = x_sample.reshape(n_s, D_MODEL)
    u_p, q_p, k_p, v_p = _inproj_call(xp2, mod_p, norm1_w[l], w_in_bf, None, n_p, F32)
    u_s, q_s, k_s, v_s = _inproj_call(xs2, mod_s, norm1_w[l], w_in_bf, _rope_tables(dseq), dseq, BF16)

    bblk, cblk, lam_r, lam_i = _s5_weights(ssm_lambda_re[l], ssm_lambda_im[l], ssm_log_dt[l],
                                           ssm_b_re[l], ssm_b_im[l], ssm_c_re[l], ssm_c_im[l])
    lr_p = jnp.broadcast_to(lam_r[:, :, None, :], (2, N_COLBLK, SUBLANES, S5_BLK_STATES))
    li_p = jnp.broadcast_to(lam_i[:, :, None, :], (2, N_COLBLK, SUBLANES, S5_BLK_STATES))
    g_p = bsz // SUBLANES
    h0_p = jnp.zeros((2, g_p, N_COLBLK * 8, SUBLANES, LANES), F32)
    y_p, hfin = _s5_call(u_p.reshape(bsz, seq, S5_WIDTH), bblk, cblk, lr_p, li_p, h0_p, SUBLANES, tb=128)

    def halves(x):
        x = x.reshape(2, 2, 2, 1, S5_BLK_STATES)
        x = jnp.broadcast_to(x, (2, 2, 2, dbs, S5_BLK_STATES))
        return x.transpose(0, 2, 1, 3, 4).reshape(2, 2, 2 * dbs, S5_BLK_STATES)

    def h0_tiles(s):
        s = s.astype(F32).reshape(dbs, 2, 2, 2, 4, LANES)
        return s.transpose(1, 3, 4, 2, 0, 5).reshape(2, 1, 2, 4, 2 * dbs, LANES)

    h0_s = jnp.concatenate([h0_tiles(state_ssm_re[:, l]), h0_tiles(state_ssm_im[:, l])], axis=3)
    h0_s = h0_s.reshape(2, 1, 16, SUBLANES, LANES)
    y_s, _ = _s5_call(u_s.reshape(dbs, dseq, S5_WIDTH), bblk, cblk, halves(lam_r), halves(lam_i), h0_s, dbs,
                      tb=256)

    lq = diff_lambda_q[l].astype(F32)
    lk = diff_lambda_k[l].astype(F32)
    lam = (jnp.exp(jnp.sum(lq[0] * lk[0])) - jnp.exp(jnp.sum(lq[1] * lk[1])) + lambda_init).reshape(1)
    subln = diff_subln_w[l].astype(F32).reshape(1, VALUE_DIM)
    ao_p = _attn_ctx_call(lam, q_p.reshape(bsz, seq, ATTN_WIDTH), k_p.reshape(bsz, seq, ATTN_WIDTH),
                          v_p.reshape(bsz, seq, ATTN_WIDTH), subln, lambda_init)
    past = cache_k.shape[2]
    ao_s = _attn_lat_call(lam, q_s.reshape(dbs, dseq, ATTN_WIDTH), k_s.reshape(dbs, dseq, ATTN_WIDTH),
                          v_s.reshape(dbs, dseq, ATTN_WIDTH),
                          cache_k[:, l].reshape(dbs, past, ATTN_WIDTH).astype(BF16),
                          cache_v[:, l].reshape(dbs, past, ATTN_WIDTH).astype(BF16), subln, lambda_init)

    wglu_bf = ssm_w_glu[l].astype(BF16)
    wout_bf = w_out[l].astype(BF16)
    wr_t = w_router[l].astype(F32).T
    rbias3 = router_bias[l].astype(F32).reshape(N_EXPERT_GROUPS, N_EXPERTS // N_EXPERT_GROUPS, 1)
    x1_p, h2_p, gates_p = _post_call(xp2, u_p, y_p, ao_p.reshape(n_p, ATTN_WIDTH), mod_p, ssm_d[l],
                                     wglu_bf, wout_bf, norm2_w[l], wr_t, rbias3, n_p)
    x1_s, h2_s, gates_s = _post_call(xs2, u_s, y_s, ao_s.reshape(n_s, ATTN_WIDTH), mod_s, ssm_d[l],
                                     wglu_bf, wout_bf, norm2_w[l], wr_t, rbias3, dseq)

    wg = w_exp_gate[l]
    wu = w_exp_up[l]
    wd = w_exp_down[l]
    sg = w_sh_gate[l].astype(BF16)
    su = w_sh_up[l].astype(BF16)
    sd = w_sh_down[l].astype(BF16)
    y_prompt, y_sample = _moe_sparse(h2_p, h2_s, gates_p, gates_s, x1_p, x1_s, mod_p, mod_s,
                                     wg, wu, wd, sg, su, sd, final_norm_w, dseq)

    new_cache_k = k_p.reshape(bsz, 1, seq, N_DIFF_HEADS, VALUE_DIM)
    new_cache_v = v_p.reshape(bsz, 1, seq, N_DIFF_HEADS, VALUE_DIM)
    hf = hfin.reshape(2, g_p, N_COLBLK, 2, 4, SUBLANES, LANES)
    hf = hf.transpose(3, 1, 5, 0, 2, 4, 6).reshape(2, bsz, 1, 2, S5_GROUPS, S5_STATE)
    return (y_prompt.reshape(bsz, seq, D_MODEL), y_sample.reshape(dbs, dseq, D_MODEL),
            new_cache_k, new_cache_v, hf[0], hf[1])
```

```python
import functools
import math

import jax
import jax.numpy as jnp
from jax import lax
from jax.experimental import pallas as pl
from jax.experimental.pallas import tpu as pltpu

F32 = jnp.float32
BF16 = jnp.bfloat16

D_MODEL = 1024
GRID_W = 64
S5_WIDTH = 512
S5_GROUP = 16
S5_GROUPS = 32
S5_STATE = 64
ATTN_WIDTH = 512
DIFF_HEAD_DIM = 64
VALUE_DIM = 128
N_DIFF_HEADS = 4
IN_WIDTH = S5_WIDTH + 3 * ATTN_WIDTH
ROT_PAIRS = DIFF_HEAD_DIM // 4
ROPE_THETA = 10000.0
N_EXPERTS = 64
TOP_K = 8
N_EXPERT_GROUPS = 8
TOPK_GROUPS = 4
EXPERT_FF = 256
ROUTED_SCALE = 2.5
NORM_EPS = 1e-6

LANES = 128
SUBLANES = 8
S5_COLBLK = 8 * S5_GROUP
S5_BLK_STATES = 8 * S5_STATE
N_COLBLK = S5_WIDTH // S5_COLBLK
VMEM_LIMIT = 56 * 1024 * 1024


def _cparams(sem):
    return pltpu.CompilerParams(dimension_semantics=sem, vmem_limit_bytes=VMEM_LIMIT)


def _ada_kernel(cond_ref, w_ref, b_ref, o_ref):
    c = cond_ref[...]
    s = c * jax.nn.sigmoid(c)
    o_ref[...] = jnp.dot(s.astype(BF16), w_ref[...].astype(BF16),
                         preferred_element_type=F32) + b_ref[...]


def _ada_call(cond8, w_ada, b_ada):
    n = w_ada.shape[1]
    tn = 1536
    return pl.pallas_call(
        _ada_kernel,
        grid=(n // tn,),
        in_specs=[pl.BlockSpec((SUBLANES, D_MODEL), lambda j: (0, 0)),
                  pl.BlockSpec((D_MODEL, tn), lambda j: (0, j)),
                  pl.BlockSpec((1, tn), lambda j: (0, j))],
        out_specs=pl.BlockSpec((SUBLANES, tn), lambda j: (0, j)),
        out_shape=jax.ShapeDtypeStruct((SUBLANES, n), F32),
        compiler_params=_cparams(("arbitrary",)),
        name="adaln",
    )(cond8, w_ada, b_ada.reshape(1, n))


def _rope_apply(t, cos, sin):
    parts = []
    for cidx in range(ATTN_WIDTH // LANES):
        xc = t[:, cidx * LANES:(cidx + 1) * LANES]
        up = pltpu.roll(xc, LANES - ROT_PAIRS, 1)
        dn = pltpu.roll(xc, ROT_PAIRS, 1)
        lane = lax.broadcasted_iota(jnp.int32, xc.shape, 1)
        partner = jnp.where((lane % (2 * ROT_PAIRS)) < ROT_PAIRS, up, dn)
        parts.append(xc * cos + partner * sin)
    return jnp.concatenate(parts, axis=1)


def _inproj_kernel(*refs, rope, kv_dtype):
    if rope:
        x_ref, mod_ref, n1_ref, w_ref, cos_ref, sin_ref, u_ref, q_ref, k_ref, v_ref = refs
    else:
        x_ref, mod_ref, n1_ref, w_ref, u_ref, q_ref, k_ref, v_ref = refs
    x = x_ref[...]
    ms = jnp.mean(x * x, axis=-1, keepdims=True)
    xn = x * lax.rsqrt(ms + NORM_EPS) * n1_ref[...]
    mod = mod_ref[0]
    shift = mod[:, 0:D_MODEL]
    scale = mod[:, D_MODEL:2 * D_MODEL]
    h = xn * (1.0 + scale) + shift
    proj = jnp.dot(h.astype(BF16), w_ref[...], preferred_element_type=F32)
    u_ref[...] = proj[:, :S5_WIDTH]
    q = proj[:, S5_WIDTH:S5_WIDTH + ATTN_WIDTH]
    k = proj[:, S5_WIDTH + ATTN_WIDTH:S5_WIDTH + 2 * ATTN_WIDTH]
    v = proj[:, S5_WIDTH + 2 * ATTN_WIDTH:]
    if rope:
        cos = cos_ref[...]
        sin = sin_ref[...]
        q = _rope_apply(q, cos, sin)
        k = _rope_apply(k, cos, sin)
    q_ref[...] = (q * (DIFF_HEAD_DIM ** -0.5)).astype(BF16)
    k_ref[...] = k.astype(kv_dtype)
    v_ref[...] = v.astype(kv_dtype)


def _inproj_call(x2, mod3, norm1_w, w_in_bf, rope_tabs, seq_len, kv_dtype, tm=512):
    n = x2.shape[0]
    bpb = seq_len // tm
    rope = rope_tabs is not None
    in_specs = [pl.BlockSpec((tm, D_MODEL), lambda i: (i, 0)),
                pl.BlockSpec((1, 1, 6 * D_MODEL), lambda i: (i // bpb, 0, 0)),
                pl.BlockSpec((1, D_MODEL), lambda i: (0, 0)),
                pl.BlockSpec((D_MODEL, IN_WIDTH), lambda i: (0, 0))]
    args = [x2, mod3, norm1_w.reshape(1, D_MODEL), w_in_bf]
    if rope:
        in_specs += [pl.BlockSpec((tm, LANES), lambda i: (i % bpb, 0)),
                     pl.BlockSpec((tm, LANES), lambda i: (i % bpb, 0))]
        args += list(rope_tabs)
    ospec = pl.BlockSpec((tm, S5_WIDTH), lambda i: (i, 0))
    return pl.pallas_call(
        functools.partial(_inproj_kernel, rope=rope, kv_dtype=kv_dtype),
        grid=(n // tm,),
        in_specs=in_specs,
        out_specs=[ospec, ospec, ospec, ospec],
        out_shape=[jax.ShapeDtypeStruct((n, S5_WIDTH), F32),
                   jax.ShapeDtypeStruct((n, ATTN_WIDTH), BF16),
                   jax.ShapeDtypeStruct((n, ATTN_WIDTH), kv_dtype),
                   jax.ShapeDtypeStruct((n, ATTN_WIDTH), kv_dtype)],
        compiler_params=_cparams(("parallel",)),
        name="inproj_rope" if rope else "inproj",
    )(*args)


def _s5_slot(b, q, nseq):
    if nseq == SUBLANES:
        return b, q * 8
    half = q // 2
    return half * nseq + b, (q % 2) * 8


def _s5_kernel(u_ref, b_ref, c_ref, lr_ref, li_ref, h0_ref, y_ref, hfin_ref, buh, hs, hst,
               *, nseq, tb, stride, nqq):
    d = pl.program_id(1)
    c = pl.program_id(2)
    nc = pl.num_programs(2)

    @pl.when(c == 0)
    def _():
        hst[...] = h0_ref[0, 0]
        hs[...] = jnp.zeros(hs.shape, F32)

    u = u_ref[...].reshape(nseq * tb, S5_WIDTH).astype(BF16)
    for q in range(N_COLBLK):
        res = jnp.dot(u[:, q * S5_COLBLK:(q + 1) * S5_COLBLK], b_ref[0, q],
                      preferred_element_type=F32)
        for b in range(nseq):
            slot, ct0 = _s5_slot(b, q, nseq)
            for j in range(8):
                buh[ct0 + j, slot * stride:slot * stride + tb, :] = (
                    res[b * tb:(b + 1) * tb, j * LANES:(j + 1) * LANES])

    for qq in range(nqq):
        lr = [lr_ref[0, qq, :, i * LANES:(i + 1) * LANES] for i in range(4)]
        li = [li_ref[0, qq, :, i * LANES:(i + 1) * LANES] for i in range(4)]
        h_init = tuple(hst[qq * 8 + i] for i in range(8))

        def body(s, carry, qq=qq, lr=lr, li=li):
            t = jnp.where(d == 0, s, tb - 1 - s)
            idx = pl.ds(t, SUBLANES, stride=stride)
            new_r, new_i = [], []
            for i in range(4):
                hr, hi = carry[i], carry[4 + i]
                bur = buh[qq * 8 + i, idx, :]
                bui = buh[qq * 8 + 4 + i, idx, :]
                nr = lr[i] * hr - li[i] * hi + bur
                ni = lr[i] * hi + li[i] * hr + bui
                hs[qq * 8 + i, idx, :] = nr
                hs[qq * 8 + 4 + i, idx, :] = ni
                new_r.append(nr)
                new_i.append(ni)
            return tuple(new_r) + tuple(new_i)

        h_fin = lax.fori_loop(0, tb, body, h_init, unroll=4)
        for i in range(8):
            hst[qq * 8 + i] = h_fin[i]

    rows = nseq * stride
    for q in range(N_COLBLK):
        slot0, ct0 = _s5_slot(0, q, nseq)
        hmat = jnp.concatenate(
            [hs[ct0 + j, slot0 * stride:slot0 * stride + rows, :] for j in range(8)], axis=1)
        yq = jnp.dot(hmat.astype(BF16), c_ref[0, q], preferred_element_type=F32)
        for b in range(nseq):
            y_ref[0, b, :, q * S5_COLBLK:(q + 1) * S5_COLBLK] = yq[b * stride:b * stride + tb, :]

    @pl.when(c == nc - 1)
    def _():
        hfin_ref[0, 0] = hst[...]


def _s5_call(u3, bblk, cblk, lr, li, h0, nseq, tb=64):
    nseq_total, seq_len, _ = u3.shape
    g = nseq_total // nseq
    nc = seq_len // tb
    stride = tb + SUBLANES // 2
    nqq = lr.shape[1]
    nct = nqq * 8

    def tmap(gi, d, c):
        return c + d * (nc - 1 - 2 * c)

    return pl.pallas_call(
        functools.partial(_s5_kernel, nseq=nseq, tb=tb, stride=stride, nqq=nqq),
        grid=(g, 2, nc),
        in_specs=[
            pl.BlockSpec((nseq, tb, S5_WIDTH), lambda gi, d, c: (gi, tmap(gi, d, c), 0)),
            pl.BlockSpec((1, N_COLBLK, S5_COLBLK, 2 * S5_BLK_STATES), lambda gi, d, c: (d, 0, 0, 0)),
            pl.BlockSpec((1, N_COLBLK, 2 * S5_BLK_STATES, S5_COLBLK), lambda gi, d, c: (d, 0, 0, 0)),
            pl.BlockSpec((1, nqq, SUBLANES, S5_BLK_STATES), lambda gi, d, c: (d, 0, 0, 0)),
            pl.BlockSpec((1, nqq, SUBLANES, S5_BLK_STATES), lambda gi, d, c: (d, 0, 0, 0)),
            pl.BlockSpec((1, 1, nct, SUBLANES, LANES), lambda gi, d, c: (d, gi, 0, 0, 0)),
        ],
        out_specs=[
            pl.BlockSpec((1, nseq, tb, S5_WIDTH), lambda gi, d, c: (d, gi, tmap(gi, d, c), 0)),
            pl.BlockSpec((1, 1, nct, SUBLANES, LANES), lambda gi, d, c: (d, gi, 0, 0, 0)),
        ],
        out_shape=[jax.ShapeDtypeStruct((2, nseq_total, seq_len, S5_WIDTH), F32),
                   jax.ShapeDtypeStruct((2, g, nct, SUBLANES, LANES), F32)],
        scratch_shapes=[pltpu.VMEM((nct, SUBLANES * stride, LANES), F32),
                        pltpu.VMEM((nct, SUBLANES * stride, LANES), F32),
                        pltpu.VMEM((nct, SUBLANES, LANES), F32)],
        compiler_params=_cparams(("parallel", "parallel", "arbitrary")),
        name="s5_scan_%d" % nseq,
    )(u3, bblk, cblk, lr, li, h0)


def _s5_weights(lam_re, lam_im, log_dt, b_re, b_im, c_re, c_im):
    dt = jnp.exp(log_dt.astype(F32))[..., None]
    lr = lam_re.astype(F32)
    li = lam_im.astype(F32)
    mag = jnp.exp(lr * dt)
    lbr = mag * jnp.cos(li * dt)
    lbi = mag * jnp.sin(li * dt)
    a = lbr - 1.0
    den = lr * lr + li * li
    cr = (a * lr + lbi * li) / den
    ci = (lbi * lr - a * li) / den
    bbr = cr[..., None] * b_re - ci[..., None] * b_im
    bbi = cr[..., None] * b_im + ci[..., None] * b_re
    eye = jnp.eye(8, dtype=F32)

    def pack_b(x):
        x = x.reshape(2, N_COLBLK, 8, S5_STATE, S5_GROUP).transpose(0, 1, 2, 4, 3)
        x = jnp.einsum('dqghn,gk->dqghkn', x, eye)
        return x.reshape(2, N_COLBLK, S5_COLBLK, S5_BLK_STATES)

    def pack_c(x):
        x = x.reshape(2, N_COLBLK, 8, S5_GROUP, S5_STATE).transpose(0, 1, 2, 4, 3)
        x = jnp.einsum('dqgnh,gk->dqgnkh', x, eye)
        return x.reshape(2, N_COLBLK, S5_BLK_STATES, S5_COLBLK)

    bblk = jnp.concatenate([pack_b(bbr), pack_b(bbi)], axis=-1).astype(BF16)
    cblk = jnp.concatenate([pack_c(c_re.astype(F32)), pack_c(-c_im.astype(F32))], axis=-2).astype(BF16)
    lam_r = lbr.reshape(2, N_COLBLK, S5_BLK_STATES)
    lam_i = lbi.reshape(2, N_COLBLK, S5_BLK_STATES)
    return bblk, cblk, lam_r, lam_i


ATTN_KC = 512


def _lane_fold(x, op):
    acc = x[:, 0:LANES]
    for j in range(1, x.shape[1] // LANES):
        acc = op(acc, x[:, j * LANES:(j + 1) * LANES])
    return acc


def _attn_ops(q, chunks, s_ref):
    lane = lax.broadcasted_iota(jnp.int32, q.shape, 1)
    dn = (((1,), (1,)), ((), ()))
    qms = [jnp.where(lane < DIFF_HEAD_DIM, q, jnp.zeros_like(q)),
           jnp.where(lane >= DIFF_HEAD_DIM, q, jnp.zeros_like(q))]
    offs = [0]
    for load_k, _ in chunks:
        offs.append(offs[-1] + load_k().shape[0])

    def score_chunk(m, ci, m128):
        s = lax.dot_general(qms[m], chunks[ci][0](), dn, preferred_element_type=F32)
        s_ref[m, :, offs[ci]:offs[ci + 1]] = s
        f = _lane_fold(s, jnp.maximum)
        return f if m128 is None else jnp.maximum(m128, f)

    def value_chunk(m, ci, mx, acc):
        p = jnp.exp(s_ref[m, :, offs[ci]:offs[ci + 1]] - mx)
        v = chunks[ci][1]()
        t = jnp.dot(p.astype(BF16), jnp.concatenate([v, jnp.ones_like(v)], axis=1), preferred_element_type=F32)
        return t if acc is None else acc + t

    return score_chunk, value_chunk


def _row_max(m128):
    return m128.max(axis=-1, keepdims=True)


def _normalized(acc):
    return acc[:, :VALUE_DIM] * (1.0 / acc[:, VALUE_DIM:VALUE_DIM + 1])


def _sub_ln(o, subln, lambda_init):
    ms = jnp.mean(o * o, axis=-1, keepdims=True)
    return o * lax.rsqrt(ms + NORM_EPS) * subln * (1.0 - lambda_init)


def _attn_ctx_kernel(lam_ref, q_ref, k_ref, v_ref, w_ref, o_ref, s_ref, *, lambda_init):
    for h in range(N_DIFF_HEADS):
        cols = slice(h * VALUE_DIM, (h + 1) * VALUE_DIM)
        chunks = [(lambda cols=cols: k_ref[0, :, cols].astype(BF16), lambda cols=cols: v_ref[0, :, cols].astype(BF16))]
        score_chunk, value_chunk = _attn_ops(q_ref[0, :, cols], chunks, s_ref.at[h])
        mx0 = _row_max(score_chunk(0, 0, None))
        m1 = score_chunk(1, 0, None)
        a0 = value_chunk(0, 0, mx0, None)
        a1 = value_chunk(1, 0, _row_max(m1), None)
        o = _normalized(a0) - lam_ref[0] * _normalized(a1)
        o_ref[0, :, cols] = _sub_ln(o, w_ref[...], lambda_init).astype(BF16)


def _attn_lat_kernel(lam_ref, q_ref, k_ref, v_ref, ck_ref, cv_ref, w_ref, o_ref, s_ref, *, lambda_init):
    chunks = []
    for c in range(k_ref.shape[1] // ATTN_KC):
        chunks.append((lambda c=c: k_ref[0, c * ATTN_KC:(c + 1) * ATTN_KC, :],
                       lambda c=c: v_ref[0, c * ATTN_KC:(c + 1) * ATTN_KC, :]))
    for c in range(ck_ref.shape[1] // ATTN_KC):
        chunks.append((lambda c=c: ck_ref[0, c * ATTN_KC:(c + 1) * ATTN_KC, :],
                       lambda c=c: cv_ref[0, c * ATTN_KC:(c + 1) * ATTN_KC, :]))
    nchunk = len(chunks)
    score_chunk, value_chunk = _attn_ops(q_ref[0], chunks, s_ref)

    m0 = m1 = a0 = a1 = None
    for ci in range(nchunk):
        m0 = score_chunk(0, ci, m0)
    mx0 = _row_max(m0)
    for ci in range(nchunk):
        m1 = score_chunk(1, ci, m1)
        a0 = value_chunk(0, ci, mx0, a0)
    mx1 = _row_max(m1)
    for ci in range(nchunk):
        a1 = value_chunk(1, ci, mx1, a1)
    o = _normalized(a0) - lam_ref[0] * _normalized(a1)
    o_ref[0] = _sub_ln(o, w_ref[...], lambda_init).astype(BF16)


def _attn_ctx_call(lam, q3, k3, v3, subln, lambda_init):
    bsz, seq_len, _ = q3.shape
    spec = pl.BlockSpec((1, seq_len, ATTN_WIDTH), lambda b: (b, 0, 0))
    return pl.pallas_call(
        functools.partial(_attn_ctx_kernel, lambda_init=lambda_init),
        grid=(bsz,),
        in_specs=[pl.BlockSpec(memory_space=pltpu.SMEM), spec, spec, spec,
                  pl.BlockSpec((1, VALUE_DIM), lambda b: (0, 0))],
        out_specs=spec,
        out_shape=jax.ShapeDtypeStruct((bsz, seq_len, ATTN_WIDTH), BF16),
        scratch_shapes=[pltpu.VMEM((N_DIFF_HEADS, 2, seq_len, seq_len), F32)],
        compiler_params=_cparams(("parallel",)),
        name="attn_ctx",
    )(lam, q3, k3, v3, subln)


def _attn_lat_call(lam, q3, k3, v3, ck3, cv3, subln, lambda_init, tq=256):
    bsz, seq_len, _ = q3.shape
    ctx_len = ck3.shape[1]
    qspec = pl.BlockSpec((1, tq, VALUE_DIM), lambda b, h, i: (b, i, h))
    kspec = pl.BlockSpec((1, seq_len, VALUE_DIM), lambda b, h, i: (b, 0, h))
    cspec = pl.BlockSpec((1, ctx_len, VALUE_DIM), lambda b, h, i: (b, 0, h))
    return pl.pallas_call(
        functools.partial(_attn_lat_kernel, lambda_init=lambda_init),
        grid=(bsz, N_DIFF_HEADS, seq_len // tq),
        in_specs=[pl.BlockSpec(memory_space=pltpu.SMEM), qspec, kspec, kspec, cspec, cspec,
                  pl.BlockSpec((1, VALUE_DIM), lambda b, h, i: (0, 0))],
        out_specs=qspec,
        out_shape=jax.ShapeDtypeStruct((bsz, seq_len, ATTN_WIDTH), BF16),
        scratch_shapes=[pltpu.VMEM((2, tq, seq_len + ctx_len), F32)],
        compiler_params=_cparams(("parallel", "parallel", "arbitrary")),
        name="attn_lat",
    )(lam, q3, k3, v3, ck3, cv3, subln)


def _route(logits_t, bias3, tm):
    ng, ge = N_EXPERT_GROUPS, N_EXPERTS // N_EXPERT_GROUPS
    neg = jnp.float32(-jnp.inf)
    sc = jax.nn.sigmoid(logits_t).reshape(ng, ge, tm)
    bi = sc + bias3
    eio = lax.broadcasted_iota(jnp.int32, (ng, ge, tm), 1).astype(F32)
    gio = lax.broadcasted_iota(jnp.int32, (ng, ge, tm), 0).astype(F32)
    m1 = bi.max(axis=1, keepdims=True)
    i1 = jnp.where(bi == m1, eio, float(ge)).min(axis=1, keepdims=True)
    m2 = jnp.where(eio == i1, neg, bi).max(axis=1, keepdims=True)
    gs = jnp.broadcast_to(m1 + m2, (ng, ge, tm))
    cnt = jnp.zeros((ng, ge, tm), F32)
    for g2 in range(ng):
        o = gs[g2:g2 + 1]
        better = (o > gs) | ((o == gs) & (gio > float(g2)))
        cnt = cnt + jnp.where(better, 1.0, 0.0)
    v = jnp.where(cnt < float(TOPK_GROUPS), bi, neg)
    eidx = gio * float(ge) + eio
    selm = jnp.zeros((ng, ge, tm), F32)
    for _ in range(TOP_K):
        m = v.max(axis=0, keepdims=True).max(axis=1, keepdims=True)
        ix = jnp.where(v == m, eidx, float(N_EXPERTS)).min(axis=0, keepdims=True).min(axis=1, keepdims=True)
        oh = eidx == ix
        selm = jnp.where(oh, 1.0, selm)
        v = jnp.where(oh, neg, v)
    selsc = selm * sc
    ssum = selsc.sum(axis=0, keepdims=True).sum(axis=1, keepdims=True)
    return (selsc / ssum * ROUTED_SCALE).reshape(N_EXPERTS, tm)


def _post_kernel(x_ref, u_ref, yf_ref, yb_ref, ao_ref, mod_ref, d_ref, wglu_ref, wout_ref, n2_ref,
                 wrt_ref, rb_ref, x1_ref, h2_ref, gates_ref, *, tm):
    u = u_ref[...]
    y = u * d_ref[...] + yf_ref[0] + yb_ref[0]
    g = jax.nn.gelu(y)
    s5 = g * jax.nn.sigmoid(jnp.dot(g.astype(BF16), wglu_ref[...], preferred_element_type=F32))
    mixed = (jnp.dot(s5.astype(BF16), wout_ref[0:S5_WIDTH, :], preferred_element_type=F32)
             + jnp.dot(ao_ref[...], wout_ref[S5_WIDTH:, :], preferred_element_type=F32))
    mod = mod_ref[0]
    gate1 = mod[:, 2 * D_MODEL:3 * D_MODEL]
    shift2 = mod[:, 3 * D_MODEL:4 * D_MODEL]
    scale2 = mod[:, 4 * D_MODEL:5 * D_MODEL]
    x1 = x_ref[...] + gate1 * mixed
    x1_ref[...] = x1
    ms = jnp.mean(x1 * x1, axis=-1, keepdims=True)
    h2 = x1 * lax.rsqrt(ms + NORM_EPS) * n2_ref[...] * (1.0 + scale2) + shift2
    h_hi = h2.astype(BF16)
    h2_ref[...] = h_hi
    h_lo = (h2 - h_hi.astype(F32)).astype(BF16)
    wr = wrt_ref[...]
    w_hi = wr.astype(BF16)
    w_lo = (wr - w_hi.astype(F32)).astype(BF16)
    dn = (((1,), (1,)), ((), ()))
    logits_t = (lax.dot_general(w_hi, h_hi, dn, preferred_element_type=F32)
                + lax.dot_general(w_lo, h_hi, dn, preferred_element_type=F32)
                + lax.dot_general(w_hi, h_lo, dn, preferred_element_type=F32))
    gates_ref[...] = _route(logits_t, rb_ref[...], tm)


def _post_call(x2, u2, y4, ao2, mod3, ssm_d, wglu_bf, wout_bf, norm2_w, wr_t, rbias3, seq_len, tm=512):
    n = x2.shape[0]
    bpb = seq_len // tm
    y3 = y4.reshape(2, n, S5_WIDTH)
    row = lambda i: (i, 0)
    const2 = lambda i: (0, 0)
    return pl.pallas_call(
        functools.partial(_post_kernel, tm=tm),
        grid=(n // tm,),
        in_specs=[pl.BlockSpec((tm, D_MODEL), row),
                  pl.BlockSpec((tm, S5_WIDTH), row),
                  pl.BlockSpec((1, tm, S5_WIDTH), lambda i: (0, i, 0)),
                  pl.BlockSpec((1, tm, S5_WIDTH), lambda i: (1, i, 0)),
                  pl.BlockSpec((tm, ATTN_WIDTH), row),
                  pl.BlockSpec((1, 1, 6 * D_MODEL), lambda i: (i // bpb, 0, 0)),
                  pl.BlockSpec((1, S5_WIDTH), const2),
                  pl.BlockSpec((S5_WIDTH, S5_WIDTH), const2),
                  pl.BlockSpec((D_MODEL, D_MODEL), const2),
                  pl.BlockSpec((1, D_MODEL), const2),
                  pl.BlockSpec((N_EXPERTS, D_MODEL), const2),
                  pl.BlockSpec((N_EXPERT_GROUPS, N_EXPERTS // N_EXPERT_GROUPS, 1), lambda i: (0, 0, 0))],
        out_specs=[pl.BlockSpec((tm, D_MODEL), row),
                   pl.BlockSpec((tm, D_MODEL), row),
                   pl.BlockSpec((N_EXPERTS, tm), lambda i: (0, i))],
        out_shape=[jax.ShapeDtypeStruct((n, D_MODEL), F32),
                   jax.ShapeDtypeStruct((n, D_MODEL), BF16),
                   jax.ShapeDtypeStruct((N_EXPERTS, n), F32)],
        compiler_params=_cparams(("parallel",)),
        name="post_mix_route",
    )(x2, u2, y3, y3, ao2, mod3, ssm_d.reshape(1, S5_WIDTH), wglu_bf, wout_bf,
      norm2_w.reshape(1, D_MODEL), wr_t, rbias3)


def _swiglu_bf(t, wg, wu, wd):
    a = jnp.dot(t, wg, preferred_element_type=F32)
    b = jnp.dot(t, wu, preferred_element_type=F32)
    act = a * jax.nn.sigmoid(a) * b
    return jnp.dot(act.astype(BF16), wd, preferred_element_type=F32)


MOE_SB = 256
MOE_SEG = 16
MOE_TF = 1024
MOE_XW = D_MODEL
MOE_STATIC_CHUNKS = 3
MOE_CHUNK = 512
MOE_DMA_SIZES = tuple(2 ** k for k in range(MOE_TF.bit_length() - 2, 3, -1))
MOE_RPAD = -(-(TOP_K * MOE_SB + N_EXPERTS * (MOE_SEG - 1) + MOE_STATIC_CHUNKS * MOE_SEG) // MOE_CHUNK) * MOE_CHUNK


MOE_PIECE_ROWS = (4 * MOE_SEG, 3 * MOE_SEG, 2 * MOE_SEG, MOE_SEG)
MOE_PIECE_WIDTH = (MOE_RPAD // (4 * MOE_SEG), N_EXPERTS, N_EXPERTS, N_EXPERTS)
MOE_NPIECE_REFS = 2 * len(MOE_PIECE_ROWS)
MOE_PACK = 4096


def _seg_copies(pieces, sb, make_copy, act):
    for ci, (width, rows) in enumerate(zip(MOE_PIECE_WIDTH, MOE_PIECE_ROWS)):
        p_ref, n_ref = pieces[2 * ci], pieces[2 * ci + 1]

        def body(k, carry, p_ref=p_ref, width=width, rows=rows):
            v = p_ref[sb * width + k]
            act(make_copy(pl.multiple_of(v & (MOE_PACK - 1), MOE_SEG),
                          pl.multiple_of(v >> (MOE_PACK.bit_length() - 1), MOE_SEG), rows))
            return carry
        lax.fori_loop(0, n_ref[sb], body, 0)


MOE_WAIT_SIZES = tuple(2 ** k for k in range((MOE_RPAD - 1).bit_length() - 1, 3, -1))


def _sub_block_rows(cnt_ref, loc_ref, sb):
    last = sb * N_EXPERTS + N_EXPERTS - 1
    return loc_ref[last] + cnt_ref[last]


def _wait_rows(rows, make_copy):
    for b in MOE_WAIT_SIZES:
        @pl.when((rows & b) != 0)
        def _(b=b):
            make_copy(b).wait()


def _build_onehot(gt, p_ref, cnt_ref, loc_ref, sb, weighted):
    t = gt.shape[1]
    selm = gt > 0.0
    r = lax.broadcasted_iota(jnp.int32, (t, t), 0)
    c = lax.broadcasted_iota(jnp.int32, (t, t), 1)
    upper = jnp.where(r < c, 1.0, 0.0).astype(BF16)
    rank = jnp.dot(jnp.where(selm, 1.0, 0.0).astype(BF16), upper, preferred_element_type=F32)
    key = jnp.where(selm, rank, -1.0)
    j16 = lax.broadcasted_iota(jnp.int32, (MOE_SEG, t), 0).astype(F32)
    p_ref[...] = jnp.zeros(p_ref.shape, BF16)

    for e in range(N_EXPERTS):
        off = loc_ref[sb * N_EXPERTS + e]
        n = cnt_ref[sb * N_EXPERTS + e]
        row = key[e:e + 1, :]
        val = gt[e:e + 1, :] if weighted else 1.0

        def put(cidx, row=row, off=off, val=val):
            first = cidx * MOE_SEG
            firstf = float(first) if isinstance(first, int) else first.astype(F32)
            tile = jnp.where(row == j16 + firstf, val, 0.0).astype(BF16)
            p_ref[pl.ds(pl.multiple_of(off + first, MOE_SEG), MOE_SEG), :] = tile

        for cidx in range(MOE_STATIC_CHUNKS):
            put(cidx)

        def extra(cidx, carry, put=put):
            put(cidx)
            return carry
        lax.fori_loop(MOE_STATIC_CHUNKS, n // MOE_SEG, extra, 0)


def _dispatch_kernel(cnt_ref, loc_ref, *refs, nsb_p, n_sb):
    pieces = refs[:MOE_NPIECE_REFS]
    padoff_ref, padn_ref, hp_ref, hs_ref, gt_ref, xe_ref, p_ref, xg_ref, z_ref, sem = refs[MOE_NPIECE_REFS:]
    s = pl.program_id(0)
    slot = s % 2

    def copies(sb, slot_, act):
        def mk(off, g, b):
            return pltpu.make_async_copy(xg_ref.at[slot_, pl.ds(off, b)], xe_ref.at[pl.ds(g, b)], sem.at[slot_])
        _seg_copies(pieces, sb, mk, act)

    def wait_all(sb, slot_):
        _wait_rows(_sub_block_rows(cnt_ref, loc_ref, sb),
                   lambda b: pltpu.make_async_copy(xg_ref.at[slot_, pl.ds(0, b)], xe_ref.at[pl.ds(0, b)],
                                                   sem.at[slot_]))

    @pl.when(s >= 2)
    def _():
        wait_all(s - 2, slot)

    gt = gt_ref[...]
    _build_onehot(gt, p_ref, cnt_ref, loc_ref, s, weighted=False)
    xext = jnp.where(s < nsb_p, hp_ref[...], hs_ref[...])
    rows = _sub_block_rows(cnt_ref, loc_ref, s)
    for i in range(MOE_RPAD // MOE_CHUNK):
        @pl.when(i * MOE_CHUNK < rows)
        def _(i=i):
            xg_ref[slot, i * MOE_CHUNK:(i + 1) * MOE_CHUNK, :] = jnp.dot(
                p_ref[i * MOE_CHUNK:(i + 1) * MOE_CHUNK, :], xext, preferred_element_type=F32).astype(BF16)
    copies(s, slot, lambda cp: cp.start())

    @pl.when(s == n_sb - 1)
    def _():
        if n_sb >= 2:
            wait_all(s - 1, 1 - slot)
        wait_all(s, slot)
        z_ref[...] = jnp.zeros(z_ref.shape, BF16)

        def pads(act):
            def body(e, carry):
                n = padn_ref[e]
                off = padoff_ref[e]
                for b in MOE_DMA_SIZES:
                    @pl.when((n & b) != 0)
                    def _(b=b):
                        done = n & (-2 * b)
                        act(pltpu.make_async_copy(
                            z_ref.at[pl.ds(0, b)],
                            xe_ref.at[pl.ds(pl.multiple_of(off + done, MOE_SEG), b)], sem.at[0]))
                return carry
            lax.fori_loop(0, N_EXPERTS, body, 0)
        pads(lambda cp: cp.start())
        pads(lambda cp: cp.wait())


def _ffn_kernel(te_ref, tidx_ref, tcode_ref, xe_ref, wg_ref, wu_ref, wd_ref, ye_ref, wgub, wdb):
    i = pl.program_id(0)
    code = tcode_ref[i]

    @pl.when(code == 2)
    def _():
        wgub[:, :EXPERT_FF] = wg_ref[0].astype(BF16)
        wgub[:, EXPERT_FF:] = wu_ref[0].astype(BF16)
        wdb[...] = wd_ref[0].astype(BF16)

    @pl.when(code != 0)
    def _():
        ab = jnp.dot(xe_ref[...], wgub[...], preferred_element_type=F32)
        a = ab[:, :EXPERT_FF]
        act = a * jax.nn.sigmoid(a) * ab[:, EXPERT_FF:]
        ye_ref[...] = jnp.dot(act.astype(BF16), wdb[...], preferred_element_type=F32).astype(BF16)


def _combine_kernel(cnt_ref, loc_ref, *refs, sb0, n_steps):
    pieces = refs[:MOE_NPIECE_REFS]
    (ye_ref, gt_ref, h_ref, x1_ref, mod_ref, sg_ref, su_ref, sd_ref, fn_ref, o_ref,
     p_ref, ys_ref, acc_ref, sem) = refs[MOE_NPIECE_REFS:]
    i = pl.program_id(0)
    slot = i % 2
    sb = sb0 + i

    def copies(sb_, slot_, act):
        def mk(off, g, b):
            return pltpu.make_async_copy(ye_ref.at[pl.ds(g, b)], ys_ref.at[slot_, pl.ds(off, b)], sem.at[slot_])
        _seg_copies(pieces, sb_, mk, act)

    @pl.when(i == 0)
    def _():
        ys_ref[...] = jnp.zeros(ys_ref.shape, BF16)
        copies(sb, slot, lambda cp: cp.start())

    @pl.when(i + 1 < n_steps)
    def _():
        copies(sb + 1, 1 - slot, lambda cp: cp.start())

    acc_ref[...] = _swiglu_bf(h_ref[...], sg_ref[...], su_ref[...], sd_ref[...])
    _build_onehot(gt_ref[...], p_ref, cnt_ref, loc_ref, sb, weighted=True)
    rows = _sub_block_rows(cnt_ref, loc_ref, sb)
    _wait_rows(rows, lambda b: pltpu.make_async_copy(ye_ref.at[pl.ds(0, b)], ys_ref.at[slot, pl.ds(0, b)],
                                                     sem.at[slot]))
    for c in range(MOE_RPAD // MOE_CHUNK):
        @pl.when(c * MOE_CHUNK < rows)
        def _(c=c):
            acc_ref[...] += lax.dot_general(
                p_ref[c * MOE_CHUNK:(c + 1) * MOE_CHUNK, :], ys_ref[slot, c * MOE_CHUNK:(c + 1) * MOE_CHUNK, :],
                (((0,), (0,)), ((), ())), preferred_element_type=F32)
    gate2 = mod_ref[0][:, 5 * D_MODEL:6 * D_MODEL]
    x2 = x1_ref[...] + gate2 * acc_ref[...]
    ms = jnp.mean(x2 * x2, axis=-1, keepdims=True)
    o_ref[...] = x2 * lax.rsqrt(ms + NORM_EPS) * fn_ref[...]


def _moe_plan(gates_t):
    ne, n = gates_t.shape
    n_sb = n // MOE_SB
    cnt = jnp.sum((gates_t > 0.0).reshape(ne, n_sb, MOE_SB), axis=-1, dtype=jnp.int32).T
    cnt16 = (cnt + MOE_SEG - 1) // MOE_SEG * MOE_SEG
    loc = jnp.cumsum(cnt16, axis=1) - cnt16
    tot = jnp.sum(cnt16, axis=0)
    totp = (tot + MOE_TF - 1) // MOE_TF * MOE_TF
    ends = jnp.cumsum(totp)
    base = ends - totp
    goff = base[None, :] + jnp.cumsum(cnt16, axis=0) - cnt16
    rows_max = TOP_K * n + n_sb * ne * (MOE_SEG - 1) + ne * (MOE_TF - MOE_SEG)
    nt_max = -(-rows_max // MOE_TF)
    tiles = jnp.arange(nt_max, dtype=jnp.int32)
    used = ends[-1]
    valid = tiles * MOE_TF < used
    tidx = jnp.where(valid, tiles, jnp.maximum(used // MOE_TF - 1, 0))
    te = jnp.sum(ends[None, :] <= (tidx * MOE_TF)[:, None], axis=1, dtype=jnp.int32)
    te = jnp.minimum(te, ne - 1)
    first = jnp.concatenate([jnp.ones((1,), jnp.bool_), te[1:] != te[:-1]])
    tcode = jnp.where(valid, 1 + first.astype(jnp.int32), 0)

    def piece_list(npieces, first_row, width, rows):
        cum = jnp.cumsum(npieces, axis=1)
        k = jnp.arange(width, dtype=jnp.int32)
        ek = jnp.minimum(jnp.sum(cum[:, None, :] <= k[None, :, None], axis=2, dtype=jnp.int32), ne - 1)
        onehot = ek[:, :, None] == jnp.arange(ne, dtype=jnp.int32)[None, None, :]
        pick = lambda a: jnp.sum(jnp.where(onehot, a[:, None, :], 0), axis=2, dtype=jnp.int32)
        row = pick(first_row) + rows * (k[None, :] - pick(cum - npieces))
        packed = (pick(goff) + row) * MOE_PACK + pick(loc) + row
        return packed.reshape(-1).astype(jnp.int32), cum[:, -1].astype(jnp.int32)

    big = MOE_PIECE_ROWS[0]
    nbig = cnt16 // big
    pieces = list(piece_list(nbig, jnp.zeros_like(cnt16), MOE_PIECE_WIDTH[0], big))
    for width, rows in zip(MOE_PIECE_WIDTH[1:], MOE_PIECE_ROWS[1:]):
        pieces += list(piece_list((cnt16 % big == rows).astype(jnp.int32), nbig * big, width, rows))
    return dict(cnt=cnt16.reshape(-1), loc=loc.reshape(-1).astype(jnp.int32), pieces=tuple(pieces),
                padoff=(base + tot).astype(jnp.int32), padn=(totp - tot).astype(jnp.int32),
                te=te, tidx=tidx.astype(jnp.int32), tcode=tcode, nt_max=nt_max)


def _dispatch_call(plan, h2_p, h2_s, gates_t):
    nsb_p = h2_p.shape[0] // MOE_SB
    n_sb = gates_t.shape[1] // MOE_SB
    grid_spec = pltpu.PrefetchScalarGridSpec(
        num_scalar_prefetch=4 + MOE_NPIECE_REFS, grid=(n_sb,),
        in_specs=[pl.BlockSpec((MOE_SB, D_MODEL), lambda s, *_: (jnp.minimum(s, nsb_p - 1), 0)),
                  pl.BlockSpec((MOE_SB, D_MODEL), lambda s, *_: (jnp.maximum(s - nsb_p, 0), 0)),
                  pl.BlockSpec((N_EXPERTS, MOE_SB), lambda s, *_: (0, s))],
        out_specs=pl.BlockSpec(memory_space=pl.ANY),
        scratch_shapes=[pltpu.VMEM((MOE_RPAD, MOE_SB), BF16),
                        pltpu.VMEM((2, MOE_RPAD, MOE_XW), BF16),
                        pltpu.VMEM((MOE_DMA_SIZES[0], MOE_XW), BF16),
                        pltpu.SemaphoreType.DMA((2,))])
    return pl.pallas_call(
        functools.partial(_dispatch_kernel, nsb_p=nsb_p, n_sb=n_sb),
        grid_spec=grid_spec,
        out_shape=jax.ShapeDtypeStruct((plan['nt_max'] * MOE_TF, MOE_XW), BF16),
        compiler_params=_cparams(("arbitrary",)),
        name="moe_dispatch",
    )(plan['cnt'], plan['loc'], *plan['pieces'], plan['padoff'], plan['padn'], h2_p, h2_s, gates_t)


def _ffn_call(plan, xe, wg, wu, wd):
    grid_spec = pltpu.PrefetchScalarGridSpec(
        num_scalar_prefetch=3, grid=(plan['nt_max'],),
        in_specs=[pl.BlockSpec((MOE_TF, MOE_XW), lambda i, te, tidx, tv: (tidx[i], 0)),
                  pl.BlockSpec((1, D_MODEL, EXPERT_FF), lambda i, te, tidx, tv: (te[i], 0, 0)),
                  pl.BlockSpec((1, D_MODEL, EXPERT_FF), lambda i, te, tidx, tv: (te[i], 0, 0)),
                  pl.BlockSpec((1, EXPERT_FF, D_MODEL), lambda i, te, tidx, tv: (te[i], 0, 0))],
        out_specs=pl.BlockSpec((MOE_TF, D_MODEL), lambda i, te, tidx, tv: (tidx[i], 0)),
        scratch_shapes=[pltpu.VMEM((D_MODEL, 2 * EXPERT_FF), BF16),
                        pltpu.VMEM((EXPERT_FF, D_MODEL), BF16)])
    return pl.pallas_call(
        _ffn_kernel,
        grid_spec=grid_spec,
        out_shape=jax.ShapeDtypeStruct((xe.shape[0], D_MODEL), BF16),
        compiler_params=_cparams(("arbitrary",)),
        name="moe_ffn",
    )(plan['te'], plan['tidx'], plan['tcode'], xe, wg, wu, wd)


def _combine_call(plan, ye, gates_t, h2, x1, mod3, sg, su, sd, final_w, sb0, seq_len):
    n = h2.shape[0]
    n_steps = n // MOE_SB
    bpb = seq_len // MOE_SB
    row = lambda i, *_: (i, 0)
    const2 = lambda i, *_: (0, 0)
    grid_spec = pltpu.PrefetchScalarGridSpec(
        num_scalar_prefetch=2 + MOE_NPIECE_REFS, grid=(n_steps,),
        in_specs=[pl.BlockSpec(memory_space=pl.ANY),
                  pl.BlockSpec((N_EXPERTS, MOE_SB), lambda i, *_: (0, sb0 + i)),
                  pl.BlockSpec((MOE_SB, D_MODEL), row),
                  pl.BlockSpec((MOE_SB, D_MODEL), row),
                  pl.BlockSpec((1, 1, 6 * D_MODEL), lambda i, *_: (i // bpb, 0, 0)),
                  pl.BlockSpec((D_MODEL, EXPERT_FF), const2),
                  pl.BlockSpec((D_MODEL, EXPERT_FF), const2),
                  pl.BlockSpec((EXPERT_FF, D_MODEL), const2),
                  pl.BlockSpec((1, D_MODEL), const2)],
        out_specs=pl.BlockSpec((MOE_SB, D_MODEL), row),
        scratch_shapes=[pltpu.VMEM((MOE_RPAD, MOE_SB), BF16),
                        pltpu.VMEM((2, MOE_RPAD, D_MODEL), BF16),
                        pltpu.VMEM((MOE_SB, D_MODEL), F32),
                        pltpu.SemaphoreType.DMA((2,))])
    return pl.pallas_call(
        functools.partial(_combine_kernel, sb0=sb0, n_steps=n_steps),
        grid_spec=grid_spec,
        out_shape=jax.ShapeDtypeStruct((n, D_MODEL), F32),
        compiler_params=_cparams(("arbitrary",)),
        name="moe_combine",
    )(plan['cnt'], plan['loc'], *plan['pieces'], ye, gates_t, h2, x1, mod3, sg, su, sd,
      final_w.reshape(1, D_MODEL))


def _moe_sparse(h2_p, h2_s, gt_p, gt_s, x1_p, x1_s, mod_p, mod_s, wg, wu, wd, sg, su, sd, final_w, dseq):
    gates_t = jnp.concatenate([gt_p, gt_s], axis=1)
    plan = _moe_plan(gates_t)
    xe = _dispatch_call(plan, h2_p, h2_s, gates_t)
    ye = _ffn_call(plan, xe, wg, wu, wd)
    n_p = h2_p.shape[0]
    y_p = _combine_call(plan, ye, gates_t, h2_p, x1_p, mod_p, sg, su, sd, final_w, 0, n_p)
    y_s = _combine_call(plan, ye, gates_t, h2_s, x1_s, mod_s, sg, su, sd, final_w, n_p // MOE_SB, dseq)
    return y_p, y_s


def _rope_tables(n_tokens):
    rows = n_tokens // GRID_W
    row = jnp.repeat(jnp.arange(rows, dtype=F32), GRID_W)
    col = jnp.tile(jnp.arange(GRID_W, dtype=F32), rows)
    freqs = ROPE_THETA ** (-jnp.arange(ROT_PAIRS, dtype=F32) / ROT_PAIRS)
    ar = row[:, None] * freqs
    ac = col[:, None] * freqs
    cos = jnp.concatenate([jnp.cos(ar), jnp.cos(ar), jnp.cos(ac), jnp.cos(ac)], axis=1)
    sin = jnp.concatenate([-jnp.sin(ar), jnp.sin(ar), -jnp.sin(ac), jnp.sin(ac)], axis=1)
    return jnp.tile(cos, (1, 2)), jnp.tile(sin, (1, 2))


def kernel(x_prompt, x_sample, c, cache_k, cache_v, state_ssm_re, state_ssm_im, c_ctx, w_ada, b_ada, norm1_w, w_in, ssm_lambda_re, ssm_lambda_im, ssm_log_dt, ssm_b_re, ssm_b_im, ssm_c_re, ssm_c_im, ssm_d, ssm_w_glu, diff_lambda_q, diff_lambda_k, diff_subln_w, w_out, norm2_w, w_router, router_bias, w_exp_gate, w_exp_up, w_exp_down, w_sh_gate, w_sh_up, w_sh_down, final_norm_w):
    depth = w_ada.shape[0]
    assert depth == 1
    l = 0
    lambda_init = 0.8 - 0.6 * math.exp(-0.3 * l)
    bsz, seq, _ = x_prompt.shape
    dbs, dseq, _ = x_sample.shape
    n_p, n_s = bsz * seq, dbs * dseq

    cond8 = jnp.zeros((SUBLANES, D_MODEL), F32).at[:dbs].set(c).at[dbs].set(c_ctx)
    mod = _ada_call(cond8, w_ada[l], b_ada[l])
    mod_s = mod[:dbs].reshape(dbs, 1, 6 * D_MODEL)
    mod_p = mod[dbs:dbs + 1].reshape(1, 1, 6 * D_MODEL)

    w_in_bf = w_in[l].astype(BF16)
    xp2 = x_prompt.reshape(n_p, D_MODEL)
    xs2 = x_sample.reshape(n_s, D_MODEL)
    u_p, q_p, k_p, v_p = _inproj_call(xp2, mod_p, norm1_w[l], w_in_bf, None, n_p, F32)
    u_s, q_s, k_s, v_s = _inproj_call(xs2, mod_s, norm1_w[l], w_in_bf, _rope_tables(dseq), dseq, BF16)

    bblk, cblk, lam_r, lam_i = _s5_weights(ssm_lambda_re[l], ssm_lambda_im[l], ssm_log_dt[l],
                                           ssm_b_re[l], ssm_b_im[l], ssm_c_re[l], ssm_c_im[l])
    lr_p = jnp.broadcast_to(lam_r[:, :, None, :], (2, N_COLBLK, SUBLANES, S5_BLK_STATES))
    li_p = jnp.broadcast_to(lam_i[:, :, None, :], (2, N_COLBLK, SUBLANES, S5_BLK_STATES))
    g_p = bsz // SUBLANES
    h0_p = jnp.zeros((2, g_p, N_COLBLK * 8, SUBLANES, LANES), F32)
    y_p, hfin = _s5_call(u_p.reshape(bsz, seq, S5_WIDTH), bblk, cblk, lr_p, li_p, h0_p, SUBLANES, tb=128)

    def halves(x):
        x = x.reshape(2, 2, 2, 1, S5_BLK_STATES)
        x = jnp.broadcast_to(x, (2, 2, 2, dbs, S5_BLK_STATES))
        return x.transpose(0, 2, 1, 3, 4).reshape(2, 2, 2 * dbs, S5_BLK_STATES)

    def h0_tiles(s):
        s = s.astype(F32).reshape(dbs, 2, 2, 2, 4, LANES)
        return s.transpose(1, 3, 4, 2, 0, 5).reshape(2, 1, 2, 4, 2 * dbs, LANES)

    h0_s = jnp.concatenate([h0_tiles(state_ssm_re[:, l]), h0_tiles(state_ssm_im[:, l])], axis=3)
    h0_s = h0_s.reshape(2, 1, 16, SUBLANES, LANES)
    y_s, _ = _s5_call(u_s.reshape(dbs, dseq, S5_WIDTH), bblk, cblk, halves(lam_r), halves(lam_i), h0_s, dbs,
                      tb=256)

    lq = diff_lambda_q[l].astype(F32)
    lk = diff_lambda_k[l].astype(F32)
    lam = (jnp.exp(jnp.sum(lq[0] * lk[0])) - jnp.exp(jnp.sum(lq[1] * lk[1])) + lambda_init).reshape(1)
    subln = diff_subln_w[l].astype(F32).reshape(1, VALUE_DIM)
    ao_p = _attn_ctx_call(lam, q_p.reshape(bsz, seq, ATTN_WIDTH), k_p.reshape(bsz, seq, ATTN_WIDTH),
                          v_p.reshape(bsz, seq, ATTN_WIDTH), subln, lambda_init)
    past = cache_k.shape[2]
    ao_s = _attn_lat_call(lam, q_s.reshape(dbs, dseq, ATTN_WIDTH), k_s.reshape(dbs, dseq, ATTN_WIDTH),
                          v_s.reshape(dbs, dseq, ATTN_WIDTH),
                          cache_k[:, l].reshape(dbs, past, ATTN_WIDTH).astype(BF16),
                          cache_v[:, l].reshape(dbs, past, ATTN_WIDTH).astype(BF16), subln, lambda_init)

    wglu_bf = ssm_w_glu[l].astype(BF16)
    wout_bf = w_out[l].astype(BF16)
    wr_t = w_router[l].astype(F32).T
    rbias3 = router_bias[l].astype(F32).reshape(N_EXPERT_GROUPS, N_EXPERTS // N_EXPERT_GROUPS, 1)
    x1_p, h2_p, gates_p = _post_call(xp2, u_p, y_p, ao_p.reshape(n_p, ATTN_WIDTH), mod_p, ssm_d[l],
                                     wglu_bf, wout_bf, norm2_w[l], wr_t, rbias3, n_p)
    x1_s, h2_s, gates_s = _post_call(xs2, u_s, y_s, ao_s.reshape(n_s, ATTN_WIDTH), mod_s, ssm_d[l],
                                     wglu_bf, wout_bf, norm2_w[l], wr_t, rbias3, dseq)

    wg = w_exp_gate[l]
    wu = w_exp_up[l]
    wd = w_exp_down[l]
    sg = w_sh_gate[l].astype(BF16)
    su = w_sh_up[l].astype(BF16)
    sd = w_sh_down[l].astype(BF16)
    y_prompt, y_sample = _moe_sparse(h2_p, h2_s, gates_p, gates_s, x1_p, x1_s, mod_p, mod_s,
                                     wg, wu, wd, sg, su, sd, final_norm_w, dseq)

    new_cache_k = k_p.reshape(bsz, 1, seq, N_DIFF_HEADS, VALUE_DIM)
    new_cache_v = v_p.reshape(bsz, 1, seq, N_DIFF_HEADS, VALUE_DIM)
    hf = hfin.reshape(2, g_p, N_COLBLK, 2, 4, SUBLANES, LANES)
    hf = hf.transpose(3, 1, 5, 0, 2, 4, 6).reshape(2, bsz, 1, 2, S5_GROUPS, S5_STATE)
    return (y_prompt.reshape(bsz, seq, D_MODEL), y_sample.reshape(dbs, dseq, D_MODEL),
            new_cache_k, new_cache_v, hf[0], hf[1])
```

```python
import functools
import math

import jax
import jax.numpy as jnp
from jax import lax
from jax.experimental import pallas as pl
from jax.experimental.pallas import tpu as pltpu

F32 = jnp.float32
BF16 = jnp.bfloat16

D_MODEL = 1024
GRID_W = 64
S5_WIDTH = 512
S5_GROUP = 16
S5_GROUPS = 32
S5_STATE = 64
ATTN_WIDTH = 512
DIFF_HEAD_DIM = 64
VALUE_DIM = 128
N_DIFF_HEADS = 4
IN_WIDTH = S5_WIDTH + 3 * ATTN_WIDTH
ROT_PAIRS = DIFF_HEAD_DIM // 4
ROPE_THETA = 10000.0
N_EXPERTS = 64
TOP_K = 8
N_EXPERT_GROUPS = 8
TOPK_GROUPS = 4
EXPERT_FF = 256
ROUTED_SCALE = 2.5
NORM_EPS = 1e-6

LANES = 128
SUBLANES = 8
S5_COLBLK = 8 * S5_GROUP
S5_BLK_STATES = 8 * S5_STATE
N_COLBLK = S5_WIDTH // S5_COLBLK
VMEM_LIMIT = 56 * 1024 * 1024


def _cparams(sem):
    return pltpu.CompilerParams(dimension_semantics=sem, vmem_limit_bytes=VMEM_LIMIT)


def _ada_kernel(cond_ref, w_ref, b_ref, o_ref):
    c = cond_ref[...]
    s = c * jax.nn.sigmoid(c)
    o_ref[...] = jnp.dot(s.astype(BF16), w_ref[...].astype(BF16),
                         preferred_element_type=F32) + b_ref[...]


def _ada_call(cond8, w_ada, b_ada):
    n = w_ada.shape[1]
    tn = 1536
    return pl.pallas_call(
        _ada_kernel,
        grid=(n // tn,),
        in_specs=[pl.BlockSpec((SUBLANES, D_MODEL), lambda j: (0, 0)),
                  pl.BlockSpec((D_MODEL, tn), lambda j: (0, j)),
                  pl.BlockSpec((1, tn), lambda j: (0, j))],
        out_specs=pl.BlockSpec((SUBLANES, tn), lambda j: (0, j)),
        out_shape=jax.ShapeDtypeStruct((SUBLANES, n), F32),
        compiler_params=_cparams(("arbitrary",)),
        name="adaln",
    )(cond8, w_ada, b_ada.reshape(1, n))


def _rope_apply(t, cos, sin):
    parts = []
    for cidx in range(ATTN_WIDTH // LANES):
        xc = t[:, cidx * LANES:(cidx + 1) * LANES]
        up = pltpu.roll(xc, LANES - ROT_PAIRS, 1)
        dn = pltpu.roll(xc, ROT_PAIRS, 1)
        lane = lax.broadcasted_iota(jnp.int32, xc.shape, 1)
        partner = jnp.where((lane % (2 * ROT_PAIRS)) < ROT_PAIRS, up, dn)
        parts.append(xc * cos + partner * sin)
    return jnp.concatenate(parts, axis=1)


def _inproj_kernel(*refs, rope, kv_dtype):
    if rope:
        x_ref, mod_ref, n1_ref, w_ref, cos_ref, sin_ref, u_ref, q_ref, k_ref, v_ref = refs
    else:
        x_ref, mod_ref, n1_ref, w_ref, u_ref, q_ref, k_ref, v_ref = refs
    x = x_ref[...]
    ms = jnp.mean(x * x, axis=-1, keepdims=True)
    xn = x * lax.rsqrt(ms + NORM_EPS) * n1_ref[...]
    mod = mod_ref[0]
    shift = mod[:, 0:D_MODEL]
    scale = mod[:, D_MODEL:2 * D_MODEL]
    h = xn * (1.0 + scale) + shift
    proj = jnp.dot(h.astype(BF16), w_ref[...], preferred_element_type=F32)
    u_ref[...] = proj[:, :S5_WIDTH]
    q = proj[:, S5_WIDTH:S5_WIDTH + ATTN_WIDTH]
    k = proj[:, S5_WIDTH + ATTN_WIDTH:S5_WIDTH + 2 * ATTN_WIDTH]
    v = proj[:, S5_WIDTH + 2 * ATTN_WIDTH:]
    if rope:
        cos = cos_ref[...]
        sin = sin_ref[...]
        q = _rope_apply(q, cos, sin)
        k = _rope_apply(k, cos, sin)
    q_ref[...] = (q * (DIFF_HEAD_DIM ** -0.5)).astype(BF16)
    k_ref[...] = k.astype(kv_dtype)
    v_ref[...] = v.astype(kv_dtype)


def _inproj_call(x2, mod3, norm1_w, w_in_bf, rope_tabs, seq_len, kv_dtype, tm=512):
    n = x2.shape[0]
    bpb = seq_len // tm
    rope = rope_tabs is not None
    in_specs = [pl.BlockSpec((tm, D_MODEL), lambda i: (i, 0)),
                pl.BlockSpec((1, 1, 6 * D_MODEL), lambda i: (i // bpb, 0, 0)),
                pl.BlockSpec((1, D_MODEL), lambda i: (0, 0)),
                pl.BlockSpec((D_MODEL, IN_WIDTH), lambda i: (0, 0))]
    args = [x2, mod3, norm1_w.reshape(1, D_MODEL), w_in_bf]
    if rope:
        in_specs += [pl.BlockSpec((tm, LANES), lambda i: (i % bpb, 0)),
                     pl.BlockSpec((tm, LANES), lambda i: (i % bpb, 0))]
        args += list(rope_tabs)
    ospec = pl.BlockSpec((tm, S5_WIDTH), lambda i: (i, 0))
    return pl.pallas_call(
        functools.partial(_inproj_kernel, rope=rope, kv_dtype=kv_dtype),
        grid=(n // tm,),
        in_specs=in_specs,
        out_specs=[ospec, ospec, ospec, ospec],
        out_shape=[jax.ShapeDtypeStruct((n, S5_WIDTH), F32),
                   jax.ShapeDtypeStruct((n, ATTN_WIDTH), BF16),
                   jax.ShapeDtypeStruct((n, ATTN_WIDTH), kv_dtype),
                   jax.ShapeDtypeStruct((n, ATTN_WIDTH), kv_dtype)],
        compiler_params=_cparams(("parallel",)),
        name="inproj_rope" if rope else "inproj",
    )(*args)


def _s5_slot(b, q, nseq):
    if nseq == SUBLANES:
        return b, q * 8
    half = q // 2
    return half * nseq + b, (q % 2) * 8


def _s5_kernel(u_ref, b_ref, c_ref, lr_ref, li_ref, h0_ref, y_ref, hfin_ref, buh, hs, hst,
               *, nseq, tb, stride, nqq):
    d = pl.program_id(1)
    c = pl.program_id(2)
    nc = pl.num_programs(2)

    @pl.when(c == 0)
    def _():
        hst[...] = h0_ref[0, 0]
        hs[...] = jnp.zeros(hs.shape, F32)

    u = u_ref[...].reshape(nseq * tb, S5_WIDTH).astype(BF16)
    for q in range(N_COLBLK):
        res = jnp.dot(u[:, q * S5_COLBLK:(q + 1) * S5_COLBLK], b_ref[0, q],
                      preferred_element_type=F32)
        for b in range(nseq):
            slot, ct0 = _s5_slot(b, q, nseq)
            for j in range(8):
                buh[ct0 + j, slot * stride:slot * stride + tb, :] = (
                    res[b * tb:(b + 1) * tb, j * LANES:(j + 1) * LANES])

    for qq in range(nqq):
        lr = [lr_ref[0, qq, :, i * LANES:(i + 1) * LANES] for i in range(4)]
        li = [li_ref[0, qq, :, i * LANES:(i + 1) * LANES] for i in range(4)]
        h_init = tuple(hst[qq * 8 + i] for i in range(8))

        def body(s, carry, qq=qq, lr=lr, li=li):
            t = jnp.where(d == 0, s, tb - 1 - s)
            idx = pl.ds(t, SUBLANES, stride=stride)
            new_r, new_i = [], []
            for i in range(4):
                hr, hi = carry[i], carry[4 + i]
                bur = buh[qq * 8 + i, idx, :]
                bui = buh[qq * 8 + 4 + i, idx, :]
                nr = lr[i] * hr - li[i] * hi + bur
                ni = lr[i] * hi + li[i] * hr + bui
                hs[qq * 8 + i, idx, :] = nr
                hs[qq * 8 + 4 + i, idx, :] = ni
                new_r.append(nr)
                new_i.append(ni)
            return tuple(new_r) + tuple(new_i)

        h_fin = lax.fori_loop(0, tb, body, h_init, unroll=4)
        for i in range(8):
            hst[qq * 8 + i] = h_fin[i]

    rows = nseq * stride
    for q in range(N_COLBLK):
        slot0, ct0 = _s5_slot(0, q, nseq)
        hmat = jnp.concatenate(
            [hs[ct0 + j, slot0 * stride:slot0 * stride + rows, :] for j in range(8)], axis=1)
        yq = jnp.dot(hmat.astype(BF16), c_ref[0, q], preferred_element_type=F32)
        for b in range(nseq):
            y_ref[0, b, :, q * S5_COLBLK:(q + 1) * S5_COLBLK] = yq[b * stride:b * stride + tb, :]

    @pl.when(c == nc - 1)
    def _():
        hfin_ref[0, 0] = hst[...]


def _s5_call(u3, bblk, cblk, lr, li, h0, nseq, tb=64):
    nseq_total, seq_len, _ = u3.shape
    g = nseq_total // nseq
    nc = seq_len // tb
    stride = tb + SUBLANES // 2
    nqq = lr.shape[1]
    nct = nqq * 8

    def tmap(gi, d, c):
        return c + d * (nc - 1 - 2 * c)

    return pl.pallas_call(
        functools.partial(_s5_kernel, nseq=nseq, tb=tb, stride=stride, nqq=nqq),
        grid=(g, 2, nc),
        in_specs=[
            pl.BlockSpec((nseq, tb, S5_WIDTH), lambda gi, d, c: (gi, tmap(gi, d, c), 0)),
            pl.BlockSpec((1, N_COLBLK, S5_COLBLK, 2 * S5_BLK_STATES), lambda gi, d, c: (d, 0, 0, 0)),
            pl.BlockSpec((1, N_COLBLK, 2 * S5_BLK_STATES, S5_COLBLK), lambda gi, d, c: (d, 0, 0, 0)),
            pl.BlockSpec((1, nqq, SUBLANES, S5_BLK_STATES), lambda gi, d, c: (d, 0, 0, 0)),
            pl.BlockSpec((1, nqq, SUBLANES, S5_BLK_STATES), lambda gi, d, c: (d, 0, 0, 0)),
            pl.BlockSpec((1, 1, nct, SUBLANES, LANES), lambda gi, d, c: (d, gi, 0, 0, 0)),
        ],
        out_specs=[
            pl.BlockSpec((1, nseq, tb, S5_WIDTH), lambda gi, d, c: (d, gi, tmap(gi, d, c), 0)),
            pl.BlockSpec((1, 1, nct, SUBLANES, LANES), lambda gi, d, c: (d, gi, 0, 0, 0)),
        ],
        out_shape=[jax.ShapeDtypeStruct((2, nseq_total, seq_len, S5_WIDTH), F32),
                   jax.ShapeDtypeStruct((2, g, nct, SUBLANES, LANES), F32)],
        scratch_shapes=[pltpu.VMEM((nct, SUBLANES * stride, LANES), F32),
                        pltpu.VMEM((nct, SUBLANES * stride, LANES), F32),
                        pltpu.VMEM((nct, SUBLANES, LANES), F32)],
        compiler_params=_cparams(("parallel", "parallel", "arbitrary")),
        name="s5_scan_%d" % nseq,
    )(u3, bblk, cblk, lr, li, h0)


def _s5_weights(lam_re, lam_im, log_dt, b_re, b_im, c_re, c_im):
    dt = jnp.exp(log_dt.astype(F32))[..., None]
    lr = lam_re.astype(F32)
    li = lam_im.astype(F32)
    mag = jnp.exp(lr * dt)
    lbr = mag * jnp.cos(li * dt)
    lbi = mag * jnp.sin(li * dt)
    a = lbr - 1.0
    den = lr * lr + li * li
    cr = (a * lr + lbi * li) / den
    ci = (lbi * lr - a * li) / den
    bbr = cr[..., None] * b_re - ci[..., None] * b_im
    bbi = cr[..., None] * b_im + ci[..., None] * b_re
    eye = jnp.eye(8, dtype=F32)

    def pack_b(x):
        x = x.reshape(2, N_COLBLK, 8, S5_STATE, S5_GROUP).transpose(0, 1, 2, 4, 3)
        x = jnp.einsum('dqghn,gk->dqghkn', x, eye)
        return x.reshape(2, N_COLBLK, S5_COLBLK, S5_BLK_STATES)

    def pack_c(x):
        x = x.reshape(2, N_COLBLK, 8, S5_GROUP, S5_STATE).transpose(0, 1, 2, 4, 3)
        x = jnp.einsum('dqgnh,gk->dqgnkh', x, eye)
        return x.reshape(2, N_COLBLK, S5_BLK_STATES, S5_COLBLK)

    bblk = jnp.concatenate([pack_b(bbr), pack_b(bbi)], axis=-1).astype(BF16)
    cblk = jnp.concatenate([pack_c(c_re.astype(F32)), pack_c(-c_im.astype(F32))], axis=-2).astype(BF16)
    lam_r = lbr.reshape(2, N_COLBLK, S5_BLK_STATES)
    lam_i = lbi.reshape(2, N_COLBLK, S5_BLK_STATES)
    return bblk, cblk, lam_r, lam_i


ATTN_KC = 512


def _lane_fold(x, op):
    acc = x[:, 0:LANES]
    for j in range(1, x.shape[1] // LANES):
        acc = op(acc, x[:, j * LANES:(j + 1) * LANES])
    return acc


def _attn_ops(q, chunks, s_ref):
    lane = lax.broadcasted_iota(jnp.int32, q.shape, 1)
    dn = (((1,), (1,)), ((), ()))
    qms = [jnp.where(lane < DIFF_HEAD_DIM, q, jnp.zeros_like(q)),
           jnp.where(lane >= DIFF_HEAD_DIM, q, jnp.zeros_like(q))]
    offs = [0]
    for load_k, _ in chunks:
        offs.append(offs[-1] + load_k().shape[0])

    def score_chunk(m, ci, m128):
        s = lax.dot_general(qms[m], chunks[ci][0](), dn, preferred_element_type=F32)
        s_ref[m, :, offs[ci]:offs[ci + 1]] = s
        f = _lane_fold(s, jnp.maximum)
        return f if m128 is None else jnp.maximum(m128, f)

    def value_chunk(m, ci, mx, acc):
        p = jnp.exp(s_ref[m, :, offs[ci]:offs[ci + 1]] - mx)
        v = chunks[ci][1]()
        t = jnp.dot(p.astype(BF16), jnp.concatenate([v, jnp.ones_like(v)], axis=1), preferred_element_type=F32)
        return t if acc is None else acc + t

    return score_chunk, value_chunk


def _row_max(m128):
    return m128.max(axis=-1, keepdims=True)


def _normalized(acc):
    return acc[:, :VALUE_DIM] * (1.0 / acc[:, VALUE_DIM:VALUE_DIM + 1])


def _sub_ln(o, subln, lambda_init):
    ms = jnp.mean(o * o, axis=-1, keepdims=True)
    return o * lax.rsqrt(ms + NORM_EPS) * subln * (1.0 - lambda_init)


def _attn_ctx_kernel(lam_ref, q_ref, k_ref, v_ref, w_ref, o_ref, s_ref, *, lambda_init):
    for h in range(N_DIFF_HEADS):
        cols = slice(h * VALUE_DIM, (h + 1) * VALUE_DIM)
        chunks = [(lambda cols=cols: k_ref[0, :, cols].astype(BF16), lambda cols=cols: v_ref[0, :, cols].astype(BF16))]
        score_chunk, value_chunk = _attn_ops(q_ref[0, :, cols], chunks, s_ref.at[h])
        mx0 = _row_max(score_chunk(0, 0, None))
        m1 = score_chunk(1, 0, None)
        a0 = value_chunk(0, 0, mx0, None)
        a1 = value_chunk(1, 0, _row_max(m1), None)
        o = _normalized(a0) - lam_ref[0] * _normalized(a1)
        o_ref[0, :, cols] = _sub_ln(o, w_ref[...], lambda_init).astype(BF16)


def _attn_lat_kernel(lam_ref, q_ref, k_ref, v_ref, ck_ref, cv_ref, w_ref, o_ref, s_ref, *, lambda_init):
    chunks = []
    for c in range(k_ref.shape[1] // ATTN_KC):
        chunks.append((lambda c=c: k_ref[0, c * ATTN_KC:(c + 1) * ATTN_KC, :],
                       lambda c=c: v_ref[0, c * ATTN_KC:(c + 1) * ATTN_KC, :]))
    for c in range(ck_ref.shape[1] // ATTN_KC):
        chunks.append((lambda c=c: ck_ref[0, c * ATTN_KC:(c + 1) * ATTN_KC, :],
                       lambda c=c: cv_ref[0, c * ATTN_KC:(c + 1) * ATTN_KC, :]))
    nchunk = len(chunks)
    score_chunk, value_chunk = _attn_ops(q_ref[0], chunks, s_ref)

    m0 = m1 = a0 = a1 = None
    for ci in range(nchunk):
        m0 = score_chunk(0, ci, m0)
    mx0 = _row_max(m0)
    for ci in range(nchunk):
        m1 = score_chunk(1, ci, m1)
        a0 = value_chunk(0, ci, mx0, a0)
    mx1 = _row_max(m1)
    for ci in range(nchunk):
        a1 = value_chunk(1, ci, mx1, a1)
    o = _normalized(a0) - lam_ref[0] * _normalized(a1)
    o_ref[0] = _sub_ln(o, w_ref[...], lambda_init).astype(BF16)


def _attn_ctx_call(lam, q3, k3, v3, subln, lambda_init):
    bsz, seq_len, _ = q3.shape
    spec = pl.BlockSpec((1, seq_len, ATTN_WIDTH), lambda b: (b, 0, 0))
    return pl.pallas_call(
        functools.partial(_attn_ctx_kernel, lambda_init=lambda_init),
        grid=(bsz,),
        in_specs=[pl.BlockSpec(memory_space=pltpu.SMEM), spec, spec, spec,
                  pl.BlockSpec((1, VALUE_DIM), lambda b: (0, 0))],
        out_specs=spec,
        out_shape=jax.ShapeDtypeStruct((bsz, seq_len, ATTN_WIDTH), BF16),
        scratch_shapes=[pltpu.VMEM((N_DIFF_HEADS, 2, seq_len, seq_len), F32)],
        compiler_params=_cparams(("parallel",)),
        name="attn_ctx",
    )(lam, q3, k3, v3, subln)


def _attn_lat_call(lam, q3, k3, v3, ck3, cv3, subln, lambda_init, tq=512):
    bsz, seq_len, _ = q3.shape
    ctx_len = ck3.shape[1]
    qspec = pl.BlockSpec((1, tq, VALUE_DIM), lambda b, h, i: (b, i, h))
    kspec = pl.BlockSpec((1, seq_len, VALUE_DIM), lambda b, h, i: (b, 0, h))
    cspec = pl.BlockSpec((1, ctx_len, VALUE_DIM), lambda b, h, i: (b, 0, h))
    return pl.pallas_call(
        functools.partial(_attn_lat_kernel, lambda_init=lambda_init),
        grid=(bsz, N_DIFF_HEADS, seq_len // tq),
        in_specs=[pl.BlockSpec(memory_space=pltpu.SMEM), qspec, kspec, kspec, cspec, cspec,
                  pl.BlockSpec((1, VALUE_DIM), lambda b, h, i: (0, 0))],
        out_specs=qspec,
        out_shape=jax.ShapeDtypeStruct((bsz, seq_len, ATTN_WIDTH), BF16),
        scratch_shapes=[pltpu.VMEM((2, tq, seq_len + ctx_len), F32)],
        compiler_params=_cparams(("parallel", "parallel", "arbitrary")),
        name="attn_lat",
    )(lam, q3, k3, v3, ck3, cv3, subln)


def _route(logits_t, bias3, tm):
    ng, ge = N_EXPERT_GROUPS, N_EXPERTS // N_EXPERT_GROUPS
    neg = jnp.float32(-jnp.inf)
    sc = jax.nn.sigmoid(logits_t).reshape(ng, ge, tm)
    bi = sc + bias3
    eio = lax.broadcasted_iota(jnp.int32, (ng, ge, tm), 1).astype(F32)
    gio = lax.broadcasted_iota(jnp.int32, (ng, ge, tm), 0).astype(F32)
    m1 = bi.max(axis=1, keepdims=True)
    i1 = jnp.where(bi == m1, eio, float(ge)).min(axis=1, keepdims=True)
    m2 = jnp.where(eio == i1, neg, bi).max(axis=1, keepdims=True)
    gs = jnp.broadcast_to(m1 + m2, (ng, ge, tm))
    cnt = jnp.zeros((ng, ge, tm), F32)
    for g2 in range(ng):
        o = gs[g2:g2 + 1]
        better = (o > gs) | ((o == gs) & (gio > float(g2)))
        cnt = cnt + jnp.where(better, 1.0, 0.0)
    v = jnp.where(cnt < float(TOPK_GROUPS), bi, neg)
    eidx = gio * float(ge) + eio
    selm = jnp.zeros((ng, ge, tm), F32)
    for _ in range(TOP_K):
        m = v.max(axis=0, keepdims=True).max(axis=1, keepdims=True)
        ix = jnp.where(v == m, eidx, float(N_EXPERTS)).min(axis=0, keepdims=True).min(axis=1, keepdims=True)
        oh = eidx == ix
        selm = jnp.where(oh, 1.0, selm)
        v = jnp.where(oh, neg, v)
    selsc = selm * sc
    ssum = selsc.sum(axis=0, keepdims=True).sum(axis=1, keepdims=True)
    return (selsc / ssum * ROUTED_SCALE).reshape(N_EXPERTS, tm)


def _post_kernel(x_ref, u_ref, yf_ref, yb_ref, ao_ref, mod_ref, d_ref, wglu_ref, wout_ref, n2_ref,
                 wrt_ref, rb_ref, x1_ref, h2_ref, gates_ref, *, tm):
    u = u_ref[...]
    y = u * d_ref[...] + yf_ref[0] + yb_ref[0]
    g = jax.nn.gelu(y)
    s5 = g * jax.nn.sigmoid(jnp.dot(g.astype(BF16), wglu_ref[...], preferred_element_type=F32))
    mixed = (jnp.dot(s5.astype(BF16), wout_ref[0:S5_WIDTH, :], preferred_element_type=F32)
             + jnp.dot(ao_ref[...], wout_ref[S5_WIDTH:, :], preferred_element_type=F32))
    mod = mod_ref[0]
    gate1 = mod[:, 2 * D_MODEL:3 * D_MODEL]
    shift2 = mod[:, 3 * D_MODEL:4 * D_MODEL]
    scale2 = mod[:, 4 * D_MODEL:5 * D_MODEL]
    x1 = x_ref[...] + gate1 * mixed
    x1_ref[...] = x1
    ms = jnp.mean(x1 * x1, axis=-1, keepdims=True)
    h2 = x1 * lax.rsqrt(ms + NORM_EPS) * n2_ref[...] * (1.0 + scale2) + shift2
    h_hi = h2.astype(BF16)
    h2_ref[...] = h_hi
    h_lo = (h2 - h_hi.astype(F32)).astype(BF16)
    wr = wrt_ref[...]
    w_hi = wr.astype(BF16)
    w_lo = (wr - w_hi.astype(F32)).astype(BF16)
    dn = (((1,), (1,)), ((), ()))
    logits_t = (lax.dot_general(w_hi, h_hi, dn, preferred_element_type=F32)
                + lax.dot_general(w_lo, h_hi, dn, preferred_element_type=F32)
                + lax.dot_general(w_hi, h_lo, dn, preferred_element_type=F32))
    gates_ref[...] = _route(logits_t, rb_ref[...], tm)


def _post_call(x2, u2, y4, ao2, mod3, ssm_d, wglu_bf, wout_bf, norm2_w, wr_t, rbias3, seq_len, tm=512):
    n = x2.shape[0]
    bpb = seq_len // tm
    y3 = y4.reshape(2, n, S5_WIDTH)
    row = lambda i: (i, 0)
    const2 = lambda i: (0, 0)
    return pl.pallas_call(
        functools.partial(_post_kernel, tm=tm),
        grid=(n // tm,),
        in_specs=[pl.BlockSpec((tm, D_MODEL), row),
                  pl.BlockSpec((tm, S5_WIDTH), row),
                  pl.BlockSpec((1, tm, S5_WIDTH), lambda i: (0, i, 0)),
                  pl.BlockSpec((1, tm, S5_WIDTH), lambda i: (1, i, 0)),
                  pl.BlockSpec((tm, ATTN_WIDTH), row),
                  pl.BlockSpec((1, 1, 6 * D_MODEL), lambda i: (i // bpb, 0, 0)),
                  pl.BlockSpec((1, S5_WIDTH), const2),
                  pl.BlockSpec((S5_WIDTH, S5_WIDTH), const2),
                  pl.BlockSpec((D_MODEL, D_MODEL), const2),
                  pl.BlockSpec((1, D_MODEL), const2),
                  pl.BlockSpec((N_EXPERTS, D_MODEL), const2),
                  pl.BlockSpec((N_EXPERT_GROUPS, N_EXPERTS // N_EXPERT_GROUPS, 1), lambda i: (0, 0, 0))],
        out_specs=[pl.BlockSpec((tm, D_MODEL), row),
                   pl.BlockSpec((tm, D_MODEL), row),
                   pl.BlockSpec((N_EXPERTS, tm), lambda i: (0, i))],
        out_shape=[jax.ShapeDtypeStruct((n, D_MODEL), F32),
                   jax.ShapeDtypeStruct((n, D_MODEL), BF16),
                   jax.ShapeDtypeStruct((N_EXPERTS, n), F32)],
        compiler_params=_cparams(("parallel",)),
        name="post_mix_route",
    )(x2, u2, y3, y3, ao2, mod3, ssm_d.reshape(1, S5_WIDTH), wglu_bf, wout_bf,
      norm2_w.reshape(1, D_MODEL), wr_t, rbias3)


def _swiglu_bf(t, wg, wu, wd):
    a = jnp.dot(t, wg, preferred_element_type=F32)
    b = jnp.dot(t, wu, preferred_element_type=F32)
    act = a * jax.nn.sigmoid(a) * b
    return jnp.dot(act.astype(BF16), wd, preferred_element_type=F32)


MOE_SB = 256
MOE_SEG = 16
MOE_TF = 1024
MOE_XW = D_MODEL
MOE_STATIC_CHUNKS = 3
MOE_CHUNK = 512
MOE_DMA_SIZES = tuple(2 ** k for k in range(MOE_TF.bit_length() - 2, 3, -1))
MOE_RPAD = -(-(TOP_K * MOE_SB + N_EXPERTS * (MOE_SEG - 1) + MOE_STATIC_CHUNKS * MOE_SEG) // MOE_CHUNK) * MOE_CHUNK


MOE_PIECE_ROWS = (4 * MOE_SEG, 3 * MOE_SEG, 2 * MOE_SEG, MOE_SEG)
MOE_PIECE_WIDTH = (MOE_RPAD // (4 * MOE_SEG), N_EXPERTS, N_EXPERTS, N_EXPERTS)
MOE_NPIECE_REFS = 2 * len(MOE_PIECE_ROWS)
MOE_PACK = 4096


def _seg_copies(pieces, sb, make_copy, act):
    for ci, (width, rows) in enumerate(zip(MOE_PIECE_WIDTH, MOE_PIECE_ROWS)):
        p_ref, n_ref = pieces[2 * ci], pieces[2 * ci + 1]

        def body(k, carry, p_ref=p_ref, width=width, rows=rows):
            v = p_ref[sb * width + k]
            act(make_copy(pl.multiple_of(v & (MOE_PACK - 1), MOE_SEG),
                          pl.multiple_of(v >> (MOE_PACK.bit_length() - 1), MOE_SEG), rows))
            return carry
        lax.fori_loop(0, n_ref[sb], body, 0)


MOE_WAIT_SIZES = tuple(2 ** k for k in range((MOE_RPAD - 1).bit_length() - 1, 3, -1))


def _sub_block_rows(cnt_ref, loc_ref, sb):
    last = sb * N_EXPERTS + N_EXPERTS - 1
    return loc_ref[last] + cnt_ref[last]


def _wait_rows(rows, make_copy):
    for b in MOE_WAIT_SIZES:
        @pl.when((rows & b) != 0)
        def _(b=b):
            make_copy(b).wait()


def _build_onehot(gt, p_ref, cnt_ref, loc_ref, sb, weighted):
    t = gt.shape[1]
    selm = gt > 0.0
    r = lax.broadcasted_iota(jnp.int32, (t, t), 0)
    c = lax.broadcasted_iota(jnp.int32, (t, t), 1)
    upper = jnp.where(r < c, 1.0, 0.0).astype(BF16)
    rank = jnp.dot(jnp.where(selm, 1.0, 0.0).astype(BF16), upper, preferred_element_type=F32)
    key = jnp.where(selm, rank, -1.0)
    j16 = lax.broadcasted_iota(jnp.int32, (MOE_SEG, t), 0).astype(F32)
    zstart = jnp.minimum(_sub_block_rows(cnt_ref, loc_ref, sb), MOE_RPAD - MOE_CHUNK)
    p_ref[pl.ds(pl.multiple_of(zstart, MOE_SEG), MOE_CHUNK), :] = jnp.zeros((MOE_CHUNK, t), BF16)

    for e in range(N_EXPERTS):
        off = loc_ref[sb * N_EXPERTS + e]
        n = cnt_ref[sb * N_EXPERTS + e]
        row = key[e:e + 1, :]
        val = gt[e:e + 1, :] if weighted else 1.0

        def put(cidx, row=row, off=off, val=val):
            first = cidx * MOE_SEG
            firstf = float(first) if isinstance(first, int) else first.astype(F32)
            tile = jnp.where(row == j16 + firstf, val, 0.0).astype(BF16)
            p_ref[pl.ds(pl.multiple_of(off + first, MOE_SEG), MOE_SEG), :] = tile

        for cidx in range(MOE_STATIC_CHUNKS):
            put(cidx)

        def extra(cidx, carry, put=put):
            put(cidx)
            return carry
        lax.fori_loop(MOE_STATIC_CHUNKS, n // MOE_SEG, extra, 0)


def _dispatch_kernel(cnt_ref, loc_ref, *refs, nsb_p, n_sb):
    pieces = refs[:MOE_NPIECE_REFS]
    padoff_ref, padn_ref, hp_ref, hs_ref, gt_ref, xe_ref, p_ref, xg_ref, z_ref, sem = refs[MOE_NPIECE_REFS:]
    s = pl.program_id(0)
    slot = s % 2

    def copies(sb, slot_, act):
        def mk(off, g, b):
            return pltpu.make_async_copy(xg_ref.at[slot_, pl.ds(off, b)], xe_ref.at[pl.ds(g, b)], sem.at[slot_])
        _seg_copies(pieces, sb, mk, act)

    def wait_all(sb, slot_):
        _wait_rows(_sub_block_rows(cnt_ref, loc_ref, sb),
                   lambda b: pltpu.make_async_copy(xg_ref.at[slot_, pl.ds(0, b)], xe_ref.at[pl.ds(0, b)],
                                                   sem.at[slot_]))

    @pl.when(s >= 2)
    def _():
        wait_all(s - 2, slot)

    gt = gt_ref[...]
    _build_onehot(gt, p_ref, cnt_ref, loc_ref, s, weighted=False)
    xext = jnp.where(s < nsb_p, hp_ref[...], hs_ref[...])
    rows = _sub_block_rows(cnt_ref, loc_ref, s)
    for i in range(MOE_RPAD // MOE_CHUNK):
        @pl.when(i * MOE_CHUNK < rows)
        def _(i=i):
            xg_ref[slot, i * MOE_CHUNK:(i + 1) * MOE_CHUNK, :] = jnp.dot(
                p_ref[i * MOE_CHUNK:(i + 1) * MOE_CHUNK, :], xext, preferred_element_type=F32).astype(BF16)
    copies(s, slot, lambda cp: cp.start())

    @pl.when(s == n_sb - 1)
    def _():
        if n_sb >= 2:
            wait_all(s - 1, 1 - slot)
        wait_all(s, slot)
        z_ref[...] = jnp.zeros(z_ref.shape, BF16)

        def pads(act):
            def body(e, carry):
                n = padn_ref[e]
                off = padoff_ref[e]
                for b in MOE_DMA_SIZES:
                    @pl.when((n & b) != 0)
                    def _(b=b):
                        done = n & (-2 * b)
                        act(pltpu.make_async_copy(
                            z_ref.at[pl.ds(0, b)],
                            xe_ref.at[pl.ds(pl.multiple_of(off + done, MOE_SEG), b)], sem.at[0]))
                return carry
            lax.fori_loop(0, N_EXPERTS, body, 0)
        pads(lambda cp: cp.start())
        pads(lambda cp: cp.wait())


def _ffn_kernel(te_ref, tidx_ref, tcode_ref, xe_ref, wg_ref, wu_ref, wd_ref, ye_ref, wgub, wdb):
    i = pl.program_id(0)
    code = tcode_ref[i]

    @pl.when(code == 2)
    def _():
        wgub[:, :EXPERT_FF] = wg_ref[0].astype(BF16)
        wgub[:, EXPERT_FF:] = wu_ref[0].astype(BF16)
        wdb[...] = wd_ref[0].astype(BF16)

    @pl.when(code != 0)
    def _():
        ab = jnp.dot(xe_ref[...], wgub[...], preferred_element_type=F32)
        a = ab[:, :EXPERT_FF]
        act = a * jax.nn.sigmoid(a) * ab[:, EXPERT_FF:]
        ye_ref[...] = jnp.dot(act.astype(BF16), wdb[...], preferred_element_type=F32).astype(BF16)


def _combine_kernel(cnt_ref, loc_ref, *refs, sb0, n_steps):
    pieces = refs[:MOE_NPIECE_REFS]
    (ye_ref, gt_ref, h_ref, x1_ref, mod_ref, sg_ref, su_ref, sd_ref, fn_ref, o_ref,
     p_ref, ys_ref, acc_ref, sem) = refs[MOE_NPIECE_REFS:]
    i = pl.program_id(0)
    slot = i % 2
    sb = sb0 + i

    def copies(sb_, slot_, act):
        def mk(off, g, b):
            return pltpu.make_async_copy(ye_ref.at[pl.ds(g, b)], ys_ref.at[slot_, pl.ds(off, b)], sem.at[slot_])
        _seg_copies(pieces, sb_, mk, act)

    @pl.when(i == 0)
    def _():
        ys_ref[...] = jnp.zeros(ys_ref.shape, BF16)
        copies(sb, slot, lambda cp: cp.start())

    @pl.when(i + 1 < n_steps)
    def _():
        copies(sb + 1, 1 - slot, lambda cp: cp.start())

    acc_ref[...] = _swiglu_bf(h_ref[...], sg_ref[...], su_ref[...], sd_ref[...])
    _build_onehot(gt_ref[...], p_ref, cnt_ref, loc_ref, sb, weighted=True)
    rows = _sub_block_rows(cnt_ref, loc_ref, sb)
    _wait_rows(rows, lambda b: pltpu.make_async_copy(ye_ref.at[pl.ds(0, b)], ys_ref.at[slot, pl.ds(0, b)],
                                                     sem.at[slot]))
    for c in range(MOE_RPAD // MOE_CHUNK):
        @pl.when(c * MOE_CHUNK < rows)
        def _(c=c):
            acc_ref[...] += lax.dot_general(
                p_ref[c * MOE_CHUNK:(c + 1) * MOE_CHUNK, :], ys_ref[slot, c * MOE_CHUNK:(c + 1) * MOE_CHUNK, :],
                (((0,), (0,)), ((), ())), preferred_element_type=F32)
    gate2 = mod_ref[0][:, 5 * D_MODEL:6 * D_MODEL]
    x2 = x1_ref[...] + gate2 * acc_ref[...]
    ms = jnp.mean(x2 * x2, axis=-1, keepdims=True)
    o_ref[...] = x2 * lax.rsqrt(ms + NORM_EPS) * fn_ref[...]


def _moe_plan(gates_t):
    ne, n = gates_t.shape
    n_sb = n // MOE_SB
    cnt = jnp.sum((gates_t > 0.0).reshape(ne, n_sb, MOE_SB), axis=-1, dtype=jnp.int32).T
    cnt16 = (cnt + MOE_SEG - 1) // MOE_SEG * MOE_SEG
    loc = jnp.cumsum(cnt16, axis=1) - cnt16
    tot = jnp.sum(cnt16, axis=0)
    totp = (tot + MOE_TF - 1) // MOE_TF * MOE_TF
    ends = jnp.cumsum(totp)
    base = ends - totp
    goff = base[None, :] + jnp.cumsum(cnt16, axis=0) - cnt16
    rows_max = TOP_K * n + n_sb * ne * (MOE_SEG - 1) + ne * (MOE_TF - MOE_SEG)
    nt_max = -(-rows_max // MOE_TF)
    tiles = jnp.arange(nt_max, dtype=jnp.int32)
    used = ends[-1]
    valid = tiles * MOE_TF < used
    tidx = jnp.where(valid, tiles, jnp.maximum(used // MOE_TF - 1, 0))
    te = jnp.sum(ends[None, :] <= (tidx * MOE_TF)[:, None], axis=1, dtype=jnp.int32)
    te = jnp.minimum(te, ne - 1)
    first = jnp.concatenate([jnp.ones((1,), jnp.bool_), te[1:] != te[:-1]])
    tcode = jnp.where(valid, 1 + first.astype(jnp.int32), 0)

    def piece_list(npieces, first_row, width, rows):
        cum = jnp.cumsum(npieces, axis=1)
        k = jnp.arange(width, dtype=jnp.int32)
        ek = jnp.minimum(jnp.sum(cum[:, None, :] <= k[None, :, None], axis=2, dtype=jnp.int32), ne - 1)
        onehot = ek[:, :, None] == jnp.arange(ne, dtype=jnp.int32)[None, None, :]
        pick = lambda a: jnp.sum(jnp.where(onehot, a[:, None, :], 0), axis=2, dtype=jnp.int32)
        row = pick(first_row) + rows * (k[None, :] - pick(cum - npieces))
        packed = (pick(goff) + row) * MOE_PACK + pick(loc) + row
        return packed.reshape(-1).astype(jnp.int32), cum[:, -1].astype(jnp.int32)

    big = MOE_PIECE_ROWS[0]
    nbig = cnt16 // big
    pieces = list(piece_list(nbig, jnp.zeros_like(cnt16), MOE_PIECE_WIDTH[0], big))
    for width, rows in zip(MOE_PIECE_WIDTH[1:], MOE_PIECE_ROWS[1:]):
        pieces += list(piece_list((cnt16 % big == rows).astype(jnp.int32), nbig * big, width, rows))
    return dict(cnt=cnt16.reshape(-1), loc=loc.reshape(-1).astype(jnp.int32), pieces=tuple(pieces),
                padoff=(base + tot).astype(jnp.int32), padn=(totp - tot).astype(jnp.int32),
                te=te, tidx=tidx.astype(jnp.int32), tcode=tcode, nt_max=nt_max)


def _dispatch_call(plan, h2_p, h2_s, gates_t):
    nsb_p = h2_p.shape[0] // MOE_SB
    n_sb = gates_t.shape[1] // MOE_SB
    grid_spec = pltpu.PrefetchScalarGridSpec(
        num_scalar_prefetch=4 + MOE_NPIECE_REFS, grid=(n_sb,),
        in_specs=[pl.BlockSpec((MOE_SB, D_MODEL), lambda s, *_: (jnp.minimum(s, nsb_p - 1), 0)),
                  pl.BlockSpec((MOE_SB, D_MODEL), lambda s, *_: (jnp.maximum(s - nsb_p, 0), 0)),
                  pl.BlockSpec((N_EXPERTS, MOE_SB), lambda s, *_: (0, s))],
        out_specs=pl.BlockSpec(memory_space=pl.ANY),
        scratch_shapes=[pltpu.VMEM((MOE_RPAD, MOE_SB), BF16),
                        pltpu.VMEM((2, MOE_RPAD, MOE_XW), BF16),
                        pltpu.VMEM((MOE_DMA_SIZES[0], MOE_XW), BF16),
                        pltpu.SemaphoreType.DMA((2,))])
    return pl.pallas_call(
        functools.partial(_dispatch_kernel, nsb_p=nsb_p, n_sb=n_sb),
        grid_spec=grid_spec,
        out_shape=jax.ShapeDtypeStruct((plan['nt_max'] * MOE_TF, MOE_XW), BF16),
        compiler_params=_cparams(("arbitrary",)),
        name="moe_dispatch",
    )(plan['cnt'], plan['loc'], *plan['pieces'], plan['padoff'], plan['padn'], h2_p, h2_s, gates_t)


def _ffn_call(plan, xe, wg, wu, wd):
    grid_spec = pltpu.PrefetchScalarGridSpec(
        num_scalar_prefetch=3, grid=(plan['nt_max'],),
        in_specs=[pl.BlockSpec((MOE_TF, MOE_XW), lambda i, te, tidx, tv: (tidx[i], 0)),
                  pl.BlockSpec((1, D_MODEL, EXPERT_FF), lambda i, te, tidx, tv: (te[i], 0, 0)),
                  pl.BlockSpec((1, D_MODEL, EXPERT_FF), lambda i, te, tidx, tv: (te[i], 0, 0)),
                  pl.BlockSpec((1, EXPERT_FF, D_MODEL), lambda i, te, tidx, tv: (te[i], 0, 0))],
        out_specs=pl.BlockSpec((MOE_TF, D_MODEL), lambda i, te, tidx, tv: (tidx[i], 0)),
        scratch_shapes=[pltpu.VMEM((D_MODEL, 2 * EXPERT_FF), BF16),
                        pltpu.VMEM((EXPERT_FF, D_MODEL), BF16)])
    return pl.pallas_call(
        _ffn_kernel,
        grid_spec=grid_spec,
        out_shape=jax.ShapeDtypeStruct((xe.shape[0], D_MODEL), BF16),
        compiler_params=_cparams(("arbitrary",)),
        name="moe_ffn",
    )(plan['te'], plan['tidx'], plan['tcode'], xe, wg, wu, wd)


def _combine_call(plan, ye, gates_t, h2, x1, mod3, sg, su, sd, final_w, sb0, seq_len):
    n = h2.shape[0]
    n_steps = n // MOE_SB
    bpb = seq_len // MOE_SB
    row = lambda i, *_: (i, 0)
    const2 = lambda i, *_: (0, 0)
    grid_spec = pltpu.PrefetchScalarGridSpec(
        num_scalar_prefetch=2 + MOE_NPIECE_REFS, grid=(n_steps,),
        in_specs=[pl.BlockSpec(memory_space=pl.ANY),
                  pl.BlockSpec((N_EXPERTS, MOE_SB), lambda i, *_: (0, sb0 + i)),
                  pl.BlockSpec((MOE_SB, D_MODEL), row),
                  pl.BlockSpec((MOE_SB, D_MODEL), row),
                  pl.BlockSpec((1, 1, 6 * D_MODEL), lambda i, *_: (i // bpb, 0, 0)),
                  pl.BlockSpec((D_MODEL, EXPERT_FF), const2),
                  pl.BlockSpec((D_MODEL, EXPERT_FF), const2),
                  pl.BlockSpec((EXPERT_FF, D_MODEL), const2),
                  pl.BlockSpec((1, D_MODEL), const2)],
        out_specs=pl.BlockSpec((MOE_SB, D_MODEL), row),
        scratch_shapes=[pltpu.VMEM((MOE_RPAD, MOE_SB), BF16),
                        pltpu.VMEM((2, MOE_RPAD, D_MODEL), BF16),
                        pltpu.VMEM((MOE_SB, D_MODEL), F32),
                        pltpu.SemaphoreType.DMA((2,))])
    return pl.pallas_call(
        functools.partial(_combine_kernel, sb0=sb0, n_steps=n_steps),
        grid_spec=grid_spec,
        out_shape=jax.ShapeDtypeStruct((n, D_MODEL), F32),
        compiler_params=_cparams(("arbitrary",)),
        name="moe_combine",
    )(plan['cnt'], plan['loc'], *plan['pieces'], ye, gates_t, h2, x1, mod3, sg, su, sd,
      final_w.reshape(1, D_MODEL))


def _moe_sparse(h2_p, h2_s, gt_p, gt_s, x1_p, x1_s, mod_p, mod_s, wg, wu, wd, sg, su, sd, final_w, dseq):
    gates_t = jnp.concatenate([gt_p, gt_s], axis=1)
    plan = _moe_plan(gates_t)
    xe = _dispatch_call(plan, h2_p, h2_s, gates_t)
    ye = _ffn_call(plan, xe, wg, wu, wd)
    n_p = h2_p.shape[0]
    y_p = _combine_call(plan, ye, gates_t, h2_p, x1_p, mod_p, sg, su, sd, final_w, 0, n_p)
    y_s = _combine_call(plan, ye, gates_t, h2_s, x1_s, mod_s, sg, su, sd, final_w, n_p // MOE_SB, dseq)
    return y_p, y_s


def _rope_tables(n_tokens):
    rows = n_tokens // GRID_W
    row = jnp.repeat(jnp.arange(rows, dtype=F32), GRID_W)
    col = jnp.tile(jnp.arange(GRID_W, dtype=F32), rows)
    freqs = ROPE_THETA ** (-jnp.arange(ROT_PAIRS, dtype=F32) / ROT_PAIRS)
    ar = row[:, None] * freqs
    ac = col[:, None] * freqs
    cos = jnp.concatenate([jnp.cos(ar), jnp.cos(ar), jnp.cos(ac), jnp.cos(ac)], axis=1)
    sin = jnp.concatenate([-jnp.sin(ar), jnp.sin(ar), -jnp.sin(ac), jnp.sin(ac)], axis=1)
    return jnp.tile(cos, (1, 2)), jnp.tile(sin, (1, 2))


def kernel(x_prompt, x_sample, c, cache_k, cache_v, state_ssm_re, state_ssm_im, c_ctx, w_ada, b_ada, norm1_w, w_in, ssm_lambda_re, ssm_lambda_im, ssm_log_dt, ssm_b_re, ssm_b_im, ssm_c_re, ssm_c_im, ssm_d, ssm_w_glu, diff_lambda_q, diff_lambda_k, diff_subln_w, w_out, norm2_w, w_router, router_bias, w_exp_gate, w_exp_up, w_exp_down, w_sh_gate, w_sh_up, w_sh_down, final_norm_w):
    depth = w_ada.shape[0]
    assert depth == 1
    l = 0
    lambda_init = 0.8 - 0.6 * math.exp(-0.3 * l)
    bsz, seq, _ = x_prompt.shape
    dbs, dseq, _ = x_sample.shape
    n_p, n_s = bsz * seq, dbs * dseq

    cond8 = jnp.zeros((SUBLANES, D_MODEL), F32).at[:dbs].set(c).at[dbs].set(c_ctx)
    mod = _ada_call(cond8, w_ada[l], b_ada[l])
    mod_s = mod[:dbs].reshape(dbs, 1, 6 * D_MODEL)
    mod_p = mod[dbs:dbs + 1].reshape(1, 1, 6 * D_MODEL)

    w_in_bf = w_in[l].astype(BF16)
    xp2 = x_prompt.reshape(n_p, D_MODEL)
    xs2 = x_sample.reshape(n_s, D_MODEL)
    u_p, q_p, k_p, v_p = _inproj_call(xp2, mod_p, norm1_w[l], w_in_bf, None, n_p, F32)
    u_s, q_s, k_s, v_s = _inproj_call(xs2, mod_s, norm1_w[l], w_in_bf, _rope_tables(dseq), dseq, BF16)

    bblk, cblk, lam_r, lam_i = _s5_weights(ssm_lambda_re[l], ssm_lambda_im[l], ssm_log_dt[l],
                                           ssm_b_re[l], ssm_b_im[l], ssm_c_re[l], ssm_c_im[l])
    lr_p = jnp.broadcast_to(lam_r[:, :, None, :], (2, N_COLBLK, SUBLANES, S5_BLK_STATES))
    li_p = jnp.broadcast_to(lam_i[:, :, None, :], (2, N_COLBLK, SUBLANES, S5_BLK_STATES))
    g_p = bsz // SUBLANES
    h0_p = jnp.zeros((2, g_p, N_COLBLK * 8, SUBLANES, LANES), F32)
    y_p, hfin = _s5_call(u_p.reshape(bsz, seq, S5_WIDTH), bblk, cblk, lr_p, li_p, h0_p, SUBLANES, tb=128)

    def halves(x):
        x = x.reshape(2, 2, 2, 1, S5_BLK_STATES)
        x = jnp.broadcast_to(x, (2, 2, 2, dbs, S5_BLK_STATES))
        return x.transpose(0, 2, 1, 3, 4).reshape(2, 2, 2 * dbs, S5_BLK_STATES)

    def h0_tiles(s):
        s = s.astype(F32).reshape(dbs, 2, 2, 2, 4, LANES)
        return s.transpose(1, 3, 4, 2, 0, 5).reshape(2, 1, 2, 4, 2 * dbs, LANES)

    h0_s = jnp.concatenate([h0_tiles(state_ssm_re[:, l]), h0_tiles(state_ssm_im[:, l])], axis=3)
    h0_s = h0_s.reshape(2, 1, 16, SUBLANES, LANES)
    y_s, _ = _s5_call(u_s.reshape(dbs, dseq, S5_WIDTH), bblk, cblk, halves(lam_r), halves(lam_i), h0_s, dbs,
                      tb=256)

    lq = diff_lambda_q[l].astype(F32)
    lk = diff_lambda_k[l].astype(F32)
    lam = (jnp.exp(jnp.sum(lq[0] * lk[0])) - jnp.exp(jnp.sum(lq[1] * lk[1])) + lambda_init).reshape(1)
    subln = diff_subln_w[l].astype(F32).reshape(1, VALUE_DIM)
    ao_p = _attn_ctx_call(lam, q_p.reshape(bsz, seq, ATTN_WIDTH), k_p.reshape(bsz, seq, ATTN_WIDTH),
                          v_p.reshape(bsz, seq, ATTN_WIDTH), subln, lambda_init)
    past = cache_k.shape[2]
    ao_s = _attn_lat_call(lam, q_s.reshape(dbs, dseq, ATTN_WIDTH), k_s.reshape(dbs, dseq, ATTN_WIDTH),
                          v_s.reshape(dbs, dseq, ATTN_WIDTH),
                          cache_k[:, l].reshape(dbs, past, ATTN_WIDTH).astype(BF16),
                          cache_v[:, l].reshape(dbs, past, ATTN_WIDTH).astype(BF16), subln, lambda_init)

    wglu_bf = ssm_w_glu[l].astype(BF16)
    wout_bf = w_out[l].astype(BF16)
    wr_t = w_router[l].astype(F32).T
    rbias3 = router_bias[l].astype(F32).reshape(N_EXPERT_GROUPS, N_EXPERTS // N_EXPERT_GROUPS, 1)
    x1_p, h2_p, gates_p = _post_call(xp2, u_p, y_p, ao_p.reshape(n_p, ATTN_WIDTH), mod_p, ssm_d[l],
                                     wglu_bf, wout_bf, norm2_w[l], wr_t, rbias3, n_p)
    x1_s, h2_s, gates_s = _post_call(xs2, u_s, y_s, ao_s.reshape(n_s, ATTN_WIDTH), mod_s, ssm_d[l],
                                     wglu_bf, wout_bf, norm2_w[l], wr_t, rbias3, dseq)

    wg = w_exp_gate[l]
    wu = w_exp_up[l]
    wd = w_exp_down[l]
    sg = w_sh_gate[l].astype(BF16)
    su = w_sh_up[l].astype(BF16)
    sd = w_sh_down[l].astype(BF16)
    y_prompt, y_sample = _moe_sparse(h2_p, h2_s, gates_p, gates_s, x1_p, x1_s, mod_p, mod_s,
                                     wg, wu, wd, sg, su, sd, final_norm_w, dseq)

    new_cache_k = k_p.reshape(bsz, 1, seq, N_DIFF_HEADS, VALUE_DIM)
    new_cache_v = v_p.reshape(bsz, 1, seq, N_DIFF_HEADS, VALUE_DIM)
    hf = hfin.reshape(2, g_p, N_COLBLK, 2, 4, SUBLANES, LANES)
    hf = hf.transpose(3, 1, 5, 0, 2, 4, 6).reshape(2, bsz, 1, 2, S5_GROUPS, S5_STATE)
    return (y_prompt.reshape(bsz, seq, D_MODEL), y_sample.reshape(dbs, dseq, D_MODEL),
            new_cache_k, new_cache_v, hf[0], hf[1])
```

```python
import functools
import math

import jax
import jax.numpy as jnp
from jax import lax
from jax.experimental import pallas as pl
from jax.experimental.pallas import tpu as pltpu

F32 = jnp.float32
BF16 = jnp.bfloat16

D_MODEL = 1024
GRID_W = 64
S5_WIDTH = 512
S5_GROUP = 16
S5_GROUPS = 32
S5_STATE = 64
ATTN_WIDTH = 512
DIFF_HEAD_DIM = 64
VALUE_DIM = 128
N_DIFF_HEADS = 4
IN_WIDTH = S5_WIDTH + 3 * ATTN_WIDTH
ROT_PAIRS = DIFF_HEAD_DIM // 4
ROPE_THETA = 10000.0
N_EXPERTS = 64
TOP_K = 8
N_EXPERT_GROUPS = 8
TOPK_GROUPS = 4
EXPERT_FF = 256
ROUTED_SCALE = 2.5
NORM_EPS = 1e-6

LANES = 128
SUBLANES = 8
S5_COLBLK = 8 * S5_GROUP
S5_BLK_STATES = 8 * S5_STATE
N_COLBLK = S5_WIDTH // S5_COLBLK
VMEM_LIMIT = 56 * 1024 * 1024


def _cparams(sem):
    return pltpu.CompilerParams(dimension_semantics=sem, vmem_limit_bytes=VMEM_LIMIT)


def _ada_kernel(cond_ref, w_ref, b_ref, o_ref):
    c = cond_ref[...]
    s = c * jax.nn.sigmoid(c)
    o_ref[...] = jnp.dot(s.astype(BF16), w_ref[...].astype(BF16),
                         preferred_element_type=F32) + b_ref[...]


def _ada_call(cond8, w_ada, b_ada):
    n = w_ada.shape[1]
    tn = 1536
    return pl.pallas_call(
        _ada_kernel,
        grid=(n // tn,),
        in_specs=[pl.BlockSpec((SUBLANES, D_MODEL), lambda j: (0, 0)),
                  pl.BlockSpec((D_MODEL, tn), lambda j: (0, j)),
                  pl.BlockSpec((1, tn), lambda j: (0, j))],
        out_specs=pl.BlockSpec((SUBLANES, tn), lambda j: (0, j)),
        out_shape=jax.ShapeDtypeStruct((SUBLANES, n), F32),
        compiler_params=_cparams(("arbitrary",)),
        name="adaln",
    )(cond8, w_ada, b_ada.reshape(1, n))


def _rope_apply(t, cos, sin):
    parts = []
    for cidx in range(ATTN_WIDTH // LANES):
        xc = t[:, cidx * LANES:(cidx + 1) * LANES]
        up = pltpu.roll(xc, LANES - ROT_PAIRS, 1)
        dn = pltpu.roll(xc, ROT_PAIRS, 1)
        lane = lax.broadcasted_iota(jnp.int32, xc.shape, 1)
        partner = jnp.where((lane % (2 * ROT_PAIRS)) < ROT_PAIRS, up, dn)
        parts.append(xc * cos + partner * sin)
    return jnp.concatenate(parts, axis=1)


def _inproj_kernel(*refs, rope, kv_dtype):
    if rope:
        x_ref, mod_ref, n1_ref, w_ref, cos_ref, sin_ref, u_ref, q_ref, k_ref, v_ref = refs
    else:
        x_ref, mod_ref, n1_ref, w_ref, u_ref, q_ref, k_ref, v_ref = refs
    x = x_ref[...]
    ms = jnp.mean(x * x, axis=-1, keepdims=True)
    xn = x * lax.rsqrt(ms + NORM_EPS) * n1_ref[...]
    mod = mod_ref[0]
    shift = mod[:, 0:D_MODEL]
    scale = mod[:, D_MODEL:2 * D_MODEL]
    h = xn * (1.0 + scale) + shift
    proj = jnp.dot(h.astype(BF16), w_ref[...], preferred_element_type=F32)
    u_ref[...] = proj[:, :S5_WIDTH]
    q = proj[:, S5_WIDTH:S5_WIDTH + ATTN_WIDTH]
    k = proj[:, S5_WIDTH + ATTN_WIDTH:S5_WIDTH + 2 * ATTN_WIDTH]
    v = proj[:, S5_WIDTH + 2 * ATTN_WIDTH:]
    if rope:
        cos = cos_ref[...]
        sin = sin_ref[...]
        q = _rope_apply(q, cos, sin)
        k = _rope_apply(k, cos, sin)
    q_ref[...] = (q * (DIFF_HEAD_DIM ** -0.5)).astype(BF16)
    k_ref[...] = k.astype(kv_dtype)
    v_ref[...] = v.astype(kv_dtype)


def _inproj_call(x2, mod3, norm1_w, w_in_bf, rope_tabs, seq_len, kv_dtype, tm=1024):
    n = x2.shape[0]
    bpb = seq_len // tm
    rope = rope_tabs is not None
    in_specs = [pl.BlockSpec((tm, D_MODEL), lambda i: (i, 0)),
                pl.BlockSpec((1, 1, 6 * D_MODEL), lambda i: (i // bpb, 0, 0)),
                pl.BlockSpec((1, D_MODEL), lambda i: (0, 0)),
                pl.BlockSpec((D_MODEL, IN_WIDTH), lambda i: (0, 0))]
    args = [x2, mod3, norm1_w.reshape(1, D_MODEL), w_in_bf]
    if rope:
        in_specs += [pl.BlockSpec((tm, LANES), lambda i: (i % bpb, 0)),
                     pl.BlockSpec((tm, LANES), lambda i: (i % bpb, 0))]
        args += list(rope_tabs)
    ospec = pl.BlockSpec((tm, S5_WIDTH), lambda i: (i, 0))
    return pl.pallas_call(
        functools.partial(_inproj_kernel, rope=rope, kv_dtype=kv_dtype),
        grid=(n // tm,),
        in_specs=in_specs,
        out_specs=[ospec, ospec, ospec, ospec],
        out_shape=[jax.ShapeDtypeStruct((n, S5_WIDTH), F32),
                   jax.ShapeDtypeStruct((n, ATTN_WIDTH), BF16),
                   jax.ShapeDtypeStruct((n, ATTN_WIDTH), kv_dtype),
                   jax.ShapeDtypeStruct((n, ATTN_WIDTH), kv_dtype)],
        compiler_params=_cparams(("parallel",)),
        name="inproj_rope" if rope else "inproj",
    )(*args)


def _s5_slot(b, q, nseq):
    if nseq == SUBLANES:
        return b, q * 8
    half = q // 2
    return half * nseq + b, (q % 2) * 8


def _s5_kernel(u_ref, b_ref, c_ref, lr_ref, li_ref, h0_ref, y_ref, hfin_ref, buh, hs, hst,
               *, nseq, tb, stride, nqq):
    d = pl.program_id(1)
    c = pl.program_id(2)
    nc = pl.num_programs(2)

    @pl.when(c == 0)
    def _():
        hst[...] = h0_ref[0, 0]
        hs[...] = jnp.zeros(hs.shape, F32)

    u = u_ref[...].reshape(nseq * tb, S5_WIDTH).astype(BF16)
    for q in range(N_COLBLK):
        res = jnp.dot(u[:, q * S5_COLBLK:(q + 1) * S5_COLBLK], b_ref[0, q],
                      preferred_element_type=F32)
        for b in range(nseq):
            slot, ct0 = _s5_slot(b, q, nseq)
            for j in range(8):
                buh[ct0 + j, slot * stride:slot * stride + tb, :] = (
                    res[b * tb:(b + 1) * tb, j * LANES:(j + 1) * LANES])

    for qq in range(nqq):
        lr = [lr_ref[0, qq, :, i * LANES:(i + 1) * LANES] for i in range(4)]
        li = [li_ref[0, qq, :, i * LANES:(i + 1) * LANES] for i in range(4)]
        h_init = tuple(hst[qq * 8 + i] for i in range(8))

        def body(s, carry, qq=qq, lr=lr, li=li):
            t = jnp.where(d == 0, s, tb - 1 - s)
            idx = pl.ds(t, SUBLANES, stride=stride)
            new_r, new_i = [], []
            for i in range(4):
                hr, hi = carry[i], carry[4 + i]
                bur = buh[qq * 8 + i, idx, :]
                bui = buh[qq * 8 + 4 + i, idx, :]
                nr = lr[i] * hr - li[i] * hi + bur
                ni = lr[i] * hi + li[i] * hr + bui
                hs[qq * 8 + i, idx, :] = nr
                hs[qq * 8 + 4 + i, idx, :] = ni
                new_r.append(nr)
                new_i.append(ni)
            return tuple(new_r) + tuple(new_i)

        h_fin = lax.fori_loop(0, tb, body, h_init, unroll=4)
        for i in range(8):
            hst[qq * 8 + i] = h_fin[i]

    rows = nseq * stride
    for q in range(N_COLBLK):
        slot0, ct0 = _s5_slot(0, q, nseq)
        hmat = jnp.concatenate(
            [hs[ct0 + j, slot0 * stride:slot0 * stride + rows, :] for j in range(8)], axis=1)
        yq = jnp.dot(hmat.astype(BF16), c_ref[0, q], preferred_element_type=F32)
        for b in range(nseq):
            y_ref[0, b, :, q * S5_COLBLK:(q + 1) * S5_COLBLK] = yq[b * stride:b * stride + tb, :]

    @pl.when(c == nc - 1)
    def _():
        hfin_ref[0, 0] = hst[...]


def _s5_call(u3, bblk, cblk, lr, li, h0, nseq, tb=64):
    nseq_total, seq_len, _ = u3.shape
    g = nseq_total // nseq
    nc = seq_len // tb
    stride = tb + SUBLANES // 2
    nqq = lr.shape[1]
    nct = nqq * 8

    def tmap(gi, d, c):
        return c + d * (nc - 1 - 2 * c)

    return pl.pallas_call(
        functools.partial(_s5_kernel, nseq=nseq, tb=tb, stride=stride, nqq=nqq),
        grid=(g, 2, nc),
        in_specs=[
            pl.BlockSpec((nseq, tb, S5_WIDTH), lambda gi, d, c: (gi, tmap(gi, d, c), 0)),
            pl.BlockSpec((1, N_COLBLK, S5_COLBLK, 2 * S5_BLK_STATES), lambda gi, d, c: (d, 0, 0, 0)),
            pl.BlockSpec((1, N_COLBLK, 2 * S5_BLK_STATES, S5_COLBLK), lambda gi, d, c: (d, 0, 0, 0)),
            pl.BlockSpec((1, nqq, SUBLANES, S5_BLK_STATES), lambda gi, d, c: (d, 0, 0, 0)),
            pl.BlockSpec((1, nqq, SUBLANES, S5_BLK_STATES), lambda gi, d, c: (d, 0, 0, 0)),
            pl.BlockSpec((1, 1, nct, SUBLANES, LANES), lambda gi, d, c: (d, gi, 0, 0, 0)),
        ],
        out_specs=[
            pl.BlockSpec((1, nseq, tb, S5_WIDTH), lambda gi, d, c: (d, gi, tmap(gi, d, c), 0)),
            pl.BlockSpec((1, 1, nct, SUBLANES, LANES), lambda gi, d, c: (d, gi, 0, 0, 0)),
        ],
        out_shape=[jax.ShapeDtypeStruct((2, nseq_total, seq_len, S5_WIDTH), F32),
                   jax.ShapeDtypeStruct((2, g, nct, SUBLANES, LANES), F32)],
        scratch_shapes=[pltpu.VMEM((nct, SUBLANES * stride, LANES), F32),
                        pltpu.VMEM((nct, SUBLANES * stride, LANES), F32),
                        pltpu.VMEM((nct, SUBLANES, LANES), F32)],
        compiler_params=_cparams(("parallel", "parallel", "arbitrary")),
        name="s5_scan_%d" % nseq,
    )(u3, bblk, cblk, lr, li, h0)


def _s5_weights(lam_re, lam_im, log_dt, b_re, b_im, c_re, c_im):
    dt = jnp.exp(log_dt.astype(F32))[..., None]
    lr = lam_re.astype(F32)
    li = lam_im.astype(F32)
    mag = jnp.exp(lr * dt)
    lbr = mag * jnp.cos(li * dt)
    lbi = mag * jnp.sin(li * dt)
    a = lbr - 1.0
    den = lr * lr + li * li
    cr = (a * lr + lbi * li) / den
    ci = (lbi * lr - a * li) / den
    bbr = cr[..., None] * b_re - ci[..., None] * b_im
    bbi = cr[..., None] * b_im + ci[..., None] * b_re
    eye = jnp.eye(8, dtype=F32)

    def pack_b(x):
        x = x.reshape(2, N_COLBLK, 8, S5_STATE, S5_GROUP).transpose(0, 1, 2, 4, 3)
        x = jnp.einsum('dqghn,gk->dqghkn', x, eye)
        return x.reshape(2, N_COLBLK, S5_COLBLK, S5_BLK_STATES)

    def pack_c(x):
        x = x.reshape(2, N_COLBLK, 8, S5_GROUP, S5_STATE).transpose(0, 1, 2, 4, 3)
        x = jnp.einsum('dqgnh,gk->dqgnkh', x, eye)
        return x.reshape(2, N_COLBLK, S5_BLK_STATES, S5_COLBLK)

    bblk = jnp.concatenate([pack_b(bbr), pack_b(bbi)], axis=-1).astype(BF16)
    cblk = jnp.concatenate([pack_c(c_re.astype(F32)), pack_c(-c_im.astype(F32))], axis=-2).astype(BF16)
    lam_r = lbr.reshape(2, N_COLBLK, S5_BLK_STATES)
    lam_i = lbi.reshape(2, N_COLBLK, S5_BLK_STATES)
    return bblk, cblk, lam_r, lam_i


ATTN_KC = 512


def _lane_fold(x, op):
    acc = x[:, 0:LANES]
    for j in range(1, x.shape[1] // LANES):
        acc = op(acc, x[:, j * LANES:(j + 1) * LANES])
    return acc


def _attn_ops(q, chunks, s_ref):
    lane = lax.broadcasted_iota(jnp.int32, q.shape, 1)
    dn = (((1,), (1,)), ((), ()))
    qms = [jnp.where(lane < DIFF_HEAD_DIM, q, jnp.zeros_like(q)),
           jnp.where(lane >= DIFF_HEAD_DIM, q, jnp.zeros_like(q))]
    offs = [0]
    for load_k, _ in chunks:
        offs.append(offs[-1] + load_k().shape[0])

    def score_chunk(m, ci, m128):
        s = lax.dot_general(qms[m], chunks[ci][0](), dn, preferred_element_type=F32)
        s_ref[m, :, offs[ci]:offs[ci + 1]] = s
        f = _lane_fold(s, jnp.maximum)
        return f if m128 is None else jnp.maximum(m128, f)

    def value_chunk(m, ci, mx, acc):
        p = jnp.exp(s_ref[m, :, offs[ci]:offs[ci + 1]] - mx)
        v = chunks[ci][1]()
        t = jnp.dot(p.astype(BF16), jnp.concatenate([v, jnp.ones_like(v)], axis=1), preferred_element_type=F32)
        return t if acc is None else acc + t

    return score_chunk, value_chunk


def _row_max(m128):
    return m128.max(axis=-1, keepdims=True)


def _normalized(acc):
    return acc[:, :VALUE_DIM] * (1.0 / acc[:, VALUE_DIM:VALUE_DIM + 1])


def _sub_ln(o, subln, lambda_init):
    ms = jnp.mean(o * o, axis=-1, keepdims=True)
    return o * lax.rsqrt(ms + NORM_EPS) * subln * (1.0 - lambda_init)


def _attn_ctx_kernel(lam_ref, q_ref, k_ref, v_ref, w_ref, o_ref, s_ref, *, lambda_init):
    for h in range(N_DIFF_HEADS):
        cols = slice(h * VALUE_DIM, (h + 1) * VALUE_DIM)
        chunks = [(lambda cols=cols: k_ref[0, :, cols].astype(BF16), lambda cols=cols: v_ref[0, :, cols].astype(BF16))]
        score_chunk, value_chunk = _attn_ops(q_ref[0, :, cols], chunks, s_ref.at[h])
        mx0 = _row_max(score_chunk(0, 0, None))
        m1 = score_chunk(1, 0, None)
        a0 = value_chunk(0, 0, mx0, None)
        a1 = value_chunk(1, 0, _row_max(m1), None)
        o = _normalized(a0) - lam_ref[0] * _normalized(a1)
        o_ref[0, :, cols] = _sub_ln(o, w_ref[...], lambda_init).astype(BF16)


def _attn_lat_kernel(lam_ref, q_ref, k_ref, v_ref, ck_ref, cv_ref, w_ref, o_ref, s_ref, *, lambda_init):
    chunks = []
    for kr, vr in ((k_ref, v_ref), (ck_ref, cv_ref)):
        kc = min(ATTN_KC, kr.shape[1])
        for c in range(kr.shape[1] // kc):
            chunks.append((lambda c=c, kr=kr, kc=kc: kr[0, c * kc:(c + 1) * kc, :],
                           lambda c=c, vr=vr, kc=kc: vr[0, c * kc:(c + 1) * kc, :]))
    nchunk = len(chunks)
    score_chunk, value_chunk = _attn_ops(q_ref[0], chunks, s_ref)

    m0 = m1 = a0 = a1 = None
    for ci in range(nchunk):
        m0 = score_chunk(0, ci, m0)
    mx0 = _row_max(m0)
    for ci in range(nchunk):
        m1 = score_chunk(1, ci, m1)
        a0 = value_chunk(0, ci, mx0, a0)
    mx1 = _row_max(m1)
    for ci in range(nchunk):
        a1 = value_chunk(1, ci, mx1, a1)
    o = _normalized(a0) - lam_ref[0] * _normalized(a1)
    o_ref[0] = _sub_ln(o, w_ref[...], lambda_init).astype(BF16)


def _attn_ctx_call(lam, q3, k3, v3, subln, lambda_init):
    bsz, seq_len, _ = q3.shape
    spec = pl.BlockSpec((1, seq_len, ATTN_WIDTH), lambda b: (b, 0, 0))
    return pl.pallas_call(
        functools.partial(_attn_ctx_kernel, lambda_init=lambda_init),
        grid=(bsz,),
        in_specs=[pl.BlockSpec(memory_space=pltpu.SMEM), spec, spec, spec,
                  pl.BlockSpec((1, VALUE_DIM), lambda b: (0, 0))],
        out_specs=spec,
        out_shape=jax.ShapeDtypeStruct((bsz, seq_len, ATTN_WIDTH), BF16),
        scratch_shapes=[pltpu.VMEM((N_DIFF_HEADS, 2, seq_len, seq_len), F32)],
        compiler_params=_cparams(("parallel",)),
        name="attn_ctx",
    )(lam, q3, k3, v3, subln)


def _attn_lat_call(lam, q3, k3, v3, ck3, cv3, subln, lambda_init, tq=1024):
    bsz, seq_len, _ = q3.shape
    ctx_len = ck3.shape[1]
    qspec = pl.BlockSpec((1, tq, VALUE_DIM), lambda b, h, i: (b, i, h))
    kspec = pl.BlockSpec((1, seq_len, VALUE_DIM), lambda b, h, i: (b, 0, h))
    cspec = pl.BlockSpec((1, ctx_len, VALUE_DIM), lambda b, h, i: (b, 0, h))
    return pl.pallas_call(
        functools.partial(_attn_lat_kernel, lambda_init=lambda_init),
        grid=(bsz, N_DIFF_HEADS, seq_len // tq),
        in_specs=[pl.BlockSpec(memory_space=pltpu.SMEM), qspec, kspec, kspec, cspec, cspec,
                  pl.BlockSpec((1, VALUE_DIM), lambda b, h, i: (0, 0))],
        out_specs=qspec,
        out_shape=jax.ShapeDtypeStruct((bsz, seq_len, ATTN_WIDTH), BF16),
        scratch_shapes=[pltpu.VMEM((2, tq, seq_len + ctx_len), F32)],
        compiler_params=_cparams(("parallel", "parallel", "arbitrary")),
        name="attn_lat",
    )(lam, q3, k3, v3, ck3, cv3, subln)


def _route(logits_t, bias3, tm):
    ng, ge = N_EXPERT_GROUPS, N_EXPERTS // N_EXPERT_GROUPS
    neg = jnp.float32(-jnp.inf)
    sc = jax.nn.sigmoid(logits_t).reshape(ng, ge, tm)
    bi = sc + bias3
    eio = lax.broadcasted_iota(jnp.int32, (ng, ge, tm), 1).astype(F32)
    gio = lax.broadcasted_iota(jnp.int32, (ng, ge, tm), 0).astype(F32)
    m1 = bi.max(axis=1, keepdims=True)
    i1 = jnp.where(bi == m1, eio, float(ge)).min(axis=1, keepdims=True)
    m2 = jnp.where(eio == i1, neg, bi).max(axis=1, keepdims=True)
    gs = jnp.broadcast_to(m1 + m2, (ng, ge, tm))
    cnt = jnp.zeros((ng, ge, tm), F32)
    for g2 in range(ng):
        o = gs[g2:g2 + 1]
        better = (o > gs) | ((o == gs) & (gio > float(g2)))
        cnt = cnt + jnp.where(better, 1.0, 0.0)
    v = jnp.where(cnt < float(TOPK_GROUPS), bi, neg)
    eidx = gio * float(ge) + eio
    selm = jnp.zeros((ng, ge, tm), F32)
    for _ in range(TOP_K):
        m = v.max(axis=0, keepdims=True).max(axis=1, keepdims=True)
        ix = jnp.where(v == m, eidx, float(N_EXPERTS)).min(axis=0, keepdims=True).min(axis=1, keepdims=True)
        oh = eidx == ix
        selm = jnp.where(oh, 1.0, selm)
        v = jnp.where(oh, neg, v)
    selsc = selm * sc
    ssum = selsc.sum(axis=0, keepdims=True).sum(axis=1, keepdims=True)
    return (selsc / ssum * ROUTED_SCALE).reshape(N_EXPERTS, tm)


def _post_kernel(x_ref, u_ref, yf_ref, yb_ref, ao_ref, mod_ref, d_ref, wglu_ref, wout_ref, n2_ref,
                 wrt_ref, rb_ref, x1_ref, h2_ref, gates_ref, *, tm):
    u = u_ref[...]
    y = u * d_ref[...] + yf_ref[0] + yb_ref[0]
    g = jax.nn.gelu(y)
    s5 = g * jax.nn.sigmoid(jnp.dot(g.astype(BF16), wglu_ref[...], preferred_element_type=F32))
    mixed = (jnp.dot(s5.astype(BF16), wout_ref[0:S5_WIDTH, :], preferred_element_type=F32)
             + jnp.dot(ao_ref[...], wout_ref[S5_WIDTH:, :], preferred_element_type=F32))
    mod = mod_ref[0]
    gate1 = mod[:, 2 * D_MODEL:3 * D_MODEL]
    shift2 = mod[:, 3 * D_MODEL:4 * D_MODEL]
    scale2 = mod[:, 4 * D_MODEL:5 * D_MODEL]
    x1 = x_ref[...] + gate1 * mixed
    x1_ref[...] = x1
    ms = jnp.mean(x1 * x1, axis=-1, keepdims=True)
    h2 = x1 * lax.rsqrt(ms + NORM_EPS) * n2_ref[...] * (1.0 + scale2) + shift2
    h_hi = h2.astype(BF16)
    h2_ref[...] = h_hi
    h_lo = (h2 - h_hi.astype(F32)).astype(BF16)
    wr = wrt_ref[...]
    w_hi = wr.astype(BF16)
    w_lo = (wr - w_hi.astype(F32)).astype(BF16)
    dn = (((1,), (1,)), ((), ()))
    logits_t = (lax.dot_general(w_hi, h_hi, dn, preferred_element_type=F32)
                + lax.dot_general(w_lo, h_hi, dn, preferred_element_type=F32)
                + lax.dot_general(w_hi, h_lo, dn, preferred_element_type=F32))
    gates_ref[...] = _route(logits_t, rb_ref[...], tm)


def _post_call(x2, u2, y4, ao2, mod3, ssm_d, wglu_bf, wout_bf, norm2_w, wr_t, rbias3, seq_len, tm=512):
    n = x2.shape[0]
    bpb = seq_len // tm
    y3 = y4.reshape(2, n, S5_WIDTH)
    row = lambda i: (i, 0)
    const2 = lambda i: (0, 0)
    return pl.pallas_call(
        functools.partial(_post_kernel, tm=tm),
        grid=(n // tm,),
        in_specs=[pl.BlockSpec((tm, D_MODEL), row),
                  pl.BlockSpec((tm, S5_WIDTH), row),
                  pl.BlockSpec((1, tm, S5_WIDTH), lambda i: (0, i, 0)),
                  pl.BlockSpec((1, tm, S5_WIDTH), lambda i: (1, i, 0)),
                  pl.BlockSpec((tm, ATTN_WIDTH), row),
                  pl.BlockSpec((1, 1, 6 * D_MODEL), lambda i: (i // bpb, 0, 0)),
                  pl.BlockSpec((1, S5_WIDTH), const2),
                  pl.BlockSpec((S5_WIDTH, S5_WIDTH), const2),
                  pl.BlockSpec((D_MODEL, D_MODEL), const2),
                  pl.BlockSpec((1, D_MODEL), const2),
                  pl.BlockSpec((N_EXPERTS, D_MODEL), const2),
                  pl.BlockSpec((N_EXPERT_GROUPS, N_EXPERTS // N_EXPERT_GROUPS, 1), lambda i: (0, 0, 0))],
        out_specs=[pl.BlockSpec((tm, D_MODEL), row),
                   pl.BlockSpec((tm, D_MODEL), row),
                   pl.BlockSpec((N_EXPERTS, tm), lambda i: (0, i))],
        out_shape=[jax.ShapeDtypeStruct((n, D_MODEL), F32),
                   jax.ShapeDtypeStruct((n, D_MODEL), BF16),
                   jax.ShapeDtypeStruct((N_EXPERTS, n), F32)],
        compiler_params=_cparams(("parallel",)),
        name="post_mix_route",
    )(x2, u2, y3, y3, ao2, mod3, ssm_d.reshape(1, S5_WIDTH), wglu_bf, wout_bf,
      norm2_w.reshape(1, D_MODEL), wr_t, rbias3)


def _swiglu_bf(t, wg, wu, wd):
    a = jnp.dot(t, wg, preferred_element_type=F32)
    b = jnp.dot(t, wu, preferred_element_type=F32)
    act = a * jax.nn.sigmoid(a) * b
    return jnp.dot(act.astype(BF16), wd, preferred_element_type=F32)


MOE_SB = 256
MOE_SEG = 16
MOE_TF = 1024
MOE_XW = D_MODEL
MOE_STATIC_CHUNKS = 3
MOE_CHUNK = 512
MOE_DMA_SIZES = tuple(2 ** k for k in range(MOE_TF.bit_length() - 2, 3, -1))
MOE_RPAD = -(-(TOP_K * MOE_SB + N_EXPERTS * (MOE_SEG - 1) + MOE_STATIC_CHUNKS * MOE_SEG) // MOE_CHUNK) * MOE_CHUNK


MOE_PIECE_ROWS = (4 * MOE_SEG, 3 * MOE_SEG, 2 * MOE_SEG, MOE_SEG)
MOE_PIECE_WIDTH = (MOE_RPAD // (4 * MOE_SEG), N_EXPERTS, N_EXPERTS, N_EXPERTS)
MOE_NPIECE_REFS = 2 * len(MOE_PIECE_ROWS)
MOE_PACK = 4096


def _seg_copies(pieces, sb, make_copy, act):
    for ci, (width, rows) in enumerate(zip(MOE_PIECE_WIDTH, MOE_PIECE_ROWS)):
        p_ref, n_ref = pieces[2 * ci], pieces[2 * ci + 1]

        def body(k, carry, p_ref=p_ref, width=width, rows=rows):
            v = p_ref[sb * width + k]
            act(make_copy(pl.multiple_of(v & (MOE_PACK - 1), MOE_SEG),
                          pl.multiple_of(v >> (MOE_PACK.bit_length() - 1), MOE_SEG), rows))
            return carry
        lax.fori_loop(0, n_ref[sb], body, 0)


MOE_WAIT_SIZES = tuple(2 ** k for k in range((MOE_RPAD - 1).bit_length() - 1, 3, -1))


def _sub_block_rows(cnt_ref, loc_ref, sb):
    last = sb * N_EXPERTS + N_EXPERTS - 1
    return loc_ref[last] + cnt_ref[last]


def _wait_rows(rows, make_copy):
    for b in MOE_WAIT_SIZES:
        @pl.when((rows & b) != 0)
        def _(b=b):
            make_copy(b).wait()


def _build_onehot(gt, p_ref, cnt_ref, loc_ref, sb, weighted):
    t = gt.shape[1]
    selm = gt > 0.0
    r = lax.broadcasted_iota(jnp.int32, (t, t), 0)
    c = lax.broadcasted_iota(jnp.int32, (t, t), 1)
    upper = jnp.where(r < c, 1.0, 0.0).astype(BF16)
    rank = jnp.dot(jnp.where(selm, 1.0, 0.0).astype(BF16), upper, preferred_element_type=F32)
    key = jnp.where(selm, rank, -1.0)
    j16 = lax.broadcasted_iota(jnp.int32, (MOE_SEG, t), 0).astype(F32)
    zstart = jnp.minimum(_sub_block_rows(cnt_ref, loc_ref, sb), MOE_RPAD - MOE_CHUNK)
    p_ref[pl.ds(pl.multiple_of(zstart, MOE_SEG), MOE_CHUNK), :] = jnp.zeros((MOE_CHUNK, t), BF16)

    for e in range(N_EXPERTS):
        off = loc_ref[sb * N_EXPERTS + e]
        n = cnt_ref[sb * N_EXPERTS + e]
        row = key[e:e + 1, :]
        val = gt[e:e + 1, :] if weighted else 1.0

        def put(cidx, row=row, off=off, val=val):
            first = cidx * MOE_SEG
            firstf = float(first) if isinstance(first, int) else first.astype(F32)
            tile = jnp.where(row == j16 + firstf, val, 0.0).astype(BF16)
            p_ref[pl.ds(pl.multiple_of(off + first, MOE_SEG), MOE_SEG), :] = tile

        for cidx in range(MOE_STATIC_CHUNKS):
            put(cidx)

        def extra(cidx, carry, put=put):
            put(cidx)
            return carry
        lax.fori_loop(MOE_STATIC_CHUNKS, n // MOE_SEG, extra, 0)


def _dispatch_kernel(cnt_ref, loc_ref, *refs, nsb_p, n_sb):
    pieces = refs[:MOE_NPIECE_REFS]
    padoff_ref, padn_ref, hp_ref, hs_ref, gt_ref, xe_ref, p_ref, xg_ref, z_ref, sem = refs[MOE_NPIECE_REFS:]
    s = pl.program_id(0)
    slot = s % 2

    def copies(sb, slot_, act):
        def mk(off, g, b):
            return pltpu.make_async_copy(xg_ref.at[slot_, pl.ds(off, b)], xe_ref.at[pl.ds(g, b)], sem.at[slot_])
        _seg_copies(pieces, sb, mk, act)

    def wait_all(sb, slot_):
        _wait_rows(_sub_block_rows(cnt_ref, loc_ref, sb),
                   lambda b: pltpu.make_async_copy(xg_ref.at[slot_, pl.ds(0, b)], xe_ref.at[pl.ds(0, b)],
                                                   sem.at[slot_]))

    @pl.when(s >= 2)
    def _():
        wait_all(s - 2, slot)

    gt = gt_ref[...]
    _build_onehot(gt, p_ref, cnt_ref, loc_ref, s, weighted=False)
    xext = jnp.where(s < nsb_p, hp_ref[...], hs_ref[...])
    rows = _sub_block_rows(cnt_ref, loc_ref, s)
    for i in range(MOE_RPAD // MOE_CHUNK):
        @pl.when(i * MOE_CHUNK < rows)
        def _(i=i):
            xg_ref[slot, i * MOE_CHUNK:(i + 1) * MOE_CHUNK, :] = jnp.dot(
                p_ref[i * MOE_CHUNK:(i + 1) * MOE_CHUNK, :], xext, preferred_element_type=F32).astype(BF16)
    copies(s, slot, lambda cp: cp.start())

    @pl.when(s == n_sb - 1)
    def _():
        if n_sb >= 2:
            wait_all(s - 1, 1 - slot)
        wait_all(s, slot)
        z_ref[...] = jnp.zeros(z_ref.shape, BF16)

        def pads(act):
            def body(e, carry):
                n = padn_ref[e]
                off = padoff_ref[e]
                for b in MOE_DMA_SIZES:
                    @pl.when((n & b) != 0)
                    def _(b=b):
                        done = n & (-2 * b)
                        act(pltpu.make_async_copy(
                            z_ref.at[pl.ds(0, b)],
                            xe_ref.at[pl.ds(pl.multiple_of(off + done, MOE_SEG), b)], sem.at[0]))
                return carry
            lax.fori_loop(0, N_EXPERTS, body, 0)
        pads(lambda cp: cp.start())
        pads(lambda cp: cp.wait())


def _ffn_kernel(te_ref, tidx_ref, tcode_ref, xe_ref, wg_ref, wu_ref, wd_ref, ye_ref, wgub, wdb):
    i = pl.program_id(0)
    code = tcode_ref[i]

    @pl.when(code == 2)
    def _():
        wgub[:, :EXPERT_FF] = wg_ref[0].astype(BF16)
        wgub[:, EXPERT_FF:] = wu_ref[0].astype(BF16)
        wdb[...] = wd_ref[0].astype(BF16)

    @pl.when(code != 0)
    def _():
        ab = jnp.dot(xe_ref[...], wgub[...], preferred_element_type=F32)
        a = ab[:, :EXPERT_FF]
        act = a * jax.nn.sigmoid(a) * ab[:, EXPERT_FF:]
        ye_ref[...] = jnp.dot(act.astype(BF16), wdb[...], preferred_element_type=F32).astype(BF16)


def _combine_kernel(cnt_ref, loc_ref, *refs, sb0, n_steps):
    pieces = refs[:MOE_NPIECE_REFS]
    (ye_ref, gt_ref, h_ref, x1_ref, mod_ref, sg_ref, su_ref, sd_ref, fn_ref, o_ref,
     p_ref, ys_ref, acc_ref, sem) = refs[MOE_NPIECE_REFS:]
    i = pl.program_id(0)
    slot = i % 2
    sb = sb0 + i

    def copies(sb_, slot_, act):
        def mk(off, g, b):
            return pltpu.make_async_copy(ye_ref.at[pl.ds(g, b)], ys_ref.at[slot_, pl.ds(off, b)], sem.at[slot_])
        _seg_copies(pieces, sb_, mk, act)

    @pl.when(i == 0)
    def _():
        ys_ref[...] = jnp.zeros(ys_ref.shape, BF16)
        copies(sb, slot, lambda cp: cp.start())

    @pl.when(i + 1 < n_steps)
    def _():
        copies(sb + 1, 1 - slot, lambda cp: cp.start())

    acc_ref[...] = _swiglu_bf(h_ref[...], sg_ref[...], su_ref[...], sd_ref[...])
    _build_onehot(gt_ref[...], p_ref, cnt_ref, loc_ref, sb, weighted=True)
    rows = _sub_block_rows(cnt_ref, loc_ref, sb)
    _wait_rows(rows, lambda b: pltpu.make_async_copy(ye_ref.at[pl.ds(0, b)], ys_ref.at[slot, pl.ds(0, b)],
                                                     sem.at[slot]))
    for c in range(MOE_RPAD // MOE_CHUNK):
        @pl.when(c * MOE_CHUNK < rows)
        def _(c=c):
            acc_ref[...] += lax.dot_general(
                p_ref[c * MOE_CHUNK:(c + 1) * MOE_CHUNK, :], ys_ref[slot, c * MOE_CHUNK:(c + 1) * MOE_CHUNK, :],
                (((0,), (0,)), ((), ())), preferred_element_type=F32)
    gate2 = mod_ref[0][:, 5 * D_MODEL:6 * D_MODEL]
    x2 = x1_ref[...] + gate2 * acc_ref[...]
    ms = jnp.mean(x2 * x2, axis=-1, keepdims=True)
    o_ref[...] = x2 * lax.rsqrt(ms + NORM_EPS) * fn_ref[...]


def _moe_plan(gates_t):
    ne, n = gates_t.shape
    n_sb = n // MOE_SB
    cnt = jnp.sum((gates_t > 0.0).reshape(ne, n_sb, MOE_SB), axis=-1, dtype=jnp.int32).T
    cnt16 = (cnt + MOE_SEG - 1) // MOE_SEG * MOE_SEG
    loc = jnp.cumsum(cnt16, axis=1) - cnt16
    tot = jnp.sum(cnt16, axis=0)
    totp = (tot + MOE_TF - 1) // MOE_TF * MOE_TF
    ends = jnp.cumsum(totp)
    base = ends - totp
    goff = base[None, :] + jnp.cumsum(cnt16, axis=0) - cnt16
    rows_max = TOP_K * n + n_sb * ne * (MOE_SEG - 1) + ne * (MOE_TF - MOE_SEG)
    nt_max = -(-rows_max // MOE_TF)
    tiles = jnp.arange(nt_max, dtype=jnp.int32)
    used = ends[-1]
    valid = tiles * MOE_TF < used
    tidx = jnp.where(valid, tiles, jnp.maximum(used // MOE_TF - 1, 0))
    te = jnp.sum(ends[None, :] <= (tidx * MOE_TF)[:, None], axis=1, dtype=jnp.int32)
    te = jnp.minimum(te, ne - 1)
    first = jnp.concatenate([jnp.ones((1,), jnp.bool_), te[1:] != te[:-1]])
    tcode = jnp.where(valid, 1 + first.astype(jnp.int32), 0)

    def piece_list(npieces, first_row, width, rows):
        cum = jnp.cumsum(npieces, axis=1)
        k = jnp.arange(width, dtype=jnp.int32)
        ek = jnp.minimum(jnp.sum(cum[:, None, :] <= k[None, :, None], axis=2, dtype=jnp.int32), ne - 1)
        onehot = ek[:, :, None] == jnp.arange(ne, dtype=jnp.int32)[None, None, :]
        pick = lambda a: jnp.sum(jnp.where(onehot, a[:, None, :], 0), axis=2, dtype=jnp.int32)
        row = pick(first_row) + rows * (k[None, :] - pick(cum - npieces))
        packed = (pick(goff) + row) * MOE_PACK + pick(loc) + row
        return packed.reshape(-1).astype(jnp.int32), cum[:, -1].astype(jnp.int32)

    big = MOE_PIECE_ROWS[0]
    nbig = cnt16 // big
    pieces = list(piece_list(nbig, jnp.zeros_like(cnt16), MOE_PIECE_WIDTH[0], big))
    for width, rows in zip(MOE_PIECE_WIDTH[1:], MOE_PIECE_ROWS[1:]):
        pieces += list(piece_list((cnt16 % big == rows).astype(jnp.int32), nbig * big, width, rows))
    return dict(cnt=cnt16.reshape(-1), loc=loc.reshape(-1).astype(jnp.int32), pieces=tuple(pieces),
                padoff=(base + tot).astype(jnp.int32), padn=(totp - tot).astype(jnp.int32),
                te=te, tidx=tidx.astype(jnp.int32), tcode=tcode, nt_max=nt_max)


def _dispatch_call(plan, h2_p, h2_s, gates_t):
    nsb_p = h2_p.shape[0] // MOE_SB
    n_sb = gates_t.shape[1] // MOE_SB
    grid_spec = pltpu.PrefetchScalarGridSpec(
        num_scalar_prefetch=4 + MOE_NPIECE_REFS, grid=(n_sb,),
        in_specs=[pl.BlockSpec((MOE_SB, D_MODEL), lambda s, *_: (jnp.minimum(s, nsb_p - 1), 0)),
                  pl.BlockSpec((MOE_SB, D_MODEL), lambda s, *_: (jnp.maximum(s - nsb_p, 0), 0)),
                  pl.BlockSpec((N_EXPERTS, MOE_SB), lambda s, *_: (0, s))],
        out_specs=pl.BlockSpec(memory_space=pl.ANY),
        scratch_shapes=[pltpu.VMEM((MOE_RPAD, MOE_SB), BF16),
                        pltpu.VMEM((2, MOE_RPAD, MOE_XW), BF16),
                        pltpu.VMEM((MOE_DMA_SIZES[0], MOE_XW), BF16),
                        pltpu.SemaphoreType.DMA((2,))])
    return pl.pallas_call(
        functools.partial(_dispatch_kernel, nsb_p=nsb_p, n_sb=n_sb),
        grid_spec=grid_spec,
        out_shape=jax.ShapeDtypeStruct((plan['nt_max'] * MOE_TF, MOE_XW), BF16),
        compiler_params=_cparams(("arbitrary",)),
        name="moe_dispatch",
    )(plan['cnt'], plan['loc'], *plan['pieces'], plan['padoff'], plan['padn'], h2_p, h2_s, gates_t)


def _ffn_call(plan, xe, wg, wu, wd):
    grid_spec = pltpu.PrefetchScalarGridSpec(
        num_scalar_prefetch=3, grid=(plan['nt_max'],),
        in_specs=[pl.BlockSpec((MOE_TF, MOE_XW), lambda i, te, tidx, tv: (tidx[i], 0)),
                  pl.BlockSpec((1, D_MODEL, EXPERT_FF), lambda i, te, tidx, tv: (te[i], 0, 0)),
                  pl.BlockSpec((1, D_MODEL, EXPERT_FF), lambda i, te, tidx, tv: (te[i], 0, 0)),
                  pl.BlockSpec((1, EXPERT_FF, D_MODEL), lambda i, te, tidx, tv: (te[i], 0, 0))],
        out_specs=pl.BlockSpec((MOE_TF, D_MODEL), lambda i, te, tidx, tv: (tidx[i], 0)),
        scratch_shapes=[pltpu.VMEM((D_MODEL, 2 * EXPERT_FF), BF16),
                        pltpu.VMEM((EXPERT_FF, D_MODEL), BF16)])
    return pl.pallas_call(
        _ffn_kernel,
        grid_spec=grid_spec,
        out_shape=jax.ShapeDtypeStruct((xe.shape[0], D_MODEL), BF16),
        compiler_params=_cparams(("arbitrary",)),
        name="moe_ffn",
    )(plan['te'], plan['tidx'], plan['tcode'], xe, wg, wu, wd)


def _combine_call(plan, ye, gates_t, h2, x1, mod3, sg, su, sd, final_w, sb0, seq_len):
    n = h2.shape[0]
    n_steps = n // MOE_SB
    bpb = seq_len // MOE_SB
    row = lambda i, *_: (i, 0)
    const2 = lambda i, *_: (0, 0)
    grid_spec = pltpu.PrefetchScalarGridSpec(
        num_scalar_prefetch=2 + MOE_NPIECE_REFS, grid=(n_steps,),
        in_specs=[pl.BlockSpec(memory_space=pl.ANY),
                  pl.BlockSpec((N_EXPERTS, MOE_SB), lambda i, *_: (0, sb0 + i)),
                  pl.BlockSpec((MOE_SB, D_MODEL), row),
                  pl.BlockSpec((MOE_SB, D_MODEL), row),
                  pl.BlockSpec((1, 1, 6 * D_MODEL), lambda i, *_: (i // bpb, 0, 0)),
                  pl.BlockSpec((D_MODEL, EXPERT_FF), const2),
                  pl.BlockSpec((D_MODEL, EXPERT_FF), const2),
                  pl.BlockSpec((EXPERT_FF, D_MODEL), const2),
                  pl.BlockSpec((1, D_MODEL), const2)],
        out_specs=pl.BlockSpec((MOE_SB, D_MODEL), row),
        scratch_shapes=[pltpu.VMEM((MOE_RPAD, MOE_SB), BF16),
                        pltpu.VMEM((2, MOE_RPAD, D_MODEL), BF16),
                        pltpu.VMEM((MOE_SB, D_MODEL), F32),
                        pltpu.SemaphoreType.DMA((2,))])
    return pl.pallas_call(
        functools.partial(_combine_kernel, sb0=sb0, n_steps=n_steps),
        grid_spec=grid_spec,
        out_shape=jax.ShapeDtypeStruct((n, D_MODEL), F32),
        compiler_params=_cparams(("arbitrary",)),
        name="moe_combine",
    )(plan['cnt'], plan['loc'], *plan['pieces'], ye, gates_t, h2, x1, mod3, sg, su, sd,
      final_w.reshape(1, D_MODEL))


def _moe_sparse(h2_p, h2_s, gt_p, gt_s, x1_p, x1_s, mod_p, mod_s, wg, wu, wd, sg, su, sd, final_w, dseq):
    gates_t = jnp.concatenate([gt_p, gt_s], axis=1)
    plan = _moe_plan(gates_t)
    xe = _dispatch_call(plan, h2_p, h2_s, gates_t)
    ye = _ffn_call(plan, xe, wg, wu, wd)
    n_p = h2_p.shape[0]
    y_p = _combine_call(plan, ye, gates_t, h2_p, x1_p, mod_p, sg, su, sd, final_w, 0, n_p)
    y_s = _combine_call(plan, ye, gates_t, h2_s, x1_s, mod_s, sg, su, sd, final_w, n_p // MOE_SB, dseq)
    return y_p, y_s


def _rope_tables(n_tokens):
    rows = n_tokens // GRID_W
    row = jnp.repeat(jnp.arange(rows, dtype=F32), GRID_W)
    col = jnp.tile(jnp.arange(GRID_W, dtype=F32), rows)
    freqs = ROPE_THETA ** (-jnp.arange(ROT_PAIRS, dtype=F32) / ROT_PAIRS)
    ar = row[:, None] * freqs
    ac = col[:, None] * freqs
    cos = jnp.concatenate([jnp.cos(ar), jnp.cos(ar), jnp.cos(ac), jnp.cos(ac)], axis=1)
    sin = jnp.concatenate([-jnp.sin(ar), jnp.sin(ar), -jnp.sin(ac), jnp.sin(ac)], axis=1)
    return jnp.tile(cos, (1, 2)), jnp.tile(sin, (1, 2))


def kernel(x_prompt, x_sample, c, cache_k, cache_v, state_ssm_re, state_ssm_im, c_ctx, w_ada, b_ada, norm1_w, w_in, ssm_lambda_re, ssm_lambda_im, ssm_log_dt, ssm_b_re, ssm_b_im, ssm_c_re, ssm_c_im, ssm_d, ssm_w_glu, diff_lambda_q, diff_lambda_k, diff_subln_w, w_out, norm2_w, w_router, router_bias, w_exp_gate, w_exp_up, w_exp_down, w_sh_gate, w_sh_up, w_sh_down, final_norm_w):
    depth = w_ada.shape[0]
    assert depth == 1
    l = 0
    lambda_init = 0.8 - 0.6 * math.exp(-0.3 * l)
    bsz, seq, _ = x_prompt.shape
    dbs, dseq, _ = x_sample.shape
    n_p, n_s = bsz * seq, dbs * dseq

    cond8 = jnp.zeros((SUBLANES, D_MODEL), F32).at[:dbs].set(c).at[dbs].set(c_ctx)
    mod = _ada_call(cond8, w_ada[l], b_ada[l])
    mod_s = mod[:dbs].reshape(dbs, 1, 6 * D_MODEL)
    mod_p = mod[dbs:dbs + 1].reshape(1, 1, 6 * D_MODEL)

    w_in_bf = w_in[l].astype(BF16)
    xp2 = x_prompt.reshape(n_p, D_MODEL)
    xs2 = x_sample.reshape(n_s, D_MODEL)
    u_p, q_p, k_p, v_p = _inproj_call(xp2, mod_p, norm1_w[l], w_in_bf, None, n_p, F32)
    u_s, q_s, k_s, v_s = _inproj_call(xs2, mod_s, norm1_w[l], w_in_bf, _rope_tables(dseq), dseq, BF16)

    bblk, cblk, lam_r, lam_i = _s5_weights(ssm_lambda_re[l], ssm_lambda_im[l], ssm_log_dt[l],
                                           ssm_b_re[l], ssm_b_im[l], ssm_c_re[l], ssm_c_im[l])
    lr_p = jnp.broadcast_to(lam_r[:, :, None, :], (2, N_COLBLK, SUBLANES, S5_BLK_STATES))
    li_p = jnp.broadcast_to(lam_i[:, :, None, :], (2, N_COLBLK, SUBLANES, S5_BLK_STATES))
    g_p = bsz // SUBLANES
    h0_p = jnp.zeros((2, g_p, N_COLBLK * 8, SUBLANES, LANES), F32)
    y_p, hfin = _s5_call(u_p.reshape(bsz, seq, S5_WIDTH), bblk, cblk, lr_p, li_p, h0_p, SUBLANES, tb=128)

    def halves(x):
        x = x.reshape(2, 2, 2, 1, S5_BLK_STATES)
        x = jnp.broadcast_to(x, (2, 2, 2, dbs, S5_BLK_STATES))
        return x.transpose(0, 2, 1, 3, 4).reshape(2, 2, 2 * dbs, S5_BLK_STATES)

    def h0_tiles(s):
        s = s.astype(F32).reshape(dbs, 2, 2, 2, 4, LANES)
        return s.transpose(1, 3, 4, 2, 0, 5).reshape(2, 1, 2, 4, 2 * dbs, LANES)

    h0_s = jnp.concatenate([h0_tiles(state_ssm_re[:, l]), h0_tiles(state_ssm_im[:, l])], axis=3)
    h0_s = h0_s.reshape(2, 1, 16, SUBLANES, LANES)
    y_s, _ = _s5_call(u_s.reshape(dbs, dseq, S5_WIDTH), bblk, cblk, halves(lam_r), halves(lam_i), h0_s, dbs,
                      tb=256)

    lq = diff_lambda_q[l].astype(F32)
    lk = diff_lambda_k[l].astype(F32)
    lam = (jnp.exp(jnp.sum(lq[0] * lk[0])) - jnp.exp(jnp.sum(lq[1] * lk[1])) + lambda_init).reshape(1)
    subln = diff_subln_w[l].astype(F32).reshape(1, VALUE_DIM)
    ao_p = _attn_ctx_call(lam, q_p.reshape(bsz, seq, ATTN_WIDTH), k_p.reshape(bsz, seq, ATTN_WIDTH),
                          v_p.reshape(bsz, seq, ATTN_WIDTH), subln, lambda_init)
    past = cache_k.shape[2]
    ao_s = _attn_lat_call(lam, q_s.reshape(dbs, dseq, ATTN_WIDTH), k_s.reshape(dbs, dseq, ATTN_WIDTH),
                          v_s.reshape(dbs, dseq, ATTN_WIDTH),
                          cache_k[:, l].reshape(dbs, past, ATTN_WIDTH).astype(BF16),
                          cache_v[:, l].reshape(dbs, past, ATTN_WIDTH).astype(BF16), subln, lambda_init)

    wglu_bf = ssm_w_glu[l].astype(BF16)
    wout_bf = w_out[l].astype(BF16)
    wr_t = w_router[l].astype(F32).T
    rbias3 = router_bias[l].astype(F32).reshape(N_EXPERT_GROUPS, N_EXPERTS // N_EXPERT_GROUPS, 1)
    x1_p, h2_p, gates_p = _post_call(xp2, u_p, y_p, ao_p.reshape(n_p, ATTN_WIDTH), mod_p, ssm_d[l],
                                     wglu_bf, wout_bf, norm2_w[l], wr_t, rbias3, n_p)
    x1_s, h2_s, gates_s = _post_call(xs2, u_s, y_s, ao_s.reshape(n_s, ATTN_WIDTH), mod_s, ssm_d[l],
                                     wglu_bf, wout_bf, norm2_w[l], wr_t, rbias3, dseq)

    wg = w_exp_gate[l]
    wu = w_exp_up[l]
    wd = w_exp_down[l]
    sg = w_sh_gate[l].astype(BF16)
    su = w_sh_up[l].astype(BF16)
    sd = w_sh_down[l].astype(BF16)
    y_prompt, y_sample = _moe_sparse(h2_p, h2_s, gates_p, gates_s, x1_p, x1_s, mod_p, mod_s,
                                     wg, wu, wd, sg, su, sd, final_norm_w, dseq)

    new_cache_k = k_p.reshape(bsz, 1, seq, N_DIFF_HEADS, VALUE_DIM)
    new_cache_v = v_p.reshape(bsz, 1, seq, N_DIFF_HEADS, VALUE_DIM)
    hf = hfin.reshape(2, g_p, N_COLBLK, 2, 4, SUBLANES, LANES)
    hf = hf.transpose(3, 1, 5, 0, 2, 4, 6).reshape(2, bsz, 1, 2, S5_GROUPS, S5_STATE)
    return (y_prompt.reshape(bsz, seq, D_MODEL), y_sample.reshape(dbs, dseq, D_MODEL),
            new_cache_k, new_cache_v, hf[0], hf[1])
```

```python
import functools
import math

import jax
import jax.numpy as jnp
from jax import lax
from jax.experimental import pallas as pl
from jax.experimental.pallas import tpu as pltpu

F32 = jnp.float32
BF16 = jnp.bfloat16

D_MODEL = 1024
GRID_W = 64
S5_WIDTH = 512
S5_GROUP = 16
S5_GROUPS = 32
S5_STATE = 64
ATTN_WIDTH = 512
DIFF_HEAD_DIM = 64
VALUE_DIM = 128
N_DIFF_HEADS = 4
IN_WIDTH = S5_WIDTH + 3 * ATTN_WIDTH
ROT_PAIRS = DIFF_HEAD_DIM // 4
ROPE_THETA = 10000.0
N_EXPERTS = 64
TOP_K = 8
N_EXPERT_GROUPS = 8
TOPK_GROUPS = 4
EXPERT_FF = 256
ROUTED_SCALE = 2.5
NORM_EPS = 1e-6

LANES = 128
SUBLANES = 8
S5_COLBLK = 8 * S5_GROUP
S5_BLK_STATES = 8 * S5_STATE
N_COLBLK = S5_WIDTH // S5_COLBLK
VMEM_LIMIT = 56 * 1024 * 1024


def _cparams(sem):
    return pltpu.CompilerParams(dimension_semantics=sem, vmem_limit_bytes=VMEM_LIMIT)


def _ada_kernel(cond_ref, w_ref, b_ref, o_ref):
    c = cond_ref[...]
    s = c * jax.nn.sigmoid(c)
    o_ref[...] = jnp.dot(s.astype(BF16), w_ref[...].astype(BF16),
                         preferred_element_type=F32) + b_ref[...]


def _ada_call(cond8, w_ada, b_ada):
    n = w_ada.shape[1]
    tn = 1536
    return pl.pallas_call(
        _ada_kernel,
        grid=(n // tn,),
        in_specs=[pl.BlockSpec((SUBLANES, D_MODEL), lambda j: (0, 0)),
                  pl.BlockSpec((D_MODEL, tn), lambda j: (0, j)),
                  pl.BlockSpec((1, tn), lambda j: (0, j))],
        out_specs=pl.BlockSpec((SUBLANES, tn), lambda j: (0, j)),
        out_shape=jax.ShapeDtypeStruct((SUBLANES, n), F32),
        compiler_params=_cparams(("arbitrary",)),
        name="adaln",
    )(cond8, w_ada, b_ada.reshape(1, n))


def _rope_apply(t, cos, sin):
    parts = []
    for cidx in range(ATTN_WIDTH // LANES):
        xc = t[:, cidx * LANES:(cidx + 1) * LANES]
        up = pltpu.roll(xc, LANES - ROT_PAIRS, 1)
        dn = pltpu.roll(xc, ROT_PAIRS, 1)
        lane = lax.broadcasted_iota(jnp.int32, xc.shape, 1)
        partner = jnp.where((lane % (2 * ROT_PAIRS)) < ROT_PAIRS, up, dn)
        parts.append(xc * cos + partner * sin)
    return jnp.concatenate(parts, axis=1)


def _inproj_kernel(*refs, rope, kv_dtype):
    if rope:
        x_ref, mod_ref, n1_ref, w_ref, cos_ref, sin_ref, u_ref, q_ref, k_ref, v_ref = refs
    else:
        x_ref, mod_ref, n1_ref, w_ref, u_ref, q_ref, k_ref, v_ref = refs
    x = x_ref[...]
    ms = jnp.mean(x * x, axis=-1, keepdims=True)
    xn = x * lax.rsqrt(ms + NORM_EPS) * n1_ref[...]
    mod = mod_ref[0]
    shift = mod[:, 0:D_MODEL]
    scale = mod[:, D_MODEL:2 * D_MODEL]
    h = xn * (1.0 + scale) + shift
    proj = jnp.dot(h.astype(BF16), w_ref[...], preferred_element_type=F32)
    u_ref[...] = proj[:, :S5_WIDTH]
    q = proj[:, S5_WIDTH:S5_WIDTH + ATTN_WIDTH]
    k = proj[:, S5_WIDTH + ATTN_WIDTH:S5_WIDTH + 2 * ATTN_WIDTH]
    v = proj[:, S5_WIDTH + 2 * ATTN_WIDTH:]
    if rope:
        cos = cos_ref[...]
        sin = sin_ref[...]
        q = _rope_apply(q, cos, sin)
        k = _rope_apply(k, cos, sin)
    q_ref[...] = (q * (DIFF_HEAD_DIM ** -0.5)).astype(BF16)
    k_ref[...] = k.astype(kv_dtype)
    v_ref[...] = v.astype(kv_dtype)


def _inproj_call(x2, mod3, norm1_w, w_in_bf, rope_tabs, seq_len, kv_dtype, tm=1024):
    n = x2.shape[0]
    bpb = seq_len // tm
    rope = rope_tabs is not None
    in_specs = [pl.BlockSpec((tm, D_MODEL), lambda i: (i, 0)),
                pl.BlockSpec((1, 1, 6 * D_MODEL), lambda i: (i // bpb, 0, 0)),
                pl.BlockSpec((1, D_MODEL), lambda i: (0, 0)),
                pl.BlockSpec((D_MODEL, IN_WIDTH), lambda i: (0, 0))]
    args = [x2, mod3, norm1_w.reshape(1, D_MODEL), w_in_bf]
    if rope:
        in_specs += [pl.BlockSpec((tm, LANES), lambda i: (i % bpb, 0)),
                     pl.BlockSpec((tm, LANES), lambda i: (i % bpb, 0))]
        args += list(rope_tabs)
    ospec = pl.BlockSpec((tm, S5_WIDTH), lambda i: (i, 0))
    return pl.pallas_call(
        functools.partial(_inproj_kernel, rope=rope, kv_dtype=kv_dtype),
        grid=(n // tm,),
        in_specs=in_specs,
        out_specs=[ospec, ospec, ospec, ospec],
        out_shape=[jax.ShapeDtypeStruct((n, S5_WIDTH), F32),
                   jax.ShapeDtypeStruct((n, ATTN_WIDTH), BF16),
                   jax.ShapeDtypeStruct((n, ATTN_WIDTH), kv_dtype),
                   jax.ShapeDtypeStruct((n, ATTN_WIDTH), kv_dtype)],
        compiler_params=_cparams(("parallel",)),
        name="inproj_rope" if rope else "inproj",
    )(*args)


def _s5_slot(b, q, nseq):
    if nseq == SUBLANES:
        return b, q * 8
    half = q // 2
    return half * nseq + b, (q % 2) * 8


def _s5_kernel(u_ref, b_ref, c_ref, lr_ref, li_ref, h0_ref, y_ref, hfin_ref, buh, hs, hst,
               *, nseq, tb, stride, nqq):
    d = pl.program_id(1)
    c = pl.program_id(2)
    nc = pl.num_programs(2)

    @pl.when(c == 0)
    def _():
        hst[...] = h0_ref[0, 0]
        hs[...] = jnp.zeros(hs.shape, F32)

    u = u_ref[...].reshape(nseq * tb, S5_WIDTH).astype(BF16)
    for q in range(N_COLBLK):
        res = jnp.dot(u[:, q * S5_COLBLK:(q + 1) * S5_COLBLK], b_ref[0, q],
                      preferred_element_type=F32)
        for b in range(nseq):
            slot, ct0 = _s5_slot(b, q, nseq)
            for j in range(8):
                buh[ct0 + j, slot * stride:slot * stride + tb, :] = (
                    res[b * tb:(b + 1) * tb, j * LANES:(j + 1) * LANES])

    for qq in range(nqq):
        lr = [lr_ref[0, qq, :, i * LANES:(i + 1) * LANES] for i in range(4)]
        li = [li_ref[0, qq, :, i * LANES:(i + 1) * LANES] for i in range(4)]
        h_init = tuple(hst[qq * 8 + i] for i in range(8))

        def body(s, carry, qq=qq, lr=lr, li=li):
            t = jnp.where(d == 0, s, tb - 1 - s)
            idx = pl.ds(t, SUBLANES, stride=stride)
            new_r, new_i = [], []
            for i in range(4):
                hr, hi = carry[i], carry[4 + i]
                bur = buh[qq * 8 + i, idx, :]
                bui = buh[qq * 8 + 4 + i, idx, :]
                nr = lr[i] * hr - li[i] * hi + bur
                ni = lr[i] * hi + li[i] * hr + bui
                hs[qq * 8 + i, idx, :] = nr
                hs[qq * 8 + 4 + i, idx, :] = ni
                new_r.append(nr)
                new_i.append(ni)
            return tuple(new_r) + tuple(new_i)

        h_fin = lax.fori_loop(0, tb, body, h_init, unroll=8)
        for i in range(8):
            hst[qq * 8 + i] = h_fin[i]

    rows = nseq * stride
    for q in range(N_COLBLK):
        slot0, ct0 = _s5_slot(0, q, nseq)
        hmat = jnp.concatenate(
            [hs[ct0 + j, slot0 * stride:slot0 * stride + rows, :] for j in range(8)], axis=1)
        yq = jnp.dot(hmat.astype(BF16), c_ref[0, q], preferred_element_type=F32)
        for b in range(nseq):
            y_ref[0, b, :, q * S5_COLBLK:(q + 1) * S5_COLBLK] = yq[b * stride:b * stride + tb, :]

    @pl.when(c == nc - 1)
    def _():
        hfin_ref[0, 0] = hst[...]


def _s5_call(u3, bblk, cblk, lr, li, h0, nseq, tb=64):
    nseq_total, seq_len, _ = u3.shape
    g = nseq_total // nseq
    nc = seq_len // tb
    stride = tb + SUBLANES // 2
    nqq = lr.shape[1]
    nct = nqq * 8

    def tmap(gi, d, c):
        return c + d * (nc - 1 - 2 * c)

    return pl.pallas_call(
        functools.partial(_s5_kernel, nseq=nseq, tb=tb, stride=stride, nqq=nqq),
        grid=(g, 2, nc),
        in_specs=[
            pl.BlockSpec((nseq, tb, S5_WIDTH), lambda gi, d, c: (gi, tmap(gi, d, c), 0)),
            pl.BlockSpec((1, N_COLBLK, S5_COLBLK, 2 * S5_BLK_STATES), lambda gi, d, c: (d, 0, 0, 0)),
            pl.BlockSpec((1, N_COLBLK, 2 * S5_BLK_STATES, S5_COLBLK), lambda gi, d, c: (d, 0, 0, 0)),
            pl.BlockSpec((1, nqq, SUBLANES, S5_BLK_STATES), lambda gi, d, c: (d, 0, 0, 0)),
            pl.BlockSpec((1, nqq, SUBLANES, S5_BLK_STATES), lambda gi, d, c: (d, 0, 0, 0)),
            pl.BlockSpec((1, 1, nct, SUBLANES, LANES), lambda gi, d, c: (d, gi, 0, 0, 0)),
        ],
        out_specs=[
            pl.BlockSpec((1, nseq, tb, S5_WIDTH), lambda gi, d, c: (d, gi, tmap(gi, d, c), 0)),
            pl.BlockSpec((1, 1, nct, SUBLANES, LANES), lambda gi, d, c: (d, gi, 0, 0, 0)),
        ],
        out_shape=[jax.ShapeDtypeStruct((2, nseq_total, seq_len, S5_WIDTH), F32),
                   jax.ShapeDtypeStruct((2, g, nct, SUBLANES, LANES), F32)],
        scratch_shapes=[pltpu.VMEM((nct, SUBLANES * stride, LANES), F32),
                        pltpu.VMEM((nct, SUBLANES * stride, LANES), F32),
                        pltpu.VMEM((nct, SUBLANES, LANES), F32)],
        compiler_params=_cparams(("parallel", "parallel", "arbitrary")),
        name="s5_scan_%d" % nseq,
    )(u3, bblk, cblk, lr, li, h0)


def _s5_weights(lam_re, lam_im, log_dt, b_re, b_im, c_re, c_im):
    dt = jnp.exp(log_dt.astype(F32))[..., None]
    lr = lam_re.astype(F32)
    li = lam_im.astype(F32)
    mag = jnp.exp(lr * dt)
    lbr = mag * jnp.cos(li * dt)
    lbi = mag * jnp.sin(li * dt)
    a = lbr - 1.0
    den = lr * lr + li * li
    cr = (a * lr + lbi * li) / den
    ci = (lbi * lr - a * li) / den
    bbr = cr[..., None] * b_re - ci[..., None] * b_im
    bbi = cr[..., None] * b_im + ci[..., None] * b_re
    eye = jnp.eye(8, dtype=F32)

    def pack_b(x):
        x = x.reshape(2, N_COLBLK, 8, S5_STATE, S5_GROUP).transpose(0, 1, 2, 4, 3)
        x = jnp.einsum('dqghn,gk->dqghkn', x, eye)
        return x.reshape(2, N_COLBLK, S5_COLBLK, S5_BLK_STATES)

    def pack_c(x):
        x = x.reshape(2, N_COLBLK, 8, S5_GROUP, S5_STATE).transpose(0, 1, 2, 4, 3)
        x = jnp.einsum('dqgnh,gk->dqgnkh', x, eye)
        return x.reshape(2, N_COLBLK, S5_BLK_STATES, S5_COLBLK)

    bblk = jnp.concatenate([pack_b(bbr), pack_b(bbi)], axis=-1).astype(BF16)
    cblk = jnp.concatenate([pack_c(c_re.astype(F32)), pack_c(-c_im.astype(F32))], axis=-2).astype(BF16)
    lam_r = lbr.reshape(2, N_COLBLK, S5_BLK_STATES)
    lam_i = lbi.reshape(2, N_COLBLK, S5_BLK_STATES)
    return bblk, cblk, lam_r, lam_i


ATTN_KC = 512


def _lane_fold(x, op):
    acc = x[:, 0:LANES]
    for j in range(1, x.shape[1] // LANES):
        acc = op(acc, x[:, j * LANES:(j + 1) * LANES])
    return acc


def _attn_ops(q, chunks, s_ref):
    lane = lax.broadcasted_iota(jnp.int32, q.shape, 1)
    dn = (((1,), (1,)), ((), ()))
    qms = [jnp.where(lane < DIFF_HEAD_DIM, q, jnp.zeros_like(q)),
           jnp.where(lane >= DIFF_HEAD_DIM, q, jnp.zeros_like(q))]
    offs = [0]
    for load_k, _ in chunks:
        offs.append(offs[-1] + load_k().shape[0])

    def score_chunk(m, ci, m128):
        s = lax.dot_general(qms[m], chunks[ci][0](), dn, preferred_element_type=F32)
        s_ref[m, :, offs[ci]:offs[ci + 1]] = s
        f = _lane_fold(s, jnp.maximum)
        return f if m128 is None else jnp.maximum(m128, f)

    def value_chunk(m, ci, mx, acc):
        p = jnp.exp(s_ref[m, :, offs[ci]:offs[ci + 1]] - mx)
        v = chunks[ci][1]()
        t = jnp.dot(p.astype(BF16), jnp.concatenate([v, jnp.ones_like(v)], axis=1), preferred_element_type=F32)
        return t if acc is None else acc + t

    return score_chunk, value_chunk


def _row_max(m128):
    return m128.max(axis=-1, keepdims=True)


def _normalized(acc):
    return acc[:, :VALUE_DIM] * (1.0 / acc[:, VALUE_DIM:VALUE_DIM + 1])


def _sub_ln(o, subln, lambda_init):
    ms = jnp.mean(o * o, axis=-1, keepdims=True)
    return o * lax.rsqrt(ms + NORM_EPS) * subln * (1.0 - lambda_init)


def _attn_ctx_kernel(lam_ref, q_ref, k_ref, v_ref, w_ref, o_ref, s_ref, *, lambda_init):
    for h in range(N_DIFF_HEADS):
        cols = slice(h * VALUE_DIM, (h + 1) * VALUE_DIM)
        chunks = [(lambda cols=cols: k_ref[0, :, cols].astype(BF16), lambda cols=cols: v_ref[0, :, cols].astype(BF16))]
        score_chunk, value_chunk = _attn_ops(q_ref[0, :, cols], chunks, s_ref.at[h])
        mx0 = _row_max(score_chunk(0, 0, None))
        m1 = score_chunk(1, 0, None)
        a0 = value_chunk(0, 0, mx0, None)
        a1 = value_chunk(1, 0, _row_max(m1), None)
        o = _normalized(a0) - lam_ref[0] * _normalized(a1)
        o_ref[0, :, cols] = _sub_ln(o, w_ref[...], lambda_init).astype(BF16)


def _attn_lat_kernel(lam_ref, q_ref, k_ref, v_ref, ck_ref, cv_ref, w_ref, o_ref, s_ref, *, lambda_init):
    chunks = []
    for kr, vr in ((k_ref, v_ref), (ck_ref, cv_ref)):
        kc = min(ATTN_KC, kr.shape[1])
        for c in range(kr.shape[1] // kc):
            chunks.append((lambda c=c, kr=kr, kc=kc: kr[0, c * kc:(c + 1) * kc, :],
                           lambda c=c, vr=vr, kc=kc: vr[0, c * kc:(c + 1) * kc, :]))
    nchunk = len(chunks)
    score_chunk, value_chunk = _attn_ops(q_ref[0], chunks, s_ref)

    m0 = m1 = a0 = a1 = None
    for ci in range(nchunk):
        m0 = score_chunk(0, ci, m0)
    mx0 = _row_max(m0)
    for ci in range(nchunk):
        m1 = score_chunk(1, ci, m1)
        a0 = value_chunk(0, ci, mx0, a0)
    mx1 = _row_max(m1)
    for ci in range(nchunk):
        a1 = value_chunk(1, ci, mx1, a1)
    o = _normalized(a0) - lam_ref[0] * _normalized(a1)
    o_ref[0] = _sub_ln(o, w_ref[...], lambda_init).astype(BF16)


def _attn_ctx_call(lam, q3, k3, v3, subln, lambda_init):
    bsz, seq_len, _ = q3.shape
    spec = pl.BlockSpec((1, seq_len, ATTN_WIDTH), lambda b: (b, 0, 0))
    return pl.pallas_call(
        functools.partial(_attn_ctx_kernel, lambda_init=lambda_init),
        grid=(bsz,),
        in_specs=[pl.BlockSpec(memory_space=pltpu.SMEM), spec, spec, spec,
                  pl.BlockSpec((1, VALUE_DIM), lambda b: (0, 0))],
        out_specs=spec,
        out_shape=jax.ShapeDtypeStruct((bsz, seq_len, ATTN_WIDTH), BF16),
        scratch_shapes=[pltpu.VMEM((N_DIFF_HEADS, 2, seq_len, seq_len), F32)],
        compiler_params=_cparams(("parallel",)),
        name="attn_ctx",
    )(lam, q3, k3, v3, subln)


def _attn_lat_call(lam, q3, k3, v3, ck3, cv3, subln, lambda_init, tq=1024):
    bsz, seq_len, _ = q3.shape
    ctx_len = ck3.shape[1]
    qspec = pl.BlockSpec((1, tq, VALUE_DIM), lambda b, h, i: (b, i, h))
    kspec = pl.BlockSpec((1, seq_len, VALUE_DIM), lambda b, h, i: (b, 0, h))
    cspec = pl.BlockSpec((1, ctx_len, VALUE_DIM), lambda b, h, i: (b, 0, h))
    return pl.pallas_call(
        functools.partial(_attn_lat_kernel, lambda_init=lambda_init),
        grid=(bsz, N_DIFF_HEADS, seq_len // tq),
        in_specs=[pl.BlockSpec(memory_space=pltpu.SMEM), qspec, kspec, kspec, cspec, cspec,
                  pl.BlockSpec((1, VALUE_DIM), lambda b, h, i: (0, 0))],
        out_specs=qspec,
        out_shape=jax.ShapeDtypeStruct((bsz, seq_len, ATTN_WIDTH), BF16),
        scratch_shapes=[pltpu.VMEM((2, tq, seq_len + ctx_len), F32)],
        compiler_params=_cparams(("parallel", "parallel", "arbitrary")),
        name="attn_lat",
    )(lam, q3, k3, v3, ck3, cv3, subln)


def _route(logits_t, bias3, tm):
    ng, ge = N_EXPERT_GROUPS, N_EXPERTS // N_EXPERT_GROUPS
    neg = jnp.float32(-jnp.inf)
    sc = jax.nn.sigmoid(logits_t).reshape(ng, ge, tm)
    bi = sc + bias3
    eio = lax.broadcasted_iota(jnp.int32, (ng, ge, tm), 1).astype(F32)
    gio = lax.broadcasted_iota(jnp.int32, (ng, ge, tm), 0).astype(F32)
    m1 = bi.max(axis=1, keepdims=True)
    i1 = jnp.where(bi == m1, eio, float(ge)).min(axis=1, keepdims=True)
    m2 = jnp.where(eio == i1, neg, bi).max(axis=1, keepdims=True)
    gs = jnp.broadcast_to(m1 + m2, (ng, ge, tm))
    cnt = jnp.zeros((ng, ge, tm), F32)
    for g2 in range(ng):
        o = gs[g2:g2 + 1]
        better = (o > gs) | ((o == gs) & (gio > float(g2)))
        cnt = cnt + jnp.where(better, 1.0, 0.0)
    v = jnp.where(cnt < float(TOPK_GROUPS), bi, neg)
    eidx = gio * float(ge) + eio
    selm = jnp.zeros((ng, ge, tm), F32)
    for _ in range(TOP_K):
        m = v.max(axis=0, keepdims=True).max(axis=1, keepdims=True)
        ix = jnp.where(v == m, eidx, float(N_EXPERTS)).min(axis=0, keepdims=True).min(axis=1, keepdims=True)
        oh = eidx == ix
        selm = jnp.where(oh, 1.0, selm)
        v = jnp.where(oh, neg, v)
    selsc = selm * sc
    ssum = selsc.sum(axis=0, keepdims=True).sum(axis=1, keepdims=True)
    return (selsc / ssum * ROUTED_SCALE).reshape(N_EXPERTS, tm)


def _post_kernel(x_ref, u_ref, yf_ref, yb_ref, ao_ref, mod_ref, d_ref, wglu_ref, wout_ref, n2_ref,
                 wrt_ref, rb_ref, x1_ref, h2_ref, gates_ref, *, tm):
    u = u_ref[...]
    y = u * d_ref[...] + yf_ref[0] + yb_ref[0]
    g = jax.nn.gelu(y)
    s5 = g * jax.nn.sigmoid(jnp.dot(g.astype(BF16), wglu_ref[...], preferred_element_type=F32))
    mixed = (jnp.dot(s5.astype(BF16), wout_ref[0:S5_WIDTH, :], preferred_element_type=F32)
             + jnp.dot(ao_ref[...], wout_ref[S5_WIDTH:, :], preferred_element_type=F32))
    mod = mod_ref[0]
    gate1 = mod[:, 2 * D_MODEL:3 * D_MODEL]
    shift2 = mod[:, 3 * D_MODEL:4 * D_MODEL]
    scale2 = mod[:, 4 * D_MODEL:5 * D_MODEL]
    x1 = x_ref[...] + gate1 * mixed
    x1_ref[...] = x1
    ms = jnp.mean(x1 * x1, axis=-1, keepdims=True)
    h2 = x1 * lax.rsqrt(ms + NORM_EPS) * n2_ref[...] * (1.0 + scale2) + shift2
    h_hi = h2.astype(BF16)
    h2_ref[...] = h_hi
    h_lo = (h2 - h_hi.astype(F32)).astype(BF16)
    wr = wrt_ref[...]
    w_hi = wr.astype(BF16)
    w_lo = (wr - w_hi.astype(F32)).astype(BF16)
    dn = (((1,), (1,)), ((), ()))
    logits_t = (lax.dot_general(w_hi, h_hi, dn, preferred_element_type=F32)
                + lax.dot_general(w_lo, h_hi, dn, preferred_element_type=F32)
                + lax.dot_general(w_hi, h_lo, dn, preferred_element_type=F32))
    gates_ref[...] = _route(logits_t, rb_ref[...], tm)


def _post_call(x2, u2, y4, ao2, mod3, ssm_d, wglu_bf, wout_bf, norm2_w, wr_t, rbias3, seq_len, tm=512):
    n = x2.shape[0]
    bpb = seq_len // tm
    y3 = y4.reshape(2, n, S5_WIDTH)
    row = lambda i: (i, 0)
    const2 = lambda i: (0, 0)
    return pl.pallas_call(
        functools.partial(_post_kernel, tm=tm),
        grid=(n // tm,),
        in_specs=[pl.BlockSpec((tm, D_MODEL), row),
                  pl.BlockSpec((tm, S5_WIDTH), row),
                  pl.BlockSpec((1, tm, S5_WIDTH), lambda i: (0, i, 0)),
                  pl.BlockSpec((1, tm, S5_WIDTH), lambda i: (1, i, 0)),
                  pl.BlockSpec((tm, ATTN_WIDTH), row),
                  pl.BlockSpec((1, 1, 6 * D_MODEL), lambda i: (i // bpb, 0, 0)),
                  pl.BlockSpec((1, S5_WIDTH), const2),
                  pl.BlockSpec((S5_WIDTH, S5_WIDTH), const2),
                  pl.BlockSpec((D_MODEL, D_MODEL), const2),
                  pl.BlockSpec((1, D_MODEL), const2),
                  pl.BlockSpec((N_EXPERTS, D_MODEL), const2),
                  pl.BlockSpec((N_EXPERT_GROUPS, N_EXPERTS // N_EXPERT_GROUPS, 1), lambda i: (0, 0, 0))],
        out_specs=[pl.BlockSpec((tm, D_MODEL), row),
                   pl.BlockSpec((tm, D_MODEL), row),
                   pl.BlockSpec((N_EXPERTS, tm), lambda i: (0, i))],
        out_shape=[jax.ShapeDtypeStruct((n, D_MODEL), F32),
                   jax.ShapeDtypeStruct((n, D_MODEL), BF16),
                   jax.ShapeDtypeStruct((N_EXPERTS, n), F32)],
        compiler_params=_cparams(("parallel",)),
        name="post_mix_route",
    )(x2, u2, y3, y3, ao2, mod3, ssm_d.reshape(1, S5_WIDTH), wglu_bf, wout_bf,
      norm2_w.reshape(1, D_MODEL), wr_t, rbias3)


def _swiglu_bf(t, wg, wu, wd):
    a = jnp.dot(t, wg, preferred_element_type=F32)
    b = jnp.dot(t, wu, preferred_element_type=F32)
    act = a * jax.nn.sigmoid(a) * b
    return jnp.dot(act.astype(BF16), wd, preferred_element_type=F32)


MOE_SB = 256
MOE_SEG = 16
MOE_TF = 1024
MOE_XW = D_MODEL
MOE_STATIC_CHUNKS = 3
MOE_CHUNK = 512
MOE_DMA_SIZES = tuple(2 ** k for k in range(MOE_TF.bit_length() - 2, 3, -1))
MOE_RPAD = -(-(TOP_K * MOE_SB + N_EXPERTS * (MOE_SEG - 1) + MOE_STATIC_CHUNKS * MOE_SEG) // MOE_CHUNK) * MOE_CHUNK


MOE_PIECE_ROWS = (4 * MOE_SEG, 3 * MOE_SEG, 2 * MOE_SEG, MOE_SEG)
MOE_PIECE_WIDTH = (MOE_RPAD // (4 * MOE_SEG), N_EXPERTS, N_EXPERTS, N_EXPERTS)
MOE_NPIECE_REFS = 2 * len(MOE_PIECE_ROWS)
MOE_PACK = 4096


def _seg_copies(pieces, sb, make_copy, act):
    for ci, (width, rows) in enumerate(zip(MOE_PIECE_WIDTH, MOE_PIECE_ROWS)):
        p_ref, n_ref = pieces[2 * ci], pieces[2 * ci + 1]

        def body(k, carry, p_ref=p_ref, width=width, rows=rows):
            v = p_ref[sb * width + k]
            act(make_copy(pl.multiple_of(v & (MOE_PACK - 1), MOE_SEG),
                          pl.multiple_of(v >> (MOE_PACK.bit_length() - 1), MOE_SEG), rows))
            return carry
        lax.fori_loop(0, n_ref[sb], body, 0)


MOE_WAIT_SIZES = tuple(2 ** k for k in range((MOE_RPAD - 1).bit_length() - 1, 3, -1))


def _sub_block_rows(cnt_ref, loc_ref, sb):
    last = sb * N_EXPERTS + N_EXPERTS - 1
    return loc_ref[last] + cnt_ref[last]


def _wait_rows(rows, make_copy):
    for b in MOE_WAIT_SIZES:
        @pl.when((rows & b) != 0)
        def _(b=b):
            make_copy(b).wait()


def _build_onehot(gt, p_ref, cnt_ref, loc_ref, sb, weighted):
    t = gt.shape[1]
    selm = gt > 0.0
    r = lax.broadcasted_iota(jnp.int32, (t, t), 0)
    c = lax.broadcasted_iota(jnp.int32, (t, t), 1)
    upper = jnp.where(r < c, 1.0, 0.0).astype(BF16)
    rank = jnp.dot(jnp.where(selm, 1.0, 0.0).astype(BF16), upper, preferred_element_type=F32)
    key = jnp.where(selm, rank, -1.0)
    j16 = lax.broadcasted_iota(jnp.int32, (MOE_SEG, t), 0).astype(F32)
    zstart = jnp.minimum(_sub_block_rows(cnt_ref, loc_ref, sb), MOE_RPAD - MOE_CHUNK)
    p_ref[pl.ds(pl.multiple_of(zstart, MOE_SEG), MOE_CHUNK), :] = jnp.zeros((MOE_CHUNK, t), BF16)

    for e in range(N_EXPERTS):
        off = loc_ref[sb * N_EXPERTS + e]
        n = cnt_ref[sb * N_EXPERTS + e]
        row = key[e:e + 1, :]
        val = gt[e:e + 1, :] if weighted else 1.0

        def put(cidx, row=row, off=off, val=val):
            first = cidx * MOE_SEG
            firstf = float(first) if isinstance(first, int) else first.astype(F32)
            tile = jnp.where(row == j16 + firstf, val, 0.0).astype(BF16)
            p_ref[pl.ds(pl.multiple_of(off + first, MOE_SEG), MOE_SEG), :] = tile

        for cidx in range(MOE_STATIC_CHUNKS):
            put(cidx)

        def extra(cidx, carry, put=put):
            put(cidx)
            return carry
        lax.fori_loop(MOE_STATIC_CHUNKS, n // MOE_SEG, extra, 0)


def _dispatch_kernel(cnt_ref, loc_ref, *refs, nsb_p, n_sb):
    pieces = refs[:MOE_NPIECE_REFS]
    padoff_ref, padn_ref, hp_ref, hs_ref, gt_ref, xe_ref, p_ref, xg_ref, z_ref, sem = refs[MOE_NPIECE_REFS:]
    s = pl.program_id(0)
    slot = s % 2

    def copies(sb, slot_, act):
        def mk(off, g, b):
            return pltpu.make_async_copy(xg_ref.at[slot_, pl.ds(off, b)], xe_ref.at[pl.ds(g, b)], sem.at[slot_])
        _seg_copies(pieces, sb, mk, act)

    def wait_all(sb, slot_):
        _wait_rows(_sub_block_rows(cnt_ref, loc_ref, sb),
                   lambda b: pltpu.make_async_copy(xg_ref.at[slot_, pl.ds(0, b)], xe_ref.at[pl.ds(0, b)],
                                                   sem.at[slot_]))

    @pl.when(s >= 2)
    def _():
        wait_all(s - 2, slot)

    gt = gt_ref[...]
    _build_onehot(gt, p_ref, cnt_ref, loc_ref, s, weighted=False)
    xext = jnp.where(s < nsb_p, hp_ref[...], hs_ref[...])
    rows = _sub_block_rows(cnt_ref, loc_ref, s)
    for i in range(MOE_RPAD // MOE_CHUNK):
        @pl.when(i * MOE_CHUNK < rows)
        def _(i=i):
            xg_ref[slot, i * MOE_CHUNK:(i + 1) * MOE_CHUNK, :] = jnp.dot(
                p_ref[i * MOE_CHUNK:(i + 1) * MOE_CHUNK, :], xext, preferred_element_type=F32).astype(BF16)
    copies(s, slot, lambda cp: cp.start())

    @pl.when(s == n_sb - 1)
    def _():
        if n_sb >= 2:
            wait_all(s - 1, 1 - slot)
        wait_all(s, slot)
        z_ref[...] = jnp.zeros(z_ref.shape, BF16)

        def pads(act):
            def body(e, carry):
                n = padn_ref[e]
                off = padoff_ref[e]
                for b in MOE_DMA_SIZES:
                    @pl.when((n & b) != 0)
                    def _(b=b):
                        done = n & (-2 * b)
                        act(pltpu.make_async_copy(
                            z_ref.at[pl.ds(0, b)],
                            xe_ref.at[pl.ds(pl.multiple_of(off + done, MOE_SEG), b)], sem.at[0]))
                return carry
            lax.fori_loop(0, N_EXPERTS, body, 0)
        pads(lambda cp: cp.start())
        pads(lambda cp: cp.wait())


def _ffn_kernel(te_ref, tidx_ref, tcode_ref, xe_ref, wg_ref, wu_ref, wd_ref, ye_ref, wgub, wdb):
    i = pl.program_id(0)
    code = tcode_ref[i]

    @pl.when(code == 2)
    def _():
        wgub[:, :EXPERT_FF] = wg_ref[0].astype(BF16)
        wgub[:, EXPERT_FF:] = wu_ref[0].astype(BF16)
        wdb[...] = wd_ref[0].astype(BF16)

    @pl.when(code != 0)
    def _():
        ab = jnp.dot(xe_ref[...], wgub[...], preferred_element_type=F32)
        a = ab[:, :EXPERT_FF]
        act = a * jax.nn.sigmoid(a) * ab[:, EXPERT_FF:]
        ye_ref[...] = jnp.dot(act.astype(BF16), wdb[...], preferred_element_type=F32).astype(BF16)


def _combine_kernel(cnt_ref, loc_ref, *refs, sb0, n_steps):
    pieces = refs[:MOE_NPIECE_REFS]
    (ye_ref, gt_ref, h_ref, x1_ref, mod_ref, sg_ref, su_ref, sd_ref, fn_ref, o_ref,
     p_ref, ys_ref, acc_ref, sem) = refs[MOE_NPIECE_REFS:]
    i = pl.program_id(0)
    slot = i % 2
    sb = sb0 + i

    def copies(sb_, slot_, act):
        def mk(off, g, b):
            return pltpu.make_async_copy(ye_ref.at[pl.ds(g, b)], ys_ref.at[slot_, pl.ds(off, b)], sem.at[slot_])
        _seg_copies(pieces, sb_, mk, act)

    @pl.when(i == 0)
    def _():
        ys_ref[...] = jnp.zeros(ys_ref.shape, BF16)
        copies(sb, slot, lambda cp: cp.start())

    @pl.when(i + 1 < n_steps)
    def _():
        copies(sb + 1, 1 - slot, lambda cp: cp.start())

    acc_ref[...] = _swiglu_bf(h_ref[...], sg_ref[...], su_ref[...], sd_ref[...])
    _build_onehot(gt_ref[...], p_ref, cnt_ref, loc_ref, sb, weighted=True)
    rows = _sub_block_rows(cnt_ref, loc_ref, sb)
    _wait_rows(rows, lambda b: pltpu.make_async_copy(ye_ref.at[pl.ds(0, b)], ys_ref.at[slot, pl.ds(0, b)],
                                                     sem.at[slot]))
    for c in range(MOE_RPAD // MOE_CHUNK):
        @pl.when(c * MOE_CHUNK < rows)
        def _(c=c):
            acc_ref[...] += lax.dot_general(
                p_ref[c * MOE_CHUNK:(c + 1) * MOE_CHUNK, :], ys_ref[slot, c * MOE_CHUNK:(c + 1) * MOE_CHUNK, :],
                (((0,), (0,)), ((), ())), preferred_element_type=F32)
    gate2 = mod_ref[0][:, 5 * D_MODEL:6 * D_MODEL]
    x2 = x1_ref[...] + gate2 * acc_ref[...]
    ms = jnp.mean(x2 * x2, axis=-1, keepdims=True)
    o_ref[...] = x2 * lax.rsqrt(ms + NORM_EPS) * fn_ref[...]


def _moe_plan(gates_t):
    ne, n = gates_t.shape
    n_sb = n // MOE_SB
    cnt = jnp.sum((gates_t > 0.0).reshape(ne, n_sb, MOE_SB), axis=-1, dtype=jnp.int32).T
    cnt16 = (cnt + MOE_SEG - 1) // MOE_SEG * MOE_SEG
    loc = jnp.cumsum(cnt16, axis=1) - cnt16
    tot = jnp.sum(cnt16, axis=0)
    totp = (tot + MOE_TF - 1) // MOE_TF * MOE_TF
    ends = jnp.cumsum(totp)
    base = ends - totp
    goff = base[None, :] + jnp.cumsum(cnt16, axis=0) - cnt16
    rows_max = TOP_K * n + n_sb * ne * (MOE_SEG - 1) + ne * (MOE_TF - MOE_SEG)
    nt_max = -(-rows_max // MOE_TF)
    tiles = jnp.arange(nt_max, dtype=jnp.int32)
    used = ends[-1]
    valid = tiles * MOE_TF < used
    tidx = jnp.where(valid, tiles, jnp.maximum(used // MOE_TF - 1, 0))
    te = jnp.sum(ends[None, :] <= (tidx * MOE_TF)[:, None], axis=1, dtype=jnp.int32)
    te = jnp.minimum(te, ne - 1)
    first = jnp.concatenate([jnp.ones((1,), jnp.bool_), te[1:] != te[:-1]])
    tcode = jnp.where(valid, 1 + first.astype(jnp.int32), 0)

    def piece_list(npieces, first_row, width, rows):
        cum = jnp.cumsum(npieces, axis=1)
        k = jnp.arange(width, dtype=jnp.int32)
        ek = jnp.minimum(jnp.sum(cum[:, None, :] <= k[None, :, None], axis=2, dtype=jnp.int32), ne - 1)
        onehot = ek[:, :, None] == jnp.arange(ne, dtype=jnp.int32)[None, None, :]
        pick = lambda a: jnp.sum(jnp.where(onehot, a[:, None, :], 0), axis=2, dtype=jnp.int32)
        row = pick(first_row) + rows * (k[None, :] - pick(cum - npieces))
        packed = (pick(goff) + row) * MOE_PACK + pick(loc) + row
        return packed.reshape(-1).astype(jnp.int32), cum[:, -1].astype(jnp.int32)

    big = MOE_PIECE_ROWS[0]
    nbig = cnt16 // big
    pieces = list(piece_list(nbig, jnp.zeros_like(cnt16), MOE_PIECE_WIDTH[0], big))
    for width, rows in zip(MOE_PIECE_WIDTH[1:], MOE_PIECE_ROWS[1:]):
        pieces += list(piece_list((cnt16 % big == rows).astype(jnp.int32), nbig * big, width, rows))
    return dict(cnt=cnt16.reshape(-1), loc=loc.reshape(-1).astype(jnp.int32), pieces=tuple(pieces),
                padoff=(base + tot).astype(jnp.int32), padn=(totp - tot).astype(jnp.int32),
                te=te, tidx=tidx.astype(jnp.int32), tcode=tcode, nt_max=nt_max)


def _dispatch_call(plan, h2_p, h2_s, gates_t):
    nsb_p = h2_p.shape[0] // MOE_SB
    n_sb = gates_t.shape[1] // MOE_SB
    grid_spec = pltpu.PrefetchScalarGridSpec(
        num_scalar_prefetch=4 + MOE_NPIECE_REFS, grid=(n_sb,),
        in_specs=[pl.BlockSpec((MOE_SB, D_MODEL), lambda s, *_: (jnp.minimum(s, nsb_p - 1), 0)),
                  pl.BlockSpec((MOE_SB, D_MODEL), lambda s, *_: (jnp.maximum(s - nsb_p, 0), 0)),
                  pl.BlockSpec((N_EXPERTS, MOE_SB), lambda s, *_: (0, s))],
        out_specs=pl.BlockSpec(memory_space=pl.ANY),
        scratch_shapes=[pltpu.VMEM((MOE_RPAD, MOE_SB), BF16),
                        pltpu.VMEM((2, MOE_RPAD, MOE_XW), BF16),
                        pltpu.VMEM((MOE_DMA_SIZES[0], MOE_XW), BF16),
                        pltpu.SemaphoreType.DMA((2,))])
    return pl.pallas_call(
        functools.partial(_dispatch_kernel, nsb_p=nsb_p, n_sb=n_sb),
        grid_spec=grid_spec,
        out_shape=jax.ShapeDtypeStruct((plan['nt_max'] * MOE_TF, MOE_XW), BF16),
        compiler_params=_cparams(("arbitrary",)),
        name="moe_dispatch",
    )(plan['cnt'], plan['loc'], *plan['pieces'], plan['padoff'], plan['padn'], h2_p, h2_s, gates_t)


def _ffn_call(plan, xe, wg, wu, wd):
    grid_spec = pltpu.PrefetchScalarGridSpec(
        num_scalar_prefetch=3, grid=(plan['nt_max'],),
        in_specs=[pl.BlockSpec((MOE_TF, MOE_XW), lambda i, te, tidx, tv: (tidx[i], 0)),
                  pl.BlockSpec((1, D_MODEL, EXPERT_FF), lambda i, te, tidx, tv: (te[i], 0, 0)),
                  pl.BlockSpec((1, D_MODEL, EXPERT_FF), lambda i, te, tidx, tv: (te[i], 0, 0)),
                  pl.BlockSpec((1, EXPERT_FF, D_MODEL), lambda i, te, tidx, tv: (te[i], 0, 0))],
        out_specs=pl.BlockSpec((MOE_TF, D_MODEL), lambda i, te, tidx, tv: (tidx[i], 0)),
        scratch_shapes=[pltpu.VMEM((D_MODEL, 2 * EXPERT_FF), BF16),
                        pltpu.VMEM((EXPERT_FF, D_MODEL), BF16)])
    return pl.pallas_call(
        _ffn_kernel,
        grid_spec=grid_spec,
        out_shape=jax.ShapeDtypeStruct((xe.shape[0], D_MODEL), BF16),
        compiler_params=_cparams(("arbitrary",)),
        name="moe_ffn",
    )(plan['te'], plan['tidx'], plan['tcode'], xe, wg, wu, wd)


def _combine_call(plan, ye, gates_t, h2, x1, mod3, sg, su, sd, final_w, sb0, seq_len):
    n = h2.shape[0]
    n_steps = n // MOE_SB
    bpb = seq_len // MOE_SB
    row = lambda i, *_: (i, 0)
    const2 = lambda i, *_: (0, 0)
    grid_spec = pltpu.PrefetchScalarGridSpec(
        num_scalar_prefetch=2 + MOE_NPIECE_REFS, grid=(n_steps,),
        in_specs=[pl.BlockSpec(memory_space=pl.ANY),
                  pl.BlockSpec((N_EXPERTS, MOE_SB), lambda i, *_: (0, sb0 + i)),
                  pl.BlockSpec((MOE_SB, D_MODEL), row),
                  pl.BlockSpec((MOE_SB, D_MODEL), row),
                  pl.BlockSpec((1, 1, 6 * D_MODEL), lambda i, *_: (i // bpb, 0, 0)),
                  pl.BlockSpec((D_MODEL, EXPERT_FF), const2),
                  pl.BlockSpec((D_MODEL, EXPERT_FF), const2),
                  pl.BlockSpec((EXPERT_FF, D_MODEL), const2),
                  pl.BlockSpec((1, D_MODEL), const2)],
        out_specs=pl.BlockSpec((MOE_SB, D_MODEL), row),
        scratch_shapes=[pltpu.VMEM((MOE_RPAD, MOE_SB), BF16),
                        pltpu.VMEM((2, MOE_RPAD, D_MODEL), BF16),
                        pltpu.VMEM((MOE_SB, D_MODEL), F32),
                        pltpu.SemaphoreType.DMA((2,))])
    return pl.pallas_call(
        functools.partial(_combine_kernel, sb0=sb0, n_steps=n_steps),
        grid_spec=grid_spec,
        out_shape=jax.ShapeDtypeStruct((n, D_MODEL), F32),
        compiler_params=_cparams(("arbitrary",)),
        name="moe_combine",
    )(plan['cnt'], plan['loc'], *plan['pieces'], ye, gates_t, h2, x1, mod3, sg, su, sd,
      final_w.reshape(1, D_MODEL))


def _moe_sparse(h2_p, h2_s, gt_p, gt_s, x1_p, x1_s, mod_p, mod_s, wg, wu, wd, sg, su, sd, final_w, dseq):
    gates_t = jnp.concatenate([gt_p, gt_s], axis=1)
    plan = _moe_plan(gates_t)
    xe = _dispatch_call(plan, h2_p, h2_s, gates_t)
    ye = _ffn_call(plan, xe, wg, wu, wd)
    n_p = h2_p.shape[0]
    y_p = _combine_call(plan, ye, gates_t, h2_p, x1_p, mod_p, sg, su, sd, final_w, 0, n_p)
    y_s = _combine_call(plan, ye, gates_t, h2_s, x1_s, mod_s, sg, su, sd, final_w, n_p // MOE_SB, dseq)
    return y_p, y_s


def _rope_tables(n_tokens):
    rows = n_tokens // GRID_W
    row = jnp.repeat(jnp.arange(rows, dtype=F32), GRID_W)
    col = jnp.tile(jnp.arange(GRID_W, dtype=F32), rows)
    freqs = ROPE_THETA ** (-jnp.arange(ROT_PAIRS, dtype=F32) / ROT_PAIRS)
    ar = row[:, None] * freqs
    ac = col[:, None] * freqs
    cos = jnp.concatenate([jnp.cos(ar), jnp.cos(ar), jnp.cos(ac), jnp.cos(ac)], axis=1)
    sin = jnp.concatenate([-jnp.sin(ar), jnp.sin(ar), -jnp.sin(ac), jnp.sin(ac)], axis=1)
    return jnp.tile(cos, (1, 2)), jnp.tile(sin, (1, 2))


def kernel(x_prompt, x_sample, c, cache_k, cache_v, state_ssm_re, state_ssm_im, c_ctx, w_ada, b_ada, norm1_w, w_in, ssm_lambda_re, ssm_lambda_im, ssm_log_dt, ssm_b_re, ssm_b_im, ssm_c_re, ssm_c_im, ssm_d, ssm_w_glu, diff_lambda_q, diff_lambda_k, diff_subln_w, w_out, norm2_w, w_router, router_bias, w_exp_gate, w_exp_up, w_exp_down, w_sh_gate, w_sh_up, w_sh_down, final_norm_w):
    depth = w_ada.shape[0]
    assert depth == 1
    l = 0
    lambda_init = 0.8 - 0.6 * math.exp(-0.3 * l)
    bsz, seq, _ = x_prompt.shape
    dbs, dseq, _ = x_sample.shape
    n_p, n_s = bsz * seq, dbs * dseq

    cond8 = jnp.zeros((SUBLANES, D_MODEL), F32).at[:dbs].set(c).at[dbs].set(c_ctx)
    mod = _ada_call(cond8, w_ada[l], b_ada[l])
    mod_s = mod[:dbs].reshape(dbs, 1, 6 * D_MODEL)
    mod_p = mod[dbs:dbs + 1].reshape(1, 1, 6 * D_MODEL)

    w_in_bf = w_in[l].astype(BF16)
    xp2 = x_prompt.reshape(n_p, D_MODEL)
    xs2 = x_sample.reshape(n_s, D_MODEL)
    u_p, q_p, k_p, v_p = _inproj_call(xp2, mod_p, norm1_w[l], w_in_bf, None, n_p, F32)
    u_s, q_s, k_s, v_s = _inproj_call(xs2, mod_s, norm1_w[l], w_in_bf, _rope_tables(dseq), dseq, BF16)

    bblk, cblk, lam_r, lam_i = _s5_weights(ssm_lambda_re[l], ssm_lambda_im[l], ssm_log_dt[l],
                                           ssm_b_re[l], ssm_b_im[l], ssm_c_re[l], ssm_c_im[l])
    lr_p = jnp.broadcast_to(lam_r[:, :, None, :], (2, N_COLBLK, SUBLANES, S5_BLK_STATES))
    li_p = jnp.broadcast_to(lam_i[:, :, None, :], (2, N_COLBLK, SUBLANES, S5_BLK_STATES))
    g_p = bsz // SUBLANES
    h0_p = jnp.zeros((2, g_p, N_COLBLK * 8, SUBLANES, LANES), F32)
    y_p, hfin = _s5_call(u_p.reshape(bsz, seq, S5_WIDTH), bblk, cblk, lr_p, li_p, h0_p, SUBLANES, tb=128)

    def halves(x):
        x = x.reshape(2, 2, 2, 1, S5_BLK_STATES)
        x = jnp.broadcast_to(x, (2, 2, 2, dbs, S5_BLK_STATES))
        return x.transpose(0, 2, 1, 3, 4).reshape(2, 2, 2 * dbs, S5_BLK_STATES)

    def h0_tiles(s):
        s = s.astype(F32).reshape(dbs, 2, 2, 2, 4, LANES)
        return s.transpose(1, 3, 4, 2, 0, 5).reshape(2, 1, 2, 4, 2 * dbs, LANES)

    h0_s = jnp.concatenate([h0_tiles(state_ssm_re[:, l]), h0_tiles(state_ssm_im[:, l])], axis=3)
    h0_s = h0_s.reshape(2, 1, 16, SUBLANES, LANES)
    y_s, _ = _s5_call(u_s.reshape(dbs, dseq, S5_WIDTH), bblk, cblk, halves(lam_r), halves(lam_i), h0_s, dbs,
                      tb=256)

    lq = diff_lambda_q[l].astype(F32)
    lk = diff_lambda_k[l].astype(F32)
    lam = (jnp.exp(jnp.sum(lq[0] * lk[0])) - jnp.exp(jnp.sum(lq[1] * lk[1])) + lambda_init).reshape(1)
    subln = diff_subln_w[l].astype(F32).reshape(1, VALUE_DIM)
    ao_p = _attn_ctx_call(lam, q_p.reshape(bsz, seq, ATTN_WIDTH), k_p.reshape(bsz, seq, ATTN_WIDTH),
                          v_p.reshape(bsz, seq, ATTN_WIDTH), subln, lambda_init)
    past = cache_k.shape[2]
    ao_s = _attn_lat_call(lam, q_s.reshape(dbs, dseq, ATTN_WIDTH), k_s.reshape(dbs, dseq, ATTN_WIDTH),
                          v_s.reshape(dbs, dseq, ATTN_WIDTH),
                          cache_k[:, l].reshape(dbs, past, ATTN_WIDTH).astype(BF16),
                          cache_v[:, l].reshape(dbs, past, ATTN_WIDTH).astype(BF16), subln, lambda_init)

    wglu_bf = ssm_w_glu[l].astype(BF16)
    wout_bf = w_out[l].astype(BF16)
    wr_t = w_router[l].astype(F32).T
    rbias3 = router_bias[l].astype(F32).reshape(N_EXPERT_GROUPS, N_EXPERTS // N_EXPERT_GROUPS, 1)
    x1_p, h2_p, gates_p = _post_call(xp2, u_p, y_p, ao_p.reshape(n_p, ATTN_WIDTH), mod_p, ssm_d[l],
                                     wglu_bf, wout_bf, norm2_w[l], wr_t, rbias3, n_p)
    x1_s, h2_s, gates_s = _post_call(xs2, u_s, y_s, ao_s.reshape(n_s, ATTN_WIDTH), mod_s, ssm_d[l],
                                     wglu_bf, wout_bf, norm2_w[l], wr_t, rbias3, dseq)

    wg = w_exp_gate[l]
    wu = w_exp_up[l]
    wd = w_exp_down[l]
    sg = w_sh_gate[l].astype(BF16)
    su = w_sh_up[l].astype(BF16)
    sd = w_sh_down[l].astype(BF16)
    y_prompt, y_sample = _moe_sparse(h2_p, h2_s, gates_p, gates_s, x1_p, x1_s, mod_p, mod_s,
                                     wg, wu, wd, sg, su, sd, final_norm_w, dseq)

    new_cache_k = k_p.reshape(bsz, 1, seq, N_DIFF_HEADS, VALUE_DIM)
    new_cache_v = v_p.reshape(bsz, 1, seq, N_DIFF_HEADS, VALUE_DIM)
    hf = hfin.reshape(2, g_p, N_COLBLK, 2, 4, SUBLANES, LANES)
    hf = hf.transpose(3, 1, 5, 0, 2, 4, 6).reshape(2, bsz, 1, 2, S5_GROUPS, S5_STATE)
    return (y_prompt.reshape(bsz, seq, D_MODEL), y_sample.reshape(dbs, dseq, D_MODEL),
            new_cache_k, new_cache_v, hf[0], hf[1])
```
